```python
import jax
import jax.numpy as jnp
from jax import lax
import numpy as np

D_MODEL = 2048
BATCH = 4
SEQ = 8192
DEPTH = 1

CTX_LEN = 256
GRID_W = 64
HEAD_DIM = D_MODEL // 16
N_HEADS_A = 8
N_KV_A = 2
GQA_GROUP = N_HEADS_A // N_KV_A
WINDOW = 128
BLOCK = 128
N_HEADS_B = 8
NB_ROWS = 8
NB_COLS = 16
D_FF = ((8 * D_MODEL // 3 + 255) // 256) * 256
CONV_W = 3
ROPE_BASE = 10000.0
EPS = 1e-6
NEG_INF = -1e30

W_QA = N_HEADS_A * HEAD_DIM
W_KA = N_KV_A * HEAD_DIM
W_B = N_HEADS_B * HEAD_DIM
SPLIT_WIDTHS = (W_QA, W_KA, W_KA, W_B, W_B, W_B, D_MODEL, D_MODEL)
SPLIT_POINTS = tuple(sum(SPLIT_WIDTHS[:i + 1]) for i in range(len(SPLIT_WIDTHS) - 1))
D_IN = sum(SPLIT_WIDTHS)

kernel_name = 'hybrid_dit_window_gqa_natten_convffn'


def rms_norm(x, g):
    xf = x.astype(jnp.float32)
    y = xf * lax.rsqrt(jnp.mean(jnp.square(xf), axis=-1, keepdims=True) + EPS)
    return (y * g.astype(jnp.float32)).astype(x.dtype)


def modulate(h, shift, scale):
    return h * (1 + scale) + shift


def axial_rope(t, row, col):
    half = t.shape[-1] // 2

    def rot(ta, pos):
        n = ta.shape[-1] // 2
        inv = ROPE_BASE ** (-jnp.arange(n, dtype=jnp.float32) / n)
        ang = pos.astype(jnp.float32)[:, None] * inv[None, :]
        cos = jnp.cos(ang)[None, :, None, :]
        sin = jnp.sin(ang)[None, :, None, :]
        t1 = ta[..., :n].astype(jnp.float32)
        t2 = ta[..., n:].astype(jnp.float32)
        return jnp.concatenate([t1 * cos - t2 * sin, t2 * cos + t1 * sin], axis=-1).astype(ta.dtype)

    return jnp.concatenate([rot(t[..., :half], row), rot(t[..., half:], col)], axis=-1)


def context_attention(qc, kc, vc, sink):
    b, l, kv, g, d = qc.shape
    s = jnp.einsum('bqkgd,bjkd->bkgqj', qc, kc, preferred_element_type=jnp.float32) * (d ** -0.5)
    if sink is not None:
        sink_col = jnp.broadcast_to(sink.astype(jnp.float32)[None, :, :, None, None], s.shape[:-1] + (1,))
        s = jnp.concatenate([s, sink_col], axis=-1)
    p = jax.nn.softmax(s, axis=-1)
    if sink is not None:
        p = p[..., :-1]
    o = jnp.einsum('bkgqj,bjkd->bqkgd', p.astype(vc.dtype), vc)
    return o.reshape(b, l, kv * g * d)


def window_gqa_latent(q, k, v, kc, vc, sink):
    b, s, _, d = q.shape
    nb = s // BLOCK
    scale = d ** -0.5
    qb = q.reshape(b, nb, BLOCK, N_KV_A, GQA_GROUP, d)

    def band(t):
        tp = jnp.pad(t, ((0, 0), (BLOCK, BLOCK), (0, 0), (0, 0))).reshape(b, nb + 2, BLOCK, N_KV_A, d)
        return jnp.concatenate([tp[:, :-2], tp[:, 1:-1], tp[:, 2:]], axis=2)

    kb, vb = band(k), band(v)
    s_loc = jnp.einsum('bnqkgd,bnjkd->bnkgqj', qb, kb, preferred_element_type=jnp.float32) * scale
    blk = jnp.arange(nb)[:, None, None]
    q_pos = blk * BLOCK + jnp.arange(BLOCK)[None, :, None]
    k_pos = (blk - 1) * BLOCK + jnp.arange(3 * BLOCK)[None, None, :]
    valid = (k_pos >= 0) & (k_pos < s) & (jnp.abs(k_pos - q_pos) <= WINDOW)
    s_loc = jnp.where(valid[None, :, None, None], s_loc, NEG_INF)
    s_ctx = jnp.einsum('bnqkgd,bjkd->bnkgqj', qb, kc, preferred_element_type=jnp.float32) * scale
    sink_col = jnp.broadcast_to(
        sink.reshape(N_KV_A, GQA_GROUP).astype(jnp.float32)[None, None, :, :, None, None],
        s_ctx.shape[:-1] + (1,))
    p = jax.nn.softmax(jnp.concatenate([s_loc, s_ctx, sink_col], axis=-1), axis=-1).astype(v.dtype)
    n_loc = 3 * BLOCK
    n_ctx = kc.shape[1]
    o = (jnp.einsum('bnkgqj,bnjkd->bnqkgd', p[..., :n_loc], vb)
         + jnp.einsum('bnkgqj,bjkd->bnqkgd', p[..., n_loc:n_loc + n_ctx], vc))
    return o.reshape(b, s, N_HEADS_A * d)


def neighbourhood_latent(q, k, v, kc, vc, rpb):
    b, s, h, d = q.shape
    rows = s // GRID_W
    kr = min(NB_ROWS, rows)
    scale = d ** -0.5
    r = jnp.arange(rows)
    row_idx = jnp.clip(r - kr // 2, 0, rows - kr)[:, None] + jnp.arange(kr)[None, :]
    cq = jnp.arange(GRID_W)
    col_start = jnp.clip(cq - NB_COLS // 2, 0, GRID_W - NB_COLS)
    col_valid = (cq[None, :] >= col_start[:, None]) & (cq[None, :] < col_start[:, None] + NB_COLS)
    qg = q.reshape(b, rows, GRID_W, h, d)
    kg = k.reshape(b, rows, GRID_W, h, d)[:, row_idx]
    vg = v.reshape(b, rows, GRID_W, h, d)[:, row_idx]
    s_nb = jnp.einsum('brqhd,brikhd->brhqik', qg, kg, preferred_element_type=jnp.float32) * scale
    dr = row_idx - r[:, None] + (NB_ROWS - 1)
    dc = jnp.clip(cq[None, :] - cq[:, None], -(NB_COLS - 1), NB_COLS - 1) + (NB_COLS - 1)
    bias = rpb[:, dr[:, :, None, None], dc[None, None, :, :]]
    bias = jnp.transpose(bias, (1, 0, 3, 2, 4)).astype(jnp.float32)
    s_nb = jnp.where(col_valid[:, None, :], s_nb + bias[None], NEG_INF)
    n_loc = kr * GRID_W
    s_nb = s_nb.reshape(b, rows, h, GRID_W, n_loc)
    s_ctx = jnp.einsum('brqhd,bjhd->brhqj', qg, kc, preferred_element_type=jnp.float32) * scale
    p = jax.nn.softmax(jnp.concatenate([s_nb, s_ctx], axis=-1), axis=-1).astype(v.dtype)
    p_nb = p[..., :n_loc].reshape(b, rows, h, GRID_W, kr, GRID_W)
    o = (jnp.einsum('brhqik,brikhd->brqhd', p_nb, vg)
         + jnp.einsum('brhqj,bjhd->brqhd', p[..., n_loc:], vc))
    return o.reshape(b, s, h * d)


def mixer_sublayer(h, hc, w_in, sink_a, rpb_b, w_br_a, w_br_b, w_o, need_ctx):
    b, s, _ = h.shape
    l = hc.shape[1]
    t = jnp.arange(s)
    row, col = t // GRID_W, t % GRID_W
    qa, ka, va, qb, kb, vb, ga, gb = jnp.split(h @ w_in, SPLIT_POINTS, axis=-1)
    qa_c, ka_c, va_c, qb_c, kb_c, vb_c, ga_c, gb_c = jnp.split(hc @ w_in, SPLIT_POINTS, axis=-1)

    qa = axial_rope(qa.reshape(b, s, N_HEADS_A, HEAD_DIM), row, col)
    ka = axial_rope(ka.reshape(b, s, N_KV_A, HEAD_DIM), row, col)
    va = va.reshape(b, s, N_KV_A, HEAD_DIM)
    ka_c = ka_c.reshape(b, l, N_KV_A, HEAD_DIM)
    va_c = va_c.reshape(b, l, N_KV_A, HEAD_DIM)
    qb = qb.reshape(b, s, N_HEADS_B, HEAD_DIM)
    kb = kb.reshape(b, s, N_HEADS_B, HEAD_DIM)
    vb = vb.reshape(b, s, N_HEADS_B, HEAD_DIM)
    kb_c = kb_c.reshape(b, l, N_HEADS_B, HEAD_DIM)
    vb_c = vb_c.reshape(b, l, N_HEADS_B, HEAD_DIM)

    oa = window_gqa_latent(qa, ka, va, ka_c, va_c, sink_a)
    ob = neighbourhood_latent(qb, kb, vb, kb_c, vb_c, rpb_b)
    y = (jax.nn.sigmoid(ga) * (oa @ w_br_a) + jax.nn.sigmoid(gb) * (ob @ w_br_b)) @ w_o

    yc = None
    if need_ctx:
        oa_c = context_attention(qa_c.reshape(b, l, N_KV_A, GQA_GROUP, HEAD_DIM), ka_c, va_c,
                                 sink_a.reshape(N_KV_A, GQA_GROUP))
        ob_c = context_attention(qb_c.reshape(b, l, N_HEADS_B, 1, HEAD_DIM), kb_c, vb_c, None)
        yc = (jax.nn.sigmoid(ga_c) * (oa_c @ w_br_a) + jax.nn.sigmoid(gb_c) * (ob_c @ w_br_b)) @ w_o
    return y, yc


def depthwise_conv(u, w, bias):
    pad = CONV_W // 2
    t = u.shape[1]
    up = jnp.pad(u, ((0, 0), (pad, pad), (0, 0)))
    out = bias
    for j in range(CONV_W):
        out = out + up[:, j:j + t] * w[j]
    return out


def conv_ffn(h, w_up, conv_w, conv_b, w_down):
    u = depthwise_conv(h @ w_up, conv_w, conv_b)
    a, g = jnp.split(u, 2, axis=-1)
    return (jax.nn.silu(g) * a) @ w_down


def setup_inputs(seed: int = 0) -> dict:
    key = jax.random.key(seed)
    ks = jax.random.split(key, 20)

    def nrm(k, shape, scale):
        return jax.random.normal(k, shape, jnp.float32) * scale

    def gain(k):
        return 1.0 + nrm(k, (DEPTH, D_MODEL), 0.05)

    return {
        'x': nrm(ks[0], (BATCH, SEQ, D_MODEL), 1.0),
        'c': nrm(ks[1], (BATCH, D_MODEL), 1.0),
        'ctx': nrm(ks[2], (BATCH, CTX_LEN, D_MODEL), 1.0),
        'c_ctx': nrm(ks[3], (D_MODEL,), 1.0),
        'w_mod': nrm(ks[4], (DEPTH, D_MODEL, 6 * D_MODEL), D_MODEL ** -0.5),
        'b_mod': nrm(ks[5], (DEPTH, 6 * D_MODEL), 0.01),
        'g_attn_pre': gain(ks[6]),
        'g_attn_post': gain(ks[7]),
        'g_ffn_pre': gain(ks[8]),
        'g_ffn_post': gain(ks[9]),
        'w_in': nrm(ks[10], (DEPTH, D_MODEL, D_IN), D_MODEL ** -0.5),
        'sink_a': nrm(ks[11], (DEPTH, N_HEADS_A), 1.0),
        'rpb_b': nrm(ks[12], (DEPTH, N_HEADS_B, 2 * NB_ROWS - 1, 2 * NB_COLS - 1), 0.5),
        'w_br_a': nrm(ks[13], (DEPTH, W_QA, D_MODEL), W_QA ** -0.5),
        'w_br_b': nrm(ks[14], (DEPTH, W_B, D_MODEL), W_B ** -0.5),
        'w_o': nrm(ks[15], (DEPTH, D_MODEL, D_MODEL), D_MODEL ** -0.5),
        'w_up': nrm(ks[16], (DEPTH, D_MODEL, 2 * D_FF), D_MODEL ** -0.5),
        'conv_w': nrm(ks[17], (DEPTH, CONV_W, 2 * D_FF), CONV_W ** -0.5),
        'conv_b': nrm(ks[18], (DEPTH, 2 * D_FF), 0.01),
        'w_down': nrm(ks[19], (DEPTH, D_FF, D_MODEL), D_FF ** -0.5),
    }


def reference(x, c, ctx, c_ctx, w_mod, b_mod, g_attn_pre, g_attn_post, g_ffn_pre, g_ffn_post,
              w_in, sink_a, rpb_b, w_br_a, w_br_b, w_o, w_up, conv_w, conv_b, w_down):
    for l in range(DEPTH):
        need_ctx = l < DEPTH - 1
        mod = jax.nn.silu(c) @ w_mod[l] + b_mod[l]
        mod_c = jax.nn.silu(c_ctx) @ w_mod[l] + b_mod[l]
        sh1, sc1, gt1, sh2, sc2, gt2 = [m[:, None, :] for m in jnp.split(mod, 6, axis=-1)]
        csh1, csc1, cgt1, csh2, csc2, cgt2 = jnp.split(mod_c, 6, axis=-1)

        h = modulate(rms_norm(x, g_attn_pre[l]), sh1, sc1)
        hc = modulate(rms_norm(ctx, g_attn_pre[l]), csh1, csc1)
        y, yc = mixer_sublayer(h, hc, w_in[l], sink_a[l], rpb_b[l], w_br_a[l], w_br_b[l], w_o[l], need_ctx)
        x = x + gt1 * rms_norm(y, g_attn_post[l])

        h = modulate(rms_norm(x, g_ffn_pre[l]), sh2, sc2)
        x = x + gt2 * rms_norm(conv_ffn(h, w_up[l], conv_w[l], conv_b[l], w_down[l]), g_ffn_post[l])

        if need_ctx:
            ctx = ctx + cgt1 * rms_norm(yc, g_attn_post[l])
            hc = modulate(rms_norm(ctx, g_ffn_pre[l]), csh2, csc2)
            ctx = ctx + cgt2 * rms_norm(conv_ffn(hc, w_up[l], conv_w[l], conv_b[l], w_down[l]), g_ffn_post[l])
    return x
```

```python
import functools

import jax
import jax.numpy as jnp
from jax import lax
from jax.experimental import pallas as pl
from jax.experimental.pallas import tpu as pltpu

F32 = jnp.float32
BF16 = jnp.bfloat16

GRID_W = 64
HEAD_DIM = 128
N_HEADS_A = 8
N_KV_A = 2
GQA_GROUP = N_HEADS_A // N_KV_A
WINDOW = 128
BLOCK = 128
N_HEADS_B = 8
NB_ROWS = 8
NB_COLS = 16
CONV_W = 3
ROPE_BASE = 10000.0
EPS = 1e-6
NEG_INF = -1e30
ATTN_SCALE = HEAD_DIM ** -0.5

W_QA = N_HEADS_A * HEAD_DIM
W_KA = N_KV_A * HEAD_DIM
W_B = N_HEADS_B * HEAD_DIM

LANES = 128
BF16_ROWS = 16
VMEM_LIMIT_BYTES = 56 * 1024 * 1024

TN = 512
ROW_GROUP_B = 4
QB_A = 512


def _rms(x, g):
    return (x * lax.rsqrt(jnp.mean(x * x, axis=-1, keepdims=True) + EPS)) * g


def _mod_kernel(c_ref, w_ref, b_ref, o_ref):
    a = jax.nn.silu(c_ref[...]).astype(BF16)
    o_ref[...] = jnp.dot(a, w_ref[...].astype(BF16), preferred_element_type=F32) + b_ref[...]


def _mod_call(cs, w, b):
    m, d = cs.shape
    n = w.shape[1]
    tn = 1536 if n % 1536 == 0 else n
    return pl.pallas_call(
        _mod_kernel,
        grid=(n // tn,),
        in_specs=[pl.BlockSpec((m, d), lambda j: (0, 0)),
                  pl.BlockSpec((d, tn), lambda j: (0, j)),
                  pl.BlockSpec((1, tn), lambda j: (0, j))],
        out_specs=pl.BlockSpec((m, tn), lambda j: (0, j)),
        out_shape=jax.ShapeDtypeStruct((m, n), F32),
        compiler_params=pltpu.CompilerParams(dimension_semantics=("arbitrary",),
                                             vmem_limit_bytes=VMEM_LIMIT_BYTES),
        name="mod",
    )(cs, w, b)


def _rpb_kernel(rpb_ref, o_ref):
    h = pl.program_id(0)
    n_dr = 2 * NB_ROWS - 1
    n_dc = 2 * NB_COLS - 1
    qc = lax.broadcasted_iota(jnp.int32, (GRID_W, LANES), 0)
    lane = lax.broadcasted_iota(jnp.int32, (GRID_W, LANES), 1)
    kc = lane & (GRID_W - 1)
    dc_idx = jnp.clip(kc - qc, -(NB_COLS - 1), NB_COLS - 1) + (NB_COLS - 1)
    base = h * (n_dr * n_dc)

    def toeplitz(dr):
        def body(dc, acc):
            return jnp.where(dc_idx == dc, rpb_ref[base + dr * n_dc + dc], acc)
        return lax.fori_loop(0, n_dc, body, jnp.zeros((GRID_W, LANES), F32))

    tw = [toeplitz(dr) for dr in range(n_dr)]
    pair = [jnp.where(lane < GRID_W, tw[d], tw[d + 1]) for d in range(n_dr - 1)]
    n_kr = 3 * ROW_GROUP_B
    dr0 = NB_ROWS - 1 - ROW_GROUP_B
    for rq in range(ROW_GROUP_B):
        for p in range(n_kr // 2):
            o_ref[0, rq * GRID_W:(rq + 1) * GRID_W, p * LANES:(p + 1) * LANES] = pair[2 * p - rq + dr0]


def _rpb_call(rpb):
    nh = rpb.shape[0]
    nq = ROW_GROUP_B * GRID_W
    nk = 3 * ROW_GROUP_B * GRID_W
    assert 2 * ROW_GROUP_B == NB_ROWS
    return pl.pallas_call(
        _rpb_kernel,
        grid=(nh,),
        in_specs=[pl.BlockSpec(memory_space=pltpu.SMEM)],
        out_specs=pl.BlockSpec((1, nq, nk), lambda h: (h, 0, 0)),
        out_shape=jax.ShapeDtypeStruct((nh, nq, nk), F32),
        compiler_params=pltpu.CompilerParams(dimension_semantics=("arbitrary",)),
        name="rpb_table",
    )(rpb.reshape(-1))


def _rope(t, cos, sin, first_half):
    partner = jnp.where(first_half, pltpu.roll(t, LANES - 32, 1), pltpu.roll(t, 32, 1))
    return t * cos + partner * sin


def _inproj_kernel(x_ref, sc_ref, sh_ref, g_ref, w_ref, cos_ref, sin_ref, o_ref, h_ref, *, j0):
    j = pl.program_id(1)

    @pl.when(j == 0)
    def _():
        h = _rms(x_ref[...], g_ref[...]) * (1 + sc_ref[0]) + sh_ref[0]
        h_ref[...] = h.astype(BF16)

    acc = jnp.dot(h_ref[...], w_ref[...], preferred_element_type=F32)
    jj = j + j0
    n_gate = 2 * x_ref.shape[1] // TN
    lane = lax.broadcasted_iota(jnp.int32, (1, HEAD_DIM), 1)
    first_half = (lane & 63) < 32

    def rope_heads(n_heads, scale):
        cos = cos_ref[...]
        sin = sin_ref[...]
        for hd in range(n_heads):
            sl = slice(hd * HEAD_DIM, (hd + 1) * HEAD_DIM)
            r = _rope(acc[:, sl], cos, sin, first_half)
            if scale is not None:
                r = r * scale
            o_ref[:, sl] = r.astype(BF16)

    @pl.when(jj < n_gate)
    def _():
        o_ref[...] = jax.nn.sigmoid(acc).astype(BF16)

    @pl.when((jj >= n_gate) & (jj < n_gate + 2))
    def _():
        rope_heads(TN // HEAD_DIM, ATTN_SCALE)

    @pl.when((jj >= n_gate + 2) & (jj < n_gate + 4))
    def _():
        o_ref[...] = (acc * ATTN_SCALE).astype(BF16)

    @pl.when((jj >= n_gate + 4) & (jj < n_gate + 8))
    def _():
        o_ref[...] = acc.astype(BF16)

    @pl.when(jj == n_gate + 8)
    def _():
        rope_heads(N_KV_A, None)
        o_ref[:, W_KA:] = acc[:, W_KA:].astype(BF16)


def _inproj_call(x2, sc, sh, g, w, cos, sin, *, tm, rows_per_mod, j0, nj):
    n, d = x2.shape
    pos_tiles = cos.shape[0] // tm
    mod_tiles = rows_per_mod // tm
    return pl.pallas_call(
        functools.partial(_inproj_kernel, j0=j0),
        grid=(n // tm, nj),
        in_specs=[pl.BlockSpec((tm, d), lambda i, j: (i, 0)),
                  pl.BlockSpec((1, 1, d), lambda i, j: (i // mod_tiles, 0, 0)),
                  pl.BlockSpec((1, 1, d), lambda i, j: (i // mod_tiles, 0, 0)),
                  pl.BlockSpec((1, d), lambda i, j: (0, 0)),
                  pl.BlockSpec((d, TN), lambda i, j: (0, j + j0)),
                  pl.BlockSpec((tm, HEAD_DIM), lambda i, j: (i % pos_tiles, 0)),
                  pl.BlockSpec((tm, HEAD_DIM), lambda i, j: (i % pos_tiles, 0))],
        out_specs=pl.BlockSpec((tm, TN), lambda i, j: (i, j)),
        out_shape=jax.ShapeDtypeStruct((n, nj * TN), BF16),
        scratch_shapes=[pltpu.VMEM((tm, d), BF16)],
        compiler_params=pltpu.CompilerParams(dimension_semantics=("arbitrary", "arbitrary"),
                                             vmem_limit_bytes=VMEM_LIMIT_BYTES),
        name="in_proj",
    )(x2, sc, sh, g, w, cos, sin)


def _attn_a_kernel(sink_ref, q_ref, kp_ref, km_ref, kn_ref, vp_ref, vm_ref, vn_ref, kc_ref, vc_ref, o_ref,
                   *, n_blocks):
    nq = pl.program_id(1)
    kvh = pl.program_id(2)
    sub = QB_A // BLOCK
    rows = GQA_GROUP * BLOCK
    n_loc = 3 * BLOCK
    n_ctx = kc_ref.shape[0]
    kband = jnp.concatenate([kp_ref[...], km_ref[...], kn_ref[...]], axis=0)
    vband = jnp.concatenate([vp_ref[...], vm_ref[...], vn_ref[...]], axis=0)
    kc = kc_ref[...]
    vc = vc_ref[...]
    r_io = lax.broadcasted_iota(jnp.int32, (rows, n_loc + n_ctx), 0)
    c_io = lax.broadcasted_iota(jnp.int32, (rows, n_loc + n_ctx), 1)
    qi_io = r_io & (BLOCK - 1)
    g_io = lax.shift_right_logical(lax.broadcasted_iota(jnp.int32, (rows, 1), 0), BLOCK.bit_length() - 1)
    sink = jnp.zeros((rows, 1), F32)
    for g in range(GQA_GROUP):
        sink = jnp.where(g_io == g, sink_ref[kvh * GQA_GROUP + g], sink)
    for qi in range(sub):
        blk = nq * sub + qi
        lo = jnp.where(blk == 0, BLOCK, 0)
        hi = jnp.where(blk == n_blocks - 1, 2 * BLOCK - 1, n_loc - 1)
        valid = (c_io >= n_loc) | ((c_io >= jnp.maximum(qi_io, lo)) & (c_io <= jnp.minimum(qi_io + 2 * WINDOW, hi)))
        q4 = jnp.concatenate(
            [q_ref[qi * BLOCK:(qi + 1) * BLOCK, g * HEAD_DIM:(g + 1) * HEAD_DIM] for g in range(GQA_GROUP)], axis=0)
        keys = jnp.concatenate([kband[qi * BLOCK:qi * BLOCK + n_loc], kc], axis=0)
        vals = jnp.concatenate([vband[qi * BLOCK:qi * BLOCK + n_loc], vc], axis=0)
        s = lax.dot_general(q4, keys, (((1,), (1,)), ((), ())), preferred_element_type=F32)
        s = jnp.where(valid, s, NEG_INF)
        m = jnp.maximum(jnp.max(s, axis=-1, keepdims=True), sink)
        e = jnp.exp(s - m)
        den = jnp.sum(e, axis=-1, keepdims=True) + jnp.exp(sink - m)
        o = jnp.dot(e.astype(BF16), vals, preferred_element_type=F32) * (1.0 / den)
        for g in range(GQA_GROUP):
            o_ref[qi * BLOCK:(qi + 1) * BLOCK, g * HEAD_DIM:(g + 1) * HEAD_DIM] = (
                o[g * BLOCK:(g + 1) * BLOCK].astype(BF16))


def _attn_a_call(proj, cproj, sink, *, batch, seq, ctx_len, q_col, k_col, v_col, ck_col, cv_col):
    nqb = seq // QB_A
    sub = QB_A // BLOCK
    n_blocks = seq // BLOCK
    gw = GQA_GROUP * HEAD_DIM

    def prev_map(b, n, k, col):
        return (b * n_blocks + jnp.maximum(n * sub - 1, 0), col // HEAD_DIM + k)

    def next_map(b, n, k, col):
        return (b * n_blocks + jnp.minimum(n * sub + sub, n_blocks - 1), col // HEAD_DIM + k)

    def main_map(b, n, k, col):
        return (b * nqb + n, col // HEAD_DIM + k)

    edge = lambda f, col: pl.BlockSpec((BLOCK, HEAD_DIM), functools.partial(f, col=col))
    main = lambda col: pl.BlockSpec((QB_A, HEAD_DIM), functools.partial(main_map, col=col))
    ctxs = lambda col: pl.BlockSpec((ctx_len, HEAD_DIM), lambda b, n, k: (b, col // HEAD_DIM + k))
    return pl.pallas_call(
        functools.partial(_attn_a_kernel, n_blocks=n_blocks),
        grid=(batch, nqb, N_KV_A),
        in_specs=[pl.BlockSpec(memory_space=pltpu.SMEM),
                  pl.BlockSpec((QB_A, gw), lambda b, n, k: (b * nqb + n, q_col // gw + k)),
                  edge(prev_map, k_col), main(k_col), edge(next_map, k_col),
                  edge(prev_map, v_col), main(v_col), edge(next_map, v_col),
                  ctxs(ck_col), ctxs(cv_col)],
        out_specs=pl.BlockSpec((QB_A, gw), lambda b, n, k: (b * nqb + n, k)),
        out_shape=jax.ShapeDtypeStruct((batch * seq, W_QA), BF16),
        compiler_params=pltpu.CompilerParams(dimension_semantics=("arbitrary",) * 3,
                                             vmem_limit_bytes=VMEM_LIMIT_BYTES),
        name="attn_a",
    )(sink, proj, proj, proj, proj, proj, proj, proj, cproj, cproj)


def _attn_b_kernel(q_ref, kp_ref, km_ref, kn_ref, vp_ref, vm_ref, vn_ref, kc_ref, vc_ref, bias_ref, o_ref,
                   *, grid_rows):
    n = pl.program_id(1)
    nq = ROW_GROUP_B * GRID_W
    nk = 3 * nq
    n_ctx = kc_ref.shape[0]
    heads = q_ref.shape[1] // HEAD_DIM
    r_io = lax.broadcasted_iota(jnp.int32, (nq, nk), 0)
    c_io = lax.broadcasted_iota(jnp.int32, (nq, nk), 1)
    log_w = GRID_W.bit_length() - 1
    r_abs = n * ROW_GROUP_B + lax.shift_right_logical(r_io, log_w)
    q_col = r_io & (GRID_W - 1)
    kr_abs = (n - 1) * ROW_GROUP_B + lax.shift_right_logical(c_io, log_w)
    k_col = c_io & (GRID_W - 1)
    r_start = jnp.clip(r_abs - NB_ROWS // 2, 0, grid_rows - NB_ROWS)
    c_start = jnp.clip(q_col - NB_COLS // 2, 0, GRID_W - NB_COLS)
    valid = ((kr_abs >= r_start) & (kr_abs < r_start + NB_ROWS)
             & (k_col >= c_start) & (k_col < c_start + NB_COLS))
    for hd in range(heads):
        sl = slice(hd * HEAD_DIM, (hd + 1) * HEAD_DIM)
        q = q_ref[:, sl]
        keys = jnp.concatenate([kp_ref[:, sl], km_ref[:, sl], kn_ref[:, sl]], axis=0)
        vals = jnp.concatenate([vp_ref[:, sl], vm_ref[:, sl], vn_ref[:, sl], vc_ref[:, sl]], axis=0)
        s_nb = lax.dot_general(q, keys, (((1,), (1,)), ((), ())), preferred_element_type=F32)
        s_nb = jnp.where(valid, s_nb + bias_ref[hd], NEG_INF)
        s_ctx = lax.dot_general(q, kc_ref[:, sl], (((1,), (1,)), ((), ())), preferred_element_type=F32)
        m = jnp.maximum(jnp.max(s_nb, axis=-1, keepdims=True), jnp.max(s_ctx, axis=-1, keepdims=True))
        e_nb = jnp.exp(s_nb - m)
        e_ctx = jnp.exp(s_ctx - m)
        den = jnp.sum(e_nb, axis=-1, keepdims=True) + jnp.sum(e_ctx, axis=-1, keepdims=True)
        e = jnp.concatenate([e_nb, e_ctx], axis=1).astype(BF16)
        o = jnp.dot(e, vals, preferred_element_type=F32) * (1.0 / den)
        o_ref[:, sl] = o.astype(BF16)


def _attn_b_call(proj, cproj, bias, *, batch, seq, ctx_len, q_col, k_col, v_col, ck_col, cv_col):
    nq = ROW_GROUP_B * GRID_W
    ng = seq // nq
    hw = TN
    hg = W_B // hw

    def row_spec(col, shift):
        def imap(b, n, h):
            return (b * ng + jnp.clip(n + shift, 0, ng - 1), col // hw + h)
        return pl.BlockSpec((nq, hw), imap)

    ctxs = lambda col: pl.BlockSpec((ctx_len, hw), lambda b, n, h: (b, col // hw + h))
    return pl.pallas_call(
        functools.partial(_attn_b_kernel, grid_rows=seq // GRID_W),
        grid=(batch, ng, hg),
        in_specs=[row_spec(q_col, 0),
                  row_spec(k_col, -1), row_spec(k_col, 0), row_spec(k_col, 1),
                  row_spec(v_col, -1), row_spec(v_col, 0), row_spec(v_col, 1),
                  ctxs(ck_col), ctxs(cv_col),
                  pl.BlockSpec((hw // HEAD_DIM, nq, bias.shape[2]), lambda b, n, h: (h, 0, 0))],
        out_specs=pl.BlockSpec((nq, hw), lambda b, n, h: (b * ng + n, h)),
        out_shape=jax.ShapeDtypeStruct((batch * seq, W_B), BF16),
        compiler_params=pltpu.CompilerParams(dimension_semantics=("arbitrary",) * 3,
                                             vmem_limit_bytes=VMEM_LIMIT_BYTES),
        name="attn_b",
    )(proj, proj, proj, proj, proj, proj, proj, cproj, cproj, bias)


def _mix_kernel(oa_ref, ob_ref, ga_ref, gb_ref, wa_ref, wb_ref, z_ref):
    ya = jnp.dot(oa_ref[...], wa_ref[...], preferred_element_type=F32)
    yb = jnp.dot(ob_ref[...], wb_ref[...], preferred_element_type=F32)
    z_ref[...] = (ga_ref[...].astype(F32) * ya + gb_ref[...].astype(F32) * yb).astype(BF16)


def _mix_call(oa, ob, proj, wa, wb, *, tm):
    n = oa.shape[0]
    d = wa.shape[1]
    const = lambda shape: pl.BlockSpec(shape, lambda i: (0, 0), pipeline_mode=pl.Buffered(1))
    return pl.pallas_call(
        _mix_kernel,
        grid=(n // tm,),
        in_specs=[pl.BlockSpec((tm, oa.shape[1]), lambda i: (i, 0)),
                  pl.BlockSpec((tm, ob.shape[1]), lambda i: (i, 0)),
                  pl.BlockSpec((tm, d), lambda i: (i, 0)),
                  pl.BlockSpec((tm, d), lambda i: (i, 1)),
                  const(wa.shape), const(wb.shape)],
        out_specs=pl.BlockSpec((tm, d), lambda i: (i, 0)),
        out_shape=jax.ShapeDtypeStruct((n, d), BF16),
        compiler_params=pltpu.CompilerParams(dimension_semantics=("arbitrary",),
                                             vmem_limit_bytes=VMEM_LIMIT_BYTES),
        name="branch_mix",
    )(oa, ob, proj, proj, wa, wb)


def _outnorm_kernel(a_ref, w_ref, x_ref, gt_ref, g_ref, o_ref, acc_ref):
    k = pl.program_id(1)
    part = jnp.dot(a_ref[...], w_ref[...], preferred_element_type=F32)

    @pl.when(k == 0)
    def _():
        acc_ref[...] = part

    @pl.when(k > 0)
    def _():
        acc_ref[...] += part

    @pl.when(k == pl.num_programs(1) - 1)
    def _():
        o_ref[...] = x_ref[...] + gt_ref[0] * _rms(acc_ref[...], g_ref[...])


def _outnorm_call(a, w, x2, gt, g, *, tm, tk, rows_per_mod):
    n, kdim = a.shape
    d = w.shape[1]
    mod_tiles = rows_per_mod // tm
    return pl.pallas_call(
        _outnorm_kernel,
        grid=(n // tm, kdim // tk),
        in_specs=[pl.BlockSpec((tm, tk), lambda i, k: (i, k)),
                  pl.BlockSpec((tk, d), lambda i, k: (k, 0)),
                  pl.BlockSpec((tm, d), lambda i, k: (i, 0)),
                  pl.BlockSpec((1, 1, d), lambda i, k: (i // mod_tiles, 0, 0)),
                  pl.BlockSpec((1, d), lambda i, k: (0, 0))],
        out_specs=pl.BlockSpec((tm, d), lambda i, k: (i, 0)),
        out_shape=jax.ShapeDtypeStruct((n, d), F32),
        scratch_shapes=[pltpu.VMEM((tm, d), F32)],
        compiler_params=pltpu.CompilerParams(dimension_semantics=("arbitrary", "arbitrary"),
                                             vmem_limit_bytes=VMEM_LIMIT_BYTES),
        name="out_norm",
    )(a, w, x2, gt, g)


def _ffn_up_kernel(xp_ref, x_ref, xn_ref, sc_ref, sh_ref, g_ref, wa_ref, wg_ref, cwa_ref, cwg_ref, cba_ref, cbg_ref,
                   o_ref, h_ref, *, tiles_per_seq):
    i = pl.program_id(0)
    j = pl.program_id(1)
    tm = x_ref.shape[0]
    halo = xp_ref.shape[0]

    @pl.when(j == 0)
    def _():
        def mod(x):
            return (_rms(x, g_ref[...]) * (1 + sc_ref[0]) + sh_ref[0]).astype(BF16)
        h_ref[0:halo] = mod(xp_ref[...])
        h_ref[halo:halo + tm] = mod(x_ref[...])
        h_ref[halo + tm:] = mod(xn_ref[...])

    first_row = jnp.where((i % tiles_per_seq) == 0, 0, -1)
    last_row = jnp.where((i % tiles_per_seq) == tiles_per_seq - 1, tm - 1, -1)
    row = lax.broadcasted_iota(jnp.int32, (tm, 1), 0)
    h = h_ref[...]
    ext = tm + 2 * halo

    def conv(w_ref, cw_ref, cb_ref):
        u = jnp.dot(h, w_ref[...], preferred_element_type=F32)
        up = pltpu.roll(u, 1, 0)[halo:halo + tm]
        un = pltpu.roll(u, ext - 1, 0)[halo:halo + tm]
        up = jnp.where(row == first_row, 0.0, up)
        un = jnp.where(row == last_row, 0.0, un)
        out = cb_ref[...] + up * cw_ref[0:1]
        out = out + u[halo:halo + tm] * cw_ref[1:2]
        return out + un * cw_ref[2:3]

    a = conv(wa_ref, cwa_ref, cba_ref)
    gte = conv(wg_ref, cwg_ref, cbg_ref)
    o_ref[...] = (jax.nn.silu(gte) * a).astype(BF16)


def _ffn_up_call(x2, sc, sh, g, w_up, conv_w, conv_b, *, tm, tn, seq):
    n, d = x2.shape
    dff = w_up.shape[1] // 2
    nj = dff // tn
    halo = BF16_ROWS
    hb = tm // halo
    n_halo = n // halo
    tiles_per_seq = seq // tm
    return pl.pallas_call(
        functools.partial(_ffn_up_kernel, tiles_per_seq=tiles_per_seq),
        grid=(n // tm, nj),
        in_specs=[pl.BlockSpec((halo, d), lambda i, j: (jnp.maximum(i * hb - 1, 0), 0)),
                  pl.BlockSpec((tm, d), lambda i, j: (i, 0)),
                  pl.BlockSpec((halo, d), lambda i, j: (jnp.minimum((i + 1) * hb, n_halo - 1), 0)),
                  pl.BlockSpec((1, 1, d), lambda i, j: (i // tiles_per_seq, 0, 0)),
                  pl.BlockSpec((1, 1, d), lambda i, j: (i // tiles_per_seq, 0, 0)),
                  pl.BlockSpec((1, d), lambda i, j: (0, 0)),
                  pl.BlockSpec((d, tn), lambda i, j: (0, j)),
                  pl.BlockSpec((d, tn), lambda i, j: (0, j + nj)),
                  pl.BlockSpec((CONV_W, tn), lambda i, j: (0, j)),
                  pl.BlockSpec((CONV_W, tn), lambda i, j: (0, j + nj)),
                  pl.BlockSpec((1, tn), lambda i, j: (0, j)),
                  pl.BlockSpec((1, tn), lambda i, j: (0, j + nj))],
        out_specs=pl.BlockSpec((tm, tn), lambda i, j: (i, j)),
        out_shape=jax.ShapeDtypeStruct((n, dff), BF16),
        scratch_shapes=[pltpu.VMEM((tm + 2 * halo, d), BF16)],
        compiler_params=pltpu.CompilerParams(dimension_semantics=("arbitrary", "arbitrary"),
                                             vmem_limit_bytes=VMEM_LIMIT_BYTES),
        name="ffn_up",
    )(x2, x2, x2, sc, sh, g, w_up, w_up, conv_w, conv_w, conv_b, conv_b)


def _rope_tables(seq):
    t = jnp.arange(seq)
    half = HEAD_DIM // 4
    inv = ROPE_BASE ** (-jnp.arange(half, dtype=F32) / half)

    def cs(pos):
        ang = pos.astype(F32)[:, None] * inv[None, :]
        c, s = jnp.cos(ang), jnp.sin(ang)
        return jnp.concatenate([c, c], axis=1), jnp.concatenate([-s, s], axis=1)

    cr, sr = cs(t // GRID_W)
    cc, sn = cs(t % GRID_W)
    return jnp.concatenate([cr, cc], axis=1), jnp.concatenate([sr, sn], axis=1)


def _pick_tile(n, pref):
    while n % pref:
        pref //= 2
    return pref


def kernel(x, c, ctx, c_ctx, w_mod, b_mod, g_attn_pre, g_attn_post, g_ffn_pre, g_ffn_post, w_in, sink_a, rpb_b,
           w_br_a, w_br_b, w_o, w_up, conv_w, conv_b, w_down):
    batch, seq, d = x.shape
    ctx_len = ctx.shape[1]
    assert w_mod.shape[0] == 1 and d == 16 * HEAD_DIM
    assert seq % QB_A == 0 and seq % (ROW_GROUP_B * GRID_W) == 0 and WINDOW == BLOCK
    n = batch * seq
    x2 = x.reshape(n, d)

    pad = (-(batch + 1)) % 8
    cs = jnp.concatenate([c, c_ctx[None, :], jnp.zeros((pad, d), F32)], axis=0)
    mod = _mod_call(cs, w_mod[0], b_mod[0][None, :])
    sh1, sc1, gt1, sh2, sc2, gt2 = [mod[:batch, k * d:(k + 1) * d].reshape(batch, 1, d) for k in range(6)]
    csh1 = mod[batch:batch + 1, 0:d].reshape(1, 1, d)
    csc1 = mod[batch:batch + 1, d:2 * d].reshape(1, 1, d)

    o_qa, o_ka, o_va, o_qb, o_kb, o_vb, o_ga = 0, W_QA, W_QA + W_KA, W_QA + 2 * W_KA, W_QA + 2 * W_KA + W_B, \
        W_QA + 2 * W_KA + 2 * W_B, W_QA + 2 * W_KA + 3 * W_B
    wi = w_in[0]
    w_perm = jnp.concatenate([wi[:, o_ga:], wi[:, o_qa:o_ka], wi[:, o_qb:o_ga], wi[:, o_ka:o_qb]], axis=1).astype(BF16)
    c_qa = 2 * d
    c_qb = c_qa + W_QA
    c_kb = c_qb + W_B
    c_vb = c_kb + W_B
    c_ka = c_vb + W_B
    c_va = c_ka + W_KA
    n_tiles = w_perm.shape[1] // TN
    cos, sin = _rope_tables(seq)
    tm = _pick_tile(seq, 1024)
    proj = _inproj_call(x2, sc1, sh1, g_attn_pre, w_perm, cos, sin, tm=tm, rows_per_mod=seq, j0=0, nj=n_tiles)
    nc = batch * ctx_len
    j0c = c_kb // TN
    cproj = _inproj_call(ctx.reshape(nc, d), csc1, csh1, g_attn_pre, w_perm,
                         jnp.ones((nc, HEAD_DIM), F32), jnp.zeros((nc, HEAD_DIM), F32),
                         tm=nc, rows_per_mod=nc, j0=j0c, nj=n_tiles - j0c)
    cc_kb, cc_vb, cc_ka, cc_va = 0, W_B, 2 * W_B, 2 * W_B + W_KA

    oa = _attn_a_call(proj, cproj, sink_a[0], batch=batch, seq=seq, ctx_len=ctx_len,
                      q_col=c_qa, k_col=c_ka, v_col=c_va, ck_col=cc_ka, cv_col=cc_va)
    bias = _rpb_call(rpb_b[0])
    ob = _attn_b_call(proj, cproj, bias, batch=batch, seq=seq, ctx_len=ctx_len,
                      q_col=c_qb, k_col=c_kb, v_col=c_vb, ck_col=cc_kb, cv_col=cc_vb)

    tm2 = _pick_tile(seq, 512)
    z = _mix_call(oa, ob, proj, w_br_a[0].astype(BF16), w_br_b[0].astype(BF16), tm=tm2)
    x1 = _outnorm_call(z, w_o[0].astype(BF16), x2, gt1, g_attn_post, tm=tm2, tk=d, rows_per_mod=seq)

    dff = w_down.shape[1]
    act = _ffn_up_call(x1, sc2, sh2, g_ffn_pre, w_up[0].astype(BF16), conv_w[0], conv_b[0][None, :],
                       tm=tm, tn=TN, seq=seq)
    tk = dff // 4 if (dff // 4) % LANES == 0 else dff
    out = _outnorm_call(act, w_down[0].astype(BF16), x1, gt2, g_ffn_post, tm=tm2, tk=tk, rows_per_mod=seq)
    return out.reshape(batch, seq, d)
```

```python
import functools

import jax
import jax.numpy as jnp
from jax import lax
from jax.experimental import pallas as pl
from jax.experimental.pallas import tpu as pltpu

F32 = jnp.float32
BF16 = jnp.bfloat16

GRID_W = 64
HEAD_DIM = 128
N_HEADS_A = 8
N_KV_A = 2
GQA_GROUP = N_HEADS_A // N_KV_A
WINDOW = 128
BLOCK = 128
N_HEADS_B = 8
NB_ROWS = 8
NB_COLS = 16
CONV_W = 3
ROPE_BASE = 10000.0
EPS = 1e-6
NEG_INF = -1e30
ATTN_SCALE = HEAD_DIM ** -0.5

W_QA = N_HEADS_A * HEAD_DIM
W_KA = N_KV_A * HEAD_DIM
W_B = N_HEADS_B * HEAD_DIM

LANES = 128
BF16_ROWS = 16
VMEM_LIMIT_BYTES = 56 * 1024 * 1024
VMEM_LIMIT_RESIDENT_BYTES = 60 * 1024 * 1024

TN = 512
ROW_CHUNKS = 4
ROW_GROUP_B = 4
QB_A = 512


def _rms(x, g):
    return (x * lax.rsqrt(jnp.mean(x * x, axis=-1, keepdims=True) + EPS)) * g


def _mod_kernel(c_ref, w_ref, b_ref, o_ref):
    a = jax.nn.silu(c_ref[...]).astype(BF16)
    o_ref[...] = jnp.dot(a, w_ref[...].astype(BF16), preferred_element_type=F32) + b_ref[...]


def _mod_call(cs, w, b):
    m, d = cs.shape
    n = w.shape[1]
    tn = 1536 if n % 1536 == 0 else n
    return pl.pallas_call(
        _mod_kernel,
        grid=(n // tn,),
        in_specs=[pl.BlockSpec((m, d), lambda j: (0, 0)),
                  pl.BlockSpec((d, tn), lambda j: (0, j)),
                  pl.BlockSpec((1, tn), lambda j: (0, j))],
        out_specs=pl.BlockSpec((m, tn), lambda j: (0, j)),
        out_shape=jax.ShapeDtypeStruct((m, n), F32),
        compiler_params=pltpu.CompilerParams(dimension_semantics=("arbitrary",),
                                             vmem_limit_bytes=VMEM_LIMIT_BYTES),
        name="mod",
    )(cs, w, b)


def _rpb_kernel(rpb_ref, o_ref):
    h = pl.program_id(0)
    n_dr = 2 * NB_ROWS - 1
    n_dc = 2 * NB_COLS - 1
    qc = lax.broadcasted_iota(jnp.int32, (GRID_W, LANES), 0)
    lane = lax.broadcasted_iota(jnp.int32, (GRID_W, LANES), 1)
    kc = lane & (GRID_W - 1)
    dc_idx = jnp.clip(kc - qc, -(NB_COLS - 1), NB_COLS - 1) + (NB_COLS - 1)
    base = h * (n_dr * n_dc)

    def toeplitz(dr):
        def body(dc, acc):
            return jnp.where(dc_idx == dc, rpb_ref[base + dr * n_dc + dc], acc)
        return lax.fori_loop(0, n_dc, body, jnp.zeros((GRID_W, LANES), F32))

    tw = [toeplitz(dr) for dr in range(n_dr)]
    pair = [jnp.where(lane < GRID_W, tw[d], tw[d + 1]) for d in range(n_dr - 1)]
    n_kr = 3 * ROW_GROUP_B
    dr0 = NB_ROWS - 1 - ROW_GROUP_B
    for rq in range(ROW_GROUP_B):
        for p in range(n_kr // 2):
            o_ref[0, rq * GRID_W:(rq + 1) * GRID_W, p * LANES:(p + 1) * LANES] = pair[2 * p - rq + dr0]


def _rpb_call(rpb):
    nh = rpb.shape[0]
    nq = ROW_GROUP_B * GRID_W
    nk = 3 * ROW_GROUP_B * GRID_W
    assert 2 * ROW_GROUP_B == NB_ROWS
    return pl.pallas_call(
        _rpb_kernel,
        grid=(nh,),
        in_specs=[pl.BlockSpec(memory_space=pltpu.SMEM)],
        out_specs=pl.BlockSpec((1, nq, nk), lambda h: (h, 0, 0)),
        out_shape=jax.ShapeDtypeStruct((nh, nq, nk), F32),
        compiler_params=pltpu.CompilerParams(dimension_semantics=("arbitrary",)),
        name="rpb_table",
    )(rpb.reshape(-1))


def _rope(t, cos, sin, first_half):
    partner = jnp.where(first_half, pltpu.roll(t, LANES - 32, 1), pltpu.roll(t, 32, 1))
    return t * cos + partner * sin


def _inproj_kernel(x_ref, sc_ref, sh_ref, g_ref, w_ref, cos_ref, sin_ref, o_ref, h_ref, *, j0):
    j = pl.program_id(1)

    @pl.when(j == 0)
    def _():
        h = _rms(x_ref[...], g_ref[...]) * (1 + sc_ref[0]) + sh_ref[0]
        h_ref[...] = h.astype(BF16)

    jj = j + j0
    n_gate = 2 * x_ref.shape[1] // TN
    lane = lax.broadcasted_iota(jnp.int32, (1, HEAD_DIM), 1)
    first_half = (lane & 63) < 32
    cm = x_ref.shape[0] // ROW_CHUNKS

    def row_chunks(epilogue):
        acc = jnp.dot(h_ref[0:cm], w_ref[...], preferred_element_type=F32)
        for c in range(ROW_CHUNKS):
            nxt = None
            if c + 1 < ROW_CHUNKS:
                nxt = jnp.dot(h_ref[(c + 1) * cm:(c + 2) * cm], w_ref[...], preferred_element_type=F32)
            epilogue(slice(c * cm, (c + 1) * cm), acc)
            acc = nxt

    def rope_heads(rows, acc, n_heads, scale):
        cos = cos_ref[rows]
        sin = sin_ref[rows]
        for hd in range(n_heads):
            sl = slice(hd * HEAD_DIM, (hd + 1) * HEAD_DIM)
            r = _rope(acc[:, sl], cos, sin, first_half)
            if scale is not None:
                r = r * scale
            o_ref[rows, sl] = r.astype(BF16)

    @pl.when(jj < n_gate)
    def _():
        def epilogue(rows, acc):
            o_ref[rows] = jax.nn.sigmoid(acc).astype(BF16)
        row_chunks(epilogue)

    @pl.when((jj >= n_gate) & (jj < n_gate + 2))
    def _():
        row_chunks(lambda rows, acc: rope_heads(rows, acc, TN // HEAD_DIM, ATTN_SCALE))

    @pl.when((jj >= n_gate + 2) & (jj < n_gate + 4))
    def _():
        def epilogue(rows, acc):
            o_ref[rows] = (acc * ATTN_SCALE).astype(BF16)
        row_chunks(epilogue)

    @pl.when((jj >= n_gate + 4) & (jj < n_gate + 8))
    def _():
        def epilogue(rows, acc):
            o_ref[rows] = acc.astype(BF16)
        row_chunks(epilogue)

    @pl.when(jj == n_gate + 8)
    def _():
        def epilogue(rows, acc):
            rope_heads(rows, acc, N_KV_A, None)
            o_ref[rows, W_KA:] = acc[:, W_KA:].astype(BF16)
        row_chunks(epilogue)


def _inproj_call(x2, sc, sh, g, w, cos, sin, *, tm, rows_per_mod, j0, nj):
    n, d = x2.shape
    pos_tiles = cos.shape[0] // tm
    mod_tiles = rows_per_mod // tm
    return pl.pallas_call(
        functools.partial(_inproj_kernel, j0=j0),
        grid=(n // tm, nj),
        in_specs=[pl.BlockSpec((tm, d), lambda i, j: (i, 0)),
                  pl.BlockSpec((1, 1, d), lambda i, j: (i // mod_tiles, 0, 0)),
                  pl.BlockSpec((1, 1, d), lambda i, j: (i // mod_tiles, 0, 0)),
                  pl.BlockSpec((1, d), lambda i, j: (0, 0)),
                  pl.BlockSpec((d, TN), lambda i, j: (0, j + j0)),
                  pl.BlockSpec((tm, HEAD_DIM), lambda i, j: (i % pos_tiles, 0)),
                  pl.BlockSpec((tm, HEAD_DIM), lambda i, j: (i % pos_tiles, 0))],
        out_specs=pl.BlockSpec((tm, TN), lambda i, j: (i, j)),
        out_shape=jax.ShapeDtypeStruct((n, nj * TN), BF16),
        scratch_shapes=[pltpu.VMEM((tm, d), BF16)],
        compiler_params=pltpu.CompilerParams(dimension_semantics=("arbitrary", "arbitrary"),
                                             vmem_limit_bytes=VMEM_LIMIT_BYTES),
        name="in_proj",
    )(x2, sc, sh, g, w, cos, sin)


def _attn_a_kernel(sink_ref, q_ref, kp_ref, km_ref, kn_ref, vp_ref, vm_ref, vn_ref, kc_ref, vc_ref, o_ref,
                   *, n_blocks):
    nq = pl.program_id(1)
    kvh = pl.program_id(2)
    sub = QB_A // BLOCK
    rows = GQA_GROUP * BLOCK
    n_loc = 3 * BLOCK
    n_ctx = kc_ref.shape[0]
    kband = jnp.concatenate([kp_ref[...], km_ref[...], kn_ref[...]], axis=0)
    vband = jnp.concatenate([vp_ref[...], vm_ref[...], vn_ref[...]], axis=0)
    kc = kc_ref[...]
    vc = vc_ref[...]
    r_io = lax.broadcasted_iota(jnp.int32, (rows, n_loc + n_ctx), 0)
    c_io = lax.broadcasted_iota(jnp.int32, (rows, n_loc + n_ctx), 1)
    qi_io = r_io & (BLOCK - 1)
    g_io = lax.shift_right_logical(lax.broadcasted_iota(jnp.int32, (rows, 1), 0), BLOCK.bit_length() - 1)
    sink = jnp.zeros((rows, 1), F32)
    for g in range(GQA_GROUP):
        sink = jnp.where(g_io == g, sink_ref[kvh * GQA_GROUP + g], sink)
    for qi in range(sub):
        blk = nq * sub + qi
        lo = jnp.where(blk == 0, BLOCK, 0)
        hi = jnp.where(blk == n_blocks - 1, 2 * BLOCK - 1, n_loc - 1)
        valid = (c_io >= n_loc) | ((c_io >= jnp.maximum(qi_io, lo)) & (c_io <= jnp.minimum(qi_io + 2 * WINDOW, hi)))
        q4 = jnp.concatenate(
            [q_ref[qi * BLOCK:(qi + 1) * BLOCK, g * HEAD_DIM:(g + 1) * HEAD_DIM] for g in range(GQA_GROUP)], axis=0)
        keys = jnp.concatenate([kband[qi * BLOCK:qi * BLOCK + n_loc], kc], axis=0)
        vals = jnp.concatenate([vband[qi * BLOCK:qi * BLOCK + n_loc], vc], axis=0)
        s = lax.dot_general(q4, keys, (((1,), (1,)), ((), ())), preferred_element_type=F32)
        s = jnp.where(valid, s, NEG_INF)
        m = jnp.maximum(jnp.max(s, axis=-1, keepdims=True), sink)
        e = jnp.exp(s - m)
        den = jnp.sum(e, axis=-1, keepdims=True) + jnp.exp(sink - m)
        o = jnp.dot(e.astype(BF16), vals, preferred_element_type=F32) * (1.0 / den)
        for g in range(GQA_GROUP):
            o_ref[qi * BLOCK:(qi + 1) * BLOCK, g * HEAD_DIM:(g + 1) * HEAD_DIM] = (
                o[g * BLOCK:(g + 1) * BLOCK].astype(BF16))


def _attn_a_call(proj, cproj, sink, *, batch, seq, ctx_len, q_col, k_col, v_col, ck_col, cv_col):
    nqb = seq // QB_A
    sub = QB_A // BLOCK
    n_blocks = seq // BLOCK
    gw = GQA_GROUP * HEAD_DIM

    def prev_map(b, n, k, col):
        return (b * n_blocks + jnp.maximum(n * sub - 1, 0), col // HEAD_DIM + k)

    def next_map(b, n, k, col):
        return (b * n_blocks + jnp.minimum(n * sub + sub, n_blocks - 1), col // HEAD_DIM + k)

    def main_map(b, n, k, col):
        return (b * nqb + n, col // HEAD_DIM + k)

    edge = lambda f, col: pl.BlockSpec((BLOCK, HEAD_DIM), functools.partial(f, col=col))
    main = lambda col: pl.BlockSpec((QB_A, HEAD_DIM), functools.partial(main_map, col=col))
    ctxs = lambda col: pl.BlockSpec((ctx_len, HEAD_DIM), lambda b, n, k: (b, col // HEAD_DIM + k))
    return pl.pallas_call(
        functools.partial(_attn_a_kernel, n_blocks=n_blocks),
        grid=(batch, nqb, N_KV_A),
        in_specs=[pl.BlockSpec(memory_space=pltpu.SMEM),
                  pl.BlockSpec((QB_A, gw), lambda b, n, k: (b * nqb + n, q_col // gw + k)),
                  edge(prev_map, k_col), main(k_col), edge(next_map, k_col),
                  edge(prev_map, v_col), main(v_col), edge(next_map, v_col),
                  ctxs(ck_col), ctxs(cv_col)],
        out_specs=pl.BlockSpec((QB_A, gw), lambda b, n, k: (b * nqb + n, k)),
        out_shape=jax.ShapeDtypeStruct((batch * seq, W_QA), BF16),
        compiler_params=pltpu.CompilerParams(dimension_semantics=("arbitrary",) * 3,
                                             vmem_limit_bytes=VMEM_LIMIT_BYTES),
        name="attn_a",
    )(sink, proj, proj, proj, proj, proj, proj, proj, cproj, cproj)


def _attn_b_kernel(q_ref, kp_ref, km_ref, kn_ref, vp_ref, vm_ref, vn_ref, kc_ref, vc_ref, bias_ref, o_ref,
                   *, grid_rows):
    n = pl.program_id(1)
    nq = ROW_GROUP_B * GRID_W
    nk = 3 * nq
    n_ctx = kc_ref.shape[0]
    heads = q_ref.shape[1] // HEAD_DIM
    r_io = lax.broadcasted_iota(jnp.int32, (nq, nk), 0)
    c_io = lax.broadcasted_iota(jnp.int32, (nq, nk), 1)
    log_w = GRID_W.bit_length() - 1
    r_abs = n * ROW_GROUP_B + lax.shift_right_logical(r_io, log_w)
    q_col = r_io & (GRID_W - 1)
    kr_abs = (n - 1) * ROW_GROUP_B + lax.shift_right_logical(c_io, log_w)
    k_col = c_io & (GRID_W - 1)
    r_start = jnp.clip(r_abs - NB_ROWS // 2, 0, grid_rows - NB_ROWS)
    c_start = jnp.clip(q_col - NB_COLS // 2, 0, GRID_W - NB_COLS)
    valid = ((kr_abs >= r_start) & (kr_abs < r_start + NB_ROWS)
             & (k_col >= c_start) & (k_col < c_start + NB_COLS))
    for hd in range(heads):
        sl = slice(hd * HEAD_DIM, (hd + 1) * HEAD_DIM)
        q = q_ref[:, sl]
        keys = jnp.concatenate([kp_ref[:, sl], km_ref[:, sl], kn_ref[:, sl]], axis=0)
        vals = jnp.concatenate([vp_ref[:, sl], vm_ref[:, sl], vn_ref[:, sl], vc_ref[:, sl]], axis=0)
        s_nb = lax.dot_general(q, keys, (((1,), (1,)), ((), ())), preferred_element_type=F32)
        s_nb = jnp.where(valid, s_nb + bias_ref[hd], NEG_INF)
        s_ctx = lax.dot_general(q, kc_ref[:, sl], (((1,), (1,)), ((), ())), preferred_element_type=F32)
        m = jnp.maximum(jnp.max(s_nb, axis=-1, keepdims=True), jnp.max(s_ctx, axis=-1, keepdims=True))
        e_nb = jnp.exp(s_nb - m)
        e_ctx = jnp.exp(s_ctx - m)
        den = jnp.sum(e_nb, axis=-1, keepdims=True) + jnp.sum(e_ctx, axis=-1, keepdims=True)
        e = jnp.concatenate([e_nb, e_ctx], axis=1).astype(BF16)
        o = jnp.dot(e, vals, preferred_element_type=F32) * (1.0 / den)
        o_ref[:, sl] = o.astype(BF16)


def _attn_b_call(proj, cproj, bias, *, batch, seq, ctx_len, q_col, k_col, v_col, ck_col, cv_col):
    nq = ROW_GROUP_B * GRID_W
    ng = seq // nq
    hw = TN
    hg = W_B // hw

    def row_spec(col, shift):
        def imap(b, n, h):
            return (b * ng + jnp.clip(n + shift, 0, ng - 1), col // hw + h)
        return pl.BlockSpec((nq, hw), imap)

    ctxs = lambda col: pl.BlockSpec((ctx_len, hw), lambda b, n, h: (b, col // hw + h))
    return pl.pallas_call(
        functools.partial(_attn_b_kernel, grid_rows=seq // GRID_W),
        grid=(batch, ng, hg),
        in_specs=[row_spec(q_col, 0),
                  row_spec(k_col, -1), row_spec(k_col, 0), row_spec(k_col, 1),
                  row_spec(v_col, -1), row_spec(v_col, 0), row_spec(v_col, 1),
                  ctxs(ck_col), ctxs(cv_col),
                  pl.BlockSpec((hw // HEAD_DIM, nq, bias.shape[2]), lambda b, n, h: (h, 0, 0))],
        out_specs=pl.BlockSpec((nq, hw), lambda b, n, h: (b * ng + n, h)),
        out_shape=jax.ShapeDtypeStruct((batch * seq, W_B), BF16),
        compiler_params=pltpu.CompilerParams(dimension_semantics=("arbitrary",) * 3,
                                             vmem_limit_bytes=VMEM_LIMIT_BYTES),
        name="attn_b",
    )(proj, proj, proj, proj, proj, proj, proj, cproj, cproj, bias)


def _mix_kernel(oa_ref, ob_ref, ga_ref, gb_ref, wa_ref, wb_ref, z_ref):
    ya = jnp.dot(oa_ref[...], wa_ref[...], preferred_element_type=F32)
    yb = jnp.dot(ob_ref[...], wb_ref[...], preferred_element_type=F32)
    z_ref[...] = (ga_ref[...].astype(F32) * ya + gb_ref[...].astype(F32) * yb).astype(BF16)


def _mix_call(oa, ob, proj, wa, wb, *, tm):
    n = oa.shape[0]
    d = wa.shape[1]
    const = lambda shape: pl.BlockSpec(shape, lambda i: (0, 0), pipeline_mode=pl.Buffered(1))
    return pl.pallas_call(
        _mix_kernel,
        grid=(n // tm,),
        in_specs=[pl.BlockSpec((tm, oa.shape[1]), lambda i: (i, 0)),
                  pl.BlockSpec((tm, ob.shape[1]), lambda i: (i, 0)),
                  pl.BlockSpec((tm, d), lambda i: (i, 0)),
                  pl.BlockSpec((tm, d), lambda i: (i, 1)),
                  const(wa.shape), const(wb.shape)],
        out_specs=pl.BlockSpec((tm, d), lambda i: (i, 0)),
        out_shape=jax.ShapeDtypeStruct((n, d), BF16),
        compiler_params=pltpu.CompilerParams(dimension_semantics=("arbitrary",),
                                             vmem_limit_bytes=VMEM_LIMIT_BYTES),
        name="branch_mix",
    )(oa, ob, proj, proj, wa, wb)


def _outnorm_kernel(a_ref, w_ref, x_ref, gt_ref, g_ref, o_ref, acc_ref):
    k = pl.program_id(1)
    part = jnp.dot(a_ref[...], w_ref[...], preferred_element_type=F32)

    @pl.when(k == 0)
    def _():
        acc_ref[...] = part

    @pl.when(k > 0)
    def _():
        acc_ref[...] += part

    @pl.when(k == pl.num_programs(1) - 1)
    def _():
        o_ref[...] = x_ref[...] + gt_ref[0] * _rms(acc_ref[...], g_ref[...])


def _outnorm_call(a, w, x2, gt, g, *, tm, tk, rows_per_mod):
    n, kdim = a.shape
    d = w.shape[1]
    mod_tiles = rows_per_mod // tm
    return pl.pallas_call(
        _outnorm_kernel,
        grid=(n // tm, kdim // tk),
        in_specs=[pl.BlockSpec((tm, tk), lambda i, k: (i, k)),
                  pl.BlockSpec((tk, d), lambda i, k: (k, 0)),
                  pl.BlockSpec((tm, d), lambda i, k: (i, 0)),
                  pl.BlockSpec((1, 1, d), lambda i, k: (i // mod_tiles, 0, 0)),
                  pl.BlockSpec((1, d), lambda i, k: (0, 0))],
        out_specs=pl.BlockSpec((tm, d), lambda i, k: (i, 0)),
        out_shape=jax.ShapeDtypeStruct((n, d), F32),
        scratch_shapes=[pltpu.VMEM((tm, d), F32)],
        compiler_params=pltpu.CompilerParams(dimension_semantics=("arbitrary", "arbitrary"),
                                             vmem_limit_bytes=VMEM_LIMIT_BYTES),
        name="out_norm",
    )(a, w, x2, gt, g)


def _rowmm_norm_kernel(a_ref, w_ref, x_ref, gt_ref, g_ref, o_ref, *, n_chunks):
    cm = a_ref.shape[0] // n_chunks
    y = jnp.dot(a_ref[0:cm], w_ref[...], preferred_element_type=F32)
    for c in range(n_chunks):
        nxt = None
        if c + 1 < n_chunks:
            nxt = jnp.dot(a_ref[(c + 1) * cm:(c + 2) * cm], w_ref[...], preferred_element_type=F32)
        rows = slice(c * cm, (c + 1) * cm)
        o_ref[rows] = x_ref[rows] + gt_ref[0] * _rms(y, g_ref[...])
        y = nxt


def _rowmm_norm_call(a, w, x2, gt, g, *, tm, n_chunks, rows_per_mod):
    n, kdim = a.shape
    d = w.shape[1]
    mod_tiles = rows_per_mod // tm
    return pl.pallas_call(
        functools.partial(_rowmm_norm_kernel, n_chunks=n_chunks),
        grid=(n // tm,),
        in_specs=[pl.BlockSpec((tm, kdim), lambda i: (i, 0)),
                  pl.BlockSpec((kdim, d), lambda i: (0, 0), pipeline_mode=pl.Buffered(1)),
                  pl.BlockSpec((tm, d), lambda i: (i, 0)),
                  pl.BlockSpec((1, 1, d), lambda i: (i // mod_tiles, 0, 0)),
                  pl.BlockSpec((1, d), lambda i: (0, 0))],
        out_specs=pl.BlockSpec((tm, d), lambda i: (i, 0)),
        out_shape=jax.ShapeDtypeStruct((n, d), F32),
        compiler_params=pltpu.CompilerParams(dimension_semantics=("arbitrary",),
                                             vmem_limit_bytes=VMEM_LIMIT_RESIDENT_BYTES),
        name="rowmm_norm",
    )(a, w, x2, gt, g)


def _ffn_up_kernel(xp_ref, x_ref, xn_ref, sc_ref, sh_ref, g_ref, wa_ref, wg_ref, cwa_ref, cwg_ref, cba_ref, cbg_ref,
                   o_ref, h_ref, *, tiles_per_seq):
    i = pl.program_id(0)
    j = pl.program_id(1)
    tm = x_ref.shape[0]
    halo = xp_ref.shape[0]

    @pl.when(j == 0)
    def _():
        def mod(x):
            return (_rms(x, g_ref[...]) * (1 + sc_ref[0]) + sh_ref[0]).astype(BF16)
        h_ref[0:halo] = mod(xp_ref[...])
        h_ref[halo:halo + tm] = mod(x_ref[...])
        h_ref[halo + tm:] = mod(xn_ref[...])

    first_row = jnp.where((i % tiles_per_seq) == 0, 0, -1)
    last_row = jnp.where((i % tiles_per_seq) == tiles_per_seq - 1, tm - 1, -1)
    row = lax.broadcasted_iota(jnp.int32, (tm, 1), 0)
    h = h_ref[...]
    ext = tm + 2 * halo

    def conv(w_ref, cw_ref, cb_ref):
        u = jnp.dot(h, w_ref[...], preferred_element_type=F32)
        up = pltpu.roll(u, 1, 0)[halo:halo + tm]
        un = pltpu.roll(u, ext - 1, 0)[halo:halo + tm]
        up = jnp.where(row == first_row, 0.0, up)
        un = jnp.where(row == last_row, 0.0, un)
        out = cb_ref[...] + up * cw_ref[0:1]
        out = out + u[halo:halo + tm] * cw_ref[1:2]
        return out + un * cw_ref[2:3]

    a = conv(wa_ref, cwa_ref, cba_ref)
    gte = conv(wg_ref, cwg_ref, cbg_ref)
    o_ref[...] = (jax.nn.silu(gte) * a).astype(BF16)


def _ffn_up_call(x2, sc, sh, g, w_up, conv_w, conv_b, *, tm, tn, seq):
    n, d = x2.shape
    dff = w_up.shape[1] // 2
    nj = dff // tn
    halo = BF16_ROWS
    hb = tm // halo
    n_halo = n // halo
    tiles_per_seq = seq // tm
    return pl.pallas_call(
        functools.partial(_ffn_up_kernel, tiles_per_seq=tiles_per_seq),
        grid=(n // tm, nj),
        in_specs=[pl.BlockSpec((halo, d), lambda i, j: (jnp.maximum(i * hb - 1, 0), 0)),
                  pl.BlockSpec((tm, d), lambda i, j: (i, 0)),
                  pl.BlockSpec((halo, d), lambda i, j: (jnp.minimum((i + 1) * hb, n_halo - 1), 0)),
                  pl.BlockSpec((1, 1, d), lambda i, j: (i // tiles_per_seq, 0, 0)),
                  pl.BlockSpec((1, 1, d), lambda i, j: (i // tiles_per_seq, 0, 0)),
                  pl.BlockSpec((1, d), lambda i, j: (0, 0)),
                  pl.BlockSpec((d, tn), lambda i, j: (0, j)),
                  pl.BlockSpec((d, tn), lambda i, j: (0, j + nj)),
                  pl.BlockSpec((CONV_W, tn), lambda i, j: (0, j)),
                  pl.BlockSpec((CONV_W, tn), lambda i, j: (0, j + nj)),
                  pl.BlockSpec((1, tn), lambda i, j: (0, j)),
                  pl.BlockSpec((1, tn), lambda i, j: (0, j + nj))],
        out_specs=pl.BlockSpec((tm, tn), lambda i, j: (i, j)),
        out_shape=jax.ShapeDtypeStruct((n, dff), BF16),
        scratch_shapes=[pltpu.VMEM((tm + 2 * halo, d), BF16)],
        compiler_params=pltpu.CompilerParams(dimension_semantics=("arbitrary", "arbitrary"),
                                             vmem_limit_bytes=VMEM_LIMIT_BYTES),
        name="ffn_up",
    )(x2, x2, x2, sc, sh, g, w_up, w_up, conv_w, conv_w, conv_b, conv_b)


def _rope_tables(seq):
    t = jnp.arange(seq)
    half = HEAD_DIM // 4
    inv = ROPE_BASE ** (-jnp.arange(half, dtype=F32) / half)

    def cs(pos):
        ang = pos.astype(F32)[:, None] * inv[None, :]
        c, s = jnp.cos(ang), jnp.sin(ang)
        return jnp.concatenate([c, c], axis=1), jnp.concatenate([-s, s], axis=1)

    cr, sr = cs(t // GRID_W)
    cc, sn = cs(t % GRID_W)
    return jnp.concatenate([cr, cc], axis=1), jnp.concatenate([sr, sn], axis=1)


def _pick_tile(n, pref):
    while n % pref:
        pref //= 2
    return pref


def kernel(x, c, ctx, c_ctx, w_mod, b_mod, g_attn_pre, g_attn_post, g_ffn_pre, g_ffn_post, w_in, sink_a, rpb_b,
           w_br_a, w_br_b, w_o, w_up, conv_w, conv_b, w_down):
    batch, seq, d = x.shape
    ctx_len = ctx.shape[1]
    assert w_mod.shape[0] == 1 and d == 16 * HEAD_DIM
    assert seq % QB_A == 0 and seq % (ROW_GROUP_B * GRID_W) == 0 and WINDOW == BLOCK
    n = batch * seq
    x2 = x.reshape(n, d)

    pad = (-(batch + 1)) % 8
    cs = jnp.concatenate([c, c_ctx[None, :], jnp.zeros((pad, d), F32)], axis=0)
    mod = _mod_call(cs, w_mod[0], b_mod[0][None, :])
    sh1, sc1, gt1, sh2, sc2, gt2 = [mod[:batch, k * d:(k + 1) * d].reshape(batch, 1, d) for k in range(6)]
    csh1 = mod[batch:batch + 1, 0:d].reshape(1, 1, d)
    csc1 = mod[batch:batch + 1, d:2 * d].reshape(1, 1, d)

    o_qa, o_ka, o_va, o_qb, o_kb, o_vb, o_ga = 0, W_QA, W_QA + W_KA, W_QA + 2 * W_KA, W_QA + 2 * W_KA + W_B, \
        W_QA + 2 * W_KA + 2 * W_B, W_QA + 2 * W_KA + 3 * W_B
    wi = w_in[0]
    w_perm = jnp.concatenate([wi[:, o_ga:], wi[:, o_qa:o_ka], wi[:, o_qb:o_ga], wi[:, o_ka:o_qb]], axis=1).astype(BF16)
    c_qa = 2 * d
    c_qb = c_qa + W_QA
    c_kb = c_qb + W_B
    c_vb = c_kb + W_B
    c_ka = c_vb + W_B
    c_va = c_ka + W_KA
    n_tiles = w_perm.shape[1] // TN
    cos, sin = _rope_tables(seq)
    tm = _pick_tile(seq, 1024)
    proj = _inproj_call(x2, sc1, sh1, g_attn_pre, w_perm, cos, sin, tm=tm, rows_per_mod=seq, j0=0, nj=n_tiles)
    nc = batch * ctx_len
    j0c = c_kb // TN
    cproj = _inproj_call(ctx.reshape(nc, d), csc1, csh1, g_attn_pre, w_perm,
                         jnp.ones((nc, HEAD_DIM), F32), jnp.zeros((nc, HEAD_DIM), F32),
                         tm=nc, rows_per_mod=nc, j0=j0c, nj=n_tiles - j0c)
    cc_kb, cc_vb, cc_ka, cc_va = 0, W_B, 2 * W_B, 2 * W_B + W_KA

    oa = _attn_a_call(proj, cproj, sink_a[0], batch=batch, seq=seq, ctx_len=ctx_len,
                      q_col=c_qa, k_col=c_ka, v_col=c_va, ck_col=cc_ka, cv_col=cc_va)
    bias = _rpb_call(rpb_b[0])
    ob = _attn_b_call(proj, cproj, bias, batch=batch, seq=seq, ctx_len=ctx_len,
                      q_col=c_qb, k_col=c_kb, v_col=c_vb, ck_col=cc_kb, cv_col=cc_vb)

    tm2 = _pick_tile(seq, 512)
    z = _mix_call(oa, ob, proj, w_br_a[0].astype(BF16), w_br_b[0].astype(BF16), tm=tm2)
    x1 = _outnorm_call(z, w_o[0].astype(BF16), x2, gt1, g_attn_post, tm=tm2, tk=d, rows_per_mod=seq)

    dff = w_down.shape[1]
    act = _ffn_up_call(x1, sc2, sh2, g_ffn_pre, w_up[0].astype(BF16), conv_w[0], conv_b[0][None, :],
                       tm=tm, tn=TN, seq=seq)
    out = _rowmm_norm_call(act, w_down[0].astype(BF16), x1, gt2, g_ffn_post, tm=_pick_tile(seq, 256), n_chunks=2,
                           rows_per_mod=seq)
    return out.reshape(batch, seq, d)
```

```python
import functools

import jax
import jax.numpy as jnp
from jax import lax
from jax.experimental import pallas as pl
from jax.experimental.pallas import tpu as pltpu

F32 = jnp.float32
BF16 = jnp.bfloat16

GRID_W = 64
HEAD_DIM = 128
N_HEADS_A = 8
N_KV_A = 2
GQA_GROUP = N_HEADS_A // N_KV_A
WINDOW = 128
BLOCK = 128
N_HEADS_B = 8
NB_ROWS = 8
NB_COLS = 16
CONV_W = 3
ROPE_BASE = 10000.0
EPS = 1e-6
NEG_INF = -1e30
ATTN_SCALE = HEAD_DIM ** -0.5

W_QA = N_HEADS_A * HEAD_DIM
W_KA = N_KV_A * HEAD_DIM
W_B = N_HEADS_B * HEAD_DIM

LANES = 128
BF16_ROWS = 16
VMEM_LIMIT_BYTES = 56 * 1024 * 1024
VMEM_LIMIT_RESIDENT_BYTES = 60 * 1024 * 1024

TN = 512
ROW_CHUNKS = 4
ROW_GROUP_B = 4
QB_A = 512


def _rms(x, g):
    return (x * lax.rsqrt(jnp.mean(x * x, axis=-1, keepdims=True) + EPS)) * g


def _mod_kernel(c_ref, w_ref, b_ref, o_ref):
    a = jax.nn.silu(c_ref[...]).astype(BF16)
    o_ref[...] = jnp.dot(a, w_ref[...].astype(BF16), preferred_element_type=F32) + b_ref[...]


def _mod_call(cs, w, b):
    m, d = cs.shape
    n = w.shape[1]
    tn = 1536 if n % 1536 == 0 else n
    return pl.pallas_call(
        _mod_kernel,
        grid=(n // tn,),
        in_specs=[pl.BlockSpec((m, d), lambda j: (0, 0)),
                  pl.BlockSpec((d, tn), lambda j: (0, j)),
                  pl.BlockSpec((1, tn), lambda j: (0, j))],
        out_specs=pl.BlockSpec((m, tn), lambda j: (0, j)),
        out_shape=jax.ShapeDtypeStruct((m, n), F32),
        compiler_params=pltpu.CompilerParams(dimension_semantics=("arbitrary",),
                                             vmem_limit_bytes=VMEM_LIMIT_BYTES),
        name="mod",
    )(cs, w, b)


def _rpb_kernel(rpb_ref, o_ref):
    h = pl.program_id(0)
    n_dr = 2 * NB_ROWS - 1
    n_dc = 2 * NB_COLS - 1
    qc = lax.broadcasted_iota(jnp.int32, (GRID_W, LANES), 0)
    lane = lax.broadcasted_iota(jnp.int32, (GRID_W, LANES), 1)
    kc = lane & (GRID_W - 1)
    dc_idx = jnp.clip(kc - qc, -(NB_COLS - 1), NB_COLS - 1) + (NB_COLS - 1)
    base = h * (n_dr * n_dc)

    def toeplitz(dr):
        def body(dc, acc):
            return jnp.where(dc_idx == dc, rpb_ref[base + dr * n_dc + dc], acc)
        return lax.fori_loop(0, n_dc, body, jnp.zeros((GRID_W, LANES), F32))

    tw = [toeplitz(dr) for dr in range(n_dr)]
    pair = [jnp.where(lane < GRID_W, tw[d], tw[d + 1]) for d in range(n_dr - 1)]
    n_kr = 3 * ROW_GROUP_B
    dr0 = NB_ROWS - 1 - ROW_GROUP_B
    for rq in range(ROW_GROUP_B):
        for p in range(n_kr // 2):
            o_ref[0, rq * GRID_W:(rq + 1) * GRID_W, p * LANES:(p + 1) * LANES] = pair[2 * p - rq + dr0]


def _rpb_call(rpb):
    nh = rpb.shape[0]
    nq = ROW_GROUP_B * GRID_W
    nk = 3 * ROW_GROUP_B * GRID_W
    assert 2 * ROW_GROUP_B == NB_ROWS
    return pl.pallas_call(
        _rpb_kernel,
        grid=(nh,),
        in_specs=[pl.BlockSpec(memory_space=pltpu.SMEM)],
        out_specs=pl.BlockSpec((1, nq, nk), lambda h: (h, 0, 0)),
        out_shape=jax.ShapeDtypeStruct((nh, nq, nk), F32),
        compiler_params=pltpu.CompilerParams(dimension_semantics=("arbitrary",)),
        name="rpb_table",
    )(rpb.reshape(-1))


def _rope(t, cos, sin, first_half):
    partner = jnp.where(first_half, pltpu.roll(t, LANES - 32, 1), pltpu.roll(t, 32, 1))
    return t * cos + partner * sin


def _inproj_kernel(x_ref, sc_ref, sh_ref, g_ref, w_ref, cos_ref, sin_ref, o_ref, h_ref, *, j0):
    j = pl.program_id(1)

    @pl.when(j == 0)
    def _():
        h = _rms(x_ref[...], g_ref[...]) * (1 + sc_ref[0]) + sh_ref[0]
        h_ref[...] = h.astype(BF16)

    jj = j + j0
    n_gate = 2 * x_ref.shape[1] // TN
    lane = lax.broadcasted_iota(jnp.int32, (1, HEAD_DIM), 1)
    first_half = (lane & 63) < 32
    cm = x_ref.shape[0] // ROW_CHUNKS

    def row_chunks(epilogue):
        acc = jnp.dot(h_ref[0:cm], w_ref[...], preferred_element_type=F32)
        for c in range(ROW_CHUNKS):
            nxt = None
            if c + 1 < ROW_CHUNKS:
                nxt = jnp.dot(h_ref[(c + 1) * cm:(c + 2) * cm], w_ref[...], preferred_element_type=F32)
            epilogue(slice(c * cm, (c + 1) * cm), acc)
            acc = nxt

    def rope_heads(rows, acc, n_heads, scale):
        cos = cos_ref[rows]
        sin = sin_ref[rows]
        for hd in range(n_heads):
            sl = slice(hd * HEAD_DIM, (hd + 1) * HEAD_DIM)
            r = _rope(acc[:, sl], cos, sin, first_half)
            if scale is not None:
                r = r * scale
            o_ref[rows, sl] = r.astype(BF16)

    @pl.when(jj < n_gate)
    def _():
        def epilogue(rows, acc):
            o_ref[rows] = jax.nn.sigmoid(acc).astype(BF16)
        row_chunks(epilogue)

    @pl.when((jj >= n_gate) & (jj < n_gate + 2))
    def _():
        row_chunks(lambda rows, acc: rope_heads(rows, acc, TN // HEAD_DIM, ATTN_SCALE))

    @pl.when((jj >= n_gate + 2) & (jj < n_gate + 4))
    def _():
        def epilogue(rows, acc):
            o_ref[rows] = (acc * ATTN_SCALE).astype(BF16)
        row_chunks(epilogue)

    @pl.when((jj >= n_gate + 4) & (jj < n_gate + 8))
    def _():
        def epilogue(rows, acc):
            o_ref[rows] = acc.astype(BF16)
        row_chunks(epilogue)

    @pl.when(jj == n_gate + 8)
    def _():
        def epilogue(rows, acc):
            rope_heads(rows, acc, N_KV_A, None)
            o_ref[rows, W_KA:] = acc[:, W_KA:].astype(BF16)
        row_chunks(epilogue)


def _inproj_call(x2, sc, sh, g, w, cos, sin, *, tm, rows_per_mod, j0, nj):
    n, d = x2.shape
    pos_tiles = cos.shape[0] // tm
    mod_tiles = rows_per_mod // tm
    return pl.pallas_call(
        functools.partial(_inproj_kernel, j0=j0),
        grid=(n // tm, nj),
        in_specs=[pl.BlockSpec((tm, d), lambda i, j: (i, 0)),
                  pl.BlockSpec((1, 1, d), lambda i, j: (i // mod_tiles, 0, 0)),
                  pl.BlockSpec((1, 1, d), lambda i, j: (i // mod_tiles, 0, 0)),
                  pl.BlockSpec((1, d), lambda i, j: (0, 0)),
                  pl.BlockSpec((d, TN), lambda i, j: (0, j + j0)),
                  pl.BlockSpec((tm, HEAD_DIM), lambda i, j: (i % pos_tiles, 0)),
                  pl.BlockSpec((tm, HEAD_DIM), lambda i, j: (i % pos_tiles, 0))],
        out_specs=pl.BlockSpec((tm, TN), lambda i, j: (i, j)),
        out_shape=jax.ShapeDtypeStruct((n, nj * TN), BF16),
        scratch_shapes=[pltpu.VMEM((tm, d), BF16)],
        compiler_params=pltpu.CompilerParams(dimension_semantics=("arbitrary", "arbitrary"),
                                             vmem_limit_bytes=VMEM_LIMIT_BYTES),
        name="in_proj",
    )(x2, sc, sh, g, w, cos, sin)


def _attn_a_kernel(sink_ref, q_ref, kp_ref, km_ref, kn_ref, vp_ref, vm_ref, vn_ref, kc_ref, vc_ref, o_ref,
                   *, n_blocks):
    nq = pl.program_id(1)
    sub = QB_A // BLOCK
    rows = GQA_GROUP * BLOCK
    n_loc = 3 * BLOCK
    r_io = lax.broadcasted_iota(jnp.int32, (rows, n_loc), 0)
    c_io = lax.broadcasted_iota(jnp.int32, (rows, n_loc), 1)
    qi_io = r_io & (BLOCK - 1)
    in_band = (c_io >= qi_io) & (c_io <= qi_io + 2 * WINDOW)
    c_row = lax.broadcasted_iota(jnp.int32, (1, n_loc), 1)
    g_io = lax.shift_right_logical(lax.broadcasted_iota(jnp.int32, (rows, 1), 0), BLOCK.bit_length() - 1)
    nt = (((1,), (1,)), ((), ()))
    for kvh in range(N_KV_A):
        hs = slice(kvh * HEAD_DIM, (kvh + 1) * HEAD_DIM)
        kband = jnp.concatenate([kp_ref[:, hs], km_ref[:, hs], kn_ref[:, hs]], axis=0)
        vband = jnp.concatenate([vp_ref[:, hs], vm_ref[:, hs], vn_ref[:, hs]], axis=0)
        kc = kc_ref[:, hs]
        vc = vc_ref[:, hs]
        sink = jnp.zeros((rows, 1), F32)
        for g in range(GQA_GROUP):
            sink = jnp.where(g_io == g, sink_ref[kvh * GQA_GROUP + g], sink)
        for qi in range(sub):
            blk = nq * sub + qi
            lo = jnp.where(blk == 0, BLOCK, 0)
            hi = jnp.where(blk == n_blocks - 1, 2 * BLOCK - 1, n_loc - 1)
            valid = in_band & ((c_row >= lo) & (c_row <= hi))
            q4 = jnp.concatenate(
                [q_ref[qi * BLOCK:(qi + 1) * BLOCK, (kvh * GQA_GROUP + g) * HEAD_DIM:(kvh * GQA_GROUP + g + 1) * HEAD_DIM]
                 for g in range(GQA_GROUP)], axis=0)
            s_loc = lax.dot_general(q4, kband[qi * BLOCK:qi * BLOCK + n_loc], nt, preferred_element_type=F32)
            s_loc = jnp.where(valid, s_loc, NEG_INF)
            s_ctx = lax.dot_general(q4, kc, nt, preferred_element_type=F32)
            m = jnp.maximum(jnp.maximum(jnp.max(s_loc, axis=-1, keepdims=True),
                                        jnp.max(s_ctx, axis=-1, keepdims=True)), sink)
            e_loc = jnp.exp(s_loc - m)
            e_ctx = jnp.exp(s_ctx - m)
            den = (jnp.sum(e_loc, axis=-1, keepdims=True) + jnp.sum(e_ctx, axis=-1, keepdims=True)
                   + jnp.exp(sink - m))
            o = (jnp.dot(e_loc.astype(BF16), vband[qi * BLOCK:qi * BLOCK + n_loc], preferred_element_type=F32)
                 + jnp.dot(e_ctx.astype(BF16), vc, preferred_element_type=F32)) * (1.0 / den)
            for g in range(GQA_GROUP):
                hd = kvh * GQA_GROUP + g
                o_ref[qi * BLOCK:(qi + 1) * BLOCK, hd * HEAD_DIM:(hd + 1) * HEAD_DIM] = (
                    o[g * BLOCK:(g + 1) * BLOCK].astype(BF16))


def _attn_a_call(proj, cproj, sink, *, batch, seq, ctx_len, q_col, k_col, v_col, ck_col, cv_col):
    nqb = seq // QB_A
    sub = QB_A // BLOCK
    n_blocks = seq // BLOCK

    def prev_map(b, n, col):
        return (b * n_blocks + jnp.maximum(n * sub - 1, 0), col // W_KA)

    def next_map(b, n, col):
        return (b * n_blocks + jnp.minimum(n * sub + sub, n_blocks - 1), col // W_KA)

    def main_map(b, n, col):
        return (b * nqb + n, col // W_KA)

    edge = lambda f, col: pl.BlockSpec((BLOCK, W_KA), functools.partial(f, col=col))
    main = lambda col: pl.BlockSpec((QB_A, W_KA), functools.partial(main_map, col=col))
    ctxs = lambda col: pl.BlockSpec((ctx_len, W_KA), lambda b, n: (b, col // W_KA))
    return pl.pallas_call(
        functools.partial(_attn_a_kernel, n_blocks=n_blocks),
        grid=(batch, nqb),
        in_specs=[pl.BlockSpec(memory_space=pltpu.SMEM),
                  pl.BlockSpec((QB_A, W_QA), lambda b, n: (b * nqb + n, q_col // W_QA)),
                  edge(prev_map, k_col), main(k_col), edge(next_map, k_col),
                  edge(prev_map, v_col), main(v_col), edge(next_map, v_col),
                  ctxs(ck_col), ctxs(cv_col)],
        out_specs=pl.BlockSpec((QB_A, W_QA), lambda b, n: (b * nqb + n, 0)),
        out_shape=jax.ShapeDtypeStruct((batch * seq, W_QA), BF16),
        compiler_params=pltpu.CompilerParams(dimension_semantics=("arbitrary",) * 2,
                                             vmem_limit_bytes=VMEM_LIMIT_BYTES),
        name="attn_a",
    )(sink, proj, proj, proj, proj, proj, proj, proj, cproj, cproj)


def _attn_b_kernel(q_ref, kp_ref, km_ref, kn_ref, vp_ref, vm_ref, vn_ref, kc_ref, vc_ref, bias_ref, o_ref,
                   *, grid_rows):
    n = pl.program_id(1)
    nq = ROW_GROUP_B * GRID_W
    nk = 3 * nq
    n_ctx = kc_ref.shape[0]
    heads = q_ref.shape[1] // HEAD_DIM
    r_io = lax.broadcasted_iota(jnp.int32, (nq, nk), 0)
    c_io = lax.broadcasted_iota(jnp.int32, (nq, nk), 1)
    log_w = GRID_W.bit_length() - 1
    r_abs = n * ROW_GROUP_B + lax.shift_right_logical(r_io, log_w)
    q_col = r_io & (GRID_W - 1)
    kr_abs = (n - 1) * ROW_GROUP_B + lax.shift_right_logical(c_io, log_w)
    k_col = c_io & (GRID_W - 1)
    r_start = jnp.clip(r_abs - NB_ROWS // 2, 0, grid_rows - NB_ROWS)
    c_start = jnp.clip(q_col - NB_COLS // 2, 0, GRID_W - NB_COLS)
    valid = ((kr_abs >= r_start) & (kr_abs < r_start + NB_ROWS)
             & (k_col >= c_start) & (k_col < c_start + NB_COLS))
    for hd in range(heads):
        sl = slice(hd * HEAD_DIM, (hd + 1) * HEAD_DIM)
        q = q_ref[:, sl]
        keys = jnp.concatenate([kp_ref[:, sl], km_ref[:, sl], kn_ref[:, sl]], axis=0)
        vals = jnp.concatenate([vp_ref[:, sl], vm_ref[:, sl], vn_ref[:, sl]], axis=0)
        s_nb = lax.dot_general(q, keys, (((1,), (1,)), ((), ())), preferred_element_type=F32)
        s_nb = jnp.where(valid, s_nb + bias_ref[hd], NEG_INF)
        s_ctx = lax.dot_general(q, kc_ref[:, sl], (((1,), (1,)), ((), ())), preferred_element_type=F32)
        m = jnp.maximum(jnp.max(s_nb, axis=-1, keepdims=True), jnp.max(s_ctx, axis=-1, keepdims=True))
        e_nb = jnp.exp(s_nb - m)
        e_ctx = jnp.exp(s_ctx - m)
        den = jnp.sum(e_nb, axis=-1, keepdims=True) + jnp.sum(e_ctx, axis=-1, keepdims=True)
        o = (jnp.dot(e_nb.astype(BF16), vals, preferred_element_type=F32)
             + jnp.dot(e_ctx.astype(BF16), vc_ref[:, sl], preferred_element_type=F32)) * (1.0 / den)
        o_ref[:, sl] = o.astype(BF16)


def _attn_b_call(proj, cproj, bias, *, batch, seq, ctx_len, q_col, k_col, v_col, ck_col, cv_col):
    nq = ROW_GROUP_B * GRID_W
    ng = seq // nq
    hw = W_B
    hg = W_B // hw

    def row_spec(col, shift):
        def imap(b, n, h):
            return (b * ng + jnp.clip(n + shift, 0, ng - 1), col // hw + h)
        return pl.BlockSpec((nq, hw), imap)

    ctxs = lambda col: pl.BlockSpec((ctx_len, hw), lambda b, n, h: (b, col // hw + h))
    return pl.pallas_call(
        functools.partial(_attn_b_kernel, grid_rows=seq // GRID_W),
        grid=(batch, ng, hg),
        in_specs=[row_spec(q_col, 0),
                  row_spec(k_col, -1), row_spec(k_col, 0), row_spec(k_col, 1),
                  row_spec(v_col, -1), row_spec(v_col, 0), row_spec(v_col, 1),
                  ctxs(ck_col), ctxs(cv_col),
                  pl.BlockSpec((hw // HEAD_DIM, nq, bias.shape[2]), lambda b, n, h: (h, 0, 0))],
        out_specs=pl.BlockSpec((nq, hw), lambda b, n, h: (b * ng + n, h)),
        out_shape=jax.ShapeDtypeStruct((batch * seq, W_B), BF16),
        compiler_params=pltpu.CompilerParams(dimension_semantics=("arbitrary",) * 3,
                                             vmem_limit_bytes=VMEM_LIMIT_BYTES),
        name="attn_b",
    )(proj, proj, proj, proj, proj, proj, proj, cproj, cproj, bias)


def _mix_kernel(oa_ref, ob_ref, ga_ref, gb_ref, wa_ref, wb_ref, z_ref):
    ya = jnp.dot(oa_ref[...], wa_ref[...], preferred_element_type=F32)
    yb = jnp.dot(ob_ref[...], wb_ref[...], preferred_element_type=F32)
    z_ref[...] = (ga_ref[...].astype(F32) * ya + gb_ref[...].astype(F32) * yb).astype(BF16)


def _mix_call(oa, ob, proj, wa, wb, *, tm):
    n = oa.shape[0]
    d = wa.shape[1]
    const = lambda shape: pl.BlockSpec(shape, lambda i: (0, 0), pipeline_mode=pl.Buffered(1))
    return pl.pallas_call(
        _mix_kernel,
        grid=(n // tm,),
        in_specs=[pl.BlockSpec((tm, oa.shape[1]), lambda i: (i, 0)),
                  pl.BlockSpec((tm, ob.shape[1]), lambda i: (i, 0)),
                  pl.BlockSpec((tm, d), lambda i: (i, 0)),
                  pl.BlockSpec((tm, d), lambda i: (i, 1)),
                  const(wa.shape), const(wb.shape)],
        out_specs=pl.BlockSpec((tm, d), lambda i: (i, 0)),
        out_shape=jax.ShapeDtypeStruct((n, d), BF16),
        compiler_params=pltpu.CompilerParams(dimension_semantics=("arbitrary",),
                                             vmem_limit_bytes=VMEM_LIMIT_BYTES),
        name="branch_mix",
    )(oa, ob, proj, proj, wa, wb)


def _outnorm_kernel(a_ref, w_ref, x_ref, gt_ref, g_ref, o_ref, acc_ref):
    k = pl.program_id(1)
    part = jnp.dot(a_ref[...], w_ref[...], preferred_element_type=F32)

    @pl.when(k == 0)
    def _():
        acc_ref[...] = part

    @pl.when(k > 0)
    def _():
        acc_ref[...] += part

    @pl.when(k == pl.num_programs(1) - 1)
    def _():
        o_ref[...] = x_ref[...] + gt_ref[0] * _rms(acc_ref[...], g_ref[...])


def _outnorm_call(a, w, x2, gt, g, *, tm, tk, rows_per_mod):
    n, kdim = a.shape
    d = w.shape[1]
    mod_tiles = rows_per_mod // tm
    return pl.pallas_call(
        _outnorm_kernel,
        grid=(n // tm, kdim // tk),
        in_specs=[pl.BlockSpec((tm, tk), lambda i, k: (i, k)),
                  pl.BlockSpec((tk, d), lambda i, k: (k, 0)),
                  pl.BlockSpec((tm, d), lambda i, k: (i, 0)),
                  pl.BlockSpec((1, 1, d), lambda i, k: (i // mod_tiles, 0, 0)),
                  pl.BlockSpec((1, d), lambda i, k: (0, 0))],
        out_specs=pl.BlockSpec((tm, d), lambda i, k: (i, 0)),
        out_shape=jax.ShapeDtypeStruct((n, d), F32),
        scratch_shapes=[pltpu.VMEM((tm, d), F32)],
        compiler_params=pltpu.CompilerParams(dimension_semantics=("arbitrary", "arbitrary"),
                                             vmem_limit_bytes=VMEM_LIMIT_BYTES),
        name="out_norm",
    )(a, w, x2, gt, g)


def _rowmm_norm_kernel(a_ref, w_ref, x_ref, gt_ref, g_ref, o_ref, *, n_chunks):
    cm = a_ref.shape[0] // n_chunks
    y = jnp.dot(a_ref[0:cm], w_ref[...], preferred_element_type=F32)
    for c in range(n_chunks):
        nxt = None
        if c + 1 < n_chunks:
            nxt = jnp.dot(a_ref[(c + 1) * cm:(c + 2) * cm], w_ref[...], preferred_element_type=F32)
        rows = slice(c * cm, (c + 1) * cm)
        o_ref[rows] = x_ref[rows] + gt_ref[0] * _rms(y, g_ref[...])
        y = nxt


def _rowmm_norm_call(a, w, x2, gt, g, *, tm, n_chunks, rows_per_mod):
    n, kdim = a.shape
    d = w.shape[1]
    mod_tiles = rows_per_mod // tm
    return pl.pallas_call(
        functools.partial(_rowmm_norm_kernel, n_chunks=n_chunks),
        grid=(n // tm,),
        in_specs=[pl.BlockSpec((tm, kdim), lambda i: (i, 0)),
                  pl.BlockSpec((kdim, d), lambda i: (0, 0), pipeline_mode=pl.Buffered(1)),
                  pl.BlockSpec((tm, d), lambda i: (i, 0)),
                  pl.BlockSpec((1, 1, d), lambda i: (i // mod_tiles, 0, 0)),
                  pl.BlockSpec((1, d), lambda i: (0, 0))],
        out_specs=pl.BlockSpec((tm, d), lambda i: (i, 0)),
        out_shape=jax.ShapeDtypeStruct((n, d), F32),
        compiler_params=pltpu.CompilerParams(dimension_semantics=("arbitrary",),
                                             vmem_limit_bytes=VMEM_LIMIT_RESIDENT_BYTES),
        name="rowmm_norm",
    )(a, w, x2, gt, g)


def _ffn_up_kernel(xp_ref, x_ref, xn_ref, sc_ref, sh_ref, g_ref, wa_ref, wg_ref, cwa_ref, cwg_ref, cba_ref, cbg_ref,
                   o_ref, h_ref, ua_ref, ug_ref, *, nj, n_tiles, tiles_per_seq):
    t = pl.program_id(0)
    tm = x_ref.shape[0]
    halo = xp_ref.shape[0]
    ext = tm + 2 * halo
    margin = 8

    @pl.when(t == 0)
    def _():
        ua_ref[...] = jnp.zeros_like(ua_ref)
        ug_ref[...] = jnp.zeros_like(ug_ref)

    @pl.when((t < n_tiles) & (t % nj == 0))
    def _():
        def mod(x):
            return (_rms(x, g_ref[...]) * (1 + sc_ref[0]) + sh_ref[0]).astype(BF16)
        h_ref[0:halo] = mod(xp_ref[...])
        h_ref[halo:halo + tm] = mod(x_ref[...])
        h_ref[halo + tm:] = mod(xn_ref[...])

    slot = t % 2
    prev = 1 - slot
    ip = jnp.maximum(t - 1, 0) // nj
    first_row = jnp.where((ip % tiles_per_seq) == 0, 0, -1)
    last_row = jnp.where((ip % tiles_per_seq) == tiles_per_seq - 1, tm - 1, -1)
    row = lax.broadcasted_iota(jnp.int32, (tm, 1), 0)

    def epilogue(lo, hi):
        cm = hi - lo
        rowc = row[lo:hi]

        def conv(u_ref, cw_ref, cb_ref):
            win = u_ref[prev, pl.ds(halo + lo - margin, cm + 2 * margin), :]
            up = pltpu.roll(win, 1, 0)[margin:margin + cm]
            un = pltpu.roll(win, cm + 2 * margin - 1, 0)[margin:margin + cm]
            up = jnp.where(rowc == first_row, 0.0, up)
            un = jnp.where(rowc == last_row, 0.0, un)
            out = cb_ref[...] + up * cw_ref[0:1]
            out = out + win[margin:margin + cm] * cw_ref[1:2]
            return out + un * cw_ref[2:3]

        a = conv(ua_ref, cwa_ref, cba_ref)
        gte = conv(ug_ref, cwg_ref, cbg_ref)
        o_ref[lo:hi] = (jax.nn.silu(gte) * a).astype(BF16)

    half = ext // 2
    third = (tm // 3) // BF16_ROWS * BF16_ROWS
    chunks = [(0, third), (third, 2 * third), (2 * third, tm)]
    ua_ref[slot, 0:half] = jnp.dot(h_ref[0:half], wa_ref[...], preferred_element_type=F32)
    epilogue(*chunks[0])
    ua_ref[slot, half:ext] = jnp.dot(h_ref[half:ext], wa_ref[...], preferred_element_type=F32)
    epilogue(*chunks[1])
    ug_ref[slot, 0:half] = jnp.dot(h_ref[0:half], wg_ref[...], preferred_element_type=F32)
    epilogue(*chunks[2])
    ug_ref[slot, half:ext] = jnp.dot(h_ref[half:ext], wg_ref[...], preferred_element_type=F32)


def _ffn_up_call(x2, sc, sh, g, w_up, conv_w, conv_b, *, tm, tn, seq):
    n, d = x2.shape
    dff = w_up.shape[1] // 2
    nj = dff // tn
    halo = BF16_ROWS
    hb = tm // halo
    n_halo = n // halo
    tiles_per_seq = seq // tm
    n_tiles = (n // tm) * nj
    assert (tm + 2 * halo) % (2 * BF16_ROWS) == 0

    def cur(t):
        tc = jnp.minimum(t, n_tiles - 1)
        return tc // nj, tc % nj

    def prv(t):
        tp = jnp.maximum(t - 1, 0)
        return tp // nj, tp % nj

    return pl.pallas_call(
        functools.partial(_ffn_up_kernel, nj=nj, n_tiles=n_tiles, tiles_per_seq=tiles_per_seq),
        grid=(n_tiles + 1,),
        in_specs=[pl.BlockSpec((halo, d), lambda t: (jnp.maximum(cur(t)[0] * hb - 1, 0), 0)),
                  pl.BlockSpec((tm, d), lambda t: (cur(t)[0], 0)),
                  pl.BlockSpec((halo, d), lambda t: (jnp.minimum((cur(t)[0] + 1) * hb, n_halo - 1), 0)),
                  pl.BlockSpec((1, 1, d), lambda t: (cur(t)[0] // tiles_per_seq, 0, 0)),
                  pl.BlockSpec((1, 1, d), lambda t: (cur(t)[0] // tiles_per_seq, 0, 0)),
                  pl.BlockSpec((1, d), lambda t: (0, 0)),
                  pl.BlockSpec((d, tn), lambda t: (0, cur(t)[1])),
                  pl.BlockSpec((d, tn), lambda t: (0, cur(t)[1] + nj)),
                  pl.BlockSpec((CONV_W, tn), lambda t: (0, prv(t)[1])),
                  pl.BlockSpec((CONV_W, tn), lambda t: (0, prv(t)[1] + nj)),
                  pl.BlockSpec((1, tn), lambda t: (0, prv(t)[1])),
                  pl.BlockSpec((1, tn), lambda t: (0, prv(t)[1] + nj))],
        out_specs=pl.BlockSpec((tm, tn), lambda t: prv(t)),
        out_shape=jax.ShapeDtypeStruct((n, dff), BF16),
        scratch_shapes=[pltpu.VMEM((tm + 2 * halo, d), BF16),
                        pltpu.VMEM((2, tm + 2 * halo, tn), F32),
                        pltpu.VMEM((2, tm + 2 * halo, tn), F32)],
        compiler_params=pltpu.CompilerParams(dimension_semantics=("arbitrary",),
                                             vmem_limit_bytes=VMEM_LIMIT_BYTES),
        name="ffn_up",
    )(x2, x2, x2, sc, sh, g, w_up, w_up, conv_w, conv_w, conv_b, conv_b)


def _rope_tables(seq):
    t = jnp.arange(seq)
    half = HEAD_DIM // 4
    inv = ROPE_BASE ** (-jnp.arange(half, dtype=F32) / half)

    def cs(pos):
        ang = pos.astype(F32)[:, None] * inv[None, :]
        c, s = jnp.cos(ang), jnp.sin(ang)
        return jnp.concatenate([c, c], axis=1), jnp.concatenate([-s, s], axis=1)

    cr, sr = cs(t // GRID_W)
    cc, sn = cs(t % GRID_W)
    return jnp.concatenate([cr, cc], axis=1), jnp.concatenate([sr, sn], axis=1)


def _pick_tile(n, pref):
    while n % pref:
        pref //= 2
    return pref


def kernel(x, c, ctx, c_ctx, w_mod, b_mod, g_attn_pre, g_attn_post, g_ffn_pre, g_ffn_post, w_in, sink_a, rpb_b,
           w_br_a, w_br_b, w_o, w_up, conv_w, conv_b, w_down):
    batch, seq, d = x.shape
    ctx_len = ctx.shape[1]
    assert w_mod.shape[0] == 1 and d == 16 * HEAD_DIM
    assert seq % QB_A == 0 and seq % (ROW_GROUP_B * GRID_W) == 0 and WINDOW == BLOCK
    n = batch * seq
    x2 = x.reshape(n, d)

    pad = (-(batch + 1)) % 8
    cs = jnp.concatenate([c, c_ctx[None, :], jnp.zeros((pad, d), F32)], axis=0)
    mod = _mod_call(cs, w_mod[0], b_mod[0][None, :])
    sh1, sc1, gt1, sh2, sc2, gt2 = [mod[:batch, k * d:(k + 1) * d].reshape(batch, 1, d) for k in range(6)]
    csh1 = mod[batch:batch + 1, 0:d].reshape(1, 1, d)
    csc1 = mod[batch:batch + 1, d:2 * d].reshape(1, 1, d)

    o_qa, o_ka, o_va, o_qb, o_kb, o_vb, o_ga = 0, W_QA, W_QA + W_KA, W_QA + 2 * W_KA, W_QA + 2 * W_KA + W_B, \
        W_QA + 2 * W_KA + 2 * W_B, W_QA + 2 * W_KA + 3 * W_B
    wi = w_in[0]
    w_perm = jnp.concatenate([wi[:, o_ga:], wi[:, o_qa:o_ka], wi[:, o_qb:o_ga], wi[:, o_ka:o_qb]], axis=1).astype(BF16)
    c_qa = 2 * d
    c_qb = c_qa + W_QA
    c_kb = c_qb + W_B
    c_vb = c_kb + W_B
    c_ka = c_vb + W_B
    c_va = c_ka + W_KA
    n_tiles = w_perm.shape[1] // TN
    cos, sin = _rope_tables(seq)
    tm = _pick_tile(seq, 1024)
    proj = _inproj_call(x2, sc1, sh1, g_attn_pre, w_perm, cos, sin, tm=tm, rows_per_mod=seq, j0=0, nj=n_tiles)
    nc = batch * ctx_len
    j0c = c_kb // TN
    cproj = _inproj_call(ctx.reshape(nc, d), csc1, csh1, g_attn_pre, w_perm,
                         jnp.ones((nc, HEAD_DIM), F32), jnp.zeros((nc, HEAD_DIM), F32),
                         tm=nc, rows_per_mod=nc, j0=j0c, nj=n_tiles - j0c)
    cc_kb, cc_vb, cc_ka, cc_va = 0, W_B, 2 * W_B, 2 * W_B + W_KA

    oa = _attn_a_call(proj, cproj, sink_a[0], batch=batch, seq=seq, ctx_len=ctx_len,
                      q_col=c_qa, k_col=c_ka, v_col=c_va, ck_col=cc_ka, cv_col=cc_va)
    bias = _rpb_call(rpb_b[0])
    ob = _attn_b_call(proj, cproj, bias, batch=batch, seq=seq, ctx_len=ctx_len,
                      q_col=c_qb, k_col=c_kb, v_col=c_vb, ck_col=cc_kb, cv_col=cc_vb)

    tm2 = _pick_tile(seq, 512)
    z = _mix_call(oa, ob, proj, w_br_a[0].astype(BF16), w_br_b[0].astype(BF16), tm=tm2)
    x1 = _outnorm_call(z, w_o[0].astype(BF16), x2, gt1, g_attn_post, tm=tm2, tk=d, rows_per_mod=seq)

    dff = w_down.shape[1]
    act = _ffn_up_call(x1, sc2, sh2, g_ffn_pre, w_up[0].astype(BF16), conv_w[0], conv_b[0][None, :],
                       tm=tm, tn=TN, seq=seq)
    out = _rowmm_norm_call(act, w_down[0].astype(BF16), x1, gt2, g_ffn_post, tm=_pick_tile(seq, 256), n_chunks=2,
                           rows_per_mod=seq)
    return out.reshape(batch, seq, d)
```

```python
import functools

import jax
import jax.numpy as jnp
from jax import lax
from jax.experimental import pallas as pl
from jax.experimental.pallas import tpu as pltpu

F32 = jnp.float32
BF16 = jnp.bfloat16

GRID_W = 64
HEAD_DIM = 128
N_HEADS_A = 8
N_KV_A = 2
GQA_GROUP = N_HEADS_A // N_KV_A
WINDOW = 128
BLOCK = 128
N_HEADS_B = 8
NB_ROWS = 8
NB_COLS = 16
CONV_W = 3
ROPE_BASE = 10000.0
EPS = 1e-6
NEG_INF = -1e30
ATTN_SCALE = HEAD_DIM ** -0.5
LOG2E = 1.4426950408889634
Q_SCALE = ATTN_SCALE * LOG2E
F32_MAX = 3.4028234663852886e38

W_QA = N_HEADS_A * HEAD_DIM
W_KA = N_KV_A * HEAD_DIM
W_B = N_HEADS_B * HEAD_DIM

LANES = 128
BF16_ROWS = 16
VMEM_LIMIT_BYTES = 56 * 1024 * 1024
VMEM_LIMIT_RESIDENT_BYTES = 60 * 1024 * 1024

TN = 512
ROW_CHUNKS = 4
ROW_GROUP_B = 4
QB_A = 512


def _rms(x, g):
    return (x * lax.rsqrt(jnp.mean(x * x, axis=-1, keepdims=True) + EPS)) * g


def _mod_kernel(c_ref, w_ref, b_ref, o_ref):
    a = jax.nn.silu(c_ref[...]).astype(BF16)
    o_ref[...] = jnp.dot(a, w_ref[...].astype(BF16), preferred_element_type=F32) + b_ref[...]


def _mod_call(cs, w, b):
    m, d = cs.shape
    n = w.shape[1]
    tn = 1536 if n % 1536 == 0 else n
    return pl.pallas_call(
        _mod_kernel,
        grid=(n // tn,),
        in_specs=[pl.BlockSpec((m, d), lambda j: (0, 0)),
                  pl.BlockSpec((d, tn), lambda j: (0, j)),
                  pl.BlockSpec((1, tn), lambda j: (0, j))],
        out_specs=pl.BlockSpec((m, tn), lambda j: (0, j)),
        out_shape=jax.ShapeDtypeStruct((m, n), F32),
        compiler_params=pltpu.CompilerParams(dimension_semantics=("arbitrary",),
                                             vmem_limit_bytes=VMEM_LIMIT_BYTES),
        name="mod",
    )(cs, w, b)


def _rpb_kernel(rpb_ref, o_ref):
    h = pl.program_id(0)
    n_dr = 2 * NB_ROWS - 1
    n_dc = 2 * NB_COLS - 1
    qc = lax.broadcasted_iota(jnp.int32, (GRID_W, LANES), 0)
    lane = lax.broadcasted_iota(jnp.int32, (GRID_W, LANES), 1)
    kc = lane & (GRID_W - 1)
    dc_idx = jnp.clip(kc - qc, -(NB_COLS - 1), NB_COLS - 1) + (NB_COLS - 1)
    base = h * (n_dr * n_dc)

    def toeplitz(dr):
        def body(dc, acc):
            return jnp.where(dc_idx == dc, rpb_ref[base + dr * n_dc + dc] * LOG2E, acc)
        return lax.fori_loop(0, n_dc, body, jnp.zeros((GRID_W, LANES), F32))

    tw = [toeplitz(dr) for dr in range(n_dr)]
    pair = [jnp.where(lane < GRID_W, tw[d], tw[d + 1]) for d in range(n_dr - 1)]
    n_kr = 3 * ROW_GROUP_B
    dr0 = NB_ROWS - 1 - ROW_GROUP_B
    for rq in range(ROW_GROUP_B):
        for p in range(n_kr // 2):
            o_ref[0, rq * GRID_W:(rq + 1) * GRID_W, p * LANES:(p + 1) * LANES] = pair[2 * p - rq + dr0]


def _rpb_call(rpb):
    nh = rpb.shape[0]
    nq = ROW_GROUP_B * GRID_W
    nk = 3 * ROW_GROUP_B * GRID_W
    assert 2 * ROW_GROUP_B == NB_ROWS
    return pl.pallas_call(
        _rpb_kernel,
        grid=(nh,),
        in_specs=[pl.BlockSpec(memory_space=pltpu.SMEM)],
        out_specs=pl.BlockSpec((1, nq, nk), lambda h: (h, 0, 0)),
        out_shape=jax.ShapeDtypeStruct((nh, nq, nk), F32),
        compiler_params=pltpu.CompilerParams(dimension_semantics=("arbitrary",)),
        name="rpb_table",
    )(rpb.reshape(-1))


def _rope(t, cos, sin, first_half):
    partner = jnp.where(first_half, pltpu.roll(t, LANES - 32, 1), pltpu.roll(t, 32, 1))
    return t * cos + partner * sin


def _inproj_kernel(x_ref, sc_ref, sh_ref, g_ref, w_ref, cos_ref, sin_ref, o_ref, h_ref, *, j0):
    j = pl.program_id(1)

    @pl.when(j == 0)
    def _():
        h = _rms(x_ref[...], g_ref[...]) * (1 + sc_ref[0]) + sh_ref[0]
        h_ref[...] = h.astype(BF16)

    jj = j + j0
    n_gate = 2 * x_ref.shape[1] // TN
    lane = lax.broadcasted_iota(jnp.int32, (1, HEAD_DIM), 1)
    first_half = (lane & 63) < 32
    cm = x_ref.shape[0] // ROW_CHUNKS

    def row_chunks(epilogue):
        acc = jnp.dot(h_ref[0:cm], w_ref[...], preferred_element_type=F32)
        for c in range(ROW_CHUNKS):
            nxt = None
            if c + 1 < ROW_CHUNKS:
                nxt = jnp.dot(h_ref[(c + 1) * cm:(c + 2) * cm], w_ref[...], preferred_element_type=F32)
            epilogue(slice(c * cm, (c + 1) * cm), acc)
            acc = nxt

    def rope_heads(rows, acc, n_heads, scale):
        cos = cos_ref[rows]
        sin = sin_ref[rows]
        for hd in range(n_heads):
            sl = slice(hd * HEAD_DIM, (hd + 1) * HEAD_DIM)
            r = _rope(acc[:, sl], cos, sin, first_half)
            if scale is not None:
                r = r * scale
            o_ref[rows, sl] = r.astype(BF16)

    @pl.when(jj < n_gate)
    def _():
        def epilogue(rows, acc):
            o_ref[rows] = jax.nn.sigmoid(acc).astype(BF16)
        row_chunks(epilogue)

    @pl.when((jj >= n_gate) & (jj < n_gate + 2))
    def _():
        row_chunks(lambda rows, acc: rope_heads(rows, acc, TN // HEAD_DIM, Q_SCALE))

    @pl.when((jj >= n_gate + 2) & (jj < n_gate + 4))
    def _():
        def epilogue(rows, acc):
            o_ref[rows] = (acc * Q_SCALE).astype(BF16)
        row_chunks(epilogue)

    @pl.when((jj >= n_gate + 4) & (jj < n_gate + 8))
    def _():
        def epilogue(rows, acc):
            o_ref[rows] = acc.astype(BF16)
        row_chunks(epilogue)

    @pl.when(jj == n_gate + 8)
    def _():
        def epilogue(rows, acc):
            rope_heads(rows, acc, N_KV_A, None)
            o_ref[rows, W_KA:] = acc[:, W_KA:].astype(BF16)
        row_chunks(epilogue)


def _inproj_call(x2, sc, sh, g, w, cos, sin, *, tm, rows_per_mod, j0, nj):
    n, d = x2.shape
    pos_tiles = cos.shape[0] // tm
    mod_tiles = rows_per_mod // tm
    return pl.pallas_call(
        functools.partial(_inproj_kernel, j0=j0),
        grid=(n // tm, nj),
        in_specs=[pl.BlockSpec((tm, d), lambda i, j: (i, 0)),
                  pl.BlockSpec((1, 1, d), lambda i, j: (i // mod_tiles, 0, 0)),
                  pl.BlockSpec((1, 1, d), lambda i, j: (i // mod_tiles, 0, 0)),
                  pl.BlockSpec((1, d), lambda i, j: (0, 0)),
                  pl.BlockSpec((None, d, TN), lambda i, j: (j + j0, 0, 0)),
                  pl.BlockSpec((tm, HEAD_DIM), lambda i, j: (i % pos_tiles, 0)),
                  pl.BlockSpec((tm, HEAD_DIM), lambda i, j: (i % pos_tiles, 0))],
        out_specs=pl.BlockSpec((tm, TN), lambda i, j: (i, j)),
        out_shape=jax.ShapeDtypeStruct((n, nj * TN), BF16),
        scratch_shapes=[pltpu.VMEM((tm, d), BF16)],
        compiler_params=pltpu.CompilerParams(dimension_semantics=("arbitrary", "arbitrary"),
                                             vmem_limit_bytes=VMEM_LIMIT_BYTES),
        name="in_proj",
    )(x2, sc, sh, g, w, cos, sin)


def _attn_a_kernel(sink_ref, q_ref, kp_ref, km_ref, kn_ref, vp_ref, vm_ref, vn_ref, kc_ref, vc_ref, o_ref,
                   *, n_blocks):
    nq = pl.program_id(1)
    sub = QB_A // BLOCK
    rows = GQA_GROUP * BLOCK
    n_loc = 3 * BLOCK
    r_io = lax.broadcasted_iota(jnp.int32, (rows, n_loc), 0)
    c_io = lax.broadcasted_iota(jnp.int32, (rows, n_loc), 1)
    qi_io = r_io & (BLOCK - 1)
    in_band = (c_io >= qi_io) & (c_io <= qi_io + 2 * WINDOW)
    c_row = lax.broadcasted_iota(jnp.int32, (1, n_loc), 1)
    g_io = lax.shift_right_logical(lax.broadcasted_iota(jnp.int32, (rows, 1), 0), BLOCK.bit_length() - 1)
    nt = (((1,), (1,)), ((), ()))
    for kvh in range(N_KV_A):
        hs = slice(kvh * HEAD_DIM, (kvh + 1) * HEAD_DIM)
        kband = jnp.concatenate([kp_ref[:, hs], km_ref[:, hs], kn_ref[:, hs]], axis=0)
        vband = jnp.concatenate([vp_ref[:, hs], vm_ref[:, hs], vn_ref[:, hs]], axis=0)
        kc = kc_ref[:, hs]
        vc = vc_ref[:, hs]
        sink = jnp.zeros((rows, 1), F32)
        for g in range(GQA_GROUP):
            sink = jnp.where(g_io == g, sink_ref[kvh * GQA_GROUP + g] * LOG2E, sink)
        for qi in range(sub):
            blk = nq * sub + qi
            lo = jnp.where(blk == 0, BLOCK, 0)
            hi = jnp.where(blk == n_blocks - 1, 2 * BLOCK - 1, n_loc - 1)
            valid = in_band & ((c_row >= lo) & (c_row <= hi))
            q4 = jnp.concatenate(
                [q_ref[qi * BLOCK:(qi + 1) * BLOCK, (kvh * GQA_GROUP + g) * HEAD_DIM:(kvh * GQA_GROUP + g + 1) * HEAD_DIM]
                 for g in range(GQA_GROUP)], axis=0)
            s_loc = lax.dot_general(q4, kband[qi * BLOCK:qi * BLOCK + n_loc], nt, preferred_element_type=F32)
            s_loc = jnp.where(valid, s_loc, NEG_INF)
            s_ctx = lax.dot_general(q4, kc, nt, preferred_element_type=F32)
            m = jnp.maximum(jnp.maximum(jnp.max(s_loc, axis=-1, keepdims=True),
                                        jnp.max(s_ctx, axis=-1, keepdims=True)), sink)
            e_loc = jnp.exp2(s_loc - m)
            e_ctx = jnp.exp2(s_ctx - m)
            den = (jnp.sum(e_loc, axis=-1, keepdims=True) + jnp.sum(e_ctx, axis=-1, keepdims=True)
                   + jnp.exp2(sink - m))
            o = (jnp.dot(e_loc.astype(BF16), vband[qi * BLOCK:qi * BLOCK + n_loc], preferred_element_type=F32)
                 + jnp.dot(e_ctx.astype(BF16), vc, preferred_element_type=F32)) * (1.0 / den)
            for g in range(GQA_GROUP):
                hd = kvh * GQA_GROUP + g
                o_ref[qi * BLOCK:(qi + 1) * BLOCK, hd * HEAD_DIM:(hd + 1) * HEAD_DIM] = (
                    o[g * BLOCK:(g + 1) * BLOCK].astype(BF16))


def _attn_a_call(proj, cproj, sink, *, batch, seq, ctx_len, q_col, k_col, v_col, ck_col, cv_col):
    nqb = seq // QB_A
    sub = QB_A // BLOCK
    n_blocks = seq // BLOCK

    def prev_map(b, n, col):
        return (b * n_blocks + jnp.maximum(n * sub - 1, 0), col // W_KA)

    def next_map(b, n, col):
        return (b * n_blocks + jnp.minimum(n * sub + sub, n_blocks - 1), col // W_KA)

    def main_map(b, n, col):
        return (b * nqb + n, col // W_KA)

    edge = lambda f, col: pl.BlockSpec((BLOCK, W_KA), functools.partial(f, col=col))
    main = lambda col: pl.BlockSpec((QB_A, W_KA), functools.partial(main_map, col=col))
    ctxs = lambda col: pl.BlockSpec((ctx_len, W_KA), lambda b, n: (b, col // W_KA))
    return pl.pallas_call(
        functools.partial(_attn_a_kernel, n_blocks=n_blocks),
        grid=(batch, nqb),
        in_specs=[pl.BlockSpec(memory_space=pltpu.SMEM),
                  pl.BlockSpec((QB_A, W_QA), lambda b, n: (b * nqb + n, q_col // W_QA)),
                  edge(prev_map, k_col), main(k_col), edge(next_map, k_col),
                  edge(prev_map, v_col), main(v_col), edge(next_map, v_col),
                  ctxs(ck_col), ctxs(cv_col)],
        out_specs=pl.BlockSpec((QB_A, W_QA), lambda b, n: (b * nqb + n, 0)),
        out_shape=jax.ShapeDtypeStruct((batch * seq, W_QA), BF16),
        compiler_params=pltpu.CompilerParams(dimension_semantics=("arbitrary",) * 2,
                                             vmem_limit_bytes=VMEM_LIMIT_BYTES),
        name="attn_a",
    )(sink, proj, proj, proj, proj, proj, proj, proj, cproj, cproj)


def _attn_b_kernel(q_ref, kp_ref, km_ref, kn_ref, vp_ref, vm_ref, vn_ref, kc_ref, vc_ref, bias_ref, o_ref,
                   *, grid_rows):
    n = pl.program_id(1)
    nq = ROW_GROUP_B * GRID_W
    nk = 3 * nq
    n_ctx = kc_ref.shape[0]
    heads = q_ref.shape[1] // HEAD_DIM
    r_io = lax.broadcasted_iota(jnp.int32, (nq, nk), 0)
    c_io = lax.broadcasted_iota(jnp.int32, (nq, nk), 1)
    log_w = GRID_W.bit_length() - 1
    r_abs = n * ROW_GROUP_B + lax.shift_right_logical(r_io, log_w)
    q_col = r_io & (GRID_W - 1)
    kr_abs = (n - 1) * ROW_GROUP_B + lax.shift_right_logical(c_io, log_w)
    k_col = c_io & (GRID_W - 1)
    r_start = jnp.clip(r_abs - NB_ROWS // 2, 0, grid_rows - NB_ROWS)
    c_start = jnp.clip(q_col - NB_COLS // 2, 0, GRID_W - NB_COLS)
    valid = ((kr_abs >= r_start) & (kr_abs < r_start + NB_ROWS)
             & (k_col >= c_start) & (k_col < c_start + NB_COLS))
    cap = jnp.where(valid, F32_MAX, NEG_INF)
    for hd in range(heads):
        sl = slice(hd * HEAD_DIM, (hd + 1) * HEAD_DIM)
        q = q_ref[:, sl]
        keys = jnp.concatenate([kp_ref[:, sl], km_ref[:, sl], kn_ref[:, sl]], axis=0)
        vals = jnp.concatenate([vp_ref[:, sl], vm_ref[:, sl], vn_ref[:, sl]], axis=0)
        s_nb = lax.dot_general(q, keys, (((1,), (1,)), ((), ())), preferred_element_type=F32)
        s_nb = jnp.minimum(s_nb + bias_ref[hd], cap)
        s_ctx = lax.dot_general(q, kc_ref[:, sl], (((1,), (1,)), ((), ())), preferred_element_type=F32)
        m = jnp.maximum(jnp.max(s_nb, axis=-1, keepdims=True), jnp.max(s_ctx, axis=-1, keepdims=True))
        e_nb = jnp.exp2(s_nb - m)
        e_ctx = jnp.exp2(s_ctx - m)
        den = jnp.sum(e_nb, axis=-1, keepdims=True) + jnp.sum(e_ctx, axis=-1, keepdims=True)
        o = (jnp.dot(e_nb.astype(BF16), vals, preferred_element_type=F32)
             + jnp.dot(e_ctx.astype(BF16), vc_ref[:, sl], preferred_element_type=F32)) * (1.0 / den)
        o_ref[:, sl] = o.astype(BF16)


def _attn_b_call(proj, cproj, bias, *, batch, seq, ctx_len, q_col, k_col, v_col, ck_col, cv_col):
    nq = ROW_GROUP_B * GRID_W
    ng = seq // nq
    hw = W_B
    hg = W_B // hw

    def row_spec(col, shift):
        def imap(b, n, h):
            return (b * ng + jnp.clip(n + shift, 0, ng - 1), col // hw + h)
        return pl.BlockSpec((nq, hw), imap)

    ctxs = lambda col: pl.BlockSpec((ctx_len, hw), lambda b, n, h: (b, col // hw + h))
    return pl.pallas_call(
        functools.partial(_attn_b_kernel, grid_rows=seq // GRID_W),
        grid=(batch, ng, hg),
        in_specs=[row_spec(q_col, 0),
                  row_spec(k_col, -1), row_spec(k_col, 0), row_spec(k_col, 1),
                  row_spec(v_col, -1), row_spec(v_col, 0), row_spec(v_col, 1),
                  ctxs(ck_col), ctxs(cv_col),
                  pl.BlockSpec((hw // HEAD_DIM, nq, bias.shape[2]), lambda b, n, h: (h, 0, 0))],
        out_specs=pl.BlockSpec((nq, hw), lambda b, n, h: (b * ng + n, h)),
        out_shape=jax.ShapeDtypeStruct((batch * seq, W_B), BF16),
        compiler_params=pltpu.CompilerParams(dimension_semantics=("arbitrary",) * 3,
                                             vmem_limit_bytes=VMEM_LIMIT_BYTES),
        name="attn_b",
    )(proj, proj, proj, proj, proj, proj, proj, cproj, cproj, bias)


def _mix_kernel(oa_ref, ob_ref, ga_ref, gb_ref, wa_ref, wb_ref, z_ref):
    ya = jnp.dot(oa_ref[...], wa_ref[...], preferred_element_type=F32)
    yb = jnp.dot(ob_ref[...], wb_ref[...], preferred_element_type=F32)
    z_ref[...] = (ga_ref[...].astype(F32) * ya + gb_ref[...].astype(F32) * yb).astype(BF16)


def _mix_call(oa, ob, proj, wa, wb, *, tm):
    n = oa.shape[0]
    d = wa.shape[1]
    const = lambda shape: pl.BlockSpec(shape, lambda i: (0, 0), pipeline_mode=pl.Buffered(1))
    return pl.pallas_call(
        _mix_kernel,
        grid=(n // tm,),
        in_specs=[pl.BlockSpec((tm, oa.shape[1]), lambda i: (i, 0)),
                  pl.BlockSpec((tm, ob.shape[1]), lambda i: (i, 0)),
                  pl.BlockSpec((tm, d), lambda i: (i, 0)),
                  pl.BlockSpec((tm, d), lambda i: (i, 1)),
                  const(wa.shape), const(wb.shape)],
        out_specs=pl.BlockSpec((tm, d), lambda i: (i, 0)),
        out_shape=jax.ShapeDtypeStruct((n, d), BF16),
        compiler_params=pltpu.CompilerParams(dimension_semantics=("arbitrary",),
                                             vmem_limit_bytes=VMEM_LIMIT_BYTES),
        name="branch_mix",
    )(oa, ob, proj, proj, wa, wb)


def _outnorm_kernel(a_ref, w_ref, x_ref, gt_ref, g_ref, o_ref, acc_ref):
    k = pl.program_id(1)
    part = jnp.dot(a_ref[...], w_ref[...], preferred_element_type=F32)

    @pl.when(k == 0)
    def _():
        acc_ref[...] = part

    @pl.when(k > 0)
    def _():
        acc_ref[...] += part

    @pl.when(k == pl.num_programs(1) - 1)
    def _():
        o_ref[...] = x_ref[...] + gt_ref[0] * _rms(acc_ref[...], g_ref[...])


def _outnorm_call(a, w, x2, gt, g, *, tm, tk, rows_per_mod):
    n, kdim = a.shape
    d = w.shape[1]
    mod_tiles = rows_per_mod // tm
    return pl.pallas_call(
        _outnorm_kernel,
        grid=(n // tm, kdim // tk),
        in_specs=[pl.BlockSpec((tm, tk), lambda i, k: (i, k)),
                  pl.BlockSpec((tk, d), lambda i, k: (k, 0)),
                  pl.BlockSpec((tm, d), lambda i, k: (i, 0)),
                  pl.BlockSpec((1, 1, d), lambda i, k: (i // mod_tiles, 0, 0)),
                  pl.BlockSpec((1, d), lambda i, k: (0, 0))],
        out_specs=pl.BlockSpec((tm, d), lambda i, k: (i, 0)),
        out_shape=jax.ShapeDtypeStruct((n, d), F32),
        scratch_shapes=[pltpu.VMEM((tm, d), F32)],
        compiler_params=pltpu.CompilerParams(dimension_semantics=("arbitrary", "arbitrary"),
                                             vmem_limit_bytes=VMEM_LIMIT_BYTES),
        name="out_norm",
    )(a, w, x2, gt, g)


def _rowmm_norm_kernel(a_ref, w_ref, x_ref, gt_ref, g_ref, o_ref, *, n_chunks):
    cm = a_ref.shape[0] // n_chunks
    y = jnp.dot(a_ref[0:cm], w_ref[...], preferred_element_type=F32)
    for c in range(n_chunks):
        nxt = None
        if c + 1 < n_chunks:
            nxt = jnp.dot(a_ref[(c + 1) * cm:(c + 2) * cm], w_ref[...], preferred_element_type=F32)
        rows = slice(c * cm, (c + 1) * cm)
        o_ref[rows] = x_ref[rows] + gt_ref[0] * _rms(y, g_ref[...])
        y = nxt


def _rowmm_norm_call(a, w, x2, gt, g, *, tm, n_chunks, rows_per_mod):
    n, kdim = a.shape
    d = w.shape[1]
    mod_tiles = rows_per_mod // tm
    return pl.pallas_call(
        functools.partial(_rowmm_norm_kernel, n_chunks=n_chunks),
        grid=(n // tm,),
        in_specs=[pl.BlockSpec((tm, kdim), lambda i: (i, 0)),
                  pl.BlockSpec((kdim, d), lambda i: (0, 0), pipeline_mode=pl.Buffered(1)),
                  pl.BlockSpec((tm, d), lambda i: (i, 0)),
                  pl.BlockSpec((1, 1, d), lambda i: (i // mod_tiles, 0, 0)),
                  pl.BlockSpec((1, d), lambda i: (0, 0))],
        out_specs=pl.BlockSpec((tm, d), lambda i: (i, 0)),
        out_shape=jax.ShapeDtypeStruct((n, d), F32),
        compiler_params=pltpu.CompilerParams(dimension_semantics=("arbitrary",),
                                             vmem_limit_bytes=VMEM_LIMIT_RESIDENT_BYTES),
        name="rowmm_norm",
    )(a, w, x2, gt, g)


def _ffn_up_kernel(xp_ref, x_ref, xn_ref, sc_ref, sh_ref, g_ref, wa_ref, wg_ref, cwa_ref, cwg_ref, cba_ref, cbg_ref,
                   o_ref, h_ref, ua_ref, ug_ref, *, nj, n_tiles, tiles_per_seq):
    t = pl.program_id(0)
    tm = x_ref.shape[0]
    halo = xp_ref.shape[0]
    ext = tm + 2 * halo
    margin = 8

    @pl.when(t == 0)
    def _():
        ua_ref[...] = jnp.zeros_like(ua_ref)
        ug_ref[...] = jnp.zeros_like(ug_ref)

    @pl.when((t < n_tiles) & (t % nj == 0))
    def _():
        def mod(x):
            return (_rms(x, g_ref[...]) * (1 + sc_ref[0]) + sh_ref[0]).astype(BF16)
        h_ref[0:halo] = mod(xp_ref[...])
        h_ref[halo:halo + tm] = mod(x_ref[...])
        h_ref[halo + tm:] = mod(xn_ref[...])

    slot = t % 2
    prev = 1 - slot
    ip = jnp.maximum(t - 1, 0) // nj
    first_row = jnp.where((ip % tiles_per_seq) == 0, 0, -1)
    last_row = jnp.where((ip % tiles_per_seq) == tiles_per_seq - 1, tm - 1, -1)
    row = lax.broadcasted_iota(jnp.int32, (tm, 1), 0)

    def epilogue(lo, hi):
        cm = hi - lo
        rowc = row[lo:hi]

        def conv(u_ref, cw_ref, cb_ref):
            win = u_ref[prev, pl.ds(halo + lo - margin, cm + 2 * margin), :]
            up = pltpu.roll(win, 1, 0)[margin:margin + cm]
            un = pltpu.roll(win, cm + 2 * margin - 1, 0)[margin:margin + cm]
            up = jnp.where(rowc == first_row, 0.0, up)
            un = jnp.where(rowc == last_row, 0.0, un)
            out = cb_ref[...] + up * cw_ref[0:1]
            out = out + win[margin:margin + cm] * cw_ref[1:2]
            return out + un * cw_ref[2:3]

        a = conv(ua_ref, cwa_ref, cba_ref)
        gte = conv(ug_ref, cwg_ref, cbg_ref)
        o_ref[lo:hi] = (jax.nn.silu(gte) * a).astype(BF16)

    half = ext // 2
    third = (tm // 3) // BF16_ROWS * BF16_ROWS
    chunks = [(0, third), (third, 2 * third), (2 * third, tm)]
    ua_ref[slot, 0:half] = jnp.dot(h_ref[0:half], wa_ref[...], preferred_element_type=F32)
    epilogue(*chunks[0])
    ua_ref[slot, half:ext] = jnp.dot(h_ref[half:ext], wa_ref[...], preferred_element_type=F32)
    epilogue(*chunks[1])
    ug_ref[slot, 0:half] = jnp.dot(h_ref[0:half], wg_ref[...], preferred_element_type=F32)
    epilogue(*chunks[2])
    ug_ref[slot, half:ext] = jnp.dot(h_ref[half:ext], wg_ref[...], preferred_element_type=F32)


def _ffn_up_call(x2, sc, sh, g, w_up, conv_w, conv_b, *, tm, seq):
    tn = w_up.shape[2]
    n, d = x2.shape
    nj = w_up.shape[0] // 2
    dff = nj * tn
    halo = BF16_ROWS
    hb = tm // halo
    n_halo = n // halo
    tiles_per_seq = seq // tm
    n_tiles = (n // tm) * nj
    assert (tm + 2 * halo) % (2 * BF16_ROWS) == 0

    def cur(t):
        tc = jnp.minimum(t, n_tiles - 1)
        return tc // nj, tc % nj

    def prv(t):
        tp = jnp.maximum(t - 1, 0)
        return tp // nj, tp % nj

    return pl.pallas_call(
        functools.partial(_ffn_up_kernel, nj=nj, n_tiles=n_tiles, tiles_per_seq=tiles_per_seq),
        grid=(n_tiles + 1,),
        in_specs=[pl.BlockSpec((halo, d), lambda t: (jnp.maximum(cur(t)[0] * hb - 1, 0), 0)),
                  pl.BlockSpec((tm, d), lambda t: (cur(t)[0], 0)),
                  pl.BlockSpec((halo, d), lambda t: (jnp.minimum((cur(t)[0] + 1) * hb, n_halo - 1), 0)),
                  pl.BlockSpec((1, 1, d), lambda t: (cur(t)[0] // tiles_per_seq, 0, 0)),
                  pl.BlockSpec((1, 1, d), lambda t: (cur(t)[0] // tiles_per_seq, 0, 0)),
                  pl.BlockSpec((1, d), lambda t: (0, 0)),
                  pl.BlockSpec((None, d, tn), lambda t: (cur(t)[1], 0, 0)),
                  pl.BlockSpec((None, d, tn), lambda t: (cur(t)[1] + nj, 0, 0)),
                  pl.BlockSpec((CONV_W, tn), lambda t: (0, prv(t)[1])),
                  pl.BlockSpec((CONV_W, tn), lambda t: (0, prv(t)[1] + nj)),
                  pl.BlockSpec((1, tn), lambda t: (0, prv(t)[1])),
                  pl.BlockSpec((1, tn), lambda t: (0, prv(t)[1] + nj))],
        out_specs=pl.BlockSpec((tm, tn), lambda t: prv(t)),
        out_shape=jax.ShapeDtypeStruct((n, dff), BF16),
        scratch_shapes=[pltpu.VMEM((tm + 2 * halo, d), BF16),
                        pltpu.VMEM((2, tm + 2 * halo, tn), F32),
                        pltpu.VMEM((2, tm + 2 * halo, tn), F32)],
        compiler_params=pltpu.CompilerParams(dimension_semantics=("arbitrary",),
                                             vmem_limit_bytes=VMEM_LIMIT_BYTES),
        name="ffn_up",
    )(x2, x2, x2, sc, sh, g, w_up, w_up, conv_w, conv_w, conv_b, conv_b)


def _rope_tables(seq):
    t = jnp.arange(seq)
    half = HEAD_DIM // 4
    inv = ROPE_BASE ** (-jnp.arange(half, dtype=F32) / half)

    def cs(pos):
        ang = pos.astype(F32)[:, None] * inv[None, :]
        c, s = jnp.cos(ang), jnp.sin(ang)
        return jnp.concatenate([c, c], axis=1), jnp.concatenate([-s, s], axis=1)

    cr, sr = cs(t // GRID_W)
    cc, sn = cs(t % GRID_W)
    return jnp.concatenate([cr, cc], axis=1), jnp.concatenate([sr, sn], axis=1)


def _col_tiles(w, tn):
    k, n = w.shape
    return w.reshape(k, n // tn, tn).transpose(1, 0, 2)


def _pick_tile(n, pref):
    while n % pref:
        pref //= 2
    return pref


def kernel(x, c, ctx, c_ctx, w_mod, b_mod, g_attn_pre, g_attn_post, g_ffn_pre, g_ffn_post, w_in, sink_a, rpb_b,
           w_br_a, w_br_b, w_o, w_up, conv_w, conv_b, w_down):
    batch, seq, d = x.shape
    ctx_len = ctx.shape[1]
    assert w_mod.shape[0] == 1 and d == 16 * HEAD_DIM
    assert seq % QB_A == 0 and seq % (ROW_GROUP_B * GRID_W) == 0 and WINDOW == BLOCK
    n = batch * seq
    x2 = x.reshape(n, d)

    pad = (-(batch + 1)) % 8
    cs = jnp.concatenate([c, c_ctx[None, :], jnp.zeros((pad, d), F32)], axis=0)
    mod = _mod_call(cs, w_mod[0], b_mod[0][None, :])
    sh1, sc1, gt1, sh2, sc2, gt2 = [mod[:batch, k * d:(k + 1) * d].reshape(batch, 1, d) for k in range(6)]
    csh1 = mod[batch:batch + 1, 0:d].reshape(1, 1, d)
    csc1 = mod[batch:batch + 1, d:2 * d].reshape(1, 1, d)

    o_qa, o_ka, o_va, o_qb, o_kb, o_vb, o_ga = 0, W_QA, W_QA + W_KA, W_QA + 2 * W_KA, W_QA + 2 * W_KA + W_B, \
        W_QA + 2 * W_KA + 2 * W_B, W_QA + 2 * W_KA + 3 * W_B
    wi = w_in[0]
    w_perm = _col_tiles(
        jnp.concatenate([wi[:, o_ga:], wi[:, o_qa:o_ka], wi[:, o_qb:o_ga], wi[:, o_ka:o_qb]], axis=1).astype(BF16), TN)
    c_qa = 2 * d
    c_qb = c_qa + W_QA
    c_kb = c_qb + W_B
    c_vb = c_kb + W_B
    c_ka = c_vb + W_B
    c_va = c_ka + W_KA
    n_tiles = w_perm.shape[0]
    cos, sin = _rope_tables(seq)
    tm = _pick_tile(seq, 1024)
    proj = _inproj_call(x2, sc1, sh1, g_attn_pre, w_perm, cos, sin, tm=tm, rows_per_mod=seq, j0=0, nj=n_tiles)
    nc = batch * ctx_len
    j0c = c_kb // TN
    cproj = _inproj_call(ctx.reshape(nc, d), csc1, csh1, g_attn_pre, w_perm,
                         jnp.ones((nc, HEAD_DIM), F32), jnp.zeros((nc, HEAD_DIM), F32),
                         tm=nc, rows_per_mod=nc, j0=j0c, nj=n_tiles - j0c)
    cc_kb, cc_vb, cc_ka, cc_va = 0, W_B, 2 * W_B, 2 * W_B + W_KA

    oa = _attn_a_call(proj, cproj, sink_a[0], batch=batch, seq=seq, ctx_len=ctx_len,
                      q_col=c_qa, k_col=c_ka, v_col=c_va, ck_col=cc_ka, cv_col=cc_va)
    bias = _rpb_call(rpb_b[0])
    ob = _attn_b_call(proj, cproj, bias, batch=batch, seq=seq, ctx_len=ctx_len,
                      q_col=c_qb, k_col=c_kb, v_col=c_vb, ck_col=cc_kb, cv_col=cc_vb)

    tm2 = _pick_tile(seq, 512)
    z = _mix_call(oa, ob, proj, w_br_a[0].astype(BF16), w_br_b[0].astype(BF16), tm=tm2)
    x1 = _outnorm_call(z, w_o[0].astype(BF16), x2, gt1, g_attn_post, tm=tm2, tk=d, rows_per_mod=seq)

    dff = w_down.shape[1]
    act = _ffn_up_call(x1, sc2, sh2, g_ffn_pre, _col_tiles(w_up[0].astype(BF16), TN), conv_w[0], conv_b[0][None, :],
                       tm=tm, seq=seq)
    out = _rowmm_norm_call(act, w_down[0].astype(BF16), x1, gt2, g_ffn_post, tm=_pick_tile(seq, 256), n_chunks=2,
                           rows_per_mod=seq)
    return out.reshape(batch, seq, d)
```

```python
import functools

import jax
import jax.numpy as jnp
from jax import lax
from jax.experimental import pallas as pl
from jax.experimental.pallas import tpu as pltpu

F32 = jnp.float32
BF16 = jnp.bfloat16

GRID_W = 64
HEAD_DIM = 128
N_HEADS_A = 8
N_KV_A = 2
GQA_GROUP = N_HEADS_A // N_KV_A
WINDOW = 128
BLOCK = 128
N_HEADS_B = 8
NB_ROWS = 8
NB_COLS = 16
CONV_W = 3
ROPE_BASE = 10000.0
EPS = 1e-6
NEG_INF = -1e30
ATTN_SCALE = HEAD_DIM ** -0.5
LOG2E = 1.4426950408889634
Q_SCALE = ATTN_SCALE * LOG2E
F32_MAX = 3.4028234663852886e38

W_QA = N_HEADS_A * HEAD_DIM
W_KA = N_KV_A * HEAD_DIM
W_B = N_HEADS_B * HEAD_DIM

LANES = 128
BF16_ROWS = 16
VMEM_LIMIT_BYTES = 56 * 1024 * 1024
VMEM_LIMIT_RESIDENT_BYTES = 60 * 1024 * 1024

TN = 512
ROW_CHUNKS = 4
ROW_GROUP_B = 4
QB_A = 512


def _rms(x, g):
    return (x * lax.rsqrt(jnp.mean(x * x, axis=-1, keepdims=True) + EPS)) * g


def _mod_kernel(c_ref, w_ref, b_ref, o_ref):
    a = jax.nn.silu(c_ref[...]).astype(BF16)
    o_ref[...] = jnp.dot(a, w_ref[...].astype(BF16), preferred_element_type=F32) + b_ref[...]


def _mod_call(cs, w, b):
    m, d = cs.shape
    n = w.shape[1]
    tn = 1536 if n % 1536 == 0 else n
    return pl.pallas_call(
        _mod_kernel,
        grid=(n // tn,),
        in_specs=[pl.BlockSpec((m, d), lambda j: (0, 0)),
                  pl.BlockSpec((d, tn), lambda j: (0, j)),
                  pl.BlockSpec((1, tn), lambda j: (0, j))],
        out_specs=pl.BlockSpec((m, tn), lambda j: (0, j)),
        out_shape=jax.ShapeDtypeStruct((m, n), F32),
        compiler_params=pltpu.CompilerParams(dimension_semantics=("arbitrary",),
                                             vmem_limit_bytes=VMEM_LIMIT_BYTES),
        name="mod",
    )(cs, w, b)


def _rpb_kernel(rpb_ref, o_ref):
    h = pl.program_id(0)
    n_dr = 2 * NB_ROWS - 1
    n_dc = 2 * NB_COLS - 1
    qc = lax.broadcasted_iota(jnp.int32, (GRID_W, LANES), 0)
    lane = lax.broadcasted_iota(jnp.int32, (GRID_W, LANES), 1)
    kc = lane & (GRID_W - 1)
    dc_idx = jnp.clip(kc - qc, -(NB_COLS - 1), NB_COLS - 1) + (NB_COLS - 1)
    base = h * (n_dr * n_dc)

    def toeplitz(dr):
        def body(dc, acc):
            return jnp.where(dc_idx == dc, rpb_ref[base + dr * n_dc + dc] * LOG2E, acc)
        return lax.fori_loop(0, n_dc, body, jnp.zeros((GRID_W, LANES), F32))

    tw = [toeplitz(dr) for dr in range(n_dr)]
    pair = [jnp.where(lane < GRID_W, tw[d], tw[d + 1]) for d in range(n_dr - 1)]
    n_kr = 3 * ROW_GROUP_B
    dr0 = NB_ROWS - 1 - ROW_GROUP_B
    for rq in range(ROW_GROUP_B):
        for p in range(n_kr // 2):
            o_ref[0, rq * GRID_W:(rq + 1) * GRID_W, p * LANES:(p + 1) * LANES] = pair[2 * p - rq + dr0]


def _rpb_call(rpb):
    nh = rpb.shape[0]
    nq = ROW_GROUP_B * GRID_W
    nk = 3 * ROW_GROUP_B * GRID_W
    assert 2 * ROW_GROUP_B == NB_ROWS
    return pl.pallas_call(
        _rpb_kernel,
        grid=(nh,),
        in_specs=[pl.BlockSpec(memory_space=pltpu.SMEM)],
        out_specs=pl.BlockSpec((1, nq, nk), lambda h: (h, 0, 0)),
        out_shape=jax.ShapeDtypeStruct((nh, nq, nk), F32),
        compiler_params=pltpu.CompilerParams(dimension_semantics=("arbitrary",)),
        name="rpb_table",
    )(rpb.reshape(-1))


def _rope(t, cos, sin, first_half):
    partner = jnp.where(first_half, pltpu.roll(t, LANES - 32, 1), pltpu.roll(t, 32, 1))
    return t * cos + partner * sin


def _inproj_kernel(x_ref, sc_ref, sh_ref, g_ref, w_ref, cos_ref, sin_ref, o_ref, h_ref, *, j0):
    j = pl.program_id(1)

    @pl.when(j == 0)
    def _():
        h = _rms(x_ref[...], g_ref[...]) * (1 + sc_ref[0]) + sh_ref[0]
        h_ref[...] = h.astype(BF16)

    jj = j + j0
    n_gate = 2 * x_ref.shape[1] // TN
    lane = lax.broadcasted_iota(jnp.int32, (1, HEAD_DIM), 1)
    first_half = (lane & 63) < 32
    cm = x_ref.shape[0] // ROW_CHUNKS

    def row_chunks(epilogue):
        acc = jnp.dot(h_ref[0:cm], w_ref[...], preferred_element_type=F32)
        for c in range(ROW_CHUNKS):
            nxt = None
            if c + 1 < ROW_CHUNKS:
                nxt = jnp.dot(h_ref[(c + 1) * cm:(c + 2) * cm], w_ref[...], preferred_element_type=F32)
            epilogue(slice(c * cm, (c + 1) * cm), acc)
            acc = nxt

    def rope_heads(rows, acc, n_heads, scale):
        cos = cos_ref[rows]
        sin = sin_ref[rows]
        for hd in range(n_heads):
            sl = slice(hd * HEAD_DIM, (hd + 1) * HEAD_DIM)
            r = _rope(acc[:, sl], cos, sin, first_half)
            if scale is not None:
                r = r * scale
            o_ref[rows, sl] = r.astype(BF16)

    @pl.when(jj < n_gate)
    def _():
        def epilogue(rows, acc):
            o_ref[rows] = jax.nn.sigmoid(acc).astype(BF16)
        row_chunks(epilogue)

    @pl.when((jj >= n_gate) & (jj < n_gate + 2))
    def _():
        row_chunks(lambda rows, acc: rope_heads(rows, acc, TN // HEAD_DIM, Q_SCALE))

    @pl.when((jj >= n_gate + 2) & (jj < n_gate + 4))
    def _():
        def epilogue(rows, acc):
            o_ref[rows] = (acc * Q_SCALE).astype(BF16)
        row_chunks(epilogue)

    @pl.when((jj >= n_gate + 4) & (jj < n_gate + 8))
    def _():
        def epilogue(rows, acc):
            o_ref[rows] = acc.astype(BF16)
        row_chunks(epilogue)

    @pl.when(jj == n_gate + 8)
    def _():
        def epilogue(rows, acc):
            rope_heads(rows, acc, N_KV_A, None)
            o_ref[rows, W_KA:] = acc[:, W_KA:].astype(BF16)
        row_chunks(epilogue)


def _inproj_call(x2, sc, sh, g, w, cos, sin, *, tm, rows_per_mod, j0, nj):
    n, d = x2.shape
    pos_tiles = cos.shape[0] // tm
    mod_tiles = rows_per_mod // tm
    return pl.pallas_call(
        functools.partial(_inproj_kernel, j0=j0),
        grid=(n // tm, nj),
        in_specs=[pl.BlockSpec((tm, d), lambda i, j: (i, 0)),
                  pl.BlockSpec((1, 1, d), lambda i, j: (i // mod_tiles, 0, 0)),
                  pl.BlockSpec((1, 1, d), lambda i, j: (i // mod_tiles, 0, 0)),
                  pl.BlockSpec((1, d), lambda i, j: (0, 0)),
                  pl.BlockSpec((d, TN), lambda i, j: (0, j + j0)),
                  pl.BlockSpec((tm, HEAD_DIM), lambda i, j: (i % pos_tiles, 0)),
                  pl.BlockSpec((tm, HEAD_DIM), lambda i, j: (i % pos_tiles, 0))],
        out_specs=pl.BlockSpec((None, tm, TN), lambda i, j: (j, i, 0)),
        out_shape=jax.ShapeDtypeStruct((nj, n, TN), BF16),
        scratch_shapes=[pltpu.VMEM((tm, d), BF16)],
        compiler_params=pltpu.CompilerParams(dimension_semantics=("arbitrary", "arbitrary"),
                                             vmem_limit_bytes=VMEM_LIMIT_BYTES),
        name="in_proj",
    )(x2, sc, sh, g, w, cos, sin)


def _attn_a_kernel(sink_ref, q_ref, kp_ref, km_ref, kn_ref, vp_ref, vm_ref, vn_ref, kc_ref, vc_ref, o_ref,
                   *, n_blocks):
    nq = pl.program_id(1)
    sub = QB_A // BLOCK
    rows = GQA_GROUP * BLOCK
    n_loc = 3 * BLOCK
    r_io = lax.broadcasted_iota(jnp.int32, (rows, n_loc), 0)
    c_io = lax.broadcasted_iota(jnp.int32, (rows, n_loc), 1)
    qi_io = r_io & (BLOCK - 1)
    in_band = (c_io >= qi_io) & (c_io <= qi_io + 2 * WINDOW)
    c_row = lax.broadcasted_iota(jnp.int32, (1, n_loc), 1)
    g_io = lax.shift_right_logical(lax.broadcasted_iota(jnp.int32, (rows, 1), 0), BLOCK.bit_length() - 1)
    nt = (((1,), (1,)), ((), ()))
    for kvh in range(N_KV_A):
        hs = slice(kvh * HEAD_DIM, (kvh + 1) * HEAD_DIM)
        kband = jnp.concatenate([kp_ref[:, hs], km_ref[:, hs], kn_ref[:, hs]], axis=0)
        vband = jnp.concatenate([vp_ref[:, hs], vm_ref[:, hs], vn_ref[:, hs]], axis=0)
        kc = kc_ref[:, hs]
        vc = vc_ref[:, hs]
        sink = jnp.zeros((rows, 1), F32)
        for g in range(GQA_GROUP):
            sink = jnp.where(g_io == g, sink_ref[kvh * GQA_GROUP + g] * LOG2E, sink)
        for qi in range(sub):
            blk = nq * sub + qi
            lo = jnp.where(blk == 0, BLOCK, 0)
            hi = jnp.where(blk == n_blocks - 1, 2 * BLOCK - 1, n_loc - 1)
            valid = in_band & ((c_row >= lo) & (c_row <= hi))
            q4 = jnp.concatenate(
                [q_ref[qi * BLOCK:(qi + 1) * BLOCK, (kvh * GQA_GROUP + g) * HEAD_DIM:(kvh * GQA_GROUP + g + 1) * HEAD_DIM]
                 for g in range(GQA_GROUP)], axis=0)
            s_loc = lax.dot_general(q4, kband[qi * BLOCK:qi * BLOCK + n_loc], nt, preferred_element_type=F32)
            s_loc = jnp.where(valid, s_loc, NEG_INF)
            s_ctx = lax.dot_general(q4, kc, nt, preferred_element_type=F32)
            m = jnp.maximum(jnp.maximum(jnp.max(s_loc, axis=-1, keepdims=True),
                                        jnp.max(s_ctx, axis=-1, keepdims=True)), sink)
            e_loc = jnp.exp2(s_loc - m)
            e_ctx = jnp.exp2(s_ctx - m)
            den = (jnp.sum(e_loc, axis=-1, keepdims=True) + jnp.sum(e_ctx, axis=-1, keepdims=True)
                   + jnp.exp2(sink - m))
            o = (jnp.dot(e_loc.astype(BF16), vband[qi * BLOCK:qi * BLOCK + n_loc], preferred_element_type=F32)
                 + jnp.dot(e_ctx.astype(BF16), vc, preferred_element_type=F32)) * (1.0 / den)
            for g in range(GQA_GROUP):
                hd = kvh * GQA_GROUP + g
                o_ref[qi * BLOCK:(qi + 1) * BLOCK, hd * HEAD_DIM:(hd + 1) * HEAD_DIM] = (
                    o[g * BLOCK:(g + 1) * BLOCK].astype(BF16))


def _attn_a_call(proj, cproj, sink, *, batch, seq, ctx_len, q_col, k_col, v_col, ck_col, cv_col):
    nqb = seq // QB_A
    sub = QB_A // BLOCK
    n_blocks = seq // BLOCK

    def prev_map(b, n, col):
        return (b * n_blocks + jnp.maximum(n * sub - 1, 0), col // W_KA)

    def next_map(b, n, col):
        return (b * n_blocks + jnp.minimum(n * sub + sub, n_blocks - 1), col // W_KA)

    def main_map(b, n, col):
        return (b * nqb + n, col // W_KA)

    edge = lambda f, col: pl.BlockSpec((BLOCK, W_KA), functools.partial(f, col=col))
    main = lambda col: pl.BlockSpec((QB_A, W_KA), functools.partial(main_map, col=col))
    ctxs = lambda col: pl.BlockSpec((ctx_len, W_KA), lambda b, n: (b, col // W_KA))
    return pl.pallas_call(
        functools.partial(_attn_a_kernel, n_blocks=n_blocks),
        grid=(batch, nqb),
        in_specs=[pl.BlockSpec(memory_space=pltpu.SMEM),
                  pl.BlockSpec((QB_A, W_QA), lambda b, n: (b * nqb + n, q_col // W_QA)),
                  edge(prev_map, k_col), main(k_col), edge(next_map, k_col),
                  edge(prev_map, v_col), main(v_col), edge(next_map, v_col),
                  ctxs(ck_col), ctxs(cv_col)],
        out_specs=pl.BlockSpec((QB_A, W_QA), lambda b, n: (b * nqb + n, 0)),
        out_shape=jax.ShapeDtypeStruct((batch * seq, W_QA), BF16),
        compiler_params=pltpu.CompilerParams(dimension_semantics=("arbitrary",) * 2,
                                             vmem_limit_bytes=VMEM_LIMIT_BYTES),
        name="attn_a",
    )(sink, proj, proj, proj, proj, proj, proj, proj, cproj, cproj)


def _attn_b_kernel(q_ref, kp_ref, km_ref, kn_ref, vp_ref, vm_ref, vn_ref, kc_ref, vc_ref, bias_ref, o_ref,
                   *, grid_rows):
    n = pl.program_id(1)
    nq = ROW_GROUP_B * GRID_W
    nk = 3 * nq
    n_ctx = kc_ref.shape[0]
    heads = q_ref.shape[1] // HEAD_DIM
    r_io = lax.broadcasted_iota(jnp.int32, (nq, nk), 0)
    c_io = lax.broadcasted_iota(jnp.int32, (nq, nk), 1)
    log_w = GRID_W.bit_length() - 1
    r_abs = n * ROW_GROUP_B + lax.shift_right_logical(r_io, log_w)
    q_col = r_io & (GRID_W - 1)
    kr_abs = (n - 1) * ROW_GROUP_B + lax.shift_right_logical(c_io, log_w)
    k_col = c_io & (GRID_W - 1)
    r_start = jnp.clip(r_abs - NB_ROWS // 2, 0, grid_rows - NB_ROWS)
    c_start = jnp.clip(q_col - NB_COLS // 2, 0, GRID_W - NB_COLS)
    valid = ((kr_abs >= r_start) & (kr_abs < r_start + NB_ROWS)
             & (k_col >= c_start) & (k_col < c_start + NB_COLS))
    cap = jnp.where(valid, F32_MAX, NEG_INF)
    for hd in range(heads):
        sl = slice(hd * HEAD_DIM, (hd + 1) * HEAD_DIM)
        q = q_ref[:, sl]
        keys = jnp.concatenate([kp_ref[:, sl], km_ref[:, sl], kn_ref[:, sl]], axis=0)
        vals = jnp.concatenate([vp_ref[:, sl], vm_ref[:, sl], vn_ref[:, sl]], axis=0)
        s_nb = lax.dot_general(q, keys, (((1,), (1,)), ((), ())), preferred_element_type=F32)
        s_nb = jnp.minimum(s_nb + bias_ref[hd], cap)
        s_ctx = lax.dot_general(q, kc_ref[:, sl], (((1,), (1,)), ((), ())), preferred_element_type=F32)
        m = jnp.maximum(jnp.max(s_nb, axis=-1, keepdims=True), jnp.max(s_ctx, axis=-1, keepdims=True))
        e_nb = jnp.exp2(s_nb - m)
        e_ctx = jnp.exp2(s_ctx - m)
        den = jnp.sum(e_nb, axis=-1, keepdims=True) + jnp.sum(e_ctx, axis=-1, keepdims=True)
        o = (jnp.dot(e_nb.astype(BF16), vals, preferred_element_type=F32)
             + jnp.dot(e_ctx.astype(BF16), vc_ref[:, sl], preferred_element_type=F32)) * (1.0 / den)
        o_ref[:, sl] = o.astype(BF16)


def _attn_b_call(proj, cproj, bias, *, batch, seq, ctx_len, q_col, k_col, v_col, ck_col, cv_col):
    nq = ROW_GROUP_B * GRID_W
    ng = seq // nq
    hw = W_B
    hg = W_B // hw

    def row_spec(col, shift):
        def imap(b, n, h):
            return (b * ng + jnp.clip(n + shift, 0, ng - 1), col // hw + h)
        return pl.BlockSpec((nq, hw), imap)

    ctxs = lambda col: pl.BlockSpec((ctx_len, hw), lambda b, n, h: (b, col // hw + h))
    return pl.pallas_call(
        functools.partial(_attn_b_kernel, grid_rows=seq // GRID_W),
        grid=(batch, ng, hg),
        in_specs=[row_spec(q_col, 0),
                  row_spec(k_col, -1), row_spec(k_col, 0), row_spec(k_col, 1),
                  row_spec(v_col, -1), row_spec(v_col, 0), row_spec(v_col, 1),
                  ctxs(ck_col), ctxs(cv_col),
                  pl.BlockSpec((hw // HEAD_DIM, nq, bias.shape[2]), lambda b, n, h: (h, 0, 0))],
        out_specs=pl.BlockSpec((nq, hw), lambda b, n, h: (b * ng + n, h)),
        out_shape=jax.ShapeDtypeStruct((batch * seq, W_B), BF16),
        compiler_params=pltpu.CompilerParams(dimension_semantics=("arbitrary",) * 3,
                                             vmem_limit_bytes=VMEM_LIMIT_BYTES),
        name="attn_b",
    )(proj, proj, proj, proj, proj, proj, proj, cproj, cproj, bias)


def _mix_kernel(oa_ref, ob_ref, ga_ref, gb_ref, wa_ref, wb_ref, z_ref):
    ya = jnp.dot(oa_ref[...], wa_ref[...], preferred_element_type=F32)
    yb = jnp.dot(ob_ref[...], wb_ref[...], preferred_element_type=F32)
    z_ref[...] = (ga_ref[...].astype(F32) * ya + gb_ref[...].astype(F32) * yb).astype(BF16)


def _mix_call(oa, ob, proj, wa, wb, *, tm):
    n = oa.shape[0]
    d = wa.shape[1]
    const = lambda shape: pl.BlockSpec(shape, lambda i: (0, 0), pipeline_mode=pl.Buffered(1))
    return pl.pallas_call(
        _mix_kernel,
        grid=(n // tm,),
        in_specs=[pl.BlockSpec((tm, oa.shape[1]), lambda i: (i, 0)),
                  pl.BlockSpec((tm, ob.shape[1]), lambda i: (i, 0)),
                  pl.BlockSpec((tm, d), lambda i: (i, 0)),
                  pl.BlockSpec((tm, d), lambda i: (i, 1)),
                  const(wa.shape), const(wb.shape)],
        out_specs=pl.BlockSpec((tm, d), lambda i: (i, 0)),
        out_shape=jax.ShapeDtypeStruct((n, d), BF16),
        compiler_params=pltpu.CompilerParams(dimension_semantics=("arbitrary",),
                                             vmem_limit_bytes=VMEM_LIMIT_BYTES),
        name="branch_mix",
    )(oa, ob, proj, proj, wa, wb)


def _outnorm_kernel(a_ref, w_ref, x_ref, gt_ref, g_ref, o_ref, acc_ref):
    k = pl.program_id(1)
    part = jnp.dot(a_ref[...], w_ref[...], preferred_element_type=F32)

    @pl.when(k == 0)
    def _():
        acc_ref[...] = part

    @pl.when(k > 0)
    def _():
        acc_ref[...] += part

    @pl.when(k == pl.num_programs(1) - 1)
    def _():
        o_ref[...] = x_ref[...] + gt_ref[0] * _rms(acc_ref[...], g_ref[...])


def _outnorm_call(a, w, x2, gt, g, *, tm, tk, rows_per_mod):
    n, kdim = a.shape
    d = w.shape[1]
    mod_tiles = rows_per_mod // tm
    return pl.pallas_call(
        _outnorm_kernel,
        grid=(n // tm, kdim // tk),
        in_specs=[pl.BlockSpec((tm, tk), lambda i, k: (i, k)),
                  pl.BlockSpec((tk, d), lambda i, k: (k, 0)),
                  pl.BlockSpec((tm, d), lambda i, k: (i, 0)),
                  pl.BlockSpec((1, 1, d), lambda i, k: (i // mod_tiles, 0, 0)),
                  pl.BlockSpec((1, d), lambda i, k: (0, 0))],
        out_specs=pl.BlockSpec((tm, d), lambda i, k: (i, 0)),
        out_shape=jax.ShapeDtypeStruct((n, d), F32),
        scratch_shapes=[pltpu.VMEM((tm, d), F32)],
        compiler_params=pltpu.CompilerParams(dimension_semantics=("arbitrary", "arbitrary"),
                                             vmem_limit_bytes=VMEM_LIMIT_BYTES),
        name="out_norm",
    )(a, w, x2, gt, g)


def _rowmm_norm_kernel(a_ref, w_ref, x_ref, gt_ref, g_ref, o_ref, *, n_chunks):
    cm = a_ref.shape[0] // n_chunks
    y = jnp.dot(a_ref[0:cm], w_ref[...], preferred_element_type=F32)
    for c in range(n_chunks):
        nxt = None
        if c + 1 < n_chunks:
            nxt = jnp.dot(a_ref[(c + 1) * cm:(c + 2) * cm], w_ref[...], preferred_element_type=F32)
        rows = slice(c * cm, (c + 1) * cm)
        o_ref[rows] = x_ref[rows] + gt_ref[0] * _rms(y, g_ref[...])
        y = nxt


def _rowmm_norm_call(a, w, x2, gt, g, *, tm, n_chunks, rows_per_mod):
    n, kdim = a.shape
    d = w.shape[1]
    mod_tiles = rows_per_mod // tm
    return pl.pallas_call(
        functools.partial(_rowmm_norm_kernel, n_chunks=n_chunks),
        grid=(n // tm,),
        in_specs=[pl.BlockSpec((tm, kdim), lambda i: (i, 0)),
                  pl.BlockSpec((kdim, d), lambda i: (0, 0), pipeline_mode=pl.Buffered(1)),
                  pl.BlockSpec((tm, d), lambda i: (i, 0)),
                  pl.BlockSpec((1, 1, d), lambda i: (i // mod_tiles, 0, 0)),
                  pl.BlockSpec((1, d), lambda i: (0, 0))],
        out_specs=pl.BlockSpec((tm, d), lambda i: (i, 0)),
        out_shape=jax.ShapeDtypeStruct((n, d), F32),
        compiler_params=pltpu.CompilerParams(dimension_semantics=("arbitrary",),
                                             vmem_limit_bytes=VMEM_LIMIT_RESIDENT_BYTES),
        name="rowmm_norm",
    )(a, w, x2, gt, g)


def _ffn_up_kernel(xp_ref, x_ref, xn_ref, sc_ref, sh_ref, g_ref, wa_ref, wg_ref, cwa_ref, cwg_ref, cba_ref, cbg_ref,
                   o_ref, h_ref, ua_ref, ug_ref, *, nj, n_tiles, tiles_per_seq):
    t = pl.program_id(0)
    tm = x_ref.shape[0]
    halo = xp_ref.shape[0]
    ext = tm + 2 * halo
    margin = 8

    @pl.when(t == 0)
    def _():
        ua_ref[...] = jnp.zeros_like(ua_ref)
        ug_ref[...] = jnp.zeros_like(ug_ref)

    @pl.when((t < n_tiles) & (t % nj == 0))
    def _():
        def mod(x):
            return (_rms(x, g_ref[...]) * (1 + sc_ref[0]) + sh_ref[0]).astype(BF16)
        h_ref[0:halo] = mod(xp_ref[...])
        h_ref[halo:halo + tm] = mod(x_ref[...])
        h_ref[halo + tm:] = mod(xn_ref[...])

    slot = t % 2
    prev = 1 - slot
    ip = jnp.maximum(t - 1, 0) // nj
    first_row = jnp.where((ip % tiles_per_seq) == 0, 0, -1)
    last_row = jnp.where((ip % tiles_per_seq) == tiles_per_seq - 1, tm - 1, -1)
    row = lax.broadcasted_iota(jnp.int32, (tm, 1), 0)

    def epilogue(lo, hi):
        cm = hi - lo
        rowc = row[lo:hi]

        def conv(u_ref, cw_ref, cb_ref):
            win = u_ref[prev, pl.ds(halo + lo - margin, cm + 2 * margin), :]
            up = pltpu.roll(win, 1, 0)[margin:margin + cm]
            un = pltpu.roll(win, cm + 2 * margin - 1, 0)[margin:margin + cm]
            up = jnp.where(rowc == first_row, 0.0, up)
            un = jnp.where(rowc == last_row, 0.0, un)
            out = cb_ref[...] + up * cw_ref[0:1]
            out = out + win[margin:margin + cm] * cw_ref[1:2]
            return out + un * cw_ref[2:3]

        a = conv(ua_ref, cwa_ref, cba_ref)
        gte = conv(ug_ref, cwg_ref, cbg_ref)
        o_ref[lo:hi] = (jax.nn.silu(gte) * a).astype(BF16)

    half = ext // 2
    third = (tm // 3) // BF16_ROWS * BF16_ROWS
    chunks = [(0, third), (third, 2 * third), (2 * third, tm)]
    ua_ref[slot, 0:half] = jnp.dot(h_ref[0:half], wa_ref[...], preferred_element_type=F32)
    epilogue(*chunks[0])
    ua_ref[slot, half:ext] = jnp.dot(h_ref[half:ext], wa_ref[...], preferred_element_type=F32)
    epilogue(*chunks[1])
    ug_ref[slot, 0:half] = jnp.dot(h_ref[0:half], wg_ref[...], preferred_element_type=F32)
    epilogue(*chunks[2])
    ug_ref[slot, half:ext] = jnp.dot(h_ref[half:ext], wg_ref[...], preferred_element_type=F32)


def _ffn_up_call(x2, sc, sh, g, w_up, conv_w, conv_b, *, tm, tn, seq):
    n, d = x2.shape
    dff = w_up.shape[1] // 2
    nj = dff // tn
    halo = BF16_ROWS
    hb = tm // halo
    n_halo = n // halo
    tiles_per_seq = seq // tm
    n_tiles = (n // tm) * nj
    assert (tm + 2 * halo) % (2 * BF16_ROWS) == 0

    def cur(t):
        tc = jnp.minimum(t, n_tiles - 1)
        return tc // nj, tc % nj

    def prv(t):
        tp = jnp.maximum(t - 1, 0)
        return tp // nj, tp % nj

    return pl.pallas_call(
        functools.partial(_ffn_up_kernel, nj=nj, n_tiles=n_tiles, tiles_per_seq=tiles_per_seq),
        grid=(n_tiles + 1,),
        in_specs=[pl.BlockSpec((halo, d), lambda t: (jnp.maximum(cur(t)[0] * hb - 1, 0), 0)),
                  pl.BlockSpec((tm, d), lambda t: (cur(t)[0], 0)),
                  pl.BlockSpec((halo, d), lambda t: (jnp.minimum((cur(t)[0] + 1) * hb, n_halo - 1), 0)),
                  pl.BlockSpec((1, 1, d), lambda t: (cur(t)[0] // tiles_per_seq, 0, 0)),
                  pl.BlockSpec((1, 1, d), lambda t: (cur(t)[0] // tiles_per_seq, 0, 0)),
                  pl.BlockSpec((1, d), lambda t: (0, 0)),
                  pl.BlockSpec((d, tn), lambda t: (0, cur(t)[1])),
                  pl.BlockSpec((d, tn), lambda t: (0, cur(t)[1] + nj)),
                  pl.BlockSpec((CONV_W, tn), lambda t: (0, prv(t)[1])),
                  pl.BlockSpec((CONV_W, tn), lambda t: (0, prv(t)[1] + nj)),
                  pl.BlockSpec((1, tn), lambda t: (0, prv(t)[1])),
                  pl.BlockSpec((1, tn), lambda t: (0, prv(t)[1] + nj))],
        out_specs=pl.BlockSpec((tm, tn), lambda t: prv(t)),
        out_shape=jax.ShapeDtypeStruct((n, dff), BF16),
        scratch_shapes=[pltpu.VMEM((tm + 2 * halo, d), BF16),
                        pltpu.VMEM((2, tm + 2 * halo, tn), F32),
                        pltpu.VMEM((2, tm + 2 * halo, tn), F32)],
        compiler_params=pltpu.CompilerParams(dimension_semantics=("arbitrary",),
                                             vmem_limit_bytes=VMEM_LIMIT_BYTES),
        name="ffn_up",
    )(x2, x2, x2, sc, sh, g, w_up, w_up, conv_w, conv_w, conv_b, conv_b)


def _rope_tables(seq):
    t = jnp.arange(seq)
    half = HEAD_DIM // 4
    inv = ROPE_BASE ** (-jnp.arange(half, dtype=F32) / half)

    def cs(pos):
        ang = pos.astype(F32)[:, None] * inv[None, :]
        c, s = jnp.cos(ang), jnp.sin(ang)
        return jnp.concatenate([c, c], axis=1), jnp.concatenate([-s, s], axis=1)

    cr, sr = cs(t // GRID_W)
    cc, sn = cs(t % GRID_W)
    return jnp.concatenate([cr, cc], axis=1), jnp.concatenate([sr, sn], axis=1)


def _pick_tile(n, pref):
    while n % pref:
        pref //= 2
    return pref


def kernel(x, c, ctx, c_ctx, w_mod, b_mod, g_attn_pre, g_attn_post, g_ffn_pre, g_ffn_post, w_in, sink_a, rpb_b,
           w_br_a, w_br_b, w_o, w_up, conv_w, conv_b, w_down):
    batch, seq, d = x.shape
    ctx_len = ctx.shape[1]
    assert w_mod.shape[0] == 1 and d == 16 * HEAD_DIM
    assert seq % QB_A == 0 and seq % (ROW_GROUP_B * GRID_W) == 0 and WINDOW == BLOCK
    n = batch * seq
    x2 = x.reshape(n, d)

    pad = (-(batch + 1)) % 8
    cs = jnp.concatenate([c, c_ctx[None, :], jnp.zeros((pad, d), F32)], axis=0)
    mod = _mod_call(cs, w_mod[0], b_mod[0][None, :])
    sh1, sc1, gt1, sh2, sc2, gt2 = [mod[:batch, k * d:(k + 1) * d].reshape(batch, 1, d) for k in range(6)]
    csh1 = mod[batch:batch + 1, 0:d].reshape(1, 1, d)
    csc1 = mod[batch:batch + 1, d:2 * d].reshape(1, 1, d)

    o_qa, o_ka, o_va, o_qb, o_kb, o_vb, o_ga = 0, W_QA, W_QA + W_KA, W_QA + 2 * W_KA, W_QA + 2 * W_KA + W_B, \
        W_QA + 2 * W_KA + 2 * W_B, W_QA + 2 * W_KA + 3 * W_B
    wi = w_in[0]
    w_perm = jnp.concatenate([wi[:, o_ga:], wi[:, o_qa:o_ka], wi[:, o_qb:o_ga], wi[:, o_ka:o_qb]], axis=1).astype(BF16)
    c_qa = 2 * d
    c_qb = c_qa + W_QA
    c_kb = c_qb + W_B
    c_vb = c_kb + W_B
    c_ka = c_vb + W_B
    c_va = c_ka + W_KA
    n_tiles = w_perm.shape[1] // TN
    cos, sin = _rope_tables(seq)
    tm = _pick_tile(seq, 1024)
    untile = lambda p: p.transpose(1, 0, 2).reshape(p.shape[1], p.shape[0] * TN)
    proj = untile(_inproj_call(x2, sc1, sh1, g_attn_pre, w_perm, cos, sin, tm=tm, rows_per_mod=seq, j0=0, nj=n_tiles))
    nc = batch * ctx_len
    j0c = c_kb // TN
    cproj = untile(_inproj_call(ctx.reshape(nc, d), csc1, csh1, g_attn_pre, w_perm,
                                jnp.ones((nc, HEAD_DIM), F32), jnp.zeros((nc, HEAD_DIM), F32),
                                tm=nc, rows_per_mod=nc, j0=j0c, nj=n_tiles - j0c))
    cc_kb, cc_vb, cc_ka, cc_va = 0, W_B, 2 * W_B, 2 * W_B + W_KA

    oa = _attn_a_call(proj, cproj, sink_a[0], batch=batch, seq=seq, ctx_len=ctx_len,
                      q_col=c_qa, k_col=c_ka, v_col=c_va, ck_col=cc_ka, cv_col=cc_va)
    bias = _rpb_call(rpb_b[0])
    ob = _attn_b_call(proj, cproj, bias, batch=batch, seq=seq, ctx_len=ctx_len,
                      q_col=c_qb, k_col=c_kb, v_col=c_vb, ck_col=cc_kb, cv_col=cc_vb)

    tm2 = _pick_tile(seq, 512)
    z = _mix_call(oa, ob, proj, w_br_a[0].astype(BF16), w_br_b[0].astype(BF16), tm=tm2)
    x1 = _outnorm_call(z, w_o[0].astype(BF16), x2, gt1, g_attn_post, tm=tm2, tk=d, rows_per_mod=seq)

    dff = w_down.shape[1]
    act = _ffn_up_call(x1, sc2, sh2, g_ffn_pre, w_up[0].astype(BF16), conv_w[0], conv_b[0][None, :],
                       tm=tm, tn=TN, seq=seq)
    out = _rowmm_norm_call(act, w_down[0].astype(BF16), x1, gt2, g_ffn_post, tm=_pick_tile(seq, 256), n_chunks=2,
                           rows_per_mod=seq)
    return out.reshape(batch, seq, d)
```

```python
import functools

import jax
import jax.numpy as jnp
from jax import lax
from jax.experimental import pallas as pl
from jax.experimental.pallas import tpu as pltpu

F32 = jnp.float32
BF16 = jnp.bfloat16

GRID_W = 64
HEAD_DIM = 128
N_HEADS_A = 8
N_KV_A = 2
GQA_GROUP = N_HEADS_A // N_KV_A
WINDOW = 128
BLOCK = 128
N_HEADS_B = 8
NB_ROWS = 8
NB_COLS = 16
CONV_W = 3
ROPE_BASE = 10000.0
EPS = 1e-6
NEG_INF = -1e30
ATTN_SCALE = HEAD_DIM ** -0.5
LOG2E = 1.4426950408889634
Q_SCALE = ATTN_SCALE * LOG2E
F32_MAX = 3.4028234663852886e38

W_QA = N_HEADS_A * HEAD_DIM
W_KA = N_KV_A * HEAD_DIM
W_B = N_HEADS_B * HEAD_DIM

LANES = 128
BF16_ROWS = 16
VMEM_LIMIT_BYTES = 56 * 1024 * 1024
VMEM_LIMIT_RESIDENT_BYTES = 60 * 1024 * 1024

TN = 512
ROW_CHUNKS = 4
ROW_GROUP_B = 4
QB_A = 512


def _rms(x, g):
    return (x * lax.rsqrt(jnp.mean(x * x, axis=-1, keepdims=True) + EPS)) * g


def _mod_kernel(c_ref, w_ref, b_ref, o_ref):
    a = jax.nn.silu(c_ref[...]).astype(BF16)
    o_ref[...] = jnp.dot(a, w_ref[...].astype(BF16), preferred_element_type=F32) + b_ref[...]


def _mod_call(cs, w, b):
    m, d = cs.shape
    n = w.shape[1]
    tn = 1536 if n % 1536 == 0 else n
    return pl.pallas_call(
        _mod_kernel,
        grid=(n // tn,),
        in_specs=[pl.BlockSpec((m, d), lambda j: (0, 0)),
                  pl.BlockSpec((d, tn), lambda j: (0, j)),
                  pl.BlockSpec((1, tn), lambda j: (0, j))],
        out_specs=pl.BlockSpec((m, tn), lambda j: (0, j)),
        out_shape=jax.ShapeDtypeStruct((m, n), F32),
        compiler_params=pltpu.CompilerParams(dimension_semantics=("arbitrary",),
                                             vmem_limit_bytes=VMEM_LIMIT_BYTES),
        name="mod",
    )(cs, w, b)


def _rpb_kernel(rpb_ref, o_ref):
    h = pl.program_id(0)
    n_dr = 2 * NB_ROWS - 1
    n_dc = 2 * NB_COLS - 1
    qc = lax.broadcasted_iota(jnp.int32, (GRID_W, LANES), 0)
    lane = lax.broadcasted_iota(jnp.int32, (GRID_W, LANES), 1)
    kc = lane & (GRID_W - 1)
    dc_idx = jnp.clip(kc - qc, -(NB_COLS - 1), NB_COLS - 1) + (NB_COLS - 1)
    base = h * (n_dr * n_dc)

    def toeplitz(dr):
        def body(dc, acc):
            return jnp.where(dc_idx == dc, rpb_ref[base + dr * n_dc + dc] * LOG2E, acc)
        return lax.fori_loop(0, n_dc, body, jnp.zeros((GRID_W, LANES), F32))

    tw = [toeplitz(dr) for dr in range(n_dr)]
    pair = [jnp.where(lane < GRID_W, tw[d], tw[d + 1]) for d in range(n_dr - 1)]
    n_kr = 3 * ROW_GROUP_B
    dr0 = NB_ROWS - 1 - ROW_GROUP_B
    for rq in range(ROW_GROUP_B):
        for p in range(n_kr // 2):
            o_ref[0, rq * GRID_W:(rq + 1) * GRID_W, p * LANES:(p + 1) * LANES] = pair[2 * p - rq + dr0]


def _rpb_call(rpb):
    nh = rpb.shape[0]
    nq = ROW_GROUP_B * GRID_W
    nk = 3 * ROW_GROUP_B * GRID_W
    assert 2 * ROW_GROUP_B == NB_ROWS
    return pl.pallas_call(
        _rpb_kernel,
        grid=(nh,),
        in_specs=[pl.BlockSpec(memory_space=pltpu.SMEM)],
        out_specs=pl.BlockSpec((1, nq, nk), lambda h: (h, 0, 0)),
        out_shape=jax.ShapeDtypeStruct((nh, nq, nk), F32),
        compiler_params=pltpu.CompilerParams(dimension_semantics=("arbitrary",)),
        name="rpb_table",
    )(rpb.reshape(-1))


def _rope(t, cos, sin, first_half):
    partner = jnp.where(first_half, pltpu.roll(t, LANES - 32, 1), pltpu.roll(t, 32, 1))
    return t * cos + partner * sin


def _inproj_kernel(x_ref, sc_ref, sh_ref, g_ref, w_ref, cos_ref, sin_ref, o_ref, h_ref, *, j0):
    j = pl.program_id(1)

    @pl.when(j == 0)
    def _():
        h = _rms(x_ref[...], g_ref[...]) * (1 + sc_ref[0]) + sh_ref[0]
        h_ref[...] = h.astype(BF16)

    jj = j + j0
    n_gate = 2 * x_ref.shape[1] // TN
    lane = lax.broadcasted_iota(jnp.int32, (1, HEAD_DIM), 1)
    first_half = (lane & 63) < 32
    cm = x_ref.shape[0] // ROW_CHUNKS
    is_gate = jj < n_gate
    is_qa = (jj >= n_gate) & (jj < n_gate + 2)
    is_ka = jj == n_gate + 8
    scale = jnp.where((jj >= n_gate) & (jj < n_gate + 4), Q_SCALE, 1.0).astype(F32)
    rotated = [is_qa | (is_ka & (hd < N_KV_A)) for hd in range(TN // HEAD_DIM)]

    def epilogue(rows, acc):
        cos = cos_ref[rows]
        sin = sin_ref[rows]
        for hd in range(TN // HEAD_DIM):
            sl = slice(hd * HEAD_DIM, (hd + 1) * HEAD_DIM)
            t = acc[:, sl]
            lin = jnp.where(rotated[hd], _rope(t, cos, sin, first_half), t) * scale
            o_ref[rows, sl] = jnp.where(is_gate, jax.nn.sigmoid(t), lin).astype(BF16)

    acc = jnp.dot(h_ref[0:cm], w_ref[...], preferred_element_type=F32)
    for c in range(ROW_CHUNKS):
        nxt = None
        if c + 1 < ROW_CHUNKS:
            nxt = jnp.dot(h_ref[(c + 1) * cm:(c + 2) * cm], w_ref[...], preferred_element_type=F32)
        epilogue(slice(c * cm, (c + 1) * cm), acc)
        acc = nxt


def _inproj_call(x2, sc, sh, g, w, cos, sin, *, tm, rows_per_mod, j0, nj):
    n, d = x2.shape
    pos_tiles = cos.shape[0] // tm
    mod_tiles = rows_per_mod // tm
    return pl.pallas_call(
        functools.partial(_inproj_kernel, j0=j0),
        grid=(n // tm, nj),
        in_specs=[pl.BlockSpec((tm, d), lambda i, j: (i, 0)),
                  pl.BlockSpec((1, 1, d), lambda i, j: (i // mod_tiles, 0, 0)),
                  pl.BlockSpec((1, 1, d), lambda i, j: (i // mod_tiles, 0, 0)),
                  pl.BlockSpec((1, d), lambda i, j: (0, 0)),
                  pl.BlockSpec((d, TN), lambda i, j: (0, j + j0)),
                  pl.BlockSpec((tm, HEAD_DIM), lambda i, j: (i % pos_tiles, 0)),
                  pl.BlockSpec((tm, HEAD_DIM), lambda i, j: (i % pos_tiles, 0))],
        out_specs=pl.BlockSpec((tm, TN), lambda i, j: (i, j)),
        out_shape=jax.ShapeDtypeStruct((n, nj * TN), BF16),
        scratch_shapes=[pltpu.VMEM((tm, d), BF16)],
        compiler_params=pltpu.CompilerParams(dimension_semantics=("arbitrary", "arbitrary"),
                                             vmem_limit_bytes=VMEM_LIMIT_BYTES),
        name="in_proj",
    )(x2, sc, sh, g, w, cos, sin)


def _attn_a_kernel(sink_ref, q_ref, kp_ref, km_ref, kn_ref, vp_ref, vm_ref, vn_ref, kc_ref, vc_ref, o_ref,
                   *, n_blocks):
    nq = pl.program_id(1)
    sub = QB_A // BLOCK
    rows = GQA_GROUP * BLOCK
    n_loc = 3 * BLOCK
    r_io = lax.broadcasted_iota(jnp.int32, (rows, n_loc), 0)
    c_io = lax.broadcasted_iota(jnp.int32, (rows, n_loc), 1)
    qi_io = r_io & (BLOCK - 1)
    in_band = (c_io >= qi_io) & (c_io <= qi_io + 2 * WINDOW)
    c_row = lax.broadcasted_iota(jnp.int32, (1, n_loc), 1)
    g_io = lax.shift_right_logical(lax.broadcasted_iota(jnp.int32, (rows, 1), 0), BLOCK.bit_length() - 1)
    nt = (((1,), (1,)), ((), ()))
    for kvh in range(N_KV_A):
        hs = slice(kvh * HEAD_DIM, (kvh + 1) * HEAD_DIM)
        kband = jnp.concatenate([kp_ref[:, hs], km_ref[:, hs], kn_ref[:, hs]], axis=0)
        vband = jnp.concatenate([vp_ref[:, hs], vm_ref[:, hs], vn_ref[:, hs]], axis=0)
        kc = kc_ref[:, hs]
        vc = vc_ref[:, hs]
        sink = jnp.zeros((rows, 1), F32)
        for g in range(GQA_GROUP):
            sink = jnp.where(g_io == g, sink_ref[kvh * GQA_GROUP + g] * LOG2E, sink)
        for qi in range(sub):
            blk = nq * sub + qi
            lo = jnp.where(blk == 0, BLOCK, 0)
            hi = jnp.where(blk == n_blocks - 1, 2 * BLOCK - 1, n_loc - 1)
            valid = in_band & ((c_row >= lo) & (c_row <= hi))
            q4 = jnp.concatenate(
                [q_ref[qi * BLOCK:(qi + 1) * BLOCK, (kvh * GQA_GROUP + g) * HEAD_DIM:(kvh * GQA_GROUP + g + 1) * HEAD_DIM]
                 for g in range(GQA_GROUP)], axis=0)
            s_loc = lax.dot_general(q4, kband[qi * BLOCK:qi * BLOCK + n_loc], nt, preferred_element_type=F32)
            s_loc = jnp.where(valid, s_loc, NEG_INF)
            s_ctx = lax.dot_general(q4, kc, nt, preferred_element_type=F32)
            m = jnp.maximum(jnp.maximum(jnp.max(s_loc, axis=-1, keepdims=True),
                                        jnp.max(s_ctx, axis=-1, keepdims=True)), sink)
            e_loc = jnp.exp2(s_loc - m)
            e_ctx = jnp.exp2(s_ctx - m)
            den = (jnp.sum(e_loc, axis=-1, keepdims=True) + jnp.sum(e_ctx, axis=-1, keepdims=True)
                   + jnp.exp2(sink - m))
            o = (jnp.dot(e_loc.astype(BF16), vband[qi * BLOCK:qi * BLOCK + n_loc], preferred_element_type=F32)
                 + jnp.dot(e_ctx.astype(BF16), vc, preferred_element_type=F32)) * (1.0 / den)
            for g in range(GQA_GROUP):
                hd = kvh * GQA_GROUP + g
                o_ref[qi * BLOCK:(qi + 1) * BLOCK, hd * HEAD_DIM:(hd + 1) * HEAD_DIM] = (
                    o[g * BLOCK:(g + 1) * BLOCK].astype(BF16))


def _attn_a_call(proj, cproj, sink, *, batch, seq, ctx_len, q_col, k_col, v_col, ck_col, cv_col):
    nqb = seq // QB_A
    sub = QB_A // BLOCK
    n_blocks = seq // BLOCK

    def prev_map(b, n, col):
        return (b * n_blocks + jnp.maximum(n * sub - 1, 0), col // W_KA)

    def next_map(b, n, col):
        return (b * n_blocks + jnp.minimum(n * sub + sub, n_blocks - 1), col // W_KA)

    def main_map(b, n, col):
        return (b * nqb + n, col // W_KA)

    edge = lambda f, col: pl.BlockSpec((BLOCK, W_KA), functools.partial(f, col=col))
    main = lambda col: pl.BlockSpec((QB_A, W_KA), functools.partial(main_map, col=col))
    ctxs = lambda col: pl.BlockSpec((ctx_len, W_KA), lambda b, n: (b, col // W_KA))
    return pl.pallas_call(
        functools.partial(_attn_a_kernel, n_blocks=n_blocks),
        grid=(batch, nqb),
        in_specs=[pl.BlockSpec(memory_space=pltpu.SMEM),
                  pl.BlockSpec((QB_A, W_QA), lambda b, n: (b * nqb + n, q_col // W_QA)),
                  edge(prev_map, k_col), main(k_col), edge(next_map, k_col),
                  edge(prev_map, v_col), main(v_col), edge(next_map, v_col),
                  ctxs(ck_col), ctxs(cv_col)],
        out_specs=pl.BlockSpec((QB_A, W_QA), lambda b, n: (b * nqb + n, 0)),
        out_shape=jax.ShapeDtypeStruct((batch * seq, W_QA), BF16),
        compiler_params=pltpu.CompilerParams(dimension_semantics=("arbitrary",) * 2,
                                             vmem_limit_bytes=VMEM_LIMIT_BYTES),
        name="attn_a",
    )(sink, proj, proj, proj, proj, proj, proj, proj, cproj, cproj)


def _attn_b_kernel(q_ref, kp_ref, km_ref, kn_ref, vp_ref, vm_ref, vn_ref, kc_ref, vc_ref, bias_ref, o_ref,
                   *, grid_rows):
    n = pl.program_id(1)
    nq = ROW_GROUP_B * GRID_W
    nk = 3 * nq
    n_ctx = kc_ref.shape[0]
    heads = q_ref.shape[1] // HEAD_DIM
    r_io = lax.broadcasted_iota(jnp.int32, (nq, nk), 0)
    c_io = lax.broadcasted_iota(jnp.int32, (nq, nk), 1)
    log_w = GRID_W.bit_length() - 1
    r_abs = n * ROW_GROUP_B + lax.shift_right_logical(r_io, log_w)
    q_col = r_io & (GRID_W - 1)
    kr_abs = (n - 1) * ROW_GROUP_B + lax.shift_right_logical(c_io, log_w)
    k_col = c_io & (GRID_W - 1)
    r_start = jnp.clip(r_abs - NB_ROWS // 2, 0, grid_rows - NB_ROWS)
    c_start = jnp.clip(q_col - NB_COLS // 2, 0, GRID_W - NB_COLS)
    valid = ((kr_abs >= r_start) & (kr_abs < r_start + NB_ROWS)
             & (k_col >= c_start) & (k_col < c_start + NB_COLS))
    cap = jnp.where(valid, F32_MAX, NEG_INF)
    for hd in range(heads):
        sl = slice(hd * HEAD_DIM, (hd + 1) * HEAD_DIM)
        q = q_ref[:, sl]
        keys = jnp.concatenate([kp_ref[:, sl], km_ref[:, sl], kn_ref[:, sl]], axis=0)
        vals = jnp.concatenate([vp_ref[:, sl], vm_ref[:, sl], vn_ref[:, sl]], axis=0)
        s_nb = lax.dot_general(q, keys, (((1,), (1,)), ((), ())), preferred_element_type=F32)
        s_nb = jnp.minimum(s_nb + bias_ref[hd], cap)
        s_ctx = lax.dot_general(q, kc_ref[:, sl], (((1,), (1,)), ((), ())), preferred_element_type=F32)
        m = jnp.maximum(jnp.max(s_nb, axis=-1, keepdims=True), jnp.max(s_ctx, axis=-1, keepdims=True))
        e_nb = jnp.exp2(s_nb - m)
        e_ctx = jnp.exp2(s_ctx - m)
        den = jnp.sum(e_nb, axis=-1, keepdims=True) + jnp.sum(e_ctx, axis=-1, keepdims=True)
        o = (jnp.dot(e_nb.astype(BF16), vals, preferred_element_type=F32)
             + jnp.dot(e_ctx.astype(BF16), vc_ref[:, sl], preferred_element_type=F32)) * (1.0 / den)
        o_ref[:, sl] = o.astype(BF16)


def _attn_b_call(proj, cproj, bias, *, batch, seq, ctx_len, q_col, k_col, v_col, ck_col, cv_col):
    nq = ROW_GROUP_B * GRID_W
    ng = seq // nq
    hw = W_B
    hg = W_B // hw

    def row_spec(col, shift):
        def imap(b, n, h):
            return (b * ng + jnp.clip(n + shift, 0, ng - 1), col // hw + h)
        return pl.BlockSpec((nq, hw), imap)

    ctxs = lambda col: pl.BlockSpec((ctx_len, hw), lambda b, n, h: (b, col // hw + h))
    return pl.pallas_call(
        functools.partial(_attn_b_kernel, grid_rows=seq // GRID_W),
        grid=(batch, ng, hg),
        in_specs=[row_spec(q_col, 0),
                  row_spec(k_col, -1), row_spec(k_col, 0), row_spec(k_col, 1),
                  row_spec(v_col, -1), row_spec(v_col, 0), row_spec(v_col, 1),
                  ctxs(ck_col), ctxs(cv_col),
                  pl.BlockSpec((hw // HEAD_DIM, nq, bias.shape[2]), lambda b, n, h: (h, 0, 0))],
        out_specs=pl.BlockSpec((nq, hw), lambda b, n, h: (b * ng + n, h)),
        out_shape=jax.ShapeDtypeStruct((batch * seq, W_B), BF16),
        compiler_params=pltpu.CompilerParams(dimension_semantics=("arbitrary",) * 3,
                                             vmem_limit_bytes=VMEM_LIMIT_BYTES),
        name="attn_b",
    )(proj, proj, proj, proj, proj, proj, proj, cproj, cproj, bias)


def _mix_kernel(oa_ref, ob_ref, ga_ref, gb_ref, wa_ref, wb_ref, z_ref):
    ya = jnp.dot(oa_ref[...], wa_ref[...], preferred_element_type=F32)
    yb = jnp.dot(ob_ref[...], wb_ref[...], preferred_element_type=F32)
    z_ref[...] = (ga_ref[...].astype(F32) * ya + gb_ref[...].astype(F32) * yb).astype(BF16)


def _mix_call(oa, ob, proj, wa, wb, *, tm):
    n = oa.shape[0]
    d = wa.shape[1]
    const = lambda shape: pl.BlockSpec(shape, lambda i: (0, 0), pipeline_mode=pl.Buffered(1))
    return pl.pallas_call(
        _mix_kernel,
        grid=(n // tm,),
        in_specs=[pl.BlockSpec((tm, oa.shape[1]), lambda i: (i, 0)),
                  pl.BlockSpec((tm, ob.shape[1]), lambda i: (i, 0)),
                  pl.BlockSpec((tm, d), lambda i: (i, 0)),
                  pl.BlockSpec((tm, d), lambda i: (i, 1)),
                  const(wa.shape), const(wb.shape)],
        out_specs=pl.BlockSpec((tm, d), lambda i: (i, 0)),
        out_shape=jax.ShapeDtypeStruct((n, d), BF16),
        compiler_params=pltpu.CompilerParams(dimension_semantics=("arbitrary",),
                                             vmem_limit_bytes=VMEM_LIMIT_BYTES),
        name="branch_mix",
    )(oa, ob, proj, proj, wa, wb)


def _outnorm_kernel(a_ref, w_ref, x_ref, gt_ref, g_ref, o_ref, acc_ref):
    k = pl.program_id(1)
    part = jnp.dot(a_ref[...], w_ref[...], preferred_element_type=F32)

    @pl.when(k == 0)
    def _():
        acc_ref[...] = part

    @pl.when(k > 0)
    def _():
        acc_ref[...] += part

    @pl.when(k == pl.num_programs(1) - 1)
    def _():
        o_ref[...] = x_ref[...] + gt_ref[0] * _rms(acc_ref[...], g_ref[...])


def _outnorm_call(a, w, x2, gt, g, *, tm, tk, rows_per_mod):
    n, kdim = a.shape
    d = w.shape[1]
    mod_tiles = rows_per_mod // tm
    return pl.pallas_call(
        _outnorm_kernel,
        grid=(n // tm, kdim // tk),
        in_specs=[pl.BlockSpec((tm, tk), lambda i, k: (i, k)),
                  pl.BlockSpec((tk, d), lambda i, k: (k, 0)),
                  pl.BlockSpec((tm, d), lambda i, k: (i, 0)),
                  pl.BlockSpec((1, 1, d), lambda i, k: (i // mod_tiles, 0, 0)),
                  pl.BlockSpec((1, d), lambda i, k: (0, 0))],
        out_specs=pl.BlockSpec((tm, d), lambda i, k: (i, 0)),
        out_shape=jax.ShapeDtypeStruct((n, d), F32),
        scratch_shapes=[pltpu.VMEM((tm, d), F32)],
        compiler_params=pltpu.CompilerParams(dimension_semantics=("arbitrary", "arbitrary"),
                                             vmem_limit_bytes=VMEM_LIMIT_BYTES),
        name="out_norm",
    )(a, w, x2, gt, g)


def _rowmm_norm_kernel(a_ref, w_ref, x_ref, gt_ref, g_ref, o_ref, *, n_chunks):
    cm = a_ref.shape[0] // n_chunks
    y = jnp.dot(a_ref[0:cm], w_ref[...], preferred_element_type=F32)
    for c in range(n_chunks):
        nxt = None
        if c + 1 < n_chunks:
            nxt = jnp.dot(a_ref[(c + 1) * cm:(c + 2) * cm], w_ref[...], preferred_element_type=F32)
        rows = slice(c * cm, (c + 1) * cm)
        o_ref[rows] = x_ref[rows] + gt_ref[0] * _rms(y, g_ref[...])
        y = nxt


def _rowmm_norm_call(a, w, x2, gt, g, *, tm, n_chunks, rows_per_mod):
    n, kdim = a.shape
    d = w.shape[1]
    mod_tiles = rows_per_mod // tm
    return pl.pallas_call(
        functools.partial(_rowmm_norm_kernel, n_chunks=n_chunks),
        grid=(n // tm,),
        in_specs=[pl.BlockSpec((tm, kdim), lambda i: (i, 0)),
                  pl.BlockSpec((kdim, d), lambda i: (0, 0), pipeline_mode=pl.Buffered(1)),
                  pl.BlockSpec((tm, d), lambda i: (i, 0)),
                  pl.BlockSpec((1, 1, d), lambda i: (i // mod_tiles, 0, 0)),
                  pl.BlockSpec((1, d), lambda i: (0, 0))],
        out_specs=pl.BlockSpec((tm, d), lambda i: (i, 0)),
        out_shape=jax.ShapeDtypeStruct((n, d), F32),
        compiler_params=pltpu.CompilerParams(dimension_semantics=("arbitrary",),
                                             vmem_limit_bytes=VMEM_LIMIT_RESIDENT_BYTES),
        name="rowmm_norm",
    )(a, w, x2, gt, g)


def _ffn_up_kernel(xp_ref, x_ref, xn_ref, sc_ref, sh_ref, g_ref, wa_ref, wg_ref, cwa_ref, cwg_ref, cba_ref, cbg_ref,
                   o_ref, h_ref, ua_ref, ug_ref, *, nj, n_tiles, tiles_per_seq):
    t = pl.program_id(0)
    tm = x_ref.shape[0]
    halo = xp_ref.shape[0]
    ext = tm + 2 * halo
    margin = 8

    @pl.when(t == 0)
    def _():
        ua_ref[...] = jnp.zeros_like(ua_ref)
        ug_ref[...] = jnp.zeros_like(ug_ref)

    @pl.when((t < n_tiles) & (t % nj == 0))
    def _():
        def mod(x):
            return (_rms(x, g_ref[...]) * (1 + sc_ref[0]) + sh_ref[0]).astype(BF16)
        h_ref[0:halo] = mod(xp_ref[...])
        h_ref[halo:halo + tm] = mod(x_ref[...])
        h_ref[halo + tm:] = mod(xn_ref[...])

    slot = t % 2
    prev = 1 - slot
    ip = jnp.maximum(t - 1, 0) // nj
    first_row = jnp.where((ip % tiles_per_seq) == 0, 0, -1)
    last_row = jnp.where((ip % tiles_per_seq) == tiles_per_seq - 1, tm - 1, -1)
    row = lax.broadcasted_iota(jnp.int32, (tm, 1), 0)

    def epilogue(lo, hi):
        cm = hi - lo
        rowc = row[lo:hi]

        def conv(u_ref, cw_ref, cb_ref):
            win = u_ref[prev, pl.ds(halo + lo - margin, cm + 2 * margin), :]
            up = pltpu.roll(win, 1, 0)[margin:margin + cm]
            un = pltpu.roll(win, cm + 2 * margin - 1, 0)[margin:margin + cm]
            up = jnp.where(rowc == first_row, 0.0, up)
            un = jnp.where(rowc == last_row, 0.0, un)
            out = cb_ref[...] + up * cw_ref[0:1]
            out = out + win[margin:margin + cm] * cw_ref[1:2]
            return out + un * cw_ref[2:3]

        a = conv(ua_ref, cwa_ref, cba_ref)
        gte = conv(ug_ref, cwg_ref, cbg_ref)
        o_ref[lo:hi] = (jax.nn.silu(gte) * a).astype(BF16)

    half = ext // 2
    third = (tm // 3) // BF16_ROWS * BF16_ROWS
    chunks = [(0, third), (third, 2 * third), (2 * third, tm)]
    ua_ref[slot, 0:half] = jnp.dot(h_ref[0:half], wa_ref[...], preferred_element_type=F32)
    epilogue(*chunks[0])
    ua_ref[slot, half:ext] = jnp.dot(h_ref[half:ext], wa_ref[...], preferred_element_type=F32)
    epilogue(*chunks[1])
    ug_ref[slot, 0:half] = jnp.dot(h_ref[0:half], wg_ref[...], preferred_element_type=F32)
    epilogue(*chunks[2])
    ug_ref[slot, half:ext] = jnp.dot(h_ref[half:ext], wg_ref[...], preferred_element_type=F32)


def _ffn_up_call(x2, sc, sh, g, w_up, conv_w, conv_b, *, tm, tn, seq):
    n, d = x2.shape
    dff = w_up.shape[1] // 2
    nj = dff // tn
    halo = BF16_ROWS
    hb = tm // halo
    n_halo = n // halo
    tiles_per_seq = seq // tm
    n_tiles = (n // tm) * nj
    assert (tm + 2 * halo) % (2 * BF16_ROWS) == 0

    def cur(t):
        tc = jnp.minimum(t, n_tiles - 1)
        return tc // nj, tc % nj

    def prv(t):
        tp = jnp.maximum(t - 1, 0)
        return tp // nj, tp % nj

    return pl.pallas_call(
        functools.partial(_ffn_up_kernel, nj=nj, n_tiles=n_tiles, tiles_per_seq=tiles_per_seq),
        grid=(n_tiles + 1,),
        in_specs=[pl.BlockSpec((halo, d), lambda t: (jnp.maximum(cur(t)[0] * hb - 1, 0), 0)),
                  pl.BlockSpec((tm, d), lambda t: (cur(t)[0], 0)),
                  pl.BlockSpec((halo, d), lambda t: (jnp.minimum((cur(t)[0] + 1) * hb, n_halo - 1), 0)),
                  pl.BlockSpec((1, 1, d), lambda t: (cur(t)[0] // tiles_per_seq, 0, 0)),
                  pl.BlockSpec((1, 1, d), lambda t: (cur(t)[0] // tiles_per_seq, 0, 0)),
                  pl.BlockSpec((1, d), lambda t: (0, 0)),
                  pl.BlockSpec((d, tn), lambda t: (0, cur(t)[1])),
                  pl.BlockSpec((d, tn), lambda t: (0, cur(t)[1] + nj)),
                  pl.BlockSpec((CONV_W, tn), lambda t: (0, prv(t)[1])),
                  pl.BlockSpec((CONV_W, tn), lambda t: (0, prv(t)[1] + nj)),
                  pl.BlockSpec((1, tn), lambda t: (0, prv(t)[1])),
                  pl.BlockSpec((1, tn), lambda t: (0, prv(t)[1] + nj))],
        out_specs=pl.BlockSpec((tm, tn), lambda t: prv(t)),
        out_shape=jax.ShapeDtypeStruct((n, dff), BF16),
        scratch_shapes=[pltpu.VMEM((tm + 2 * halo, d), BF16),
                        pltpu.VMEM((2, tm + 2 * halo, tn), F32),
                        pltpu.VMEM((2, tm + 2 * halo, tn), F32)],
        compiler_params=pltpu.CompilerParams(dimension_semantics=("arbitrary",),
                                             vmem_limit_bytes=VMEM_LIMIT_BYTES),
        name="ffn_up",
    )(x2, x2, x2, sc, sh, g, w_up, w_up, conv_w, conv_w, conv_b, conv_b)


def _rope_tables(seq):
    t = jnp.arange(seq)
    half = HEAD_DIM // 4
    inv = ROPE_BASE ** (-jnp.arange(half, dtype=F32) / half)

    def cs(pos):
        ang = pos.astype(F32)[:, None] * inv[None, :]
        c, s = jnp.cos(ang), jnp.sin(ang)
        return jnp.concatenate([c, c], axis=1), jnp.concatenate([-s, s], axis=1)

    cr, sr = cs(t // GRID_W)
    cc, sn = cs(t % GRID_W)
    return jnp.concatenate([cr, cc], axis=1), jnp.concatenate([sr, sn], axis=1)


def _pick_tile(n, pref):
    while n % pref:
        pref //= 2
    return pref


def kernel(x, c, ctx, c_ctx, w_mod, b_mod, g_attn_pre, g_attn_post, g_ffn_pre, g_ffn_post, w_in, sink_a, rpb_b,
           w_br_a, w_br_b, w_o, w_up, conv_w, conv_b, w_down):
    batch, seq, d = x.shape
    ctx_len = ctx.shape[1]
    assert w_mod.shape[0] == 1 and d == 16 * HEAD_DIM
    assert seq % QB_A == 0 and seq % (ROW_GROUP_B * GRID_W) == 0 and WINDOW == BLOCK
    n = batch * seq
    x2 = x.reshape(n, d)

    pad = (-(batch + 1)) % 8
    cs = jnp.concatenate([c, c_ctx[None, :], jnp.zeros((pad, d), F32)], axis=0)
    mod = _mod_call(cs, w_mod[0], b_mod[0][None, :])
    sh1, sc1, gt1, sh2, sc2, gt2 = [mod[:batch, k * d:(k + 1) * d].reshape(batch, 1, d) for k in range(6)]
    csh1 = mod[batch:batch + 1, 0:d].reshape(1, 1, d)
    csc1 = mod[batch:batch + 1, d:2 * d].reshape(1, 1, d)

    o_qa, o_ka, o_va, o_qb, o_kb, o_vb, o_ga = 0, W_QA, W_QA + W_KA, W_QA + 2 * W_KA, W_QA + 2 * W_KA + W_B, \
        W_QA + 2 * W_KA + 2 * W_B, W_QA + 2 * W_KA + 3 * W_B
    wi = w_in[0]
    w_perm = jnp.concatenate([wi[:, o_ga:], wi[:, o_qa:o_ka], wi[:, o_qb:o_ga], wi[:, o_ka:o_qb]], axis=1).astype(BF16)
    c_qa = 2 * d
    c_qb = c_qa + W_QA
    c_kb = c_qb + W_B
    c_vb = c_kb + W_B
    c_ka = c_vb + W_B
    c_va = c_ka + W_KA
    n_tiles = w_perm.shape[1] // TN
    cos, sin = _rope_tables(seq)
    tm = _pick_tile(seq, 1024)
    proj = _inproj_call(x2, sc1, sh1, g_attn_pre, w_perm, cos, sin, tm=tm, rows_per_mod=seq, j0=0, nj=n_tiles)
    nc = batch * ctx_len
    j0c = c_kb // TN
    cproj = _inproj_call(ctx.reshape(nc, d), csc1, csh1, g_attn_pre, w_perm,
                         jnp.ones((nc, HEAD_DIM), F32), jnp.zeros((nc, HEAD_DIM), F32),
                         tm=nc, rows_per_mod=nc, j0=j0c, nj=n_tiles - j0c)
    cc_kb, cc_vb, cc_ka, cc_va = 0, W_B, 2 * W_B, 2 * W_B + W_KA

    oa = _attn_a_call(proj, cproj, sink_a[0], batch=batch, seq=seq, ctx_len=ctx_len,
                      q_col=c_qa, k_col=c_ka, v_col=c_va, ck_col=cc_ka, cv_col=cc_va)
    bias = _rpb_call(rpb_b[0])
    ob = _attn_b_call(proj, cproj, bias, batch=batch, seq=seq, ctx_len=ctx_len,
                      q_col=c_qb, k_col=c_kb, v_col=c_vb, ck_col=cc_kb, cv_col=cc_vb)

    tm2 = _pick_tile(seq, 512)
    z = _mix_call(oa, ob, proj, w_br_a[0].astype(BF16), w_br_b[0].astype(BF16), tm=tm2)
    x1 = _outnorm_call(z, w_o[0].astype(BF16), x2, gt1, g_attn_post, tm=tm2, tk=d, rows_per_mod=seq)

    dff = w_down.shape[1]
    act = _ffn_up_call(x1, sc2, sh2, g_ffn_pre, w_up[0].astype(BF16), conv_w[0], conv_b[0][None, :],
                       tm=tm, tn=TN, seq=seq)
    out = _rowmm_norm_call(act, w_down[0].astype(BF16), x1, gt2, g_ffn_post, tm=_pick_tile(seq, 256), n_chunks=2,
                           rows_per_mod=seq)
    return out.reshape(batch, seq, d)
```

```python
import functools

import jax
import jax.numpy as jnp
from jax import lax
from jax.experimental import pallas as pl
from jax.experimental.pallas import tpu as pltpu

F32 = jnp.float32
BF16 = jnp.bfloat16

GRID_W = 64
HEAD_DIM = 128
N_HEADS_A = 8
N_KV_A = 2
GQA_GROUP = N_HEADS_A // N_KV_A
WINDOW = 128
BLOCK = 128
N_HEADS_B = 8
NB_ROWS = 8
NB_COLS = 16
CONV_W = 3
ROPE_BASE = 10000.0
EPS = 1e-6
NEG_INF = -1e30
ATTN_SCALE = HEAD_DIM ** -0.5
LOG2E = 1.4426950408889634
Q_SCALE = ATTN_SCALE * LOG2E
F32_MAX = 3.4028234663852886e38

W_QA = N_HEADS_A * HEAD_DIM
W_KA = N_KV_A * HEAD_DIM
W_B = N_HEADS_B * HEAD_DIM

LANES = 128
BF16_ROWS = 16
VMEM_LIMIT_BYTES = 56 * 1024 * 1024
VMEM_LIMIT_RESIDENT_BYTES = 60 * 1024 * 1024

TN = 512
TN_GATE = 1024
ROW_CHUNK = 256
ROW_GROUP_B = 4
QB_A = 512


def _rms(x, g):
    return (x * lax.rsqrt(jnp.mean(x * x, axis=-1, keepdims=True) + EPS)) * g


def _mod_kernel(c_ref, w_ref, b_ref, o_ref):
    a = jax.nn.silu(c_ref[...]).astype(BF16)
    o_ref[...] = jnp.dot(a, w_ref[...].astype(BF16), preferred_element_type=F32) + b_ref[...]


def _mod_call(cs, w, b):
    m, d = cs.shape
    n = w.shape[1]
    tn = 1536 if n % 1536 == 0 else n
    return pl.pallas_call(
        _mod_kernel,
        grid=(n // tn,),
        in_specs=[pl.BlockSpec((m, d), lambda j: (0, 0)),
                  pl.BlockSpec((d, tn), lambda j: (0, j)),
                  pl.BlockSpec((1, tn), lambda j: (0, j))],
        out_specs=pl.BlockSpec((m, tn), lambda j: (0, j)),
        out_shape=jax.ShapeDtypeStruct((m, n), F32),
        compiler_params=pltpu.CompilerParams(dimension_semantics=("arbitrary",),
                                             vmem_limit_bytes=VMEM_LIMIT_BYTES),
        name="mod",
    )(cs, w, b)


def _rpb_kernel(rpb_ref, o_ref):
    h = pl.program_id(0)
    n_dr = 2 * NB_ROWS - 1
    n_dc = 2 * NB_COLS - 1
    qc = lax.broadcasted_iota(jnp.int32, (GRID_W, LANES), 0)
    lane = lax.broadcasted_iota(jnp.int32, (GRID_W, LANES), 1)
    kc = lane & (GRID_W - 1)
    dc_idx = jnp.clip(kc - qc, -(NB_COLS - 1), NB_COLS - 1) + (NB_COLS - 1)
    base = h * (n_dr * n_dc)

    def toeplitz(dr):
        def body(dc, acc):
            return jnp.where(dc_idx == dc, rpb_ref[base + dr * n_dc + dc] * LOG2E, acc)
        return lax.fori_loop(0, n_dc, body, jnp.zeros((GRID_W, LANES), F32))

    tw = [toeplitz(dr) for dr in range(n_dr)]
    pair = [jnp.where(lane < GRID_W, tw[d], tw[d + 1]) for d in range(n_dr - 1)]
    n_kr = 3 * ROW_GROUP_B
    dr0 = NB_ROWS - 1 - ROW_GROUP_B
    for rq in range(ROW_GROUP_B):
        for p in range(n_kr // 2):
            o_ref[0, rq * GRID_W:(rq + 1) * GRID_W, p * LANES:(p + 1) * LANES] = pair[2 * p - rq + dr0]


def _rpb_call(rpb):
    nh = rpb.shape[0]
    nq = ROW_GROUP_B * GRID_W
    nk = 3 * ROW_GROUP_B * GRID_W
    assert 2 * ROW_GROUP_B == NB_ROWS
    return pl.pallas_call(
        _rpb_kernel,
        grid=(nh,),
        in_specs=[pl.BlockSpec(memory_space=pltpu.SMEM)],
        out_specs=pl.BlockSpec((1, nq, nk), lambda h: (h, 0, 0)),
        out_shape=jax.ShapeDtypeStruct((nh, nq, nk), F32),
        compiler_params=pltpu.CompilerParams(dimension_semantics=("arbitrary",)),
        name="rpb_table",
    )(rpb.reshape(-1))


def _rope(t, cos, sin, first_half):
    partner = jnp.where(first_half, pltpu.roll(t, LANES - 32, 1), pltpu.roll(t, 32, 1))
    return t * cos + partner * sin


def _row_chunked_matmul(h_ref, w_ref, epilogue):
    n_chunks = h_ref.shape[0] // ROW_CHUNK
    acc = jnp.dot(h_ref[0:ROW_CHUNK], w_ref[...], preferred_element_type=F32)
    for c in range(n_chunks):
        nxt = None
        if c + 1 < n_chunks:
            nxt = jnp.dot(h_ref[(c + 1) * ROW_CHUNK:(c + 2) * ROW_CHUNK], w_ref[...], preferred_element_type=F32)
        epilogue(slice(c * ROW_CHUNK, (c + 1) * ROW_CHUNK), acc)
        acc = nxt


def _gate_kernel(x_ref, sc_ref, sh_ref, g_ref, w_ref, o_ref, h_ref):
    @pl.when(pl.program_id(1) == 0)
    def _():
        h = _rms(x_ref[...], g_ref[...]) * (1 + sc_ref[0]) + sh_ref[0]
        h_ref[...] = h.astype(BF16)

    def epilogue(rows, acc):
        o_ref[rows] = jax.nn.sigmoid(acc).astype(BF16)
    _row_chunked_matmul(h_ref, w_ref, epilogue)


def _gate_call(x2, sc, sh, g, w, *, tm, tn, n_cols, rows_per_mod):
    n, d = x2.shape
    mod_tiles = rows_per_mod // tm
    return pl.pallas_call(
        _gate_kernel,
        grid=(n // tm, n_cols // tn),
        in_specs=[pl.BlockSpec((tm, d), lambda i, j: (i, 0)),
                  pl.BlockSpec((1, 1, d), lambda i, j: (i // mod_tiles, 0, 0)),
                  pl.BlockSpec((1, 1, d), lambda i, j: (i // mod_tiles, 0, 0)),
                  pl.BlockSpec((1, d), lambda i, j: (0, 0)),
                  pl.BlockSpec((d, tn), lambda i, j: (0, j))],
        out_specs=[pl.BlockSpec((tm, tn), lambda i, j: (i, j)),
                   pl.BlockSpec((tm, d), lambda i, j: (i, 0))],
        out_shape=[jax.ShapeDtypeStruct((n, n_cols), BF16), jax.ShapeDtypeStruct((n, d), BF16)],
        compiler_params=pltpu.CompilerParams(dimension_semantics=("arbitrary", "arbitrary"),
                                             vmem_limit_bytes=VMEM_LIMIT_BYTES),
        name="gate_proj",
    )(x2, sc, sh, g, w)


def _qkv_kernel(h_ref, w_ref, cos_ref, sin_ref, o_ref, *, j0, n_gate):
    _proj_tile(h_ref, w_ref, cos_ref, sin_ref, o_ref, pl.program_id(1) + j0, n_gate)


def _qkv_call(h, w, cos, sin, *, tm, j0, nj):
    n, d = h.shape
    pos_tiles = cos.shape[0] // tm
    return pl.pallas_call(
        functools.partial(_qkv_kernel, j0=j0, n_gate=2 * d // TN),
        grid=(n // tm, nj),
        in_specs=[pl.BlockSpec((tm, d), lambda i, j: (i, 0)),
                  pl.BlockSpec((d, TN), lambda i, j: (0, j + j0)),
                  pl.BlockSpec((tm, HEAD_DIM), lambda i, j: (i % pos_tiles, 0)),
                  pl.BlockSpec((tm, HEAD_DIM), lambda i, j: (i % pos_tiles, 0))],
        out_specs=pl.BlockSpec((tm, TN), lambda i, j: (i, j)),
        out_shape=jax.ShapeDtypeStruct((n, nj * TN), BF16),
        compiler_params=pltpu.CompilerParams(dimension_semantics=("arbitrary", "arbitrary"),
                                             vmem_limit_bytes=VMEM_LIMIT_BYTES),
        name="qkv_proj",
    )(h, w, cos, sin)


def _inproj_kernel(x_ref, sc_ref, sh_ref, g_ref, w_ref, cos_ref, sin_ref, o_ref, h_ref, *, j0):
    j = pl.program_id(1)

    @pl.when(j == 0)
    def _():
        h = _rms(x_ref[...], g_ref[...]) * (1 + sc_ref[0]) + sh_ref[0]
        h_ref[...] = h.astype(BF16)

    _proj_tile(h_ref, w_ref, cos_ref, sin_ref, o_ref, j + j0, 2 * x_ref.shape[1] // TN)


def _proj_tile(h_ref, w_ref, cos_ref, sin_ref, o_ref, jj, n_gate):
    lane = lax.broadcasted_iota(jnp.int32, (1, HEAD_DIM), 1)
    first_half = (lane & 63) < 32

    def row_chunks(epilogue):
        _row_chunked_matmul(h_ref, w_ref, epilogue)

    def rope_heads(rows, acc, n_heads, scale):
        cos = cos_ref[rows]
        sin = sin_ref[rows]
        for hd in range(n_heads):
            sl = slice(hd * HEAD_DIM, (hd + 1) * HEAD_DIM)
            r = _rope(acc[:, sl], cos, sin, first_half)
            if scale is not None:
                r = r * scale
            o_ref[rows, sl] = r.astype(BF16)

    @pl.when((jj >= n_gate) & (jj < n_gate + 2))
    def _():
        row_chunks(lambda rows, acc: rope_heads(rows, acc, TN // HEAD_DIM, Q_SCALE))

    @pl.when((jj >= n_gate + 2) & (jj < n_gate + 4))
    def _():
        def epilogue(rows, acc):
            o_ref[rows] = (acc * Q_SCALE).astype(BF16)
        row_chunks(epilogue)

    @pl.when((jj >= n_gate + 4) & (jj < n_gate + 8))
    def _():
        def epilogue(rows, acc):
            o_ref[rows] = acc.astype(BF16)
        row_chunks(epilogue)

    @pl.when(jj == n_gate + 8)
    def _():
        def epilogue(rows, acc):
            rope_heads(rows, acc, N_KV_A, None)
            o_ref[rows, W_KA:] = acc[:, W_KA:].astype(BF16)
        row_chunks(epilogue)


def _inproj_call(x2, sc, sh, g, w, cos, sin, *, tm, rows_per_mod, j0, nj):
    n, d = x2.shape
    pos_tiles = cos.shape[0] // tm
    mod_tiles = rows_per_mod // tm
    return pl.pallas_call(
        functools.partial(_inproj_kernel, j0=j0),
        grid=(n // tm, nj),
        in_specs=[pl.BlockSpec((tm, d), lambda i, j: (i, 0)),
                  pl.BlockSpec((1, 1, d), lambda i, j: (i // mod_tiles, 0, 0)),
                  pl.BlockSpec((1, 1, d), lambda i, j: (i // mod_tiles, 0, 0)),
                  pl.BlockSpec((1, d), lambda i, j: (0, 0)),
                  pl.BlockSpec((d, TN), lambda i, j: (0, j + j0)),
                  pl.BlockSpec((tm, HEAD_DIM), lambda i, j: (i % pos_tiles, 0)),
                  pl.BlockSpec((tm, HEAD_DIM), lambda i, j: (i % pos_tiles, 0))],
        out_specs=pl.BlockSpec((tm, TN), lambda i, j: (i, j)),
        out_shape=jax.ShapeDtypeStruct((n, nj * TN), BF16),
        scratch_shapes=[pltpu.VMEM((tm, d), BF16)],
        compiler_params=pltpu.CompilerParams(dimension_semantics=("arbitrary", "arbitrary"),
                                             vmem_limit_bytes=VMEM_LIMIT_BYTES),
        name="in_proj",
    )(x2, sc, sh, g, w, cos, sin)


def _attn_a_kernel(sink_ref, q_ref, kp_ref, km_ref, kn_ref, vp_ref, vm_ref, vn_ref, kc_ref, vc_ref, o_ref,
                   *, n_blocks):
    nq = pl.program_id(1)
    sub = QB_A // BLOCK
    rows = GQA_GROUP * BLOCK
    n_loc = 3 * BLOCK
    r_io = lax.broadcasted_iota(jnp.int32, (rows, n_loc), 0)
    c_io = lax.broadcasted_iota(jnp.int32, (rows, n_loc), 1)
    qi_io = r_io & (BLOCK - 1)
    in_band = (c_io >= qi_io) & (c_io <= qi_io + 2 * WINDOW)
    c_row = lax.broadcasted_iota(jnp.int32, (1, n_loc), 1)
    g_io = lax.shift_right_logical(lax.broadcasted_iota(jnp.int32, (rows, 1), 0), BLOCK.bit_length() - 1)
    nt = (((1,), (1,)), ((), ()))
    for kvh in range(N_KV_A):
        hs = slice(kvh * HEAD_DIM, (kvh + 1) * HEAD_DIM)
        kband = jnp.concatenate([kp_ref[:, hs], km_ref[:, hs], kn_ref[:, hs]], axis=0)
        vband = jnp.concatenate([vp_ref[:, hs], vm_ref[:, hs], vn_ref[:, hs]], axis=0)
        kc = kc_ref[:, hs]
        vc = vc_ref[:, hs]
        sink = jnp.zeros((rows, 1), F32)
        for g in range(GQA_GROUP):
            sink = jnp.where(g_io == g, sink_ref[kvh * GQA_GROUP + g] * LOG2E, sink)
        for qi in range(sub):
            blk = nq * sub + qi
            lo = jnp.where(blk == 0, BLOCK, 0)
            hi = jnp.where(blk == n_blocks - 1, 2 * BLOCK - 1, n_loc - 1)
            valid = in_band & ((c_row >= lo) & (c_row <= hi))
            q4 = jnp.concatenate(
                [q_ref[qi * BLOCK:(qi + 1) * BLOCK, (kvh * GQA_GROUP + g) * HEAD_DIM:(kvh * GQA_GROUP + g + 1) * HEAD_DIM]
                 for g in range(GQA_GROUP)], axis=0)
            s_loc = lax.dot_general(q4, kband[qi * BLOCK:qi * BLOCK + n_loc], nt, preferred_element_type=F32)
            s_loc = jnp.where(valid, s_loc, NEG_INF)
            s_ctx = lax.dot_general(q4, kc, nt, preferred_element_type=F32)
            m = jnp.maximum(jnp.maximum(jnp.max(s_loc, axis=-1, keepdims=True),
                                        jnp.max(s_ctx, axis=-1, keepdims=True)), sink)
            e_loc = jnp.exp2(s_loc - m)
            e_ctx = jnp.exp2(s_ctx - m)
            den = (jnp.sum(e_loc, axis=-1, keepdims=True) + jnp.sum(e_ctx, axis=-1, keepdims=True)
                   + jnp.exp2(sink - m))
            o = (jnp.dot(e_loc.astype(BF16), vband[qi * BLOCK:qi * BLOCK + n_loc], preferred_element_type=F32)
                 + jnp.dot(e_ctx.astype(BF16), vc, preferred_element_type=F32)) * (1.0 / den)
            for g in range(GQA_GROUP):
                hd = kvh * GQA_GROUP + g
                o_ref[qi * BLOCK:(qi + 1) * BLOCK, hd * HEAD_DIM:(hd + 1) * HEAD_DIM] = (
                    o[g * BLOCK:(g + 1) * BLOCK].astype(BF16))


def _attn_a_call(proj, cproj, sink, *, batch, seq, ctx_len, q_col, k_col, v_col, ck_col, cv_col):
    nqb = seq // QB_A
    sub = QB_A // BLOCK
    n_blocks = seq // BLOCK

    def prev_map(b, n, col):
        return (b * n_blocks + jnp.maximum(n * sub - 1, 0), col // W_KA)

    def next_map(b, n, col):
        return (b * n_blocks + jnp.minimum(n * sub + sub, n_blocks - 1), col // W_KA)

    def main_map(b, n, col):
        return (b * nqb + n, col // W_KA)

    edge = lambda f, col: pl.BlockSpec((BLOCK, W_KA), functools.partial(f, col=col))
    main = lambda col: pl.BlockSpec((QB_A, W_KA), functools.partial(main_map, col=col))
    ctxs = lambda col: pl.BlockSpec((ctx_len, W_KA), lambda b, n: (b, col // W_KA))
    return pl.pallas_call(
        functools.partial(_attn_a_kernel, n_blocks=n_blocks),
        grid=(batch, nqb),
        in_specs=[pl.BlockSpec(memory_space=pltpu.SMEM),
                  pl.BlockSpec((QB_A, W_QA), lambda b, n: (b * nqb + n, q_col // W_QA)),
                  edge(prev_map, k_col), main(k_col), edge(next_map, k_col),
                  edge(prev_map, v_col), main(v_col), edge(next_map, v_col),
                  ctxs(ck_col), ctxs(cv_col)],
        out_specs=pl.BlockSpec((QB_A, W_QA), lambda b, n: (b * nqb + n, 0)),
        out_shape=jax.ShapeDtypeStruct((batch * seq, W_QA), BF16),
        compiler_params=pltpu.CompilerParams(dimension_semantics=("arbitrary",) * 2,
                                             vmem_limit_bytes=VMEM_LIMIT_BYTES),
        name="attn_a",
    )(sink, proj, proj, proj, proj, proj, proj, proj, cproj, cproj)


def _attn_b_kernel(q_ref, kp_ref, km_ref, kn_ref, vp_ref, vm_ref, vn_ref, kc_ref, vc_ref, bias_ref, o_ref,
                   *, grid_rows):
    n = pl.program_id(1)
    nq = ROW_GROUP_B * GRID_W
    nk = 3 * nq
    n_ctx = kc_ref.shape[0]
    heads = q_ref.shape[1] // HEAD_DIM
    r_io = lax.broadcasted_iota(jnp.int32, (nq, nk), 0)
    c_io = lax.broadcasted_iota(jnp.int32, (nq, nk), 1)
    log_w = GRID_W.bit_length() - 1
    r_abs = n * ROW_GROUP_B + lax.shift_right_logical(r_io, log_w)
    q_col = r_io & (GRID_W - 1)
    kr_abs = (n - 1) * ROW_GROUP_B + lax.shift_right_logical(c_io, log_w)
    k_col = c_io & (GRID_W - 1)
    r_start = jnp.clip(r_abs - NB_ROWS // 2, 0, grid_rows - NB_ROWS)
    c_start = jnp.clip(q_col - NB_COLS // 2, 0, GRID_W - NB_COLS)
    valid = ((kr_abs >= r_start) & (kr_abs < r_start + NB_ROWS)
             & (k_col >= c_start) & (k_col < c_start + NB_COLS))
    cap = jnp.where(valid, F32_MAX, NEG_INF)
    for hd in range(heads):
        sl = slice(hd * HEAD_DIM, (hd + 1) * HEAD_DIM)
        q = q_ref[:, sl]
        keys = jnp.concatenate([kp_ref[:, sl], km_ref[:, sl], kn_ref[:, sl]], axis=0)
        vals = jnp.concatenate([vp_ref[:, sl], vm_ref[:, sl], vn_ref[:, sl]], axis=0)
        s_nb = lax.dot_general(q, keys, (((1,), (1,)), ((), ())), preferred_element_type=F32)
        s_nb = jnp.minimum(s_nb + bias_ref[hd], cap)
        s_ctx = lax.dot_general(q, kc_ref[:, sl], (((1,), (1,)), ((), ())), preferred_element_type=F32)
        m = jnp.maximum(jnp.max(s_nb, axis=-1, keepdims=True), jnp.max(s_ctx, axis=-1, keepdims=True))
        e_nb = jnp.exp2(s_nb - m)
        e_ctx = jnp.exp2(s_ctx - m)
        den = jnp.sum(e_nb, axis=-1, keepdims=True) + jnp.sum(e_ctx, axis=-1, keepdims=True)
        o = (jnp.dot(e_nb.astype(BF16), vals, preferred_element_type=F32)
             + jnp.dot(e_ctx.astype(BF16), vc_ref[:, sl], preferred_element_type=F32)) * (1.0 / den)
        o_ref[:, sl] = o.astype(BF16)


def _attn_b_call(proj, cproj, bias, *, batch, seq, ctx_len, q_col, k_col, v_col, ck_col, cv_col):
    nq = ROW_GROUP_B * GRID_W
    ng = seq // nq
    hw = W_B
    hg = W_B // hw

    def row_spec(col, shift):
        def imap(b, n, h):
            return (b * ng + jnp.clip(n + shift, 0, ng - 1), col // hw + h)
        return pl.BlockSpec((nq, hw), imap)

    ctxs = lambda col: pl.BlockSpec((ctx_len, hw), lambda b, n, h: (b, col // hw + h))
    return pl.pallas_call(
        functools.partial(_attn_b_kernel, grid_rows=seq // GRID_W),
        grid=(batch, ng, hg),
        in_specs=[row_spec(q_col, 0),
                  row_spec(k_col, -1), row_spec(k_col, 0), row_spec(k_col, 1),
                  row_spec(v_col, -1), row_spec(v_col, 0), row_spec(v_col, 1),
                  ctxs(ck_col), ctxs(cv_col),
                  pl.BlockSpec((hw // HEAD_DIM, nq, bias.shape[2]), lambda b, n, h: (h, 0, 0))],
        out_specs=pl.BlockSpec((nq, hw), lambda b, n, h: (b * ng + n, h)),
        out_shape=jax.ShapeDtypeStruct((batch * seq, W_B), BF16),
        compiler_params=pltpu.CompilerParams(dimension_semantics=("arbitrary",) * 3,
                                             vmem_limit_bytes=VMEM_LIMIT_BYTES),
        name="attn_b",
    )(proj, proj, proj, proj, proj, proj, proj, cproj, cproj, bias)


def _mix_kernel(oa_ref, ob_ref, ga_ref, gb_ref, wa_ref, wb_ref, z_ref):
    ya = jnp.dot(oa_ref[...], wa_ref[...], preferred_element_type=F32)
    yb = jnp.dot(ob_ref[...], wb_ref[...], preferred_element_type=F32)
    z_ref[...] = (ga_ref[...].astype(F32) * ya + gb_ref[...].astype(F32) * yb).astype(BF16)


def _mix_call(oa, ob, proj, wa, wb, *, tm):
    n = oa.shape[0]
    d = wa.shape[1]
    const = lambda shape: pl.BlockSpec(shape, lambda i: (0, 0), pipeline_mode=pl.Buffered(1))
    return pl.pallas_call(
        _mix_kernel,
        grid=(n // tm,),
        in_specs=[pl.BlockSpec((tm, oa.shape[1]), lambda i: (i, 0)),
                  pl.BlockSpec((tm, ob.shape[1]), lambda i: (i, 0)),
                  pl.BlockSpec((tm, d), lambda i: (i, 0)),
                  pl.BlockSpec((tm, d), lambda i: (i, 1)),
                  const(wa.shape), const(wb.shape)],
        out_specs=pl.BlockSpec((tm, d), lambda i: (i, 0)),
        out_shape=jax.ShapeDtypeStruct((n, d), BF16),
        compiler_params=pltpu.CompilerParams(dimension_semantics=("arbitrary",),
                                             vmem_limit_bytes=VMEM_LIMIT_BYTES),
        name="branch_mix",
    )(oa, ob, proj, proj, wa, wb)


def _outnorm_kernel(a_ref, w_ref, x_ref, gt_ref, g_ref, o_ref, acc_ref):
    k = pl.program_id(1)
    part = jnp.dot(a_ref[...], w_ref[...], preferred_element_type=F32)

    @pl.when(k == 0)
    def _():
        acc_ref[...] = part

    @pl.when(k > 0)
    def _():
        acc_ref[...] += part

    @pl.when(k == pl.num_programs(1) - 1)
    def _():
        o_ref[...] = x_ref[...] + gt_ref[0] * _rms(acc_ref[...], g_ref[...])


def _outnorm_call(a, w, x2, gt, g, *, tm, tk, rows_per_mod):
    n, kdim = a.shape
    d = w.shape[1]
    mod_tiles = rows_per_mod // tm
    return pl.pallas_call(
        _outnorm_kernel,
        grid=(n // tm, kdim // tk),
        in_specs=[pl.BlockSpec((tm, tk), lambda i, k: (i, k)),
                  pl.BlockSpec((tk, d), lambda i, k: (k, 0)),
                  pl.BlockSpec((tm, d), lambda i, k: (i, 0)),
                  pl.BlockSpec((1, 1, d), lambda i, k: (i // mod_tiles, 0, 0)),
                  pl.BlockSpec((1, d), lambda i, k: (0, 0))],
        out_specs=pl.BlockSpec((tm, d), lambda i, k: (i, 0)),
        out_shape=jax.ShapeDtypeStruct((n, d), F32),
        scratch_shapes=[pltpu.VMEM((tm, d), F32)],
        compiler_params=pltpu.CompilerParams(dimension_semantics=("arbitrary", "arbitrary"),
                                             vmem_limit_bytes=VMEM_LIMIT_BYTES),
        name="out_norm",
    )(a, w, x2, gt, g)


def _rowmm_norm_kernel(a_ref, w_ref, x_ref, gt_ref, g_ref, o_ref, *, n_chunks):
    cm = a_ref.shape[0] // n_chunks
    y = jnp.dot(a_ref[0:cm], w_ref[...], preferred_element_type=F32)
    for c in range(n_chunks):
        nxt = None
        if c + 1 < n_chunks:
            nxt = jnp.dot(a_ref[(c + 1) * cm:(c + 2) * cm], w_ref[...], preferred_element_type=F32)
        rows = slice(c * cm, (c + 1) * cm)
        o_ref[rows] = x_ref[rows] + gt_ref[0] * _rms(y, g_ref[...])
        y = nxt


def _rowmm_norm_call(a, w, x2, gt, g, *, tm, n_chunks, rows_per_mod):
    n, kdim = a.shape
    d = w.shape[1]
    mod_tiles = rows_per_mod // tm
    return pl.pallas_call(
        functools.partial(_rowmm_norm_kernel, n_chunks=n_chunks),
        grid=(n // tm,),
        in_specs=[pl.BlockSpec((tm, kdim), lambda i: (i, 0)),
                  pl.BlockSpec((kdim, d), lambda i: (0, 0), pipeline_mode=pl.Buffered(1)),
                  pl.BlockSpec((tm, d), lambda i: (i, 0)),
                  pl.BlockSpec((1, 1, d), lambda i: (i // mod_tiles, 0, 0)),
                  pl.BlockSpec((1, d), lambda i: (0, 0))],
        out_specs=pl.BlockSpec((tm, d), lambda i: (i, 0)),
        out_shape=jax.ShapeDtypeStruct((n, d), F32),
        compiler_params=pltpu.CompilerParams(dimension_semantics=("arbitrary",),
                                             vmem_limit_bytes=VMEM_LIMIT_RESIDENT_BYTES),
        name="rowmm_norm",
    )(a, w, x2, gt, g)


def _ffn_up_kernel(xp_ref, x_ref, xn_ref, sc_ref, sh_ref, g_ref, wa_ref, wg_ref, cwa_ref, cwg_ref, cba_ref, cbg_ref,
                   o_ref, h_ref, ua_ref, ug_ref, *, nj, n_tiles, tiles_per_seq):
    t = pl.program_id(0)
    tm = x_ref.shape[0]
    halo = xp_ref.shape[0]
    ext = tm + 2 * halo
    margin = 8

    @pl.when(t == 0)
    def _():
        ua_ref[...] = jnp.zeros_like(ua_ref)
        ug_ref[...] = jnp.zeros_like(ug_ref)

    @pl.when((t < n_tiles) & (t % nj == 0))
    def _():
        def mod(x):
            return (_rms(x, g_ref[...]) * (1 + sc_ref[0]) + sh_ref[0]).astype(BF16)
        h_ref[0:halo] = mod(xp_ref[...])
        h_ref[halo:halo + tm] = mod(x_ref[...])
        h_ref[halo + tm:] = mod(xn_ref[...])

    slot = t % 2
    prev = 1 - slot
    ip = jnp.maximum(t - 1, 0) // nj
    first_row = jnp.where((ip % tiles_per_seq) == 0, 0, -1)
    last_row = jnp.where((ip % tiles_per_seq) == tiles_per_seq - 1, tm - 1, -1)
    row = lax.broadcasted_iota(jnp.int32, (tm, 1), 0)

    def epilogue(lo, hi):
        cm = hi - lo
        rowc = row[lo:hi]

        def conv(u_ref, cw_ref, cb_ref):
            win = u_ref[prev, pl.ds(halo + lo - margin, cm + 2 * margin), :]
            up = pltpu.roll(win, 1, 0)[margin:margin + cm]
            un = pltpu.roll(win, cm + 2 * margin - 1, 0)[margin:margin + cm]
            up = jnp.where(rowc == first_row, 0.0, up)
            un = jnp.where(rowc == last_row, 0.0, un)
            out = cb_ref[...] + up * cw_ref[0:1]
            out = out + win[margin:margin + cm] * cw_ref[1:2]
            return out + un * cw_ref[2:3]

        a = conv(ua_ref, cwa_ref, cba_ref)
        gte = conv(ug_ref, cwg_ref, cbg_ref)
        o_ref[lo:hi] = (jax.nn.silu(gte) * a).astype(BF16)

    half = ext // 2
    third = (tm // 3) // BF16_ROWS * BF16_ROWS
    chunks = [(0, third), (third, 2 * third), (2 * third, tm)]
    ua_ref[slot, 0:half] = jnp.dot(h_ref[0:half], wa_ref[...], preferred_element_type=F32)
    epilogue(*chunks[0])
    ua_ref[slot, half:ext] = jnp.dot(h_ref[half:ext], wa_ref[...], preferred_element_type=F32)
    epilogue(*chunks[1])
    ug_ref[slot, 0:half] = jnp.dot(h_ref[0:half], wg_ref[...], preferred_element_type=F32)
    epilogue(*chunks[2])
    ug_ref[slot, half:ext] = jnp.dot(h_ref[half:ext], wg_ref[...], preferred_element_type=F32)


def _ffn_up_call(x2, sc, sh, g, w_up, conv_w, conv_b, *, tm, tn, seq):
    n, d = x2.shape
    dff = w_up.shape[1] // 2
    nj = dff // tn
    halo = BF16_ROWS
    hb = tm // halo
    n_halo = n // halo
    tiles_per_seq = seq // tm
    n_tiles = (n // tm) * nj
    assert (tm + 2 * halo) % (2 * BF16_ROWS) == 0

    def cur(t):
        tc = jnp.minimum(t, n_tiles - 1)
        return tc // nj, tc % nj

    def prv(t):
        tp = jnp.maximum(t - 1, 0)
        return tp // nj, tp % nj

    return pl.pallas_call(
        functools.partial(_ffn_up_kernel, nj=nj, n_tiles=n_tiles, tiles_per_seq=tiles_per_seq),
        grid=(n_tiles + 1,),
        in_specs=[pl.BlockSpec((halo, d), lambda t: (jnp.maximum(cur(t)[0] * hb - 1, 0), 0)),
                  pl.BlockSpec((tm, d), lambda t: (cur(t)[0], 0)),
                  pl.BlockSpec((halo, d), lambda t: (jnp.minimum((cur(t)[0] + 1) * hb, n_halo - 1), 0)),
                  pl.BlockSpec((1, 1, d), lambda t: (cur(t)[0] // tiles_per_seq, 0, 0)),
                  pl.BlockSpec((1, 1, d), lambda t: (cur(t)[0] // tiles_per_seq, 0, 0)),
                  pl.BlockSpec((1, d), lambda t: (0, 0)),
                  pl.BlockSpec((d, tn), lambda t: (0, cur(t)[1])),
                  pl.BlockSpec((d, tn), lambda t: (0, cur(t)[1] + nj)),
                  pl.BlockSpec((CONV_W, tn), lambda t: (0, prv(t)[1])),
                  pl.BlockSpec((CONV_W, tn), lambda t: (0, prv(t)[1] + nj)),
                  pl.BlockSpec((1, tn), lambda t: (0, prv(t)[1])),
                  pl.BlockSpec((1, tn), lambda t: (0, prv(t)[1] + nj))],
        out_specs=pl.BlockSpec((tm, tn), lambda t: prv(t)),
        out_shape=jax.ShapeDtypeStruct((n, dff), BF16),
        scratch_shapes=[pltpu.VMEM((tm + 2 * halo, d), BF16),
                        pltpu.VMEM((2, tm + 2 * halo, tn), F32),
                        pltpu.VMEM((2, tm + 2 * halo, tn), F32)],
        compiler_params=pltpu.CompilerParams(dimension_semantics=("arbitrary",),
                                             vmem_limit_bytes=VMEM_LIMIT_BYTES),
        name="ffn_up",
    )(x2, x2, x2, sc, sh, g, w_up, w_up, conv_w, conv_w, conv_b, conv_b)


def _rope_tables(seq):
    t = jnp.arange(seq)
    half = HEAD_DIM // 4
    inv = ROPE_BASE ** (-jnp.arange(half, dtype=F32) / half)

    def cs(pos):
        ang = pos.astype(F32)[:, None] * inv[None, :]
        c, s = jnp.cos(ang), jnp.sin(ang)
        return jnp.concatenate([c, c], axis=1), jnp.concatenate([-s, s], axis=1)

    cr, sr = cs(t // GRID_W)
    cc, sn = cs(t % GRID_W)
    return jnp.concatenate([cr, cc], axis=1), jnp.concatenate([sr, sn], axis=1)


def _pick_tile(n, pref):
    while n % pref:
        pref //= 2
    return pref


def kernel(x, c, ctx, c_ctx, w_mod, b_mod, g_attn_pre, g_attn_post, g_ffn_pre, g_ffn_post, w_in, sink_a, rpb_b,
           w_br_a, w_br_b, w_o, w_up, conv_w, conv_b, w_down):
    batch, seq, d = x.shape
    ctx_len = ctx.shape[1]
    assert w_mod.shape[0] == 1 and d == 16 * HEAD_DIM
    assert seq % QB_A == 0 and seq % (ROW_GROUP_B * GRID_W) == 0 and WINDOW == BLOCK
    n = batch * seq
    x2 = x.reshape(n, d)

    pad = (-(batch + 1)) % 8
    cs = jnp.concatenate([c, c_ctx[None, :], jnp.zeros((pad, d), F32)], axis=0)
    mod = _mod_call(cs, w_mod[0], b_mod[0][None, :])
    sh1, sc1, gt1, sh2, sc2, gt2 = [mod[:batch, k * d:(k + 1) * d].reshape(batch, 1, d) for k in range(6)]
    csh1 = mod[batch:batch + 1, 0:d].reshape(1, 1, d)
    csc1 = mod[batch:batch + 1, d:2 * d].reshape(1, 1, d)

    o_qa, o_ka, o_va, o_qb, o_kb, o_vb, o_ga = 0, W_QA, W_QA + W_KA, W_QA + 2 * W_KA, W_QA + 2 * W_KA + W_B, \
        W_QA + 2 * W_KA + 2 * W_B, W_QA + 2 * W_KA + 3 * W_B
    wi = w_in[0]
    w_perm = jnp.concatenate([wi[:, o_ga:], wi[:, o_qa:o_ka], wi[:, o_qb:o_ga], wi[:, o_ka:o_qb]], axis=1).astype(BF16)
    n_gate_cols = 2 * d
    n_tiles = w_perm.shape[1] // TN
    j0q = n_gate_cols // TN
    c_qa = 0
    c_qb = c_qa + W_QA
    c_kb = c_qb + W_B
    c_vb = c_kb + W_B
    c_ka = c_vb + W_B
    c_va = c_ka + W_KA
    cos, sin = _rope_tables(seq)
    tm = _pick_tile(seq, 1024)
    gates, h1 = _gate_call(x2, sc1, sh1, g_attn_pre, w_perm, tm=tm, tn=TN_GATE, n_cols=n_gate_cols, rows_per_mod=seq)
    proj = _qkv_call(h1, w_perm, cos, sin, tm=_pick_tile(seq, 2048), j0=j0q, nj=n_tiles - j0q)
    nc = batch * ctx_len
    j0c = j0q + c_kb // TN
    cproj = _inproj_call(ctx.reshape(nc, d), csc1, csh1, g_attn_pre, w_perm,
                         jnp.ones((nc, HEAD_DIM), F32), jnp.zeros((nc, HEAD_DIM), F32),
                         tm=nc, rows_per_mod=nc, j0=j0c, nj=n_tiles - j0c)
    cc_kb, cc_vb, cc_ka, cc_va = 0, W_B, 2 * W_B, 2 * W_B + W_KA

    oa = _attn_a_call(proj, cproj, sink_a[0], batch=batch, seq=seq, ctx_len=ctx_len,
                      q_col=c_qa, k_col=c_ka, v_col=c_va, ck_col=cc_ka, cv_col=cc_va)
    bias = _rpb_call(rpb_b[0])
    ob = _attn_b_call(proj, cproj, bias, batch=batch, seq=seq, ctx_len=ctx_len,
                      q_col=c_qb, k_col=c_kb, v_col=c_vb, ck_col=cc_kb, cv_col=cc_vb)

    tm2 = _pick_tile(seq, 512)
    z = _mix_call(oa, ob, gates, w_br_a[0].astype(BF16), w_br_b[0].astype(BF16), tm=tm2)
    x1 = _outnorm_call(z, w_o[0].astype(BF16), x2, gt1, g_attn_post, tm=tm2, tk=d, rows_per_mod=seq)

    dff = w_down.shape[1]
    act = _ffn_up_call(x1, sc2, sh2, g_ffn_pre, w_up[0].astype(BF16), conv_w[0], conv_b[0][None, :],
                       tm=tm, tn=TN, seq=seq)
    out = _rowmm_norm_call(act, w_down[0].astype(BF16), x1, gt2, g_ffn_post, tm=_pick_tile(seq, 256), n_chunks=2,
                           rows_per_mod=seq)
    return out.reshape(batch, seq, d)
```

```python
import functools

import jax
import jax.numpy as jnp
from jax import lax
from jax.experimental import pallas as pl
from jax.experimental.pallas import tpu as pltpu

F32 = jnp.float32
BF16 = jnp.bfloat16

GRID_W = 64
HEAD_DIM = 128
N_HEADS_A = 8
N_KV_A = 2
GQA_GROUP = N_HEADS_A // N_KV_A
WINDOW = 128
BLOCK = 128
N_HEADS_B = 8
NB_ROWS = 8
NB_COLS = 16
CONV_W = 3
ROPE_BASE = 10000.0
EPS = 1e-6
NEG_INF = -1e30
ATTN_SCALE = HEAD_DIM ** -0.5
LOG2E = 1.4426950408889634
Q_SCALE = ATTN_SCALE * LOG2E
F32_MAX = 3.4028234663852886e38

W_QA = N_HEADS_A * HEAD_DIM
W_KA = N_KV_A * HEAD_DIM
W_B = N_HEADS_B * HEAD_DIM

LANES = 128
BF16_ROWS = 16
VMEM_LIMIT_BYTES = 56 * 1024 * 1024
VMEM_LIMIT_RESIDENT_BYTES = 60 * 1024 * 1024

TN = 512
TN_GATE = 1024
ROW_CHUNK = 256
ROW_GROUP_B = 4
QB_A = 512


def _rms(x, g):
    return (x * lax.rsqrt(jnp.mean(x * x, axis=-1, keepdims=True) + EPS)) * g


def _mod_kernel(c_ref, w_ref, b_ref, o_ref):
    a = jax.nn.silu(c_ref[...]).astype(BF16)
    o_ref[...] = jnp.dot(a, w_ref[...].astype(BF16), preferred_element_type=F32) + b_ref[...]


def _mod_call(cs, w, b):
    m, d = cs.shape
    n = w.shape[1]
    tn = 1536 if n % 1536 == 0 else n
    return pl.pallas_call(
        _mod_kernel,
        grid=(n // tn,),
        in_specs=[pl.BlockSpec((m, d), lambda j: (0, 0)),
                  pl.BlockSpec((d, tn), lambda j: (0, j)),
                  pl.BlockSpec((1, tn), lambda j: (0, j))],
        out_specs=pl.BlockSpec((m, tn), lambda j: (0, j)),
        out_shape=jax.ShapeDtypeStruct((m, n), F32),
        compiler_params=pltpu.CompilerParams(dimension_semantics=("arbitrary",),
                                             vmem_limit_bytes=VMEM_LIMIT_BYTES),
        name="mod",
    )(cs, w, b)


def _rpb_kernel(rpb_ref, o_ref):
    h = pl.program_id(0)
    n_dr = 2 * NB_ROWS - 1
    n_dc = 2 * NB_COLS - 1
    qc = lax.broadcasted_iota(jnp.int32, (GRID_W, LANES), 0)
    lane = lax.broadcasted_iota(jnp.int32, (GRID_W, LANES), 1)
    kc = lane & (GRID_W - 1)
    dc_idx = jnp.clip(kc - qc, -(NB_COLS - 1), NB_COLS - 1) + (NB_COLS - 1)
    base = h * (n_dr * n_dc)

    def toeplitz(dr):
        def body(dc, acc):
            return jnp.where(dc_idx == dc, rpb_ref[base + dr * n_dc + dc] * LOG2E, acc)
        return lax.fori_loop(0, n_dc, body, jnp.zeros((GRID_W, LANES), F32))

    tw = [toeplitz(dr) for dr in range(n_dr)]
    pair = [jnp.where(lane < GRID_W, tw[d], tw[d + 1]) for d in range(n_dr - 1)]
    n_kr = 3 * ROW_GROUP_B
    dr0 = NB_ROWS - 1 - ROW_GROUP_B
    for rq in range(ROW_GROUP_B):
        for p in range(n_kr // 2):
            o_ref[0, rq * GRID_W:(rq + 1) * GRID_W, p * LANES:(p + 1) * LANES] = pair[2 * p - rq + dr0]


def _rpb_call(rpb):
    nh = rpb.shape[0]
    nq = ROW_GROUP_B * GRID_W
    nk = 3 * ROW_GROUP_B * GRID_W
    assert 2 * ROW_GROUP_B == NB_ROWS
    return pl.pallas_call(
        _rpb_kernel,
        grid=(nh,),
        in_specs=[pl.BlockSpec(memory_space=pltpu.SMEM)],
        out_specs=pl.BlockSpec((1, nq, nk), lambda h: (h, 0, 0)),
        out_shape=jax.ShapeDtypeStruct((nh, nq, nk), F32),
        compiler_params=pltpu.CompilerParams(dimension_semantics=("arbitrary",)),
        name="rpb_table",
    )(rpb.reshape(-1))


def _rope(t, cos, sin, first_half):
    partner = jnp.where(first_half, pltpu.roll(t, LANES - 32, 1), pltpu.roll(t, 32, 1))
    return t * cos + partner * sin


def _row_chunked_matmul(h_ref, w_ref, epilogue, between=None):
    n_chunks = h_ref.shape[0] // ROW_CHUNK
    acc = jnp.dot(h_ref[0:ROW_CHUNK], w_ref[...], preferred_element_type=F32)
    for c in range(n_chunks):
        nxt = None
        if c + 1 < n_chunks:
            nxt = jnp.dot(h_ref[(c + 1) * ROW_CHUNK:(c + 2) * ROW_CHUNK], w_ref[...], preferred_element_type=F32)
        epilogue(slice(c * ROW_CHUNK, (c + 1) * ROW_CHUNK), acc)
        if c == 0 and between is not None:
            between()
        acc = nxt


def _gate_kernel(x_ref, sc_ref, sh_ref, g_ref, w_ref, o_ref, hout_ref, hs_ref, *, n_rows):
    i = pl.program_id(0)
    j = pl.program_id(1)
    pr = x_ref.shape[0]

    def norm_piece():
        h = (_rms(x_ref[...], g_ref[...]) * (1 + sc_ref[0]) + sh_ref[0]).astype(BF16)
        hout_ref[...] = h
        hs_ref[i % 2, pl.ds(pl.multiple_of(j * pr, pr), pr), :] = h

    def epilogue(rows, acc):
        o_ref[rows] = jax.nn.sigmoid(acc).astype(BF16)

    @pl.when(i == 0)
    def _():
        norm_piece()

    @pl.when((i > 0) & (i < n_rows))
    def _():
        _row_chunked_matmul(hs_ref.at[(i + 1) % 2], w_ref, epilogue, between=norm_piece)

    @pl.when(i == n_rows)
    def _():
        _row_chunked_matmul(hs_ref.at[(i + 1) % 2], w_ref, epilogue)


def _gate_call(x2, sc, sh, g, w, *, tm, tn, n_cols, rows_per_mod):
    n, d = x2.shape
    n_rows = n // tm
    nj = n_cols // tn
    pr = tm // nj
    assert pr * nj == tm and pr % BF16_ROWS == 0
    mod_tiles = rows_per_mod // tm
    last = n_rows * nj - 1
    piece_map = lambda i, j: (jnp.minimum(i * nj + j, last), 0)
    mod_map = lambda i, j: (jnp.minimum(i, n_rows - 1) // mod_tiles, 0, 0)
    return pl.pallas_call(
        functools.partial(_gate_kernel, n_rows=n_rows),
        grid=(n_rows + 1, nj),
        in_specs=[pl.BlockSpec((pr, d), piece_map),
                  pl.BlockSpec((1, 1, d), mod_map),
                  pl.BlockSpec((1, 1, d), mod_map),
                  pl.BlockSpec((1, d), lambda i, j: (0, 0)),
                  pl.BlockSpec((d, tn), lambda i, j: (0, j))],
        out_specs=[pl.BlockSpec((tm, tn), lambda i, j: (jnp.maximum(i - 1, 0), jnp.where(i > 0, j, 0))),
                   pl.BlockSpec((pr, d), piece_map)],
        out_shape=[jax.ShapeDtypeStruct((n, n_cols), BF16), jax.ShapeDtypeStruct((n, d), BF16)],
        scratch_shapes=[pltpu.VMEM((2, tm, d), BF16)],
        compiler_params=pltpu.CompilerParams(dimension_semantics=("arbitrary", "arbitrary"),
                                             vmem_limit_bytes=VMEM_LIMIT_BYTES),
        name="gate_proj",
    )(x2, sc, sh, g, w)


def _qkv_kernel(h_ref, w_ref, cos_ref, sin_ref, o_ref, *, j0, n_gate):
    _proj_tile(h_ref, w_ref, cos_ref, sin_ref, o_ref, pl.program_id(1) + j0, n_gate)


def _qkv_call(h, w, cos, sin, *, tm, j0, nj):
    n, d = h.shape
    pos_tiles = cos.shape[0] // tm
    return pl.pallas_call(
        functools.partial(_qkv_kernel, j0=j0, n_gate=2 * d // TN),
        grid=(n // tm, nj),
        in_specs=[pl.BlockSpec((tm, d), lambda i, j: (i, 0)),
                  pl.BlockSpec((d, TN), lambda i, j: (0, j + j0)),
                  pl.BlockSpec((tm, HEAD_DIM), lambda i, j: (i % pos_tiles, 0)),
                  pl.BlockSpec((tm, HEAD_DIM), lambda i, j: (i % pos_tiles, 0))],
        out_specs=pl.BlockSpec((tm, TN), lambda i, j: (i, j)),
        out_shape=jax.ShapeDtypeStruct((n, nj * TN), BF16),
        compiler_params=pltpu.CompilerParams(dimension_semantics=("arbitrary", "arbitrary"),
                                             vmem_limit_bytes=VMEM_LIMIT_BYTES),
        name="qkv_proj",
    )(h, w, cos, sin)


def _inproj_kernel(x_ref, sc_ref, sh_ref, g_ref, w_ref, cos_ref, sin_ref, o_ref, h_ref, *, j0):
    j = pl.program_id(1)

    @pl.when(j == 0)
    def _():
        h = _rms(x_ref[...], g_ref[...]) * (1 + sc_ref[0]) + sh_ref[0]
        h_ref[...] = h.astype(BF16)

    _proj_tile(h_ref, w_ref, cos_ref, sin_ref, o_ref, j + j0, 2 * x_ref.shape[1] // TN)


def _proj_tile(h_ref, w_ref, cos_ref, sin_ref, o_ref, jj, n_gate):
    lane = lax.broadcasted_iota(jnp.int32, (1, HEAD_DIM), 1)
    first_half = (lane & 63) < 32

    def row_chunks(epilogue):
        _row_chunked_matmul(h_ref, w_ref, epilogue)

    def rope_heads(rows, acc, n_heads, scale):
        cos = cos_ref[rows]
        sin = sin_ref[rows]
        for hd in range(n_heads):
            sl = slice(hd * HEAD_DIM, (hd + 1) * HEAD_DIM)
            r = _rope(acc[:, sl], cos, sin, first_half)
            if scale is not None:
                r = r * scale
            o_ref[rows, sl] = r.astype(BF16)

    @pl.when((jj >= n_gate) & (jj < n_gate + 2))
    def _():
        row_chunks(lambda rows, acc: rope_heads(rows, acc, TN // HEAD_DIM, Q_SCALE))

    @pl.when((jj >= n_gate + 2) & (jj < n_gate + 4))
    def _():
        def epilogue(rows, acc):
            o_ref[rows] = (acc * Q_SCALE).astype(BF16)
        row_chunks(epilogue)

    @pl.when((jj >= n_gate + 4) & (jj < n_gate + 8))
    def _():
        def epilogue(rows, acc):
            o_ref[rows] = acc.astype(BF16)
        row_chunks(epilogue)

    @pl.when(jj == n_gate + 8)
    def _():
        def epilogue(rows, acc):
            rope_heads(rows, acc, N_KV_A, None)
            o_ref[rows, W_KA:] = acc[:, W_KA:].astype(BF16)
        row_chunks(epilogue)


def _inproj_call(x2, sc, sh, g, w, cos, sin, *, tm, rows_per_mod, j0, nj):
    n, d = x2.shape
    pos_tiles = cos.shape[0] // tm
    mod_tiles = rows_per_mod // tm
    return pl.pallas_call(
        functools.partial(_inproj_kernel, j0=j0),
        grid=(n // tm, nj),
        in_specs=[pl.BlockSpec((tm, d), lambda i, j: (i, 0)),
                  pl.BlockSpec((1, 1, d), lambda i, j: (i // mod_tiles, 0, 0)),
                  pl.BlockSpec((1, 1, d), lambda i, j: (i // mod_tiles, 0, 0)),
                  pl.BlockSpec((1, d), lambda i, j: (0, 0)),
                  pl.BlockSpec((d, TN), lambda i, j: (0, j + j0)),
                  pl.BlockSpec((tm, HEAD_DIM), lambda i, j: (i % pos_tiles, 0)),
                  pl.BlockSpec((tm, HEAD_DIM), lambda i, j: (i % pos_tiles, 0))],
        out_specs=pl.BlockSpec((tm, TN), lambda i, j: (i, j)),
        out_shape=jax.ShapeDtypeStruct((n, nj * TN), BF16),
        scratch_shapes=[pltpu.VMEM((tm, d), BF16)],
        compiler_params=pltpu.CompilerParams(dimension_semantics=("arbitrary", "arbitrary"),
                                             vmem_limit_bytes=VMEM_LIMIT_BYTES),
        name="in_proj",
    )(x2, sc, sh, g, w, cos, sin)


def _attn_a_kernel(sink_ref, q_ref, kp_ref, km_ref, kn_ref, vp_ref, vm_ref, vn_ref, kc_ref, vc_ref, o_ref,
                   *, n_blocks):
    nq = pl.program_id(1)
    sub = QB_A // BLOCK
    rows = GQA_GROUP * BLOCK
    n_loc = 3 * BLOCK
    r_io = lax.broadcasted_iota(jnp.int32, (rows, n_loc), 0)
    c_io = lax.broadcasted_iota(jnp.int32, (rows, n_loc), 1)
    qi_io = r_io & (BLOCK - 1)
    in_band = (c_io >= qi_io) & (c_io <= qi_io + 2 * WINDOW)
    c_row = lax.broadcasted_iota(jnp.int32, (1, n_loc), 1)
    g_io = lax.shift_right_logical(lax.broadcasted_iota(jnp.int32, (rows, 1), 0), BLOCK.bit_length() - 1)
    nt = (((1,), (1,)), ((), ()))
    for kvh in range(N_KV_A):
        hs = slice(kvh * HEAD_DIM, (kvh + 1) * HEAD_DIM)
        kband = jnp.concatenate([kp_ref[:, hs], km_ref[:, hs], kn_ref[:, hs]], axis=0)
        vband = jnp.concatenate([vp_ref[:, hs], vm_ref[:, hs], vn_ref[:, hs]], axis=0)
        kc = kc_ref[:, hs]
        vc = vc_ref[:, hs]
        sink = jnp.zeros((rows, 1), F32)
        for g in range(GQA_GROUP):
            sink = jnp.where(g_io == g, sink_ref[kvh * GQA_GROUP + g] * LOG2E, sink)
        for qi in range(sub):
            blk = nq * sub + qi
            lo = jnp.where(blk == 0, BLOCK, 0)
            hi = jnp.where(blk == n_blocks - 1, 2 * BLOCK - 1, n_loc - 1)
            valid = in_band & ((c_row >= lo) & (c_row <= hi))
            q4 = jnp.concatenate(
                [q_ref[qi * BLOCK:(qi + 1) * BLOCK, (kvh * GQA_GROUP + g) * HEAD_DIM:(kvh * GQA_GROUP + g + 1) * HEAD_DIM]
                 for g in range(GQA_GROUP)], axis=0)
            s_loc = lax.dot_general(q4, kband[qi * BLOCK:qi * BLOCK + n_loc], nt, preferred_element_type=F32)
            s_loc = jnp.where(valid, s_loc, NEG_INF)
            s_ctx = lax.dot_general(q4, kc, nt, preferred_element_type=F32)
            m = jnp.maximum(jnp.maximum(jnp.max(s_loc, axis=-1, keepdims=True),
                                        jnp.max(s_ctx, axis=-1, keepdims=True)), sink)
            e_loc = jnp.exp2(s_loc - m)
            e_ctx = jnp.exp2(s_ctx - m)
            den = (jnp.sum(e_loc, axis=-1, keepdims=True) + jnp.sum(e_ctx, axis=-1, keepdims=True)
                   + jnp.exp2(sink - m))
            o = (jnp.dot(e_loc.astype(BF16), vband[qi * BLOCK:qi * BLOCK + n_loc], preferred_element_type=F32)
                 + jnp.dot(e_ctx.astype(BF16), vc, preferred_element_type=F32)) * (1.0 / den)
            for g in range(GQA_GROUP):
                hd = kvh * GQA_GROUP + g
                o_ref[qi * BLOCK:(qi + 1) * BLOCK, hd * HEAD_DIM:(hd + 1) * HEAD_DIM] = (
                    o[g * BLOCK:(g + 1) * BLOCK].astype(BF16))


def _attn_a_call(proj, cproj, sink, *, batch, seq, ctx_len, q_col, k_col, v_col, ck_col, cv_col):
    nqb = seq // QB_A
    sub = QB_A // BLOCK
    n_blocks = seq // BLOCK

    def prev_map(b, n, col):
        return (b * n_blocks + jnp.maximum(n * sub - 1, 0), col // W_KA)

    def next_map(b, n, col):
        return (b * n_blocks + jnp.minimum(n * sub + sub, n_blocks - 1), col // W_KA)

    def main_map(b, n, col):
        return (b * nqb + n, col // W_KA)

    edge = lambda f, col: pl.BlockSpec((BLOCK, W_KA), functools.partial(f, col=col))
    main = lambda col: pl.BlockSpec((QB_A, W_KA), functools.partial(main_map, col=col))
    ctxs = lambda col: pl.BlockSpec((ctx_len, W_KA), lambda b, n: (b, col // W_KA))
    return pl.pallas_call(
        functools.partial(_attn_a_kernel, n_blocks=n_blocks),
        grid=(batch, nqb),
        in_specs=[pl.BlockSpec(memory_space=pltpu.SMEM),
                  pl.BlockSpec((QB_A, W_QA), lambda b, n: (b * nqb + n, q_col // W_QA)),
                  edge(prev_map, k_col), main(k_col), edge(next_map, k_col),
                  edge(prev_map, v_col), main(v_col), edge(next_map, v_col),
                  ctxs(ck_col), ctxs(cv_col)],
        out_specs=pl.BlockSpec((QB_A, W_QA), lambda b, n: (b * nqb + n, 0)),
        out_shape=jax.ShapeDtypeStruct((batch * seq, W_QA), BF16),
        compiler_params=pltpu.CompilerParams(dimension_semantics=("arbitrary",) * 2,
                                             vmem_limit_bytes=VMEM_LIMIT_BYTES),
        name="attn_a",
    )(sink, proj, proj, proj, proj, proj, proj, proj, cproj, cproj)


def _attn_b_kernel(q_ref, kp_ref, km_ref, kn_ref, vp_ref, vm_ref, vn_ref, kc_ref, vc_ref, bias_ref, o_ref,
                   *, grid_rows):
    n = pl.program_id(1)
    nq = ROW_GROUP_B * GRID_W
    nk = 3 * nq
    n_ctx = kc_ref.shape[0]
    heads = q_ref.shape[1] // HEAD_DIM
    r_io = lax.broadcasted_iota(jnp.int32, (nq, nk), 0)
    c_io = lax.broadcasted_iota(jnp.int32, (nq, nk), 1)
    log_w = GRID_W.bit_length() - 1
    r_abs = n * ROW_GROUP_B + lax.shift_right_logical(r_io, log_w)
    q_col = r_io & (GRID_W - 1)
    kr_abs = (n - 1) * ROW_GROUP_B + lax.shift_right_logical(c_io, log_w)
    k_col = c_io & (GRID_W - 1)
    r_start = jnp.clip(r_abs - NB_ROWS // 2, 0, grid_rows - NB_ROWS)
    c_start = jnp.clip(q_col - NB_COLS // 2, 0, GRID_W - NB_COLS)
    valid = ((kr_abs >= r_start) & (kr_abs < r_start + NB_ROWS)
             & (k_col >= c_start) & (k_col < c_start + NB_COLS))
    cap = jnp.where(valid, F32_MAX, NEG_INF)
    for hd in range(heads):
        sl = slice(hd * HEAD_DIM, (hd + 1) * HEAD_DIM)
        q = q_ref[:, sl]
        keys = jnp.concatenate([kp_ref[:, sl], km_ref[:, sl], kn_ref[:, sl]], axis=0)
        vals = jnp.concatenate([vp_ref[:, sl], vm_ref[:, sl], vn_ref[:, sl]], axis=0)
        s_nb = lax.dot_general(q, keys, (((1,), (1,)), ((), ())), preferred_element_type=F32)
        s_nb = jnp.minimum(s_nb + bias_ref[hd], cap)
        s_ctx = lax.dot_general(q, kc_ref[:, sl], (((1,), (1,)), ((), ())), preferred_element_type=F32)
        m = jnp.maximum(jnp.max(s_nb, axis=-1, keepdims=True), jnp.max(s_ctx, axis=-1, keepdims=True))
        e_nb = jnp.exp2(s_nb - m)
        e_ctx = jnp.exp2(s_ctx - m)
        den = jnp.sum(e_nb, axis=-1, keepdims=True) + jnp.sum(e_ctx, axis=-1, keepdims=True)
        o = (jnp.dot(e_nb.astype(BF16), vals, preferred_element_type=F32)
             + jnp.dot(e_ctx.astype(BF16), vc_ref[:, sl], preferred_element_type=F32)) * (1.0 / den)
        o_ref[:, sl] = o.astype(BF16)


def _attn_b_call(proj, cproj, bias, *, batch, seq, ctx_len, q_col, k_col, v_col, ck_col, cv_col):
    nq = ROW_GROUP_B * GRID_W
    ng = seq // nq
    hw = W_B
    hg = W_B // hw

    def row_spec(col, shift):
        def imap(b, n, h):
            return (b * ng + jnp.clip(n + shift, 0, ng - 1), col // hw + h)
        return pl.BlockSpec((nq, hw), imap)

    ctxs = lambda col: pl.BlockSpec((ctx_len, hw), lambda b, n, h: (b, col // hw + h))
    return pl.pallas_call(
        functools.partial(_attn_b_kernel, grid_rows=seq // GRID_W),
        grid=(batch, ng, hg),
        in_specs=[row_spec(q_col, 0),
                  row_spec(k_col, -1), row_spec(k_col, 0), row_spec(k_col, 1),
                  row_spec(v_col, -1), row_spec(v_col, 0), row_spec(v_col, 1),
                  ctxs(ck_col), ctxs(cv_col),
                  pl.BlockSpec((hw // HEAD_DIM, nq, bias.shape[2]), lambda b, n, h: (h, 0, 0))],
        out_specs=pl.BlockSpec((nq, hw), lambda b, n, h: (b * ng + n, h)),
        out_shape=jax.ShapeDtypeStruct((batch * seq, W_B), BF16),
        compiler_params=pltpu.CompilerParams(dimension_semantics=("arbitrary",) * 3,
                                             vmem_limit_bytes=VMEM_LIMIT_BYTES),
        name="attn_b",
    )(proj, proj, proj, proj, proj, proj, proj, cproj, cproj, bias)


def _mix_kernel(oa_ref, ob_ref, ga_ref, gb_ref, wa_ref, wb_ref, z_ref):
    ya = jnp.dot(oa_ref[...], wa_ref[...], preferred_element_type=F32)
    yb = jnp.dot(ob_ref[...], wb_ref[...], preferred_element_type=F32)
    z_ref[...] = (ga_ref[...].astype(F32) * ya + gb_ref[...].astype(F32) * yb).astype(BF16)


def _mix_call(oa, ob, proj, wa, wb, *, tm):
    n = oa.shape[0]
    d = wa.shape[1]
    const = lambda shape: pl.BlockSpec(shape, lambda i: (0, 0), pipeline_mode=pl.Buffered(1))
    return pl.pallas_call(
        _mix_kernel,
        grid=(n // tm,),
        in_specs=[pl.BlockSpec((tm, oa.shape[1]), lambda i: (i, 0)),
                  pl.BlockSpec((tm, ob.shape[1]), lambda i: (i, 0)),
                  pl.BlockSpec((tm, d), lambda i: (i, 0)),
                  pl.BlockSpec((tm, d), lambda i: (i, 1)),
                  const(wa.shape), const(wb.shape)],
        out_specs=pl.BlockSpec((tm, d), lambda i: (i, 0)),
        out_shape=jax.ShapeDtypeStruct((n, d), BF16),
        compiler_params=pltpu.CompilerParams(dimension_semantics=("arbitrary",),
                                             vmem_limit_bytes=VMEM_LIMIT_BYTES),
        name="branch_mix",
    )(oa, ob, proj, proj, wa, wb)


def _outnorm_kernel(a_ref, w_ref, x_ref, gt_ref, g_ref, o_ref, acc_ref):
    k = pl.program_id(1)
    part = jnp.dot(a_ref[...], w_ref[...], preferred_element_type=F32)

    @pl.when(k == 0)
    def _():
        acc_ref[...] = part

    @pl.when(k > 0)
    def _():
        acc_ref[...] += part

    @pl.when(k == pl.num_programs(1) - 1)
    def _():
        o_ref[...] = x_ref[...] + gt_ref[0] * _rms(acc_ref[...], g_ref[...])


def _outnorm_call(a, w, x2, gt, g, *, tm, tk, rows_per_mod):
    n, kdim = a.shape
    d = w.shape[1]
    mod_tiles = rows_per_mod // tm
    return pl.pallas_call(
        _outnorm_kernel,
        grid=(n // tm, kdim // tk),
        in_specs=[pl.BlockSpec((tm, tk), lambda i, k: (i, k)),
                  pl.BlockSpec((tk, d), lambda i, k: (k, 0)),
                  pl.BlockSpec((tm, d), lambda i, k: (i, 0)),
                  pl.BlockSpec((1, 1, d), lambda i, k: (i // mod_tiles, 0, 0)),
                  pl.BlockSpec((1, d), lambda i, k: (0, 0))],
        out_specs=pl.BlockSpec((tm, d), lambda i, k: (i, 0)),
        out_shape=jax.ShapeDtypeStruct((n, d), F32),
        scratch_shapes=[pltpu.VMEM((tm, d), F32)],
        compiler_params=pltpu.CompilerParams(dimension_semantics=("arbitrary", "arbitrary"),
                                             vmem_limit_bytes=VMEM_LIMIT_BYTES),
        name="out_norm",
    )(a, w, x2, gt, g)


def _rowmm_norm_kernel(a_ref, w_ref, x_ref, gt_ref, g_ref, o_ref, *, n_chunks):
    cm = a_ref.shape[0] // n_chunks
    y = jnp.dot(a_ref[0:cm], w_ref[...], preferred_element_type=F32)
    for c in range(n_chunks):
        nxt = None
        if c + 1 < n_chunks:
            nxt = jnp.dot(a_ref[(c + 1) * cm:(c + 2) * cm], w_ref[...], preferred_element_type=F32)
        rows = slice(c * cm, (c + 1) * cm)
        o_ref[rows] = x_ref[rows] + gt_ref[0] * _rms(y, g_ref[...])
        y = nxt


def _rowmm_norm_call(a, w, x2, gt, g, *, tm, n_chunks, rows_per_mod):
    n, kdim = a.shape
    d = w.shape[1]
    mod_tiles = rows_per_mod // tm
    return pl.pallas_call(
        functools.partial(_rowmm_norm_kernel, n_chunks=n_chunks),
        grid=(n // tm,),
        in_specs=[pl.BlockSpec((tm, kdim), lambda i: (i, 0)),
                  pl.BlockSpec((kdim, d), lambda i: (0, 0), pipeline_mode=pl.Buffered(1)),
                  pl.BlockSpec((tm, d), lambda i: (i, 0)),
                  pl.BlockSpec((1, 1, d), lambda i: (i // mod_tiles, 0, 0)),
                  pl.BlockSpec((1, d), lambda i: (0, 0))],
        out_specs=pl.BlockSpec((tm, d), lambda i: (i, 0)),
        out_shape=jax.ShapeDtypeStruct((n, d), F32),
        compiler_params=pltpu.CompilerParams(dimension_semantics=("arbitrary",),
                                             vmem_limit_bytes=VMEM_LIMIT_RESIDENT_BYTES),
        name="rowmm_norm",
    )(a, w, x2, gt, g)


def _ffn_up_kernel(xp_ref, x_ref, xn_ref, sc_ref, sh_ref, g_ref, wa_ref, wg_ref, cwa_ref, cwg_ref, cba_ref, cbg_ref,
                   o_ref, h_ref, ua_ref, ug_ref, *, nj, n_tiles, tiles_per_seq):
    t = pl.program_id(0)
    tm = x_ref.shape[0]
    halo = xp_ref.shape[0]
    ext = tm + 2 * halo
    margin = 8

    @pl.when(t == 0)
    def _():
        ua_ref[...] = jnp.zeros_like(ua_ref)
        ug_ref[...] = jnp.zeros_like(ug_ref)

    @pl.when((t < n_tiles) & (t % nj == 0))
    def _():
        def mod(x):
            return (_rms(x, g_ref[...]) * (1 + sc_ref[0]) + sh_ref[0]).astype(BF16)
        h_ref[0:halo] = mod(xp_ref[...])
        h_ref[halo:halo + tm] = mod(x_ref[...])
        h_ref[halo + tm:] = mod(xn_ref[...])

    slot = t % 2
    prev = 1 - slot
    ip = jnp.maximum(t - 1, 0) // nj
    first_row = jnp.where((ip % tiles_per_seq) == 0, 0, -1)
    last_row = jnp.where((ip % tiles_per_seq) == tiles_per_seq - 1, tm - 1, -1)
    row = lax.broadcasted_iota(jnp.int32, (tm, 1), 0)

    def epilogue(lo, hi):
        cm = hi - lo
        rowc = row[lo:hi]

        def conv(u_ref, cw_ref, cb_ref):
            win = u_ref[prev, pl.ds(halo + lo - margin, cm + 2 * margin), :]
            up = pltpu.roll(win, 1, 0)[margin:margin + cm]
            un = pltpu.roll(win, cm + 2 * margin - 1, 0)[margin:margin + cm]
            up = jnp.where(rowc == first_row, 0.0, up)
            un = jnp.where(rowc == last_row, 0.0, un)
            out = cb_ref[...] + up * cw_ref[0:1]
            out = out + win[margin:margin + cm] * cw_ref[1:2]
            return out + un * cw_ref[2:3]

        a = conv(ua_ref, cwa_ref, cba_ref)
        gte = conv(ug_ref, cwg_ref, cbg_ref)
        o_ref[lo:hi] = (jax.nn.silu(gte) * a).astype(BF16)

    half = ext // 2
    third = (tm // 3) // BF16_ROWS * BF16_ROWS
    chunks = [(0, third), (third, 2 * third), (2 * third, tm)]
    ua_ref[slot, 0:half] = jnp.dot(h_ref[0:half], wa_ref[...], preferred_element_type=F32)
    epilogue(*chunks[0])
    ua_ref[slot, half:ext] = jnp.dot(h_ref[half:ext], wa_ref[...], preferred_element_type=F32)
    epilogue(*chunks[1])
    ug_ref[slot, 0:half] = jnp.dot(h_ref[0:half], wg_ref[...], preferred_element_type=F32)
    epilogue(*chunks[2])
    ug_ref[slot, half:ext] = jnp.dot(h_ref[half:ext], wg_ref[...], preferred_element_type=F32)


def _ffn_up_call(x2, sc, sh, g, w_up, conv_w, conv_b, *, tm, tn, seq):
    n, d = x2.shape
    dff = w_up.shape[1] // 2
    nj = dff // tn
    halo = BF16_ROWS
    hb = tm // halo
    n_halo = n // halo
    tiles_per_seq = seq // tm
    n_tiles = (n // tm) * nj
    assert (tm + 2 * halo) % (2 * BF16_ROWS) == 0

    def cur(t):
        tc = jnp.minimum(t, n_tiles - 1)
        return tc // nj, tc % nj

    def prv(t):
        tp = jnp.maximum(t - 1, 0)
        return tp // nj, tp % nj

    return pl.pallas_call(
        functools.partial(_ffn_up_kernel, nj=nj, n_tiles=n_tiles, tiles_per_seq=tiles_per_seq),
        grid=(n_tiles + 1,),
        in_specs=[pl.BlockSpec((halo, d), lambda t: (jnp.maximum(cur(t)[0] * hb - 1, 0), 0)),
                  pl.BlockSpec((tm, d), lambda t: (cur(t)[0], 0)),
                  pl.BlockSpec((halo, d), lambda t: (jnp.minimum((cur(t)[0] + 1) * hb, n_halo - 1), 0)),
                  pl.BlockSpec((1, 1, d), lambda t: (cur(t)[0] // tiles_per_seq, 0, 0)),
                  pl.BlockSpec((1, 1, d), lambda t: (cur(t)[0] // tiles_per_seq, 0, 0)),
                  pl.BlockSpec((1, d), lambda t: (0, 0)),
                  pl.BlockSpec((d, tn), lambda t: (0, cur(t)[1])),
                  pl.BlockSpec((d, tn), lambda t: (0, cur(t)[1] + nj)),
                  pl.BlockSpec((CONV_W, tn), lambda t: (0, prv(t)[1])),
                  pl.BlockSpec((CONV_W, tn), lambda t: (0, prv(t)[1] + nj)),
                  pl.BlockSpec((1, tn), lambda t: (0, prv(t)[1])),
                  pl.BlockSpec((1, tn), lambda t: (0, prv(t)[1] + nj))],
        out_specs=pl.BlockSpec((tm, tn), lambda t: prv(t)),
        out_shape=jax.ShapeDtypeStruct((n, dff), BF16),
        scratch_shapes=[pltpu.VMEM((tm + 2 * halo, d), BF16),
                        pltpu.VMEM((2, tm + 2 * halo, tn), F32),
                        pltpu.VMEM((2, tm + 2 * halo, tn), F32)],
        compiler_params=pltpu.CompilerParams(dimension_semantics=("arbitrary",),
                                             vmem_limit_bytes=VMEM_LIMIT_BYTES),
        name="ffn_up",
    )(x2, x2, x2, sc, sh, g, w_up, w_up, conv_w, conv_w, conv_b, conv_b)


def _rope_tables(seq):
    t = jnp.arange(seq)
    half = HEAD_DIM // 4
    inv = ROPE_BASE ** (-jnp.arange(half, dtype=F32) / half)

    def cs(pos):
        ang = pos.astype(F32)[:, None] * inv[None, :]
        c, s = jnp.cos(ang), jnp.sin(ang)
        return jnp.concatenate([c, c], axis=1), jnp.concatenate([-s, s], axis=1)

    cr, sr = cs(t // GRID_W)
    cc, sn = cs(t % GRID_W)
    return jnp.concatenate([cr, cc], axis=1), jnp.concatenate([sr, sn], axis=1)


def _pick_tile(n, pref):
    while n % pref:
        pref //= 2
    return pref


def kernel(x, c, ctx, c_ctx, w_mod, b_mod, g_attn_pre, g_attn_post, g_ffn_pre, g_ffn_post, w_in, sink_a, rpb_b,
           w_br_a, w_br_b, w_o, w_up, conv_w, conv_b, w_down):
    batch, seq, d = x.shape
    ctx_len = ctx.shape[1]
    assert w_mod.shape[0] == 1 and d == 16 * HEAD_DIM
    assert seq % QB_A == 0 and seq % (ROW_GROUP_B * GRID_W) == 0 and WINDOW == BLOCK
    n = batch * seq
    x2 = x.reshape(n, d)

    pad = (-(batch + 1)) % 8
    cs = jnp.concatenate([c, c_ctx[None, :], jnp.zeros((pad, d), F32)], axis=0)
    mod = _mod_call(cs, w_mod[0], b_mod[0][None, :])
    sh1, sc1, gt1, sh2, sc2, gt2 = [mod[:batch, k * d:(k + 1) * d].reshape(batch, 1, d) for k in range(6)]
    csh1 = mod[batch:batch + 1, 0:d].reshape(1, 1, d)
    csc1 = mod[batch:batch + 1, d:2 * d].reshape(1, 1, d)

    o_qa, o_ka, o_va, o_qb, o_kb, o_vb, o_ga = 0, W_QA, W_QA + W_KA, W_QA + 2 * W_KA, W_QA + 2 * W_KA + W_B, \
        W_QA + 2 * W_KA + 2 * W_B, W_QA + 2 * W_KA + 3 * W_B
    wi = w_in[0]
    w_perm = jnp.concatenate([wi[:, o_ga:], wi[:, o_qa:o_ka], wi[:, o_qb:o_ga], wi[:, o_ka:o_qb]], axis=1).astype(BF16)
    n_gate_cols = 2 * d
    n_tiles = w_perm.shape[1] // TN
    j0q = n_gate_cols // TN
    c_qa = 0
    c_qb = c_qa + W_QA
    c_kb = c_qb + W_B
    c_vb = c_kb + W_B
    c_ka = c_vb + W_B
    c_va = c_ka + W_KA
    cos, sin = _rope_tables(seq)
    tm = _pick_tile(seq, 1024)
    gates, h1 = _gate_call(x2, sc1, sh1, g_attn_pre, w_perm, tm=tm, tn=TN_GATE, n_cols=n_gate_cols, rows_per_mod=seq)
    proj = _qkv_call(h1, w_perm, cos, sin, tm=_pick_tile(seq, 2048), j0=j0q, nj=n_tiles - j0q)
    nc = batch * ctx_len
    j0c = j0q + c_kb // TN
    cproj = _inproj_call(ctx.reshape(nc, d), csc1, csh1, g_attn_pre, w_perm,
                         jnp.ones((nc, HEAD_DIM), F32), jnp.zeros((nc, HEAD_DIM), F32),
                         tm=nc, rows_per_mod=nc, j0=j0c, nj=n_tiles - j0c)
    cc_kb, cc_vb, cc_ka, cc_va = 0, W_B, 2 * W_B, 2 * W_B + W_KA

    oa = _attn_a_call(proj, cproj, sink_a[0], batch=batch, seq=seq, ctx_len=ctx_len,
                      q_col=c_qa, k_col=c_ka, v_col=c_va, ck_col=cc_ka, cv_col=cc_va)
    bias = _rpb_call(rpb_b[0])
    ob = _attn_b_call(proj, cproj, bias, batch=batch, seq=seq, ctx_len=ctx_len,
                      q_col=c_qb, k_col=c_kb, v_col=c_vb, ck_col=cc_kb, cv_col=cc_vb)

    tm2 = _pick_tile(seq, 512)
    z = _mix_call(oa, ob, gates, w_br_a[0].astype(BF16), w_br_b[0].astype(BF16), tm=tm2)
    x1 = _outnorm_call(z, w_o[0].astype(BF16), x2, gt1, g_attn_post, tm=tm2, tk=d, rows_per_mod=seq)

    dff = w_down.shape[1]
    act = _ffn_up_call(x1, sc2, sh2, g_ffn_pre, w_up[0].astype(BF16), conv_w[0], conv_b[0][None, :],
                       tm=tm, tn=TN, seq=seq)
    out = _rowmm_norm_call(act, w_down[0].astype(BF16), x1, gt2, g_ffn_post, tm=_pick_tile(seq, 256), n_chunks=2,
                           rows_per_mod=seq)
    return out.reshape(batch, seq, d)
```

```python
import functools

import jax
import jax.numpy as jnp
from jax import lax
from jax.experimental import pallas as pl
from jax.experimental.pallas import tpu as pltpu

F32 = jnp.float32
BF16 = jnp.bfloat16

GRID_W = 64
HEAD_DIM = 128
N_HEADS_A = 8
N_KV_A = 2
GQA_GROUP = N_HEADS_A // N_KV_A
WINDOW = 128
BLOCK = 128
N_HEADS_B = 8
NB_ROWS = 8
NB_COLS = 16
CONV_W = 3
ROPE_BASE = 10000.0
EPS = 1e-6
NEG_INF = -1e30
ATTN_SCALE = HEAD_DIM ** -0.5
LOG2E = 1.4426950408889634
Q_SCALE = ATTN_SCALE * LOG2E
F32_MAX = 3.4028234663852886e38

W_QA = N_HEADS_A * HEAD_DIM
W_KA = N_KV_A * HEAD_DIM
W_B = N_HEADS_B * HEAD_DIM

LANES = 128
BF16_ROWS = 16
VMEM_LIMIT_BYTES = 56 * 1024 * 1024
VMEM_LIMIT_RESIDENT_BYTES = 60 * 1024 * 1024

TN = 512
TN_GATE = 1024
ROW_CHUNK = 256
ROW_GROUP_B = 4
QB_A = 512


def _rms(x, g):
    return (x * lax.rsqrt(jnp.mean(x * x, axis=-1, keepdims=True) + EPS)) * g


def _mod_kernel(c_ref, w_ref, b_ref, o_ref):
    a = jax.nn.silu(c_ref[...]).astype(BF16)
    o_ref[...] = jnp.dot(a, w_ref[...].astype(BF16), preferred_element_type=F32) + b_ref[...]


def _mod_call(cs, w, b):
    m, d = cs.shape
    n = w.shape[1]
    tn = 1536 if n % 1536 == 0 else n
    return pl.pallas_call(
        _mod_kernel,
        grid=(n // tn,),
        in_specs=[pl.BlockSpec((m, d), lambda j: (0, 0)),
                  pl.BlockSpec((d, tn), lambda j: (0, j)),
                  pl.BlockSpec((1, tn), lambda j: (0, j))],
        out_specs=pl.BlockSpec((m, tn), lambda j: (0, j)),
        out_shape=jax.ShapeDtypeStruct((m, n), F32),
        compiler_params=pltpu.CompilerParams(dimension_semantics=("arbitrary",),
                                             vmem_limit_bytes=VMEM_LIMIT_BYTES),
        name="mod",
    )(cs, w, b)


def _rpb_kernel(rpb_ref, o_ref):
    h = pl.program_id(0)
    n_dr = 2 * NB_ROWS - 1
    n_dc = 2 * NB_COLS - 1
    qc = lax.broadcasted_iota(jnp.int32, (GRID_W, LANES), 0)
    lane = lax.broadcasted_iota(jnp.int32, (GRID_W, LANES), 1)
    kc = lane & (GRID_W - 1)
    dc_idx = jnp.clip(kc - qc, -(NB_COLS - 1), NB_COLS - 1) + (NB_COLS - 1)
    base = h * (n_dr * n_dc)

    def toeplitz(dr):
        def body(dc, acc):
            return jnp.where(dc_idx == dc, rpb_ref[base + dr * n_dc + dc] * LOG2E, acc)
        return lax.fori_loop(0, n_dc, body, jnp.zeros((GRID_W, LANES), F32))

    tw = [toeplitz(dr) for dr in range(n_dr)]
    pair = [jnp.where(lane < GRID_W, tw[d], tw[d + 1]) for d in range(n_dr - 1)]
    n_kr = 3 * ROW_GROUP_B
    dr0 = NB_ROWS - 1 - ROW_GROUP_B
    for rq in range(ROW_GROUP_B):
        for p in range(n_kr // 2):
            o_ref[0, rq * GRID_W:(rq + 1) * GRID_W, p * LANES:(p + 1) * LANES] = pair[2 * p - rq + dr0]


def _rpb_call(rpb):
    nh = rpb.shape[0]
    nq = ROW_GROUP_B * GRID_W
    nk = 3 * ROW_GROUP_B * GRID_W
    assert 2 * ROW_GROUP_B == NB_ROWS
    return pl.pallas_call(
        _rpb_kernel,
        grid=(nh,),
        in_specs=[pl.BlockSpec(memory_space=pltpu.SMEM)],
        out_specs=pl.BlockSpec((1, nq, nk), lambda h: (h, 0, 0)),
        out_shape=jax.ShapeDtypeStruct((nh, nq, nk), F32),
        compiler_params=pltpu.CompilerParams(dimension_semantics=("arbitrary",)),
        name="rpb_table",
    )(rpb.reshape(-1))


def _rope(t, cos, sin, first_half):
    partner = jnp.where(first_half, pltpu.roll(t, LANES - 32, 1), pltpu.roll(t, 32, 1))
    return t * cos + partner * sin


def _row_chunked_matmul(h_ref, w_ref, epilogue, between=None):
    n_chunks = h_ref.shape[0] // ROW_CHUNK
    acc = jnp.dot(h_ref[0:ROW_CHUNK], w_ref[...], preferred_element_type=F32)
    for c in range(n_chunks):
        nxt = None
        if c + 1 < n_chunks:
            nxt = jnp.dot(h_ref[(c + 1) * ROW_CHUNK:(c + 2) * ROW_CHUNK], w_ref[...], preferred_element_type=F32)
        epilogue(slice(c * ROW_CHUNK, (c + 1) * ROW_CHUNK), acc)
        if c == 0 and between is not None:
            between()
        acc = nxt


def _gate_kernel(x_ref, sc_ref, sh_ref, g_ref, w_ref, o_ref, hout_ref, hs_ref, *, n_rows):
    i = pl.program_id(0)
    j = pl.program_id(1)
    pr = x_ref.shape[0]

    def norm_piece():
        h = (_rms(x_ref[...], g_ref[...]) * (1 + sc_ref[0]) + sh_ref[0]).astype(BF16)
        hout_ref[...] = h
        hs_ref[i % 2, pl.ds(pl.multiple_of(j * pr, pr), pr), :] = h

    def epilogue(rows, acc):
        o_ref[rows] = jax.nn.sigmoid(acc).astype(BF16)

    @pl.when(i == 0)
    def _():
        norm_piece()

    @pl.when((i > 0) & (i < n_rows))
    def _():
        _row_chunked_matmul(hs_ref.at[(i + 1) % 2], w_ref, epilogue, between=norm_piece)

    @pl.when(i == n_rows)
    def _():
        _row_chunked_matmul(hs_ref.at[(i + 1) % 2], w_ref, epilogue)


def _gate_call(x2, sc, sh, g, w, *, tm, tn, n_cols, rows_per_mod):
    n, d = x2.shape
    n_rows = n // tm
    nj = n_cols // tn
    pr = tm // nj
    assert pr * nj == tm and pr % BF16_ROWS == 0
    mod_tiles = rows_per_mod // tm
    last = n_rows * nj - 1
    piece_map = lambda i, j: (jnp.minimum(i * nj + j, last), 0)
    mod_map = lambda i, j: (jnp.minimum(i, n_rows - 1) // mod_tiles, 0, 0)
    return pl.pallas_call(
        functools.partial(_gate_kernel, n_rows=n_rows),
        grid=(n_rows + 1, nj),
        in_specs=[pl.BlockSpec((pr, d), piece_map),
                  pl.BlockSpec((1, 1, d), mod_map),
                  pl.BlockSpec((1, 1, d), mod_map),
                  pl.BlockSpec((1, d), lambda i, j: (0, 0)),
                  pl.BlockSpec((d, tn), lambda i, j: (0, j))],
        out_specs=[pl.BlockSpec((tm, tn), lambda i, j: (jnp.maximum(i - 1, 0), jnp.where(i > 0, j, 0))),
                   pl.BlockSpec((pr, d), piece_map)],
        out_shape=[jax.ShapeDtypeStruct((n, n_cols), BF16), jax.ShapeDtypeStruct((n, d), BF16)],
        scratch_shapes=[pltpu.VMEM((2, tm, d), BF16)],
        compiler_params=pltpu.CompilerParams(dimension_semantics=("arbitrary", "arbitrary"),
                                             vmem_limit_bytes=VMEM_LIMIT_BYTES),
        name="gate_proj",
    )(x2, sc, sh, g, w)


def _qkv_kernel(h_ref, w_ref, cos_ref, sin_ref, o_ref, *, j0, n_gate):
    _proj_tile(h_ref, w_ref, cos_ref, sin_ref, o_ref, pl.program_id(1) + j0, n_gate)


def _qkv_call(h, w, cos, sin, *, tm, j0, nj):
    n, d = h.shape
    pos_tiles = cos.shape[0] // tm
    return pl.pallas_call(
        functools.partial(_qkv_kernel, j0=j0, n_gate=2 * d // TN),
        grid=(n // tm, nj),
        in_specs=[pl.BlockSpec((tm, d), lambda i, j: (i, 0)),
                  pl.BlockSpec((d, TN), lambda i, j: (0, j + j0)),
                  pl.BlockSpec((tm, HEAD_DIM), lambda i, j: (i % pos_tiles, 0)),
                  pl.BlockSpec((tm, HEAD_DIM), lambda i, j: (i % pos_tiles, 0))],
        out_specs=pl.BlockSpec((tm, TN), lambda i, j: (i, j)),
        out_shape=jax.ShapeDtypeStruct((n, nj * TN), BF16),
        compiler_params=pltpu.CompilerParams(dimension_semantics=("arbitrary", "arbitrary"),
                                             vmem_limit_bytes=VMEM_LIMIT_BYTES),
        name="qkv_proj",
    )(h, w, cos, sin)


def _inproj_kernel(x_ref, sc_ref, sh_ref, g_ref, w_ref, cos_ref, sin_ref, o_ref, h_ref, *, j0):
    j = pl.program_id(1)

    @pl.when(j == 0)
    def _():
        h = _rms(x_ref[...], g_ref[...]) * (1 + sc_ref[0]) + sh_ref[0]
        h_ref[...] = h.astype(BF16)

    _proj_tile(h_ref, w_ref, cos_ref, sin_ref, o_ref, j + j0, 2 * x_ref.shape[1] // TN)


def _proj_tile(h_ref, w_ref, cos_ref, sin_ref, o_ref, jj, n_gate):
    lane = lax.broadcasted_iota(jnp.int32, (1, HEAD_DIM), 1)
    first_half = (lane & 63) < 32

    def row_chunks(epilogue):
        _row_chunked_matmul(h_ref, w_ref, epilogue)

    def rope_heads(rows, acc, n_heads, scale):
        cos = cos_ref[rows]
        sin = sin_ref[rows]
        for hd in range(n_heads):
            sl = slice(hd * HEAD_DIM, (hd + 1) * HEAD_DIM)
            r = _rope(acc[:, sl], cos, sin, first_half)
            if scale is not None:
                r = r * scale
            o_ref[rows, sl] = r.astype(BF16)

    @pl.when((jj >= n_gate) & (jj < n_gate + 2))
    def _():
        row_chunks(lambda rows, acc: rope_heads(rows, acc, TN // HEAD_DIM, Q_SCALE))

    @pl.when((jj >= n_gate + 2) & (jj < n_gate + 4))
    def _():
        def epilogue(rows, acc):
            o_ref[rows] = (acc * Q_SCALE).astype(BF16)
        row_chunks(epilogue)

    @pl.when((jj >= n_gate + 4) & (jj < n_gate + 8))
    def _():
        def epilogue(rows, acc):
            o_ref[rows] = acc.astype(BF16)
        row_chunks(epilogue)

    @pl.when(jj == n_gate + 8)
    def _():
        def epilogue(rows, acc):
            rope_heads(rows, acc, N_KV_A, None)
            o_ref[rows, W_KA:] = acc[:, W_KA:].astype(BF16)
        row_chunks(epilogue)


def _inproj_call(x2, sc, sh, g, w, cos, sin, *, tm, rows_per_mod, j0, nj):
    n, d = x2.shape
    pos_tiles = cos.shape[0] // tm
    mod_tiles = rows_per_mod // tm
    return pl.pallas_call(
        functools.partial(_inproj_kernel, j0=j0),
        grid=(n // tm, nj),
        in_specs=[pl.BlockSpec((tm, d), lambda i, j: (i, 0)),
                  pl.BlockSpec((1, 1, d), lambda i, j: (i // mod_tiles, 0, 0)),
                  pl.BlockSpec((1, 1, d), lambda i, j: (i // mod_tiles, 0, 0)),
                  pl.BlockSpec((1, d), lambda i, j: (0, 0)),
                  pl.BlockSpec((d, TN), lambda i, j: (0, j + j0)),
                  pl.BlockSpec((tm, HEAD_DIM), lambda i, j: (i % pos_tiles, 0)),
                  pl.BlockSpec((tm, HEAD_DIM), lambda i, j: (i % pos_tiles, 0))],
        out_specs=pl.BlockSpec((tm, TN), lambda i, j: (i, j)),
        out_shape=jax.ShapeDtypeStruct((n, nj * TN), BF16),
        scratch_shapes=[pltpu.VMEM((tm, d), BF16)],
        compiler_params=pltpu.CompilerParams(dimension_semantics=("arbitrary", "arbitrary"),
                                             vmem_limit_bytes=VMEM_LIMIT_BYTES),
        name="in_proj",
    )(x2, sc, sh, g, w, cos, sin)


def _attn_a_kernel(sink_ref, q_ref, kp_ref, km_ref, kn_ref, vp_ref, vm_ref, vn_ref, kc_ref, vc_ref, o_ref,
                   *, n_blocks):
    nq = pl.program_id(1)
    sub = QB_A // BLOCK
    rows = GQA_GROUP * BLOCK
    n_loc = 3 * BLOCK
    r_io = lax.broadcasted_iota(jnp.int32, (rows, n_loc), 0)
    c_io = lax.broadcasted_iota(jnp.int32, (rows, n_loc), 1)
    qi_io = r_io & (BLOCK - 1)
    in_band = (c_io >= qi_io) & (c_io <= qi_io + 2 * WINDOW)
    c_row = lax.broadcasted_iota(jnp.int32, (1, n_loc), 1)
    g_io = lax.shift_right_logical(lax.broadcasted_iota(jnp.int32, (rows, 1), 0), BLOCK.bit_length() - 1)
    nt = (((1,), (1,)), ((), ()))
    for kvh in range(N_KV_A):
        hs = slice(kvh * HEAD_DIM, (kvh + 1) * HEAD_DIM)
        kband = jnp.concatenate([kp_ref[:, hs], km_ref[:, hs], kn_ref[:, hs]], axis=0)
        vband = jnp.concatenate([vp_ref[:, hs], vm_ref[:, hs], vn_ref[:, hs]], axis=0)
        kc = kc_ref[:, hs]
        vc = vc_ref[:, hs]
        sink = jnp.zeros((rows, 1), F32)
        for g in range(GQA_GROUP):
            sink = jnp.where(g_io == g, sink_ref[kvh * GQA_GROUP + g] * LOG2E, sink)
        for qi in range(sub):
            blk = nq * sub + qi
            lo = jnp.where(blk == 0, BLOCK, 0)
            hi = jnp.where(blk == n_blocks - 1, 2 * BLOCK - 1, n_loc - 1)
            valid = in_band & ((c_row >= lo) & (c_row <= hi))
            q4 = jnp.concatenate(
                [q_ref[qi * BLOCK:(qi + 1) * BLOCK, (kvh * GQA_GROUP + g) * HEAD_DIM:(kvh * GQA_GROUP + g + 1) * HEAD_DIM]
                 for g in range(GQA_GROUP)], axis=0)
            s_loc = lax.dot_general(q4, kband[qi * BLOCK:qi * BLOCK + n_loc], nt, preferred_element_type=F32)
            s_loc = jnp.where(valid, s_loc, NEG_INF)
            s_ctx = lax.dot_general(q4, kc, nt, preferred_element_type=F32)
            m = jnp.maximum(jnp.maximum(jnp.max(s_loc, axis=-1, keepdims=True),
                                        jnp.max(s_ctx, axis=-1, keepdims=True)), sink)
            e_loc = jnp.exp2(s_loc - m)
            e_ctx = jnp.exp2(s_ctx - m)
            den = (jnp.sum(e_loc, axis=-1, keepdims=True) + jnp.sum(e_ctx, axis=-1, keepdims=True)
                   + jnp.exp2(sink - m))
            o = (jnp.dot(e_loc.astype(BF16), vband[qi * BLOCK:qi * BLOCK + n_loc], preferred_element_type=F32)
                 + jnp.dot(e_ctx.astype(BF16), vc, preferred_element_type=F32)) * (1.0 / den)
            for g in range(GQA_GROUP):
                hd = kvh * GQA_GROUP + g
                o_ref[qi * BLOCK:(qi + 1) * BLOCK, hd * HEAD_DIM:(hd + 1) * HEAD_DIM] = (
                    o[g * BLOCK:(g + 1) * BLOCK].astype(BF16))


def _attn_a_call(proj, cproj, sink, *, batch, seq, ctx_len, q_col, k_col, v_col, ck_col, cv_col):
    nqb = seq // QB_A
    sub = QB_A // BLOCK
    n_blocks = seq // BLOCK

    def prev_map(b, n, col):
        return (b * n_blocks + jnp.maximum(n * sub - 1, 0), col // W_KA)

    def next_map(b, n, col):
        return (b * n_blocks + jnp.minimum(n * sub + sub, n_blocks - 1), col // W_KA)

    def main_map(b, n, col):
        return (b * nqb + n, col // W_KA)

    edge = lambda f, col: pl.BlockSpec((BLOCK, W_KA), functools.partial(f, col=col))
    main = lambda col: pl.BlockSpec((QB_A, W_KA), functools.partial(main_map, col=col))
    ctxs = lambda col: pl.BlockSpec((ctx_len, W_KA), lambda b, n: (b, col // W_KA))
    return pl.pallas_call(
        functools.partial(_attn_a_kernel, n_blocks=n_blocks),
        grid=(batch, nqb),
        in_specs=[pl.BlockSpec(memory_space=pltpu.SMEM),
                  pl.BlockSpec((QB_A, W_QA), lambda b, n: (b * nqb + n, q_col // W_QA)),
                  edge(prev_map, k_col), main(k_col), edge(next_map, k_col),
                  edge(prev_map, v_col), main(v_col), edge(next_map, v_col),
                  ctxs(ck_col), ctxs(cv_col)],
        out_specs=pl.BlockSpec((QB_A, W_QA), lambda b, n: (b * nqb + n, 0)),
        out_shape=jax.ShapeDtypeStruct((batch * seq, W_QA), BF16),
        compiler_params=pltpu.CompilerParams(dimension_semantics=("arbitrary",) * 2,
                                             vmem_limit_bytes=VMEM_LIMIT_BYTES),
        name="attn_a",
    )(sink, proj, proj, proj, proj, proj, proj, proj, cproj, cproj)


def _attn_b_kernel(q_ref, kp_ref, km_ref, kn_ref, vp_ref, vm_ref, vn_ref, kc_ref, vc_ref, bias_ref, o_ref,
                   *, grid_rows):
    n = pl.program_id(1)
    nq = ROW_GROUP_B * GRID_W
    nk = 3 * nq
    n_ctx = kc_ref.shape[0]
    heads = q_ref.shape[1] // HEAD_DIM
    r_io = lax.broadcasted_iota(jnp.int32, (nq, nk), 0)
    c_io = lax.broadcasted_iota(jnp.int32, (nq, nk), 1)
    log_w = GRID_W.bit_length() - 1
    r_abs = n * ROW_GROUP_B + lax.shift_right_logical(r_io, log_w)
    q_col = r_io & (GRID_W - 1)
    kr_abs = (n - 1) * ROW_GROUP_B + lax.shift_right_logical(c_io, log_w)
    k_col = c_io & (GRID_W - 1)
    r_start = jnp.clip(r_abs - NB_ROWS // 2, 0, grid_rows - NB_ROWS)
    c_start = jnp.clip(q_col - NB_COLS // 2, 0, GRID_W - NB_COLS)
    valid = ((kr_abs >= r_start) & (kr_abs < r_start + NB_ROWS)
             & (k_col >= c_start) & (k_col < c_start + NB_COLS))
    cap = jnp.where(valid, F32_MAX, NEG_INF)
    for hd in range(heads):
        sl = slice(hd * HEAD_DIM, (hd + 1) * HEAD_DIM)
        q = q_ref[:, sl]
        keys = jnp.concatenate([kp_ref[:, sl], km_ref[:, sl], kn_ref[:, sl]], axis=0)
        vals = jnp.concatenate([vp_ref[:, sl], vm_ref[:, sl], vn_ref[:, sl]], axis=0)
        s_nb = lax.dot_general(q, keys, (((1,), (1,)), ((), ())), preferred_element_type=F32)
        s_nb = jnp.minimum(s_nb + bias_ref[hd], cap)
        s_ctx = lax.dot_general(q, kc_ref[:, sl], (((1,), (1,)), ((), ())), preferred_element_type=F32)
        m = jnp.maximum(jnp.max(s_nb, axis=-1, keepdims=True), jnp.max(s_ctx, axis=-1, keepdims=True))
        e_nb = jnp.exp2(s_nb - m)
        e_ctx = jnp.exp2(s_ctx - m)
        den = jnp.sum(e_nb, axis=-1, keepdims=True) + jnp.sum(e_ctx, axis=-1, keepdims=True)
        o = (jnp.dot(e_nb.astype(BF16), vals, preferred_element_type=F32)
             + jnp.dot(e_ctx.astype(BF16), vc_ref[:, sl], preferred_element_type=F32)) * (1.0 / den)
        o_ref[:, sl] = o.astype(BF16)


def _attn_b_call(proj, cproj, bias, *, batch, seq, ctx_len, q_col, k_col, v_col, ck_col, cv_col):
    nq = ROW_GROUP_B * GRID_W
    ng = seq // nq
    hw = W_B
    hg = W_B // hw

    def row_spec(col, shift):
        def imap(b, n, h):
            return (b * ng + jnp.clip(n + shift, 0, ng - 1), col // hw + h)
        return pl.BlockSpec((nq, hw), imap)

    ctxs = lambda col: pl.BlockSpec((ctx_len, hw), lambda b, n, h: (b, col // hw + h))
    return pl.pallas_call(
        functools.partial(_attn_b_kernel, grid_rows=seq // GRID_W),
        grid=(batch, ng, hg),
        in_specs=[row_spec(q_col, 0),
                  row_spec(k_col, -1), row_spec(k_col, 0), row_spec(k_col, 1),
                  row_spec(v_col, -1), row_spec(v_col, 0), row_spec(v_col, 1),
                  ctxs(ck_col), ctxs(cv_col),
                  pl.BlockSpec((hw // HEAD_DIM, nq, bias.shape[2]), lambda b, n, h: (h, 0, 0))],
        out_specs=pl.BlockSpec((nq, hw), lambda b, n, h: (b * ng + n, h)),
        out_shape=jax.ShapeDtypeStruct((batch * seq, W_B), BF16),
        compiler_params=pltpu.CompilerParams(dimension_semantics=("arbitrary",) * 3,
                                             vmem_limit_bytes=VMEM_LIMIT_BYTES),
        name="attn_b",
    )(proj, proj, proj, proj, proj, proj, proj, cproj, cproj, bias)


def _mix_kernel(oa_ref, ob_ref, ga_ref, gb_ref, wa_ref, wb_ref, z_ref):
    ya = jnp.dot(oa_ref[...], wa_ref[...], preferred_element_type=F32)
    yb = jnp.dot(ob_ref[...], wb_ref[...], preferred_element_type=F32)
    z_ref[...] = (ga_ref[...].astype(F32) * ya + gb_ref[...].astype(F32) * yb).astype(BF16)


def _mix_call(oa, ob, proj, wa, wb, *, tm):
    n = oa.shape[0]
    d = wa.shape[1]
    const = lambda shape: pl.BlockSpec(shape, lambda i: (0, 0), pipeline_mode=pl.Buffered(1))
    return pl.pallas_call(
        _mix_kernel,
        grid=(n // tm,),
        in_specs=[pl.BlockSpec((tm, oa.shape[1]), lambda i: (i, 0)),
                  pl.BlockSpec((tm, ob.shape[1]), lambda i: (i, 0)),
                  pl.BlockSpec((tm, d), lambda i: (i, 0)),
                  pl.BlockSpec((tm, d), lambda i: (i, 1)),
                  const(wa.shape), const(wb.shape)],
        out_specs=pl.BlockSpec((tm, d), lambda i: (i, 0)),
        out_shape=jax.ShapeDtypeStruct((n, d), BF16),
        compiler_params=pltpu.CompilerParams(dimension_semantics=("arbitrary",),
                                             vmem_limit_bytes=VMEM_LIMIT_BYTES),
        name="branch_mix",
    )(oa, ob, proj, proj, wa, wb)


def _outnorm_kernel(a_ref, w_ref, x_ref, gt_ref, g_ref, o_ref, acc_ref):
    k = pl.program_id(1)
    part = jnp.dot(a_ref[...], w_ref[...], preferred_element_type=F32)

    @pl.when(k == 0)
    def _():
        acc_ref[...] = part

    @pl.when(k > 0)
    def _():
        acc_ref[...] += part

    @pl.when(k == pl.num_programs(1) - 1)
    def _():
        o_ref[...] = x_ref[...] + gt_ref[0] * _rms(acc_ref[...], g_ref[...])


def _outnorm_call(a, w, x2, gt, g, *, tm, tk, rows_per_mod):
    n, kdim = a.shape
    d = w.shape[1]
    mod_tiles = rows_per_mod // tm
    return pl.pallas_call(
        _outnorm_kernel,
        grid=(n // tm, kdim // tk),
        in_specs=[pl.BlockSpec((tm, tk), lambda i, k: (i, k)),
                  pl.BlockSpec((tk, d), lambda i, k: (k, 0)),
                  pl.BlockSpec((tm, d), lambda i, k: (i, 0)),
                  pl.BlockSpec((1, 1, d), lambda i, k: (i // mod_tiles, 0, 0)),
                  pl.BlockSpec((1, d), lambda i, k: (0, 0))],
        out_specs=pl.BlockSpec((tm, d), lambda i, k: (i, 0)),
        out_shape=jax.ShapeDtypeStruct((n, d), F32),
        scratch_shapes=[pltpu.VMEM((tm, d), F32)],
        compiler_params=pltpu.CompilerParams(dimension_semantics=("arbitrary", "arbitrary"),
                                             vmem_limit_bytes=VMEM_LIMIT_BYTES),
        name="out_norm",
    )(a, w, x2, gt, g)


def _rowmm_norm_kernel(a_ref, w_ref, x_ref, gt_ref, g_ref, o_ref, *, n_chunks):
    cm = a_ref.shape[0] // n_chunks
    y = jnp.dot(a_ref[0:cm], w_ref[...], preferred_element_type=F32)
    for c in range(n_chunks):
        nxt = None
        if c + 1 < n_chunks:
            nxt = jnp.dot(a_ref[(c + 1) * cm:(c + 2) * cm], w_ref[...], preferred_element_type=F32)
        rows = slice(c * cm, (c + 1) * cm)
        o_ref[rows] = x_ref[rows] + gt_ref[0] * _rms(y, g_ref[...])
        y = nxt


def _rowmm_norm_call(a, w, x2, gt, g, *, tm, n_chunks, rows_per_mod):
    n, kdim = a.shape
    d = w.shape[1]
    mod_tiles = rows_per_mod // tm
    return pl.pallas_call(
        functools.partial(_rowmm_norm_kernel, n_chunks=n_chunks),
        grid=(n // tm,),
        in_specs=[pl.BlockSpec((tm, kdim), lambda i: (i, 0)),
                  pl.BlockSpec((kdim, d), lambda i: (0, 0), pipeline_mode=pl.Buffered(1)),
                  pl.BlockSpec((tm, d), lambda i: (i, 0)),
                  pl.BlockSpec((1, 1, d), lambda i: (i // mod_tiles, 0, 0)),
                  pl.BlockSpec((1, d), lambda i: (0, 0))],
        out_specs=pl.BlockSpec((tm, d), lambda i: (i, 0)),
        out_shape=jax.ShapeDtypeStruct((n, d), F32),
        compiler_params=pltpu.CompilerParams(dimension_semantics=("arbitrary",),
                                             vmem_limit_bytes=VMEM_LIMIT_RESIDENT_BYTES),
        name="rowmm_norm",
    )(a, w, x2, gt, g)


def _ffn_up_kernel(x0p_ref, x0_ref, x0n_ref, sc0_ref, sh0_ref, xp_ref, xm_ref, xn_ref, sc_ref, sh_ref, g_ref,
                   wa_ref, wg_ref, cwa_ref, cwg_ref, cba_ref, cbg_ref,
                   o_ref, h_cur, h_nxt, ua_ref, ug_ref, *, nj, n_tiles, tiles_per_seq, n_pieces):
    t = pl.program_id(0)
    tm = x0_ref.shape[0]
    halo = xp_ref.shape[0]
    pr = xm_ref.shape[0]
    ext = tm + 2 * halo
    margin = 8

    def mod(x, sc, sh):
        return (_rms(x, g_ref[...]) * (1 + sc[0]) + sh[0]).astype(BF16)

    @pl.when(t == 0)
    def _():
        ua_ref[...] = jnp.zeros_like(ua_ref)
        ug_ref[...] = jnp.zeros_like(ug_ref)
        h_cur[0:halo] = mod(x0p_ref[...], sc0_ref, sh0_ref)
        h_cur[halo:halo + tm] = mod(x0_ref[...], sc0_ref, sh0_ref)
        h_cur[halo + tm:ext] = mod(x0n_ref[...], sc0_ref, sh0_ref)

    tc = jnp.minimum(t, n_tiles - 1)

    @pl.when((t > 0) & (t < n_tiles) & (t % nj == 0))
    def _():
        h_cur[...] = h_nxt[...]

    def next_row_piece():
        p = jnp.minimum(tc % nj, n_pieces - 1)
        h_nxt[0:halo] = mod(xp_ref[...], sc_ref, sh_ref)
        h_nxt[pl.ds(pl.multiple_of(halo + p * pr, BF16_ROWS), pr), :] = mod(xm_ref[...], sc_ref, sh_ref)
        h_nxt[halo + tm:ext] = mod(xn_ref[...], sc_ref, sh_ref)

    slot = t % 2
    prev = 1 - slot
    ip = jnp.maximum(t - 1, 0) // nj
    first_row = jnp.where((ip % tiles_per_seq) == 0, 0, -1)
    last_row = jnp.where((ip % tiles_per_seq) == tiles_per_seq - 1, tm - 1, -1)
    row = lax.broadcasted_iota(jnp.int32, (tm, 1), 0)

    def epilogue(lo, hi):
        cm = hi - lo
        rowc = row[lo:hi]

        def conv(u_ref, cw_ref, cb_ref):
            win = u_ref[prev, pl.ds(halo + lo - margin, cm + 2 * margin), :]
            up = pltpu.roll(win, 1, 0)[margin:margin + cm]
            un = pltpu.roll(win, cm + 2 * margin - 1, 0)[margin:margin + cm]
            up = jnp.where(rowc == first_row, 0.0, up)
            un = jnp.where(rowc == last_row, 0.0, un)
            out = cb_ref[...] + up * cw_ref[0:1]
            out = out + win[margin:margin + cm] * cw_ref[1:2]
            return out + un * cw_ref[2:3]

        a = conv(ua_ref, cwa_ref, cba_ref)
        gte = conv(ug_ref, cwg_ref, cbg_ref)
        o_ref[lo:hi] = (jax.nn.silu(gte) * a).astype(BF16)

    half = ext // 2
    third = (tm // 3) // BF16_ROWS * BF16_ROWS
    chunks = [(0, third), (third, 2 * third), (2 * third, tm)]
    ua_ref[slot, 0:half] = jnp.dot(h_cur[0:half], wa_ref[...], preferred_element_type=F32)
    next_row_piece()
    epilogue(*chunks[0])
    ua_ref[slot, half:ext] = jnp.dot(h_cur[half:ext], wa_ref[...], preferred_element_type=F32)
    epilogue(*chunks[1])
    ug_ref[slot, 0:half] = jnp.dot(h_cur[0:half], wg_ref[...], preferred_element_type=F32)
    epilogue(*chunks[2])
    ug_ref[slot, half:ext] = jnp.dot(h_cur[half:ext], wg_ref[...], preferred_element_type=F32)


def _ffn_up_call(x2, sc, sh, g, w_up, conv_w, conv_b, *, tm, tn, seq):
    n, d = x2.shape
    dff = w_up.shape[1] // 2
    nj = dff // tn
    halo = BF16_ROWS
    hb = tm // halo
    n_halo = n // halo
    tiles_per_seq = seq // tm
    n_rows = n // tm
    n_tiles = n_rows * nj
    pr = 128
    n_pieces = tm // pr
    assert (tm + 2 * halo) % (2 * BF16_ROWS) == 0 and n_pieces <= nj and n_pieces * pr == tm

    def cur(t):
        tc = jnp.minimum(t, n_tiles - 1)
        return tc // nj, tc % nj

    def prv(t):
        tp = jnp.maximum(t - 1, 0)
        return tp // nj, tp % nj

    def nxt(t):
        return jnp.minimum(cur(t)[0] + 1, n_rows - 1)

    once = lambda shape, imap: pl.BlockSpec(shape, imap, pipeline_mode=pl.Buffered(1))
    return pl.pallas_call(
        functools.partial(_ffn_up_kernel, nj=nj, n_tiles=n_tiles, tiles_per_seq=tiles_per_seq, n_pieces=n_pieces),
        grid=(n_tiles + 1,),
        in_specs=[once((halo, d), lambda t: (0, 0)),
                  once((tm, d), lambda t: (0, 0)),
                  once((halo, d), lambda t: (jnp.minimum(hb, n_halo - 1), 0)),
                  once((1, 1, d), lambda t: (0, 0, 0)),
                  once((1, 1, d), lambda t: (0, 0, 0)),
                  pl.BlockSpec((halo, d), lambda t: (jnp.maximum(nxt(t) * hb - 1, 0), 0)),
                  pl.BlockSpec((pr, d), lambda t: (nxt(t) * n_pieces + jnp.minimum(cur(t)[1], n_pieces - 1), 0)),
                  pl.BlockSpec((halo, d), lambda t: (jnp.minimum((nxt(t) + 1) * hb, n_halo - 1), 0)),
                  pl.BlockSpec((1, 1, d), lambda t: (nxt(t) // tiles_per_seq, 0, 0)),
                  pl.BlockSpec((1, 1, d), lambda t: (nxt(t) // tiles_per_seq, 0, 0)),
                  pl.BlockSpec((1, d), lambda t: (0, 0)),
                  pl.BlockSpec((d, tn), lambda t: (0, cur(t)[1])),
                  pl.BlockSpec((d, tn), lambda t: (0, cur(t)[1] + nj)),
                  pl.BlockSpec((CONV_W, tn), lambda t: (0, prv(t)[1])),
                  pl.BlockSpec((CONV_W, tn), lambda t: (0, prv(t)[1] + nj)),
                  pl.BlockSpec((1, tn), lambda t: (0, prv(t)[1])),
                  pl.BlockSpec((1, tn), lambda t: (0, prv(t)[1] + nj))],
        out_specs=pl.BlockSpec((tm, tn), lambda t: prv(t)),
        out_shape=jax.ShapeDtypeStruct((n, dff), BF16),
        scratch_shapes=[pltpu.VMEM((tm + 2 * halo, d), BF16),
                        pltpu.VMEM((tm + 2 * halo, d), BF16),
                        pltpu.VMEM((2, tm + 2 * halo, tn), F32),
                        pltpu.VMEM((2, tm + 2 * halo, tn), F32)],
        compiler_params=pltpu.CompilerParams(dimension_semantics=("arbitrary",),
                                             vmem_limit_bytes=VMEM_LIMIT_BYTES),
        name="ffn_up",
    )(x2, x2, x2, sc, sh, x2, x2, x2, sc, sh, g, w_up, w_up, conv_w, conv_w, conv_b, conv_b)


def _rope_tables(seq):
    t = jnp.arange(seq)
    half = HEAD_DIM // 4
    inv = ROPE_BASE ** (-jnp.arange(half, dtype=F32) / half)

    def cs(pos):
        ang = pos.astype(F32)[:, None] * inv[None, :]
        c, s = jnp.cos(ang), jnp.sin(ang)
        return jnp.concatenate([c, c], axis=1), jnp.concatenate([-s, s], axis=1)

    cr, sr = cs(t // GRID_W)
    cc, sn = cs(t % GRID_W)
    return jnp.concatenate([cr, cc], axis=1), jnp.concatenate([sr, sn], axis=1)


def _pick_tile(n, pref):
    while n % pref:
        pref //= 2
    return pref


def kernel(x, c, ctx, c_ctx, w_mod, b_mod, g_attn_pre, g_attn_post, g_ffn_pre, g_ffn_post, w_in, sink_a, rpb_b,
           w_br_a, w_br_b, w_o, w_up, conv_w, conv_b, w_down):
    batch, seq, d = x.shape
    ctx_len = ctx.shape[1]
    assert w_mod.shape[0] == 1 and d == 16 * HEAD_DIM
    assert seq % QB_A == 0 and seq % (ROW_GROUP_B * GRID_W) == 0 and WINDOW == BLOCK
    n = batch * seq
    x2 = x.reshape(n, d)

    pad = (-(batch + 1)) % 8
    cs = jnp.concatenate([c, c_ctx[None, :], jnp.zeros((pad, d), F32)], axis=0)
    mod = _mod_call(cs, w_mod[0], b_mod[0][None, :])
    sh1, sc1, gt1, sh2, sc2, gt2 = [mod[:batch, k * d:(k + 1) * d].reshape(batch, 1, d) for k in range(6)]
    csh1 = mod[batch:batch + 1, 0:d].reshape(1, 1, d)
    csc1 = mod[batch:batch + 1, d:2 * d].reshape(1, 1, d)

    o_qa, o_ka, o_va, o_qb, o_kb, o_vb, o_ga = 0, W_QA, W_QA + W_KA, W_QA + 2 * W_KA, W_QA + 2 * W_KA + W_B, \
        W_QA + 2 * W_KA + 2 * W_B, W_QA + 2 * W_KA + 3 * W_B
    wi = w_in[0]
    w_perm = jnp.concatenate([wi[:, o_ga:], wi[:, o_qa:o_ka], wi[:, o_qb:o_ga], wi[:, o_ka:o_qb]], axis=1).astype(BF16)
    n_gate_cols = 2 * d
    n_tiles = w_perm.shape[1] // TN
    j0q = n_gate_cols // TN
    c_qa = 0
    c_qb = c_qa + W_QA
    c_kb = c_qb + W_B
    c_vb = c_kb + W_B
    c_ka = c_vb + W_B
    c_va = c_ka + W_KA
    cos, sin = _rope_tables(seq)
    tm = _pick_tile(seq, 1024)
    gates, h1 = _gate_call(x2, sc1, sh1, g_attn_pre, w_perm, tm=tm, tn=TN_GATE, n_cols=n_gate_cols, rows_per_mod=seq)
    proj = _qkv_call(h1, w_perm, cos, sin, tm=_pick_tile(seq, 2048), j0=j0q, nj=n_tiles - j0q)
    nc = batch * ctx_len
    j0c = j0q + c_kb // TN
    cproj = _inproj_call(ctx.reshape(nc, d), csc1, csh1, g_attn_pre, w_perm,
                         jnp.ones((nc, HEAD_DIM), F32), jnp.zeros((nc, HEAD_DIM), F32),
                         tm=nc, rows_per_mod=nc, j0=j0c, nj=n_tiles - j0c)
    cc_kb, cc_vb, cc_ka, cc_va = 0, W_B, 2 * W_B, 2 * W_B + W_KA

    oa = _attn_a_call(proj, cproj, sink_a[0], batch=batch, seq=seq, ctx_len=ctx_len,
                      q_col=c_qa, k_col=c_ka, v_col=c_va, ck_col=cc_ka, cv_col=cc_va)
    bias = _rpb_call(rpb_b[0])
    ob = _attn_b_call(proj, cproj, bias, batch=batch, seq=seq, ctx_len=ctx_len,
                      q_col=c_qb, k_col=c_kb, v_col=c_vb, ck_col=cc_kb, cv_col=cc_vb)

    tm2 = _pick_tile(seq, 512)
    z = _mix_call(oa, ob, gates, w_br_a[0].astype(BF16), w_br_b[0].astype(BF16), tm=tm2)
    x1 = _outnorm_call(z, w_o[0].astype(BF16), x2, gt1, g_attn_post, tm=tm2, tk=d, rows_per_mod=seq)

    dff = w_down.shape[1]
    act = _ffn_up_call(x1, sc2, sh2, g_ffn_pre, w_up[0].astype(BF16), conv_w[0], conv_b[0][None, :],
                       tm=tm, tn=TN, seq=seq)
    out = _rowmm_norm_call(act, w_down[0].astype(BF16), x1, gt2, g_ffn_post, tm=_pick_tile(seq, 256), n_chunks=2,
                           rows_per_mod=seq)
    return out.reshape(batch, seq, d)
```

```python
import functools

import jax
import jax.numpy as jnp
from jax import lax
from jax.experimental import pallas as pl
from jax.experimental.pallas import tpu as pltpu

F32 = jnp.float32
BF16 = jnp.bfloat16

GRID_W = 64
HEAD_DIM = 128
N_HEADS_A = 8
N_KV_A = 2
GQA_GROUP = N_HEADS_A // N_KV_A
WINDOW = 128
BLOCK = 128
N_HEADS_B = 8
NB_ROWS = 8
NB_COLS = 16
CONV_W = 3
ROPE_BASE = 10000.0
EPS = 1e-6
NEG_INF = -1e30
ATTN_SCALE = HEAD_DIM ** -0.5
LOG2E = 1.4426950408889634
Q_SCALE = ATTN_SCALE * LOG2E
F32_MAX = 3.4028234663852886e38

W_QA = N_HEADS_A * HEAD_DIM
W_KA = N_KV_A * HEAD_DIM
W_B = N_HEADS_B * HEAD_DIM

LANES = 128
BF16_ROWS = 16
VMEM_LIMIT_BYTES = 56 * 1024 * 1024
VMEM_LIMIT_RESIDENT_BYTES = 60 * 1024 * 1024

TN = 512
TN_GATE = 1024
ROW_CHUNK = 256
ROW_GROUP_B = 4
GROUPS_PER_STEP_B = 2
QB_A = 1024


def _rms(x, g):
    return (x * lax.rsqrt(jnp.mean(x * x, axis=-1, keepdims=True) + EPS)) * g


def _mod_kernel(c_ref, w_ref, b_ref, o_ref):
    a = jax.nn.silu(c_ref[...]).astype(BF16)
    o_ref[...] = jnp.dot(a, w_ref[...].astype(BF16), preferred_element_type=F32) + b_ref[...]


def _mod_call(cs, w, b):
    m, d = cs.shape
    n = w.shape[1]
    tn = 1536 if n % 1536 == 0 else n
    return pl.pallas_call(
        _mod_kernel,
        grid=(n // tn,),
        in_specs=[pl.BlockSpec((m, d), lambda j: (0, 0)),
                  pl.BlockSpec((d, tn), lambda j: (0, j)),
                  pl.BlockSpec((1, tn), lambda j: (0, j))],
        out_specs=pl.BlockSpec((m, tn), lambda j: (0, j)),
        out_shape=jax.ShapeDtypeStruct((m, n), F32),
        compiler_params=pltpu.CompilerParams(dimension_semantics=("arbitrary",),
                                             vmem_limit_bytes=VMEM_LIMIT_BYTES),
        name="mod",
    )(cs, w, b)


def _rpb_kernel(rpb_ref, o_ref):
    h = pl.program_id(0)
    n_dr = 2 * NB_ROWS - 1
    n_dc = 2 * NB_COLS - 1
    qc = lax.broadcasted_iota(jnp.int32, (GRID_W, LANES), 0)
    lane = lax.broadcasted_iota(jnp.int32, (GRID_W, LANES), 1)
    kc = lane & (GRID_W - 1)
    dc_idx = jnp.clip(kc - qc, -(NB_COLS - 1), NB_COLS - 1) + (NB_COLS - 1)
    base = h * (n_dr * n_dc)

    def toeplitz(dr):
        def body(dc, acc):
            return jnp.where(dc_idx == dc, rpb_ref[base + dr * n_dc + dc] * LOG2E, acc)
        return lax.fori_loop(0, n_dc, body, jnp.zeros((GRID_W, LANES), F32))

    tw = [toeplitz(dr) for dr in range(n_dr)]
    pair = [jnp.where(lane < GRID_W, tw[d], tw[d + 1]) for d in range(n_dr - 1)]
    n_kr = 3 * ROW_GROUP_B
    dr0 = NB_ROWS - 1 - ROW_GROUP_B
    for rq in range(ROW_GROUP_B):
        for p in range(n_kr // 2):
            o_ref[0, rq * GRID_W:(rq + 1) * GRID_W, p * LANES:(p + 1) * LANES] = pair[2 * p - rq + dr0]


def _rpb_call(rpb):
    nh = rpb.shape[0]
    nq = ROW_GROUP_B * GRID_W
    nk = 3 * ROW_GROUP_B * GRID_W
    assert 2 * ROW_GROUP_B == NB_ROWS
    return pl.pallas_call(
        _rpb_kernel,
        grid=(nh,),
        in_specs=[pl.BlockSpec(memory_space=pltpu.SMEM)],
        out_specs=pl.BlockSpec((1, nq, nk), lambda h: (h, 0, 0)),
        out_shape=jax.ShapeDtypeStruct((nh, nq, nk), F32),
        compiler_params=pltpu.CompilerParams(dimension_semantics=("arbitrary",)),
        name="rpb_table",
    )(rpb.reshape(-1))


def _rope(t, cos, sin, first_half):
    partner = jnp.where(first_half, pltpu.roll(t, LANES - 32, 1), pltpu.roll(t, 32, 1))
    return t * cos + partner * sin


def _row_chunked_matmul(h_ref, w_ref, epilogue, between=None):
    n_chunks = h_ref.shape[0] // ROW_CHUNK
    acc = jnp.dot(h_ref[0:ROW_CHUNK], w_ref[...], preferred_element_type=F32)
    for c in range(n_chunks):
        nxt = None
        if c + 1 < n_chunks:
            nxt = jnp.dot(h_ref[(c + 1) * ROW_CHUNK:(c + 2) * ROW_CHUNK], w_ref[...], preferred_element_type=F32)
        epilogue(slice(c * ROW_CHUNK, (c + 1) * ROW_CHUNK), acc)
        if c == 0 and between is not None:
            between()
        acc = nxt


def _gate_kernel(x_ref, sc_ref, sh_ref, g_ref, w_ref, o_ref, hout_ref, hs_ref, *, n_rows):
    i = pl.program_id(0)
    j = pl.program_id(1)
    pr = x_ref.shape[0]

    def norm_piece():
        h = (_rms(x_ref[...], g_ref[...]) * (1 + sc_ref[0]) + sh_ref[0]).astype(BF16)
        hout_ref[...] = h
        hs_ref[i % 2, pl.ds(pl.multiple_of(j * pr, pr), pr), :] = h

    def epilogue(rows, acc):
        o_ref[rows] = jax.nn.sigmoid(acc).astype(BF16)

    @pl.when(i == 0)
    def _():
        norm_piece()

    @pl.when((i > 0) & (i < n_rows))
    def _():
        _row_chunked_matmul(hs_ref.at[(i + 1) % 2], w_ref, epilogue, between=norm_piece)

    @pl.when(i == n_rows)
    def _():
        _row_chunked_matmul(hs_ref.at[(i + 1) % 2], w_ref, epilogue)


def _gate_call(x2, sc, sh, g, w, *, tm, tn, n_cols, rows_per_mod):
    n, d = x2.shape
    n_rows = n // tm
    nj = n_cols // tn
    pr = tm // nj
    assert pr * nj == tm and pr % BF16_ROWS == 0
    mod_tiles = rows_per_mod // tm
    last = n_rows * nj - 1
    piece_map = lambda i, j: (jnp.minimum(i * nj + j, last), 0)
    mod_map = lambda i, j: (jnp.minimum(i, n_rows - 1) // mod_tiles, 0, 0)
    return pl.pallas_call(
        functools.partial(_gate_kernel, n_rows=n_rows),
        grid=(n_rows + 1, nj),
        in_specs=[pl.BlockSpec((pr, d), piece_map),
                  pl.BlockSpec((1, 1, d), mod_map),
                  pl.BlockSpec((1, 1, d), mod_map),
                  pl.BlockSpec((1, d), lambda i, j: (0, 0)),
                  pl.BlockSpec((d, tn), lambda i, j: (0, j))],
        out_specs=[pl.BlockSpec((tm, tn), lambda i, j: (jnp.maximum(i - 1, 0), jnp.where(i > 0, j, 0))),
                   pl.BlockSpec((pr, d), piece_map)],
        out_shape=[jax.ShapeDtypeStruct((n, n_cols), BF16), jax.ShapeDtypeStruct((n, d), BF16)],
        scratch_shapes=[pltpu.VMEM((2, tm, d), BF16)],
        compiler_params=pltpu.CompilerParams(dimension_semantics=("arbitrary", "arbitrary"),
                                             vmem_limit_bytes=VMEM_LIMIT_BYTES),
        name="gate_proj",
    )(x2, sc, sh, g, w)


def _qkv_kernel(h_ref, w_ref, cos_ref, sin_ref, o_ref, *, j0, n_gate):
    _proj_tile(h_ref, w_ref, cos_ref, sin_ref, o_ref, pl.program_id(1) + j0, n_gate)


def _qkv_call(h, w, cos, sin, *, tm, j0, nj):
    n, d = h.shape
    pos_tiles = cos.shape[0] // tm
    return pl.pallas_call(
        functools.partial(_qkv_kernel, j0=j0, n_gate=2 * d // TN),
        grid=(n // tm, nj),
        in_specs=[pl.BlockSpec((tm, d), lambda i, j: (i, 0)),
                  pl.BlockSpec((d, TN), lambda i, j: (0, j + j0)),
                  pl.BlockSpec((tm, HEAD_DIM), lambda i, j: (i % pos_tiles, 0)),
                  pl.BlockSpec((tm, HEAD_DIM), lambda i, j: (i % pos_tiles, 0))],
        out_specs=pl.BlockSpec((tm, TN), lambda i, j: (i, j)),
        out_shape=jax.ShapeDtypeStruct((n, nj * TN), BF16),
        compiler_params=pltpu.CompilerParams(dimension_semantics=("arbitrary", "arbitrary"),
                                             vmem_limit_bytes=VMEM_LIMIT_BYTES),
        name="qkv_proj",
    )(h, w, cos, sin)


def _inproj_kernel(x_ref, sc_ref, sh_ref, g_ref, w_ref, cos_ref, sin_ref, o_ref, h_ref, *, j0):
    j = pl.program_id(1)

    @pl.when(j == 0)
    def _():
        h = _rms(x_ref[...], g_ref[...]) * (1 + sc_ref[0]) + sh_ref[0]
        h_ref[...] = h.astype(BF16)

    _proj_tile(h_ref, w_ref, cos_ref, sin_ref, o_ref, j + j0, 2 * x_ref.shape[1] // TN)


def _proj_tile(h_ref, w_ref, cos_ref, sin_ref, o_ref, jj, n_gate):
    lane = lax.broadcasted_iota(jnp.int32, (1, HEAD_DIM), 1)
    first_half = (lane & 63) < 32

    def row_chunks(epilogue):
        _row_chunked_matmul(h_ref, w_ref, epilogue)

    def rope_heads(rows, acc, n_heads, scale):
        cos = cos_ref[rows]
        sin = sin_ref[rows]
        for hd in range(n_heads):
            sl = slice(hd * HEAD_DIM, (hd + 1) * HEAD_DIM)
            r = _rope(acc[:, sl], cos, sin, first_half)
            if scale is not None:
                r = r * scale
            o_ref[rows, sl] = r.astype(BF16)

    @pl.when((jj >= n_gate) & (jj < n_gate + 2))
    def _():
        row_chunks(lambda rows, acc: rope_heads(rows, acc, TN // HEAD_DIM, Q_SCALE))

    @pl.when((jj >= n_gate + 2) & (jj < n_gate + 4))
    def _():
        def epilogue(rows, acc):
            o_ref[rows] = (acc * Q_SCALE).astype(BF16)
        row_chunks(epilogue)

    @pl.when((jj >= n_gate + 4) & (jj < n_gate + 8))
    def _():
        def epilogue(rows, acc):
            o_ref[rows] = acc.astype(BF16)
        row_chunks(epilogue)

    @pl.when(jj == n_gate + 8)
    def _():
        def epilogue(rows, acc):
            rope_heads(rows, acc, N_KV_A, None)
            o_ref[rows, W_KA:] = acc[:, W_KA:].astype(BF16)
        row_chunks(epilogue)


def _inproj_call(x2, sc, sh, g, w, cos, sin, *, tm, rows_per_mod, j0, nj):
    n, d = x2.shape
    pos_tiles = cos.shape[0] // tm
    mod_tiles = rows_per_mod // tm
    return pl.pallas_call(
        functools.partial(_inproj_kernel, j0=j0),
        grid=(n // tm, nj),
        in_specs=[pl.BlockSpec((tm, d), lambda i, j: (i, 0)),
                  pl.BlockSpec((1, 1, d), lambda i, j: (i // mod_tiles, 0, 0)),
                  pl.BlockSpec((1, 1, d), lambda i, j: (i // mod_tiles, 0, 0)),
                  pl.BlockSpec((1, d), lambda i, j: (0, 0)),
                  pl.BlockSpec((d, TN), lambda i, j: (0, j + j0)),
                  pl.BlockSpec((tm, HEAD_DIM), lambda i, j: (i % pos_tiles, 0)),
                  pl.BlockSpec((tm, HEAD_DIM), lambda i, j: (i % pos_tiles, 0))],
        out_specs=pl.BlockSpec((tm, TN), lambda i, j: (i, j)),
        out_shape=jax.ShapeDtypeStruct((n, nj * TN), BF16),
        scratch_shapes=[pltpu.VMEM((tm, d), BF16)],
        compiler_params=pltpu.CompilerParams(dimension_semantics=("arbitrary", "arbitrary"),
                                             vmem_limit_bytes=VMEM_LIMIT_BYTES),
        name="in_proj",
    )(x2, sc, sh, g, w, cos, sin)


def _attn_a_kernel(sink_ref, q_ref, kp_ref, km_ref, kn_ref, vp_ref, vm_ref, vn_ref, kc_ref, vc_ref, o_ref,
                   *, n_blocks):
    nq = pl.program_id(1)
    sub = QB_A // BLOCK
    rows = GQA_GROUP * BLOCK
    n_loc = 3 * BLOCK
    r_io = lax.broadcasted_iota(jnp.int32, (rows, n_loc), 0)
    c_io = lax.broadcasted_iota(jnp.int32, (rows, n_loc), 1)
    qi_io = r_io & (BLOCK - 1)
    in_band = (c_io >= qi_io) & (c_io <= qi_io + 2 * WINDOW)
    c_row = lax.broadcasted_iota(jnp.int32, (1, n_loc), 1)
    g_io = lax.shift_right_logical(lax.broadcasted_iota(jnp.int32, (rows, 1), 0), BLOCK.bit_length() - 1)
    nt = (((1,), (1,)), ((), ()))
    for kvh in range(N_KV_A):
        hs = slice(kvh * HEAD_DIM, (kvh + 1) * HEAD_DIM)
        kband = jnp.concatenate([kp_ref[:, hs], km_ref[:, hs], kn_ref[:, hs]], axis=0)
        vband = jnp.concatenate([vp_ref[:, hs], vm_ref[:, hs], vn_ref[:, hs]], axis=0)
        kc = kc_ref[:, hs]
        vc = vc_ref[:, hs]
        sink = jnp.zeros((rows, 1), F32)
        for g in range(GQA_GROUP):
            sink = jnp.where(g_io == g, sink_ref[kvh * GQA_GROUP + g] * LOG2E, sink)
        for qi in range(sub):
            blk = nq * sub + qi
            lo = jnp.where(blk == 0, BLOCK, 0)
            hi = jnp.where(blk == n_blocks - 1, 2 * BLOCK - 1, n_loc - 1)
            valid = in_band & ((c_row >= lo) & (c_row <= hi))
            q4 = jnp.concatenate(
                [q_ref[qi * BLOCK:(qi + 1) * BLOCK, (kvh * GQA_GROUP + g) * HEAD_DIM:(kvh * GQA_GROUP + g + 1) * HEAD_DIM]
                 for g in range(GQA_GROUP)], axis=0)
            s_loc = lax.dot_general(q4, kband[qi * BLOCK:qi * BLOCK + n_loc], nt, preferred_element_type=F32)
            s_loc = jnp.where(valid, s_loc, NEG_INF)
            s_ctx = lax.dot_general(q4, kc, nt, preferred_element_type=F32)
            m = jnp.maximum(jnp.maximum(jnp.max(s_loc, axis=-1, keepdims=True),
                                        jnp.max(s_ctx, axis=-1, keepdims=True)), sink)
            e_loc = jnp.exp2(s_loc - m)
            e_ctx = jnp.exp2(s_ctx - m)
            den = (jnp.sum(e_loc, axis=-1, keepdims=True) + jnp.sum(e_ctx, axis=-1, keepdims=True)
                   + jnp.exp2(sink - m))
            o = (jnp.dot(e_loc.astype(BF16), vband[qi * BLOCK:qi * BLOCK + n_loc], preferred_element_type=F32)
                 + jnp.dot(e_ctx.astype(BF16), vc, preferred_element_type=F32)) * (1.0 / den)
            for g in range(GQA_GROUP):
                hd = kvh * GQA_GROUP + g
                o_ref[qi * BLOCK:(qi + 1) * BLOCK, hd * HEAD_DIM:(hd + 1) * HEAD_DIM] = (
                    o[g * BLOCK:(g + 1) * BLOCK].astype(BF16))


def _attn_a_call(proj, cproj, sink, *, batch, seq, ctx_len, q_col, k_col, v_col, ck_col, cv_col):
    nqb = seq // QB_A
    sub = QB_A // BLOCK
    n_blocks = seq // BLOCK

    def prev_map(b, n, col):
        return (b * n_blocks + jnp.maximum(n * sub - 1, 0), col // W_KA)

    def next_map(b, n, col):
        return (b * n_blocks + jnp.minimum(n * sub + sub, n_blocks - 1), col // W_KA)

    def main_map(b, n, col):
        return (b * nqb + n, col // W_KA)

    edge = lambda f, col: pl.BlockSpec((BLOCK, W_KA), functools.partial(f, col=col))
    main = lambda col: pl.BlockSpec((QB_A, W_KA), functools.partial(main_map, col=col))
    ctxs = lambda col: pl.BlockSpec((ctx_len, W_KA), lambda b, n: (b, col // W_KA))
    return pl.pallas_call(
        functools.partial(_attn_a_kernel, n_blocks=n_blocks),
        grid=(batch, nqb),
        in_specs=[pl.BlockSpec(memory_space=pltpu.SMEM),
                  pl.BlockSpec((QB_A, W_QA), lambda b, n: (b * nqb + n, q_col // W_QA)),
                  edge(prev_map, k_col), main(k_col), edge(next_map, k_col),
                  edge(prev_map, v_col), main(v_col), edge(next_map, v_col),
                  ctxs(ck_col), ctxs(cv_col)],
        out_specs=pl.BlockSpec((QB_A, W_QA), lambda b, n: (b * nqb + n, 0)),
        out_shape=jax.ShapeDtypeStruct((batch * seq, W_QA), BF16),
        compiler_params=pltpu.CompilerParams(dimension_semantics=("arbitrary",) * 2,
                                             vmem_limit_bytes=VMEM_LIMIT_BYTES),
        name="attn_a",
    )(sink, proj, proj, proj, proj, proj, proj, proj, cproj, cproj)


def _attn_b_kernel(q_ref, kp_ref, km_ref, kn_ref, vp_ref, vm_ref, vn_ref, kc_ref, vc_ref, bias_ref, o_ref,
                   *, grid_rows):
    n = pl.program_id(1)
    nq = ROW_GROUP_B * GRID_W
    nk = 3 * nq
    heads = q_ref.shape[1] // HEAD_DIM
    r_io = lax.broadcasted_iota(jnp.int32, (nq, nk), 0)
    c_io = lax.broadcasted_iota(jnp.int32, (nq, nk), 1)
    log_w = GRID_W.bit_length() - 1
    q_col = r_io & (GRID_W - 1)
    k_col = c_io & (GRID_W - 1)
    c_start = jnp.clip(q_col - NB_COLS // 2, 0, GRID_W - NB_COLS)
    col_ok = (k_col >= c_start) & (k_col < c_start + NB_COLS)
    nt = (((1,), (1,)), ((), ()))
    for grp in range(GROUPS_PER_STEP_B):
        gi = n * GROUPS_PER_STEP_B + grp
        r_abs = gi * ROW_GROUP_B + lax.shift_right_logical(r_io, log_w)
        kr_abs = (gi - 1) * ROW_GROUP_B + lax.shift_right_logical(c_io, log_w)
        r_start = jnp.clip(r_abs - NB_ROWS // 2, 0, grid_rows - NB_ROWS)
        valid = (kr_abs >= r_start) & (kr_abs < r_start + NB_ROWS) & col_ok
        cap = jnp.where(valid, F32_MAX, NEG_INF)
        rows = slice(grp * nq, (grp + 1) * nq)
        for hd in range(heads):
            sl = slice(hd * HEAD_DIM, (hd + 1) * HEAD_DIM)
            q = q_ref[rows, sl]
            keys = jnp.concatenate([kp_ref[:, sl], km_ref[:, sl], kn_ref[:, sl]], axis=0)[grp * nq:grp * nq + nk]
            vals = jnp.concatenate([vp_ref[:, sl], vm_ref[:, sl], vn_ref[:, sl]], axis=0)[grp * nq:grp * nq + nk]
            s_nb = lax.dot_general(q, keys, nt, preferred_element_type=F32)
            s_nb = jnp.minimum(s_nb + bias_ref[hd], cap)
            s_ctx = lax.dot_general(q, kc_ref[:, sl], nt, preferred_element_type=F32)
            m = jnp.maximum(jnp.max(s_nb, axis=-1, keepdims=True), jnp.max(s_ctx, axis=-1, keepdims=True))
            e_nb = jnp.exp2(s_nb - m)
            e_ctx = jnp.exp2(s_ctx - m)
            den = jnp.sum(e_nb, axis=-1, keepdims=True) + jnp.sum(e_ctx, axis=-1, keepdims=True)
            o = (jnp.dot(e_nb.astype(BF16), vals, preferred_element_type=F32)
                 + jnp.dot(e_ctx.astype(BF16), vc_ref[:, sl], preferred_element_type=F32)) * (1.0 / den)
            o_ref[rows, sl] = o.astype(BF16)


def _attn_b_call(proj, cproj, bias, *, batch, seq, ctx_len, q_col, k_col, v_col, ck_col, cv_col):
    nq = ROW_GROUP_B * GRID_W
    ng = seq // nq
    gps = GROUPS_PER_STEP_B
    ns = ng // gps
    hw = W_B

    def edge(col, first):
        def imap(b, n):
            g = jnp.maximum(n * gps - 1, 0) if first else jnp.minimum(n * gps + gps, ng - 1)
            return (b * ng + g, col // hw)
        return pl.BlockSpec((nq, hw), imap)

    main = lambda col: pl.BlockSpec((gps * nq, hw), lambda b, n: (b * ns + n, col // hw))
    ctxs = lambda col: pl.BlockSpec((ctx_len, hw), lambda b, n: (b, col // hw))
    return pl.pallas_call(
        functools.partial(_attn_b_kernel, grid_rows=seq // GRID_W),
        grid=(batch, ns),
        in_specs=[main(q_col),
                  edge(k_col, True), main(k_col), edge(k_col, False),
                  edge(v_col, True), main(v_col), edge(v_col, False),
                  ctxs(ck_col), ctxs(cv_col),
                  pl.BlockSpec((hw // HEAD_DIM, nq, bias.shape[2]), lambda b, n: (0, 0, 0))],
        out_specs=pl.BlockSpec((gps * nq, hw), lambda b, n: (b * ns + n, 0)),
        out_shape=jax.ShapeDtypeStruct((batch * seq, W_B), BF16),
        compiler_params=pltpu.CompilerParams(dimension_semantics=("arbitrary",) * 2,
                                             vmem_limit_bytes=VMEM_LIMIT_BYTES),
        name="attn_b",
    )(proj, proj, proj, proj, proj, proj, proj, cproj, cproj, bias)


def _mix_kernel(oa_ref, ob_ref, ga_ref, gb_ref, wa_ref, wb_ref, z_ref):
    ya = jnp.dot(oa_ref[...], wa_ref[...], preferred_element_type=F32)
    yb = jnp.dot(ob_ref[...], wb_ref[...], preferred_element_type=F32)
    z_ref[...] = (ga_ref[...].astype(F32) * ya + gb_ref[...].astype(F32) * yb).astype(BF16)


def _mix_call(oa, ob, proj, wa, wb, *, tm):
    n = oa.shape[0]
    d = wa.shape[1]
    const = lambda shape: pl.BlockSpec(shape, lambda i: (0, 0), pipeline_mode=pl.Buffered(1))
    return pl.pallas_call(
        _mix_kernel,
        grid=(n // tm,),
        in_specs=[pl.BlockSpec((tm, oa.shape[1]), lambda i: (i, 0)),
                  pl.BlockSpec((tm, ob.shape[1]), lambda i: (i, 0)),
                  pl.BlockSpec((tm, d), lambda i: (i, 0)),
                  pl.BlockSpec((tm, d), lambda i: (i, 1)),
                  const(wa.shape), const(wb.shape)],
        out_specs=pl.BlockSpec((tm, d), lambda i: (i, 0)),
        out_shape=jax.ShapeDtypeStruct((n, d), BF16),
        compiler_params=pltpu.CompilerParams(dimension_semantics=("arbitrary",),
                                             vmem_limit_bytes=VMEM_LIMIT_BYTES),
        name="branch_mix",
    )(oa, ob, proj, proj, wa, wb)


def _outnorm_kernel(a_ref, w_ref, x_ref, gt_ref, g_ref, o_ref):
    y = jnp.dot(a_ref[...], w_ref[...], preferred_element_type=F32)
    o_ref[...] = x_ref[...] + gt_ref[0] * _rms(y, g_ref[...])


def _outnorm_call(a, w, x2, gt, g, *, tm, rows_per_mod):
    n, kdim = a.shape
    d = w.shape[1]
    mod_tiles = rows_per_mod // tm
    return pl.pallas_call(
        _outnorm_kernel,
        grid=(n // tm,),
        in_specs=[pl.BlockSpec((tm, kdim), lambda i: (i, 0)),
                  pl.BlockSpec((kdim, d), lambda i: (0, 0), pipeline_mode=pl.Buffered(1)),
                  pl.BlockSpec((tm, d), lambda i: (i, 0)),
                  pl.BlockSpec((1, 1, d), lambda i: (i // mod_tiles, 0, 0)),
                  pl.BlockSpec((1, d), lambda i: (0, 0))],
        out_specs=pl.BlockSpec((tm, d), lambda i: (i, 0)),
        out_shape=jax.ShapeDtypeStruct((n, d), F32),
        compiler_params=pltpu.CompilerParams(dimension_semantics=("arbitrary",),
                                             vmem_limit_bytes=VMEM_LIMIT_BYTES),
        name="out_norm",
    )(a, w, x2, gt, g)


def _rowmm_norm_kernel(a_ref, w_ref, x_ref, gt_ref, g_ref, o_ref, *, n_chunks):
    cm = a_ref.shape[0] // n_chunks
    y = jnp.dot(a_ref[0:cm], w_ref[...], preferred_element_type=F32)
    for c in range(n_chunks):
        nxt = None
        if c + 1 < n_chunks:
            nxt = jnp.dot(a_ref[(c + 1) * cm:(c + 2) * cm], w_ref[...], preferred_element_type=F32)
        rows = slice(c * cm, (c + 1) * cm)
        o_ref[rows] = x_ref[rows] + gt_ref[0] * _rms(y, g_ref[...])
        y = nxt


def _rowmm_norm_call(a, w, x2, gt, g, *, tm, n_chunks, rows_per_mod):
    n, kdim = a.shape
    d = w.shape[1]
    mod_tiles = rows_per_mod // tm
    return pl.pallas_call(
        functools.partial(_rowmm_norm_kernel, n_chunks=n_chunks),
        grid=(n // tm,),
        in_specs=[pl.BlockSpec((tm, kdim), lambda i: (i, 0)),
                  pl.BlockSpec((kdim, d), lambda i: (0, 0), pipeline_mode=pl.Buffered(1)),
                  pl.BlockSpec((tm, d), lambda i: (i, 0)),
                  pl.BlockSpec((1, 1, d), lambda i: (i // mod_tiles, 0, 0)),
                  pl.BlockSpec((1, d), lambda i: (0, 0))],
        out_specs=pl.BlockSpec((tm, d), lambda i: (i, 0)),
        out_shape=jax.ShapeDtypeStruct((n, d), F32),
        compiler_params=pltpu.CompilerParams(dimension_semantics=("arbitrary",),
                                             vmem_limit_bytes=VMEM_LIMIT_RESIDENT_BYTES),
        name="rowmm_norm",
    )(a, w, x2, gt, g)


def _ffn_up_kernel(x0p_ref, x0_ref, x0n_ref, sc0_ref, sh0_ref, xp_ref, xm_ref, xn_ref, sc_ref, sh_ref, g_ref,
                   wa_ref, wg_ref, cwa_ref, cwg_ref, cba_ref, cbg_ref,
                   o_ref, h_cur, h_nxt, ua_ref, ug_ref, *, nj, n_tiles, tiles_per_seq, n_pieces):
    t = pl.program_id(0)
    tm = x0_ref.shape[0]
    halo = xp_ref.shape[0]
    pr = xm_ref.shape[0]
    ext = tm + 2 * halo
    margin = 8

    def mod(x, sc, sh):
        return (_rms(x, g_ref[...]) * (1 + sc[0]) + sh[0]).astype(BF16)

    @pl.when(t == 0)
    def _():
        ua_ref[...] = jnp.zeros_like(ua_ref)
        ug_ref[...] = jnp.zeros_like(ug_ref)
        h_cur[0:halo] = mod(x0p_ref[...], sc0_ref, sh0_ref)
        h_cur[halo:halo + tm] = mod(x0_ref[...], sc0_ref, sh0_ref)
        h_cur[halo + tm:ext] = mod(x0n_ref[...], sc0_ref, sh0_ref)

    tc = jnp.minimum(t, n_tiles - 1)

    @pl.when((t > 0) & (t < n_tiles) & (t % nj == 0))
    def _():
        h_cur[...] = h_nxt[...]

    def next_row_piece():
        p = jnp.minimum(tc % nj, n_pieces - 1)
        h_nxt[0:halo] = mod(xp_ref[...], sc_ref, sh_ref)
        h_nxt[pl.ds(pl.multiple_of(halo + p * pr, BF16_ROWS), pr), :] = mod(xm_ref[...], sc_ref, sh_ref)
        h_nxt[halo + tm:ext] = mod(xn_ref[...], sc_ref, sh_ref)

    slot = t % 2
    prev = 1 - slot
    ip = jnp.maximum(t - 1, 0) // nj
    first_row = jnp.where((ip % tiles_per_seq) == 0, 0, -1)
    last_row = jnp.where((ip % tiles_per_seq) == tiles_per_seq - 1, tm - 1, -1)
    row = lax.broadcasted_iota(jnp.int32, (tm, 1), 0)

    def epilogue(lo, hi):
        cm = hi - lo
        rowc = row[lo:hi]

        def conv(u_ref, cw_ref, cb_ref):
            win = u_ref[prev, pl.ds(halo + lo - margin, cm + 2 * margin), :]
            up = pltpu.roll(win, 1, 0)[margin:margin + cm]
            un = pltpu.roll(win, cm + 2 * margin - 1, 0)[margin:margin + cm]
            up = jnp.where(rowc == first_row, 0.0, up)
            un = jnp.where(rowc == last_row, 0.0, un)
            out = cb_ref[...] + up * cw_ref[0:1]
            out = out + win[margin:margin + cm] * cw_ref[1:2]
            return out + un * cw_ref[2:3]

        a = conv(ua_ref, cwa_ref, cba_ref)
        gte = conv(ug_ref, cwg_ref, cbg_ref)
        o_ref[lo:hi] = (jax.nn.silu(gte) * a).astype(BF16)

    half = ext // 2
    third = (tm // 3) // BF16_ROWS * BF16_ROWS
    chunks = [(0, third), (third, 2 * third), (2 * third, tm)]
    ua_ref[slot, 0:half] = jnp.dot(h_cur[0:half], wa_ref[...], preferred_element_type=F32)
    next_row_piece()
    epilogue(*chunks[0])
    ua_ref[slot, half:ext] = jnp.dot(h_cur[half:ext], wa_ref[...], preferred_element_type=F32)
    epilogue(*chunks[1])
    ug_ref[slot, 0:half] = jnp.dot(h_cur[0:half], wg_ref[...], preferred_element_type=F32)
    epilogue(*chunks[2])
    ug_ref[slot, half:ext] = jnp.dot(h_cur[half:ext], wg_ref[...], preferred_element_type=F32)


def _ffn_up_call(x2, sc, sh, g, w_up, conv_w, conv_b, *, tm, tn, seq):
    n, d = x2.shape
    dff = w_up.shape[1] // 2
    nj = dff // tn
    halo = BF16_ROWS
    hb = tm // halo
    n_halo = n // halo
    tiles_per_seq = seq // tm
    n_rows = n // tm
    n_tiles = n_rows * nj
    pr = 128
    n_pieces = tm // pr
    assert (tm + 2 * halo) % (2 * BF16_ROWS) == 0 and n_pieces <= nj and n_pieces * pr == tm

    def cur(t):
        tc = jnp.minimum(t, n_tiles - 1)
        return tc // nj, tc % nj

    def prv(t):
        tp = jnp.maximum(t - 1, 0)
        return tp // nj, tp % nj

    def nxt(t):
        return jnp.minimum(cur(t)[0] + 1, n_rows - 1)

    once = lambda shape, imap: pl.BlockSpec(shape, imap, pipeline_mode=pl.Buffered(1))
    return pl.pallas_call(
        functools.partial(_ffn_up_kernel, nj=nj, n_tiles=n_tiles, tiles_per_seq=tiles_per_seq, n_pieces=n_pieces),
        grid=(n_tiles + 1,),
        in_specs=[once((halo, d), lambda t: (0, 0)),
                  once((tm, d), lambda t: (0, 0)),
                  once((halo, d), lambda t: (jnp.minimum(hb, n_halo - 1), 0)),
                  once((1, 1, d), lambda t: (0, 0, 0)),
                  once((1, 1, d), lambda t: (0, 0, 0)),
                  pl.BlockSpec((halo, d), lambda t: (jnp.maximum(nxt(t) * hb - 1, 0), 0)),
                  pl.BlockSpec((pr, d), lambda t: (nxt(t) * n_pieces + jnp.minimum(cur(t)[1], n_pieces - 1), 0)),
                  pl.BlockSpec((halo, d), lambda t: (jnp.minimum((nxt(t) + 1) * hb, n_halo - 1), 0)),
                  pl.BlockSpec((1, 1, d), lambda t: (nxt(t) // tiles_per_seq, 0, 0)),
                  pl.BlockSpec((1, 1, d), lambda t: (nxt(t) // tiles_per_seq, 0, 0)),
                  pl.BlockSpec((1, d), lambda t: (0, 0)),
                  pl.BlockSpec((d, tn), lambda t: (0, cur(t)[1])),
                  pl.BlockSpec((d, tn), lambda t: (0, cur(t)[1] + nj)),
                  pl.BlockSpec((CONV_W, tn), lambda t: (0, prv(t)[1])),
                  pl.BlockSpec((CONV_W, tn), lambda t: (0, prv(t)[1] + nj)),
                  pl.BlockSpec((1, tn), lambda t: (0, prv(t)[1])),
                  pl.BlockSpec((1, tn), lambda t: (0, prv(t)[1] + nj))],
        out_specs=pl.BlockSpec((tm, tn), lambda t: prv(t)),
        out_shape=jax.ShapeDtypeStruct((n, dff), BF16),
        scratch_shapes=[pltpu.VMEM((tm + 2 * halo, d), BF16),
                        pltpu.VMEM((tm + 2 * halo, d), BF16),
                        pltpu.VMEM((2, tm + 2 * halo, tn), F32),
                        pltpu.VMEM((2, tm + 2 * halo, tn), F32)],
        compiler_params=pltpu.CompilerParams(dimension_semantics=("arbitrary",),
                                             vmem_limit_bytes=VMEM_LIMIT_BYTES),
        name="ffn_up",
    )(x2, x2, x2, sc, sh, x2, x2, x2, sc, sh, g, w_up, w_up, conv_w, conv_w, conv_b, conv_b)


def _rope_tables(seq):
    t = jnp.arange(seq)
    half = HEAD_DIM // 4
    inv = ROPE_BASE ** (-jnp.arange(half, dtype=F32) / half)

    def cs(pos):
        ang = pos.astype(F32)[:, None] * inv[None, :]
        c, s = jnp.cos(ang), jnp.sin(ang)
        return jnp.concatenate([c, c], axis=1), jnp.concatenate([-s, s], axis=1)

    cr, sr = cs(t // GRID_W)
    cc, sn = cs(t % GRID_W)
    return jnp.concatenate([cr, cc], axis=1), jnp.concatenate([sr, sn], axis=1)


def _pick_tile(n, pref):
    while n % pref:
        pref //= 2
    return pref


def kernel(x, c, ctx, c_ctx, w_mod, b_mod, g_attn_pre, g_attn_post, g_ffn_pre, g_ffn_post, w_in, sink_a, rpb_b,
           w_br_a, w_br_b, w_o, w_up, conv_w, conv_b, w_down):
    batch, seq, d = x.shape
    ctx_len = ctx.shape[1]
    assert w_mod.shape[0] == 1 and d == 16 * HEAD_DIM
    assert seq % QB_A == 0 and seq % (GROUPS_PER_STEP_B * ROW_GROUP_B * GRID_W) == 0 and WINDOW == BLOCK
    n = batch * seq
    x2 = x.reshape(n, d)

    pad = (-(batch + 1)) % 8
    cs = jnp.concatenate([c, c_ctx[None, :], jnp.zeros((pad, d), F32)], axis=0)
    mod = _mod_call(cs, w_mod[0], b_mod[0][None, :])
    sh1, sc1, gt1, sh2, sc2, gt2 = [mod[:batch, k * d:(k + 1) * d].reshape(batch, 1, d) for k in range(6)]
    csh1 = mod[batch:batch + 1, 0:d].reshape(1, 1, d)
    csc1 = mod[batch:batch + 1, d:2 * d].reshape(1, 1, d)

    o_qa, o_ka, o_va, o_qb, o_kb, o_vb, o_ga = 0, W_QA, W_QA + W_KA, W_QA + 2 * W_KA, W_QA + 2 * W_KA + W_B, \
        W_QA + 2 * W_KA + 2 * W_B, W_QA + 2 * W_KA + 3 * W_B
    wi = w_in[0]
    w_perm = jnp.concatenate([wi[:, o_ga:], wi[:, o_qa:o_ka], wi[:, o_qb:o_ga], wi[:, o_ka:o_qb]], axis=1).astype(BF16)
    n_gate_cols = 2 * d
    n_tiles = w_perm.shape[1] // TN
    j0q = n_gate_cols // TN
    c_qa = 0
    c_qb = c_qa + W_QA
    c_kb = c_qb + W_B
    c_vb = c_kb + W_B
    c_ka = c_vb + W_B
    c_va = c_ka + W_KA
    cos, sin = _rope_tables(seq)
    tm = _pick_tile(seq, 1024)
    gates, h1 = _gate_call(x2, sc1, sh1, g_attn_pre, w_perm, tm=tm, tn=TN_GATE, n_cols=n_gate_cols, rows_per_mod=seq)
    proj = _qkv_call(h1, w_perm, cos, sin, tm=_pick_tile(seq, 2048), j0=j0q, nj=n_tiles - j0q)
    nc = batch * ctx_len
    j0c = j0q + c_kb // TN
    cproj = _inproj_call(ctx.reshape(nc, d), csc1, csh1, g_attn_pre, w_perm,
                         jnp.ones((nc, HEAD_DIM), F32), jnp.zeros((nc, HEAD_DIM), F32),
                         tm=nc, rows_per_mod=nc, j0=j0c, nj=n_tiles - j0c)
    cc_kb, cc_vb, cc_ka, cc_va = 0, W_B, 2 * W_B, 2 * W_B + W_KA

    oa = _attn_a_call(proj, cproj, sink_a[0], batch=batch, seq=seq, ctx_len=ctx_len,
                      q_col=c_qa, k_col=c_ka, v_col=c_va, ck_col=cc_ka, cv_col=cc_va)
    bias = _rpb_call(rpb_b[0])
    ob = _attn_b_call(proj, cproj, bias, batch=batch, seq=seq, ctx_len=ctx_len,
                      q_col=c_qb, k_col=c_kb, v_col=c_vb, ck_col=cc_kb, cv_col=cc_vb)

    tm2 = _pick_tile(seq, 512)
    z = _mix_call(oa, ob, gates, w_br_a[0].astype(BF16), w_br_b[0].astype(BF16), tm=tm2)
    x1 = _outnorm_call(z, w_o[0].astype(BF16), x2, gt1, g_attn_post, tm=tm2, rows_per_mod=seq)

    dff = w_down.shape[1]
    act = _ffn_up_call(x1, sc2, sh2, g_ffn_pre, w_up[0].astype(BF16), conv_w[0], conv_b[0][None, :],
                       tm=tm, tn=TN, seq=seq)
    out = _rowmm_norm_call(act, w_down[0].astype(BF16), x1, gt2, g_ffn_post, tm=_pick_tile(seq, 256), n_chunks=2,
                           rows_per_mod=seq)
    return out.reshape(batch, seq, d)
```

```python
import functools

import numpy as np
import jax
import jax.numpy as jnp
from jax import lax
from jax.experimental import pallas as pl
from jax.experimental.pallas import tpu as pltpu

F32 = jnp.float32
BF16 = jnp.bfloat16

GRID_W = 64
HEAD_DIM = 128
N_HEADS_A = 8
N_KV_A = 2
GQA_GROUP = N_HEADS_A // N_KV_A
WINDOW = 128
BLOCK = 128
N_HEADS_B = 8
NB_ROWS = 8
NB_COLS = 16
CONV_W = 3
ROPE_BASE = 10000.0
EPS = 1e-6
NEG_INF = -1e30
ATTN_SCALE = HEAD_DIM ** -0.5
LOG2E = 1.4426950408889634
Q_SCALE = ATTN_SCALE * LOG2E
F32_MAX = 3.4028234663852886e38

W_QA = N_HEADS_A * HEAD_DIM
W_KA = N_KV_A * HEAD_DIM
W_B = N_HEADS_B * HEAD_DIM

LANES = 128
BF16_ROWS = 16
VMEM_LIMIT_BYTES = 56 * 1024 * 1024
VMEM_LIMIT_RESIDENT_BYTES = 60 * 1024 * 1024

TN = 512
TN_GATE = 1024
ROW_CHUNK = 256
ROW_GROUP_B = 4
GROUPS_PER_STEP_B = 2
QB_A = 1024


def _rms(x, g):
    return (x * lax.rsqrt(jnp.mean(x * x, axis=-1, keepdims=True) + EPS)) * g


def _mod_kernel(c_ref, w_ref, b_ref, o_ref):
    a = jax.nn.silu(c_ref[...]).astype(BF16)
    o_ref[...] = jnp.dot(a, w_ref[...].astype(BF16), preferred_element_type=F32) + b_ref[...]


def _mod_call(cs, w, b):
    m, d = cs.shape
    n = w.shape[1]
    tn = 1536 if n % 1536 == 0 else n
    return pl.pallas_call(
        _mod_kernel,
        grid=(n // tn,),
        in_specs=[pl.BlockSpec((m, d), lambda j: (0, 0)),
                  pl.BlockSpec((d, tn), lambda j: (0, j)),
                  pl.BlockSpec((1, tn), lambda j: (0, j))],
        out_specs=pl.BlockSpec((m, tn), lambda j: (0, j)),
        out_shape=jax.ShapeDtypeStruct((m, n), F32),
        compiler_params=pltpu.CompilerParams(dimension_semantics=("arbitrary",),
                                             vmem_limit_bytes=VMEM_LIMIT_BYTES),
        name="mod",
    )(cs, w, b)


def _rpb_kernel(rpb_ref, o_ref):
    n_dr = 2 * NB_ROWS - 1
    n_dc = 2 * NB_COLS - 1
    r = rpb_ref[0] * LOG2E
    m = lax.broadcasted_iota(jnp.int32, r.shape, 1)

    def dc_index(delta):
        return jnp.clip(delta, -(NB_COLS - 1), NB_COLS - 1) + (NB_COLS - 1)

    idx_lo = dc_index(jnp.where(m < GRID_W, m, m - LANES))
    idx_hi = dc_index(m - GRID_W)
    base_lo = jnp.zeros(r.shape, F32)
    base_hi = jnp.zeros(r.shape, F32)
    for dc in range(n_dc):
        col = jnp.broadcast_to(r[:, dc:dc + 1], r.shape)
        base_lo = jnp.where(idx_lo == dc, col, base_lo)
        base_hi = jnp.where(idx_hi == dc, col, base_hi)

    lane = lax.broadcasted_iota(jnp.int32, (GRID_W, LANES), 1)

    def rotated(base, dr):
        return pltpu.roll(jnp.broadcast_to(base[dr:dr + 1], (GRID_W, LANES)), 0, 1, stride=1, stride_axis=0)

    pair = [jnp.where(lane < GRID_W, rotated(base_lo, d), rotated(base_hi, d + 1)) for d in range(n_dr - 1)]
    n_kr = 3 * ROW_GROUP_B
    dr0 = NB_ROWS - 1 - ROW_GROUP_B
    for rq in range(ROW_GROUP_B):
        for p in range(n_kr // 2):
            o_ref[0, rq * GRID_W:(rq + 1) * GRID_W, p * LANES:(p + 1) * LANES] = pair[2 * p - rq + dr0]


def _rpb_call(rpb):
    nh = rpb.shape[0]
    nq = ROW_GROUP_B * GRID_W
    nk = 3 * ROW_GROUP_B * GRID_W
    assert 2 * ROW_GROUP_B == NB_ROWS and 2 * GRID_W == LANES
    n_dr, n_dc = rpb.shape[1], rpb.shape[2]
    rows = -(-n_dr // 8) * 8
    rpb_tiles = jnp.pad(rpb, ((0, 0), (0, rows - n_dr), (0, LANES - n_dc)))
    return pl.pallas_call(
        _rpb_kernel,
        grid=(nh,),
        in_specs=[pl.BlockSpec((1, rows, LANES), lambda h: (h, 0, 0))],
        out_specs=pl.BlockSpec((1, nq, nk), lambda h: (h, 0, 0)),
        out_shape=jax.ShapeDtypeStruct((nh, nq, nk), F32),
        compiler_params=pltpu.CompilerParams(dimension_semantics=("arbitrary",)),
        name="rpb_table",
    )(rpb_tiles)


def _rope(t, cos, sin, first_half):
    partner = jnp.where(first_half, pltpu.roll(t, LANES - 32, 1), pltpu.roll(t, 32, 1))
    return t * cos + partner * sin


def _row_chunked_matmul(h_ref, w_ref, epilogue, between=None):
    n_chunks = h_ref.shape[0] // ROW_CHUNK
    acc = jnp.dot(h_ref[0:ROW_CHUNK], w_ref[...], preferred_element_type=F32)
    for c in range(n_chunks):
        nxt = None
        if c + 1 < n_chunks:
            nxt = jnp.dot(h_ref[(c + 1) * ROW_CHUNK:(c + 2) * ROW_CHUNK], w_ref[...], preferred_element_type=F32)
        epilogue(slice(c * ROW_CHUNK, (c + 1) * ROW_CHUNK), acc)
        if c == 0 and between is not None:
            between()
        acc = nxt


def _gate_kernel(x_ref, sc_ref, sh_ref, g_ref, w_ref, o_ref, hout_ref, hs_ref, *, n_rows):
    i = pl.program_id(0)
    j = pl.program_id(1)
    pr = x_ref.shape[0]

    def norm_piece():
        h = (_rms(x_ref[...], g_ref[...]) * (1 + sc_ref[0]) + sh_ref[0]).astype(BF16)
        hout_ref[...] = h
        hs_ref[i % 2, pl.ds(pl.multiple_of(j * pr, pr), pr), :] = h

    def epilogue(rows, acc):
        o_ref[rows] = jax.nn.sigmoid(acc).astype(BF16)

    @pl.when(i == 0)
    def _():
        norm_piece()

    @pl.when((i > 0) & (i < n_rows))
    def _():
        _row_chunked_matmul(hs_ref.at[(i + 1) % 2], w_ref, epilogue, between=norm_piece)

    @pl.when(i == n_rows)
    def _():
        _row_chunked_matmul(hs_ref.at[(i + 1) % 2], w_ref, epilogue)


def _gate_call(x2, sc, sh, g, w, *, tm, tn, n_cols, rows_per_mod):
    n, d = x2.shape
    n_rows = n // tm
    nj = n_cols // tn
    pr = tm // nj
    assert pr * nj == tm and pr % BF16_ROWS == 0
    mod_tiles = rows_per_mod // tm
    last = n_rows * nj - 1
    piece_map = lambda i, j: (jnp.minimum(i * nj + j, last), 0)
    mod_map = lambda i, j: (jnp.minimum(i, n_rows - 1) // mod_tiles, 0, 0)
    return pl.pallas_call(
        functools.partial(_gate_kernel, n_rows=n_rows),
        grid=(n_rows + 1, nj),
        in_specs=[pl.BlockSpec((pr, d), piece_map),
                  pl.BlockSpec((1, 1, d), mod_map),
                  pl.BlockSpec((1, 1, d), mod_map),
                  pl.BlockSpec((1, d), lambda i, j: (0, 0)),
                  pl.BlockSpec((d, tn), lambda i, j: (0, j))],
        out_specs=[pl.BlockSpec((tm, tn), lambda i, j: (jnp.maximum(i - 1, 0), jnp.where(i > 0, j, 0))),
                   pl.BlockSpec((pr, d), piece_map)],
        out_shape=[jax.ShapeDtypeStruct((n, n_cols), BF16), jax.ShapeDtypeStruct((n, d), BF16)],
        scratch_shapes=[pltpu.VMEM((2, tm, d), BF16)],
        compiler_params=pltpu.CompilerParams(dimension_semantics=("arbitrary", "arbitrary"),
                                             vmem_limit_bytes=VMEM_LIMIT_BYTES),
        name="gate_proj",
    )(x2, sc, sh, g, w)


def _qkv_kernel(h_ref, w_ref, cos_ref, sin_ref, o_ref, *, j0, n_gate):
    _proj_tile(h_ref, w_ref, cos_ref, sin_ref, o_ref, pl.program_id(1) + j0, n_gate)


def _qkv_call(h, w, cos, sin, *, tm, j0, nj):
    n, d = h.shape
    pos_tiles = cos.shape[0] // tm
    return pl.pallas_call(
        functools.partial(_qkv_kernel, j0=j0, n_gate=2 * d // TN),
        grid=(n // tm, nj),
        in_specs=[pl.BlockSpec((tm, d), lambda i, j: (i, 0)),
                  pl.BlockSpec((d, TN), lambda i, j: (0, j + j0)),
                  pl.BlockSpec((tm, HEAD_DIM), lambda i, j: (i % pos_tiles, 0)),
                  pl.BlockSpec((tm, HEAD_DIM), lambda i, j: (i % pos_tiles, 0))],
        out_specs=pl.BlockSpec((tm, TN), lambda i, j: (i, j)),
        out_shape=jax.ShapeDtypeStruct((n, nj * TN), BF16),
        compiler_params=pltpu.CompilerParams(dimension_semantics=("arbitrary", "arbitrary"),
                                             vmem_limit_bytes=VMEM_LIMIT_BYTES),
        name="qkv_proj",
    )(h, w, cos, sin)


def _inproj_kernel(x_ref, sc_ref, sh_ref, g_ref, w_ref, cos_ref, sin_ref, o_ref, h_ref, *, j0):
    j = pl.program_id(1)

    @pl.when(j == 0)
    def _():
        h = _rms(x_ref[...], g_ref[...]) * (1 + sc_ref[0]) + sh_ref[0]
        h_ref[...] = h.astype(BF16)

    _proj_tile(h_ref, w_ref, cos_ref, sin_ref, o_ref, j + j0, 2 * x_ref.shape[1] // TN)


def _proj_tile(h_ref, w_ref, cos_ref, sin_ref, o_ref, jj, n_gate):
    lane = lax.broadcasted_iota(jnp.int32, (1, HEAD_DIM), 1)
    first_half = (lane & 63) < 32

    def row_chunks(epilogue):
        _row_chunked_matmul(h_ref, w_ref, epilogue)

    def rope_heads(rows, acc, n_heads, scale):
        cos = cos_ref[rows]
        sin = sin_ref[rows]
        for hd in range(n_heads):
            sl = slice(hd * HEAD_DIM, (hd + 1) * HEAD_DIM)
            r = _rope(acc[:, sl], cos, sin, first_half)
            if scale is not None:
                r = r * scale
            o_ref[rows, sl] = r.astype(BF16)

    @pl.when((jj >= n_gate) & (jj < n_gate + 2))
    def _():
        row_chunks(lambda rows, acc: rope_heads(rows, acc, TN // HEAD_DIM, Q_SCALE))

    @pl.when((jj >= n_gate + 2) & (jj < n_gate + 4))
    def _():
        def epilogue(rows, acc):
            o_ref[rows] = (acc * Q_SCALE).astype(BF16)
        row_chunks(epilogue)

    @pl.when((jj >= n_gate + 4) & (jj < n_gate + 8))
    def _():
        def epilogue(rows, acc):
            o_ref[rows] = acc.astype(BF16)
        row_chunks(epilogue)

    @pl.when(jj == n_gate + 8)
    def _():
        def epilogue(rows, acc):
            rope_heads(rows, acc, N_KV_A, None)
            o_ref[rows, W_KA:] = acc[:, W_KA:].astype(BF16)
        row_chunks(epilogue)


def _inproj_call(x2, sc, sh, g, w, cos, sin, *, tm, rows_per_mod, j0, nj):
    n, d = x2.shape
    pos_tiles = cos.shape[0] // tm
    mod_tiles = rows_per_mod // tm
    return pl.pallas_call(
        functools.partial(_inproj_kernel, j0=j0),
        grid=(n // tm, nj),
        in_specs=[pl.BlockSpec((tm, d), lambda i, j: (i, 0)),
                  pl.BlockSpec((1, 1, d), lambda i, j: (i // mod_tiles, 0, 0)),
                  pl.BlockSpec((1, 1, d), lambda i, j: (i // mod_tiles, 0, 0)),
                  pl.BlockSpec((1, d), lambda i, j: (0, 0)),
                  pl.BlockSpec((d, TN), lambda i, j: (0, j + j0)),
                  pl.BlockSpec((tm, HEAD_DIM), lambda i, j: (i % pos_tiles, 0)),
                  pl.BlockSpec((tm, HEAD_DIM), lambda i, j: (i % pos_tiles, 0))],
        out_specs=pl.BlockSpec((tm, TN), lambda i, j: (i, j)),
        out_shape=jax.ShapeDtypeStruct((n, nj * TN), BF16),
        scratch_shapes=[pltpu.VMEM((tm, d), BF16)],
        compiler_params=pltpu.CompilerParams(dimension_semantics=("arbitrary", "arbitrary"),
                                             vmem_limit_bytes=VMEM_LIMIT_BYTES),
        name="in_proj",
    )(x2, sc, sh, g, w, cos, sin)


def _attn_a_kernel(sink_ref, q_ref, kp_ref, km_ref, kn_ref, vp_ref, vm_ref, vn_ref, kc_ref, vc_ref, o_ref,
                   *, n_blocks):
    nq = pl.program_id(1)
    sub = QB_A // BLOCK
    rows = GQA_GROUP * BLOCK
    n_loc = 3 * BLOCK
    r_io = lax.broadcasted_iota(jnp.int32, (rows, n_loc), 0)
    c_io = lax.broadcasted_iota(jnp.int32, (rows, n_loc), 1)
    qi_io = r_io & (BLOCK - 1)
    in_band = (c_io >= qi_io) & (c_io <= qi_io + 2 * WINDOW)
    c_row = lax.broadcasted_iota(jnp.int32, (1, n_loc), 1)
    g_io = lax.shift_right_logical(lax.broadcasted_iota(jnp.int32, (rows, 1), 0), BLOCK.bit_length() - 1)
    nt = (((1,), (1,)), ((), ()))
    for kvh in range(N_KV_A):
        hs = slice(kvh * HEAD_DIM, (kvh + 1) * HEAD_DIM)
        kband = jnp.concatenate([kp_ref[:, hs], km_ref[:, hs], kn_ref[:, hs]], axis=0)
        vband = jnp.concatenate([vp_ref[:, hs], vm_ref[:, hs], vn_ref[:, hs]], axis=0)
        kc = kc_ref[:, hs]
        vc = vc_ref[:, hs]
        sink = jnp.zeros((rows, 1), F32)
        for g in range(GQA_GROUP):
            sink = jnp.where(g_io == g, sink_ref[kvh * GQA_GROUP + g] * LOG2E, sink)
        for qi in range(sub):
            blk = nq * sub + qi
            lo = jnp.where(blk == 0, BLOCK, 0)
            hi = jnp.where(blk == n_blocks - 1, 2 * BLOCK - 1, n_loc - 1)
            valid = in_band & ((c_row >= lo) & (c_row <= hi))
            q4 = jnp.concatenate(
                [q_ref[qi * BLOCK:(qi + 1) * BLOCK, (kvh * GQA_GROUP + g) * HEAD_DIM:(kvh * GQA_GROUP + g + 1) * HEAD_DIM]
                 for g in range(GQA_GROUP)], axis=0)
            s_loc = lax.dot_general(q4, kband[qi * BLOCK:qi * BLOCK + n_loc], nt, preferred_element_type=F32)
            s_loc = jnp.where(valid, s_loc, NEG_INF)
            s_ctx = lax.dot_general(q4, kc, nt, preferred_element_type=F32)
            m = jnp.maximum(jnp.maximum(jnp.max(s_loc, axis=-1, keepdims=True),
                                        jnp.max(s_ctx, axis=-1, keepdims=True)), sink)
            e_loc = jnp.exp2(s_loc - m)
            e_ctx = jnp.exp2(s_ctx - m)
            den = (jnp.sum(e_loc, axis=-1, keepdims=True) + jnp.sum(e_ctx, axis=-1, keepdims=True)
                   + jnp.exp2(sink - m))
            o = (jnp.dot(e_loc.astype(BF16), vband[qi * BLOCK:qi * BLOCK + n_loc], preferred_element_type=F32)
                 + jnp.dot(e_ctx.astype(BF16), vc, preferred_element_type=F32)) * (1.0 / den)
            for g in range(GQA_GROUP):
                hd = kvh * GQA_GROUP + g
                o_ref[qi * BLOCK:(qi + 1) * BLOCK, hd * HEAD_DIM:(hd + 1) * HEAD_DIM] = (
                    o[g * BLOCK:(g + 1) * BLOCK].astype(BF16))


def _attn_a_call(proj, cproj, sink, *, batch, seq, ctx_len, q_col, k_col, v_col, ck_col, cv_col):
    nqb = seq // QB_A
    sub = QB_A // BLOCK
    n_blocks = seq // BLOCK

    def prev_map(b, n, col):
        return (b * n_blocks + jnp.maximum(n * sub - 1, 0), col // W_KA)

    def next_map(b, n, col):
        return (b * n_blocks + jnp.minimum(n * sub + sub, n_blocks - 1), col // W_KA)

    def main_map(b, n, col):
        return (b * nqb + n, col // W_KA)

    edge = lambda f, col: pl.BlockSpec((BLOCK, W_KA), functools.partial(f, col=col))
    main = lambda col: pl.BlockSpec((QB_A, W_KA), functools.partial(main_map, col=col))
    ctxs = lambda col: pl.BlockSpec((ctx_len, W_KA), lambda b, n: (b, col // W_KA))
    return pl.pallas_call(
        functools.partial(_attn_a_kernel, n_blocks=n_blocks),
        grid=(batch, nqb),
        in_specs=[pl.BlockSpec(memory_space=pltpu.SMEM),
                  pl.BlockSpec((QB_A, W_QA), lambda b, n: (b * nqb + n, q_col // W_QA)),
                  edge(prev_map, k_col), main(k_col), edge(next_map, k_col),
                  edge(prev_map, v_col), main(v_col), edge(next_map, v_col),
                  ctxs(ck_col), ctxs(cv_col)],
        out_specs=pl.BlockSpec((QB_A, W_QA), lambda b, n: (b * nqb + n, 0)),
        out_shape=jax.ShapeDtypeStruct((batch * seq, W_QA), BF16),
        compiler_params=pltpu.CompilerParams(dimension_semantics=("arbitrary",) * 2,
                                             vmem_limit_bytes=VMEM_LIMIT_BYTES),
        name="attn_a",
    )(sink, proj, proj, proj, proj, proj, proj, proj, cproj, cproj)


def _attn_b_kernel(q_ref, kp_ref, km_ref, kn_ref, vp_ref, vm_ref, vn_ref, kc_ref, vc_ref, bias_ref, o_ref,
                   *, grid_rows):
    n = pl.program_id(1)
    nq = ROW_GROUP_B * GRID_W
    nk = 3 * nq
    heads = q_ref.shape[1] // HEAD_DIM
    r_io = lax.broadcasted_iota(jnp.int32, (nq, nk), 0)
    c_io = lax.broadcasted_iota(jnp.int32, (nq, nk), 1)
    log_w = GRID_W.bit_length() - 1
    q_col = r_io & (GRID_W - 1)
    k_col = c_io & (GRID_W - 1)
    c_start = jnp.clip(q_col - NB_COLS // 2, 0, GRID_W - NB_COLS)
    col_ok = (k_col >= c_start) & (k_col < c_start + NB_COLS)
    nt = (((1,), (1,)), ((), ()))
    for grp in range(GROUPS_PER_STEP_B):
        gi = n * GROUPS_PER_STEP_B + grp
        r_abs = gi * ROW_GROUP_B + lax.shift_right_logical(r_io, log_w)
        kr_abs = (gi - 1) * ROW_GROUP_B + lax.shift_right_logical(c_io, log_w)
        r_start = jnp.clip(r_abs - NB_ROWS // 2, 0, grid_rows - NB_ROWS)
        valid = (kr_abs >= r_start) & (kr_abs < r_start + NB_ROWS) & col_ok
        cap = jnp.where(valid, F32_MAX, NEG_INF)
        rows = slice(grp * nq, (grp + 1) * nq)
        for hd in range(heads):
            sl = slice(hd * HEAD_DIM, (hd + 1) * HEAD_DIM)
            q = q_ref[rows, sl]
            keys = jnp.concatenate([kp_ref[:, sl], km_ref[:, sl], kn_ref[:, sl]], axis=0)[grp * nq:grp * nq + nk]
            vals = jnp.concatenate([vp_ref[:, sl], vm_ref[:, sl], vn_ref[:, sl]], axis=0)[grp * nq:grp * nq + nk]
            s_nb = lax.dot_general(q, keys, nt, preferred_element_type=F32)
            s_nb = jnp.minimum(s_nb + bias_ref[hd], cap)
            s_ctx = lax.dot_general(q, kc_ref[:, sl], nt, preferred_element_type=F32)
            m = jnp.maximum(jnp.max(s_nb, axis=-1, keepdims=True), jnp.max(s_ctx, axis=-1, keepdims=True))
            e_nb = jnp.exp2(s_nb - m)
            e_ctx = jnp.exp2(s_ctx - m)
            den = jnp.sum(e_nb, axis=-1, keepdims=True) + jnp.sum(e_ctx, axis=-1, keepdims=True)
            o = (jnp.dot(e_nb.astype(BF16), vals, preferred_element_type=F32)
                 + jnp.dot(e_ctx.astype(BF16), vc_ref[:, sl], preferred_element_type=F32)) * (1.0 / den)
            o_ref[rows, sl] = o.astype(BF16)


def _attn_b_call(proj, cproj, bias, *, batch, seq, ctx_len, q_col, k_col, v_col, ck_col, cv_col):
    nq = ROW_GROUP_B * GRID_W
    ng = seq // nq
    gps = GROUPS_PER_STEP_B
    ns = ng // gps
    hw = W_B

    def edge(col, first):
        def imap(b, n):
            g = jnp.maximum(n * gps - 1, 0) if first else jnp.minimum(n * gps + gps, ng - 1)
            return (b * ng + g, col // hw)
        return pl.BlockSpec((nq, hw), imap)

    main = lambda col: pl.BlockSpec((gps * nq, hw), lambda b, n: (b * ns + n, col // hw))
    ctxs = lambda col: pl.BlockSpec((ctx_len, hw), lambda b, n: (b, col // hw))
    return pl.pallas_call(
        functools.partial(_attn_b_kernel, grid_rows=seq // GRID_W),
        grid=(batch, ns),
        in_specs=[main(q_col),
                  edge(k_col, True), main(k_col), edge(k_col, False),
                  edge(v_col, True), main(v_col), edge(v_col, False),
                  ctxs(ck_col), ctxs(cv_col),
                  pl.BlockSpec((hw // HEAD_DIM, nq, bias.shape[2]), lambda b, n: (0, 0, 0))],
        out_specs=pl.BlockSpec((gps * nq, hw), lambda b, n: (b * ns + n, 0)),
        out_shape=jax.ShapeDtypeStruct((batch * seq, W_B), BF16),
        compiler_params=pltpu.CompilerParams(dimension_semantics=("arbitrary",) * 2,
                                             vmem_limit_bytes=VMEM_LIMIT_BYTES),
        name="attn_b",
    )(proj, proj, proj, proj, proj, proj, proj, cproj, cproj, bias)


def _resident_bf16(w_ref, wb_ref):
    @pl.when(pl.program_id(0) == 0)
    def _():
        wb_ref[...] = w_ref[...].astype(BF16)


def _mix_kernel(oa_ref, ob_ref, ga_ref, gb_ref, wa_ref, wb_ref, z_ref, wa_bf, wb_bf):
    _resident_bf16(wa_ref, wa_bf)
    _resident_bf16(wb_ref, wb_bf)
    ya = jnp.dot(oa_ref[...], wa_bf[...], preferred_element_type=F32)
    yb = jnp.dot(ob_ref[...], wb_bf[...], preferred_element_type=F32)
    z_ref[...] = (ga_ref[...].astype(F32) * ya + gb_ref[...].astype(F32) * yb).astype(BF16)


def _mix_call(oa, ob, proj, wa, wb, *, tm):
    n = oa.shape[0]
    d = wa.shape[1]
    const = lambda shape: pl.BlockSpec(shape, lambda i: (0, 0), pipeline_mode=pl.Buffered(1))
    return pl.pallas_call(
        _mix_kernel,
        grid=(n // tm,),
        in_specs=[pl.BlockSpec((tm, oa.shape[1]), lambda i: (i, 0)),
                  pl.BlockSpec((tm, ob.shape[1]), lambda i: (i, 0)),
                  pl.BlockSpec((tm, d), lambda i: (i, 0)),
                  pl.BlockSpec((tm, d), lambda i: (i, 1)),
                  const(wa.shape), const(wb.shape)],
        out_specs=pl.BlockSpec((tm, d), lambda i: (i, 0)),
        out_shape=jax.ShapeDtypeStruct((n, d), BF16),
        scratch_shapes=[pltpu.VMEM(wa.shape, BF16), pltpu.VMEM(wb.shape, BF16)],
        compiler_params=pltpu.CompilerParams(dimension_semantics=("arbitrary",),
                                             vmem_limit_bytes=VMEM_LIMIT_BYTES),
        name="branch_mix",
    )(oa, ob, proj, proj, wa, wb)


def _outnorm_kernel(a_ref, w_ref, x_ref, gt_ref, g_ref, o_ref, w_bf):
    _resident_bf16(w_ref, w_bf)
    y = jnp.dot(a_ref[...], w_bf[...], preferred_element_type=F32)
    o_ref[...] = x_ref[...] + gt_ref[0] * _rms(y, g_ref[...])


def _outnorm_call(a, w, x2, gt, g, *, tm, rows_per_mod):
    n, kdim = a.shape
    d = w.shape[1]
    mod_tiles = rows_per_mod // tm
    return pl.pallas_call(
        _outnorm_kernel,
        grid=(n // tm,),
        in_specs=[pl.BlockSpec((tm, kdim), lambda i: (i, 0)),
                  pl.BlockSpec((kdim, d), lambda i: (0, 0), pipeline_mode=pl.Buffered(1)),
                  pl.BlockSpec((tm, d), lambda i: (i, 0)),
                  pl.BlockSpec((1, 1, d), lambda i: (i // mod_tiles, 0, 0)),
                  pl.BlockSpec((1, d), lambda i: (0, 0))],
        out_specs=pl.BlockSpec((tm, d), lambda i: (i, 0)),
        out_shape=jax.ShapeDtypeStruct((n, d), F32),
        scratch_shapes=[pltpu.VMEM(w.shape, BF16)],
        compiler_params=pltpu.CompilerParams(dimension_semantics=("arbitrary",),
                                             vmem_limit_bytes=VMEM_LIMIT_BYTES),
        name="out_norm",
    )(a, w, x2, gt, g)


def _rowmm_norm_kernel(a_ref, w_ref, x_ref, gt_ref, g_ref, o_ref, *, n_chunks):
    cm = a_ref.shape[0] // n_chunks
    y = jnp.dot(a_ref[0:cm], w_ref[...], preferred_element_type=F32)
    for c in range(n_chunks):
        nxt = None
        if c + 1 < n_chunks:
            nxt = jnp.dot(a_ref[(c + 1) * cm:(c + 2) * cm], w_ref[...], preferred_element_type=F32)
        rows = slice(c * cm, (c + 1) * cm)
        o_ref[rows] = x_ref[rows] + gt_ref[0] * _rms(y, g_ref[...])
        y = nxt


def _rowmm_norm_call(a, w, x2, gt, g, *, tm, n_chunks, rows_per_mod):
    n, kdim = a.shape
    d = w.shape[1]
    mod_tiles = rows_per_mod // tm
    return pl.pallas_call(
        functools.partial(_rowmm_norm_kernel, n_chunks=n_chunks),
        grid=(n // tm,),
        in_specs=[pl.BlockSpec((tm, kdim), lambda i: (i, 0)),
                  pl.BlockSpec((kdim, d), lambda i: (0, 0), pipeline_mode=pl.Buffered(1)),
                  pl.BlockSpec((tm, d), lambda i: (i, 0)),
                  pl.BlockSpec((1, 1, d), lambda i: (i // mod_tiles, 0, 0)),
                  pl.BlockSpec((1, d), lambda i: (0, 0))],
        out_specs=pl.BlockSpec((tm, d), lambda i: (i, 0)),
        out_shape=jax.ShapeDtypeStruct((n, d), F32),
        compiler_params=pltpu.CompilerParams(dimension_semantics=("arbitrary",),
                                             vmem_limit_bytes=VMEM_LIMIT_RESIDENT_BYTES),
        name="rowmm_norm",
    )(a, w, x2, gt, g)


def _ffn_up_kernel(x0p_ref, x0_ref, x0n_ref, sc0_ref, sh0_ref, xp_ref, xm_ref, xn_ref, sc_ref, sh_ref, g_ref,
                   wa_ref, wg_ref, cwa_ref, cwg_ref, cba_ref, cbg_ref,
                   o_ref, h_cur, h_nxt, ua_ref, ug_ref, *, nj, n_tiles, tiles_per_seq, n_pieces):
    t = pl.program_id(0)
    tm = x0_ref.shape[0]
    halo = xp_ref.shape[0]
    pr = xm_ref.shape[0]
    ext = tm + 2 * halo
    margin = 8

    def mod(x, sc, sh):
        return (_rms(x, g_ref[...]) * (1 + sc[0]) + sh[0]).astype(BF16)

    @pl.when(t == 0)
    def _():
        ua_ref[...] = jnp.zeros_like(ua_ref)
        ug_ref[...] = jnp.zeros_like(ug_ref)
        h_cur[0:halo] = mod(x0p_ref[...], sc0_ref, sh0_ref)
        h_cur[halo:halo + tm] = mod(x0_ref[...], sc0_ref, sh0_ref)
        h_cur[halo + tm:ext] = mod(x0n_ref[...], sc0_ref, sh0_ref)

    tc = jnp.minimum(t, n_tiles - 1)

    @pl.when((t > 0) & (t < n_tiles) & (t % nj == 0))
    def _():
        h_cur[...] = h_nxt[...]

    def next_row_piece():
        p = jnp.minimum(tc % nj, n_pieces - 1)
        h_nxt[0:halo] = mod(xp_ref[...], sc_ref, sh_ref)
        h_nxt[pl.ds(pl.multiple_of(halo + p * pr, BF16_ROWS), pr), :] = mod(xm_ref[...], sc_ref, sh_ref)
        h_nxt[halo + tm:ext] = mod(xn_ref[...], sc_ref, sh_ref)

    slot = t % 2
    prev = 1 - slot
    ip = jnp.maximum(t - 1, 0) // nj
    first_row = jnp.where((ip % tiles_per_seq) == 0, 0, -1)
    last_row = jnp.where((ip % tiles_per_seq) == tiles_per_seq - 1, tm - 1, -1)
    row = lax.broadcasted_iota(jnp.int32, (tm, 1), 0)

    def epilogue(lo, hi):
        cm = hi - lo
        rowc = row[lo:hi]

        def conv(u_ref, cw_ref, cb_ref):
            win = u_ref[prev, pl.ds(halo + lo - margin, cm + 2 * margin), :]
            up = pltpu.roll(win, 1, 0)[margin:margin + cm]
            un = pltpu.roll(win, cm + 2 * margin - 1, 0)[margin:margin + cm]
            up = jnp.where(rowc == first_row, 0.0, up)
            un = jnp.where(rowc == last_row, 0.0, un)
            out = cb_ref[...] + up * cw_ref[0:1]
            out = out + win[margin:margin + cm] * cw_ref[1:2]
            return out + un * cw_ref[2:3]

        a = conv(ua_ref, cwa_ref, cba_ref)
        gte = conv(ug_ref, cwg_ref, cbg_ref)
        o_ref[lo:hi] = (jax.nn.silu(gte) * a).astype(BF16)

    half = ext // 2
    third = (tm // 3) // BF16_ROWS * BF16_ROWS
    chunks = [(0, third), (third, 2 * third), (2 * third, tm)]
    ua_ref[slot, 0:half] = jnp.dot(h_cur[0:half], wa_ref[...], preferred_element_type=F32)
    next_row_piece()
    epilogue(*chunks[0])
    ua_ref[slot, half:ext] = jnp.dot(h_cur[half:ext], wa_ref[...], preferred_element_type=F32)
    epilogue(*chunks[1])
    ug_ref[slot, 0:half] = jnp.dot(h_cur[0:half], wg_ref[...], preferred_element_type=F32)
    epilogue(*chunks[2])
    ug_ref[slot, half:ext] = jnp.dot(h_cur[half:ext], wg_ref[...], preferred_element_type=F32)


def _ffn_up_call(x2, sc, sh, g, w_up, conv_w, conv_b, *, tm, tn, seq):
    n, d = x2.shape
    dff = w_up.shape[1] // 2
    nj = dff // tn
    halo = BF16_ROWS
    hb = tm // halo
    n_halo = n // halo
    tiles_per_seq = seq // tm
    n_rows = n // tm
    n_tiles = n_rows * nj
    pr = 128
    n_pieces = tm // pr
    assert (tm + 2 * halo) % (2 * BF16_ROWS) == 0 and n_pieces <= nj and n_pieces * pr == tm

    def cur(t):
        tc = jnp.minimum(t, n_tiles - 1)
        return tc // nj, tc % nj

    def prv(t):
        tp = jnp.maximum(t - 1, 0)
        return tp // nj, tp % nj

    def nxt(t):
        return jnp.minimum(cur(t)[0] + 1, n_rows - 1)

    once = lambda shape, imap: pl.BlockSpec(shape, imap, pipeline_mode=pl.Buffered(1))
    return pl.pallas_call(
        functools.partial(_ffn_up_kernel, nj=nj, n_tiles=n_tiles, tiles_per_seq=tiles_per_seq, n_pieces=n_pieces),
        grid=(n_tiles + 1,),
        in_specs=[once((halo, d), lambda t: (0, 0)),
                  once((tm, d), lambda t: (0, 0)),
                  once((halo, d), lambda t: (jnp.minimum(hb, n_halo - 1), 0)),
                  once((1, 1, d), lambda t: (0, 0, 0)),
                  once((1, 1, d), lambda t: (0, 0, 0)),
                  pl.BlockSpec((halo, d), lambda t: (jnp.maximum(nxt(t) * hb - 1, 0), 0)),
                  pl.BlockSpec((pr, d), lambda t: (nxt(t) * n_pieces + jnp.minimum(cur(t)[1], n_pieces - 1), 0)),
                  pl.BlockSpec((halo, d), lambda t: (jnp.minimum((nxt(t) + 1) * hb, n_halo - 1), 0)),
                  pl.BlockSpec((1, 1, d), lambda t: (nxt(t) // tiles_per_seq, 0, 0)),
                  pl.BlockSpec((1, 1, d), lambda t: (nxt(t) // tiles_per_seq, 0, 0)),
                  pl.BlockSpec((1, d), lambda t: (0, 0)),
                  pl.BlockSpec((d, tn), lambda t: (0, cur(t)[1])),
                  pl.BlockSpec((d, tn), lambda t: (0, cur(t)[1] + nj)),
                  pl.BlockSpec((CONV_W, tn), lambda t: (0, prv(t)[1])),
                  pl.BlockSpec((CONV_W, tn), lambda t: (0, prv(t)[1] + nj)),
                  pl.BlockSpec((1, tn), lambda t: (0, prv(t)[1])),
                  pl.BlockSpec((1, tn), lambda t: (0, prv(t)[1] + nj))],
        out_specs=pl.BlockSpec((tm, tn), lambda t: prv(t)),
        out_shape=jax.ShapeDtypeStruct((n, dff), BF16),
        scratch_shapes=[pltpu.VMEM((tm + 2 * halo, d), BF16),
                        pltpu.VMEM((tm + 2 * halo, d), BF16),
                        pltpu.VMEM((2, tm + 2 * halo, tn), F32),
                        pltpu.VMEM((2, tm + 2 * halo, tn), F32)],
        compiler_params=pltpu.CompilerParams(dimension_semantics=("arbitrary",),
                                             vmem_limit_bytes=VMEM_LIMIT_BYTES),
        name="ffn_up",
    )(x2, x2, x2, sc, sh, x2, x2, x2, sc, sh, g, w_up, w_up, conv_w, conv_w, conv_b, conv_b)


def _rope_tables(seq):
    t = np.arange(seq)
    half = HEAD_DIM // 4
    inv = np.float32(ROPE_BASE) ** (-np.arange(half, dtype=np.float32) / np.float32(half))

    def cs(pos):
        ang = pos.astype(np.float32)[:, None] * inv[None, :]
        c, s = np.cos(ang), np.sin(ang)
        return np.concatenate([c, c], axis=1), np.concatenate([-s, s], axis=1)

    cr, sr = cs(t // GRID_W)
    cc, sn = cs(t % GRID_W)
    return (jnp.asarray(np.concatenate([cr, cc], axis=1), F32), jnp.asarray(np.concatenate([sr, sn], axis=1), F32))


def _pick_tile(n, pref):
    while n % pref:
        pref //= 2
    return pref


def kernel(x, c, ctx, c_ctx, w_mod, b_mod, g_attn_pre, g_attn_post, g_ffn_pre, g_ffn_post, w_in, sink_a, rpb_b,
           w_br_a, w_br_b, w_o, w_up, conv_w, conv_b, w_down):
    batch, seq, d = x.shape
    ctx_len = ctx.shape[1]
    assert w_mod.shape[0] == 1 and d == 16 * HEAD_DIM
    assert seq % QB_A == 0 and seq % (GROUPS_PER_STEP_B * ROW_GROUP_B * GRID_W) == 0 and WINDOW == BLOCK
    n = batch * seq
    x2 = x.reshape(n, d)

    pad = (-(batch + 1)) % 8
    cs = jnp.concatenate([c, c_ctx[None, :], jnp.zeros((pad, d), F32)], axis=0)
    mod = _mod_call(cs, w_mod[0], b_mod[0][None, :])
    sh1, sc1, gt1, sh2, sc2, gt2 = [mod[:batch, k * d:(k + 1) * d].reshape(batch, 1, d) for k in range(6)]
    csh1 = mod[batch:batch + 1, 0:d].reshape(1, 1, d)
    csc1 = mod[batch:batch + 1, d:2 * d].reshape(1, 1, d)

    o_qa, o_ka, o_va, o_qb, o_kb, o_vb, o_ga = 0, W_QA, W_QA + W_KA, W_QA + 2 * W_KA, W_QA + 2 * W_KA + W_B, \
        W_QA + 2 * W_KA + 2 * W_B, W_QA + 2 * W_KA + 3 * W_B
    wi = w_in[0]
    w_perm = jnp.concatenate([wi[:, o_ga:], wi[:, o_qa:o_ka], wi[:, o_qb:o_ga], wi[:, o_ka:o_qb]], axis=1).astype(BF16)
    n_gate_cols = 2 * d
    n_tiles = w_perm.shape[1] // TN
    j0q = n_gate_cols // TN
    c_qa = 0
    c_qb = c_qa + W_QA
    c_kb = c_qb + W_B
    c_vb = c_kb + W_B
    c_ka = c_vb + W_B
    c_va = c_ka + W_KA
    cos, sin = _rope_tables(seq)
    tm = _pick_tile(seq, 1024)
    gates, h1 = _gate_call(x2, sc1, sh1, g_attn_pre, w_perm, tm=tm, tn=TN_GATE, n_cols=n_gate_cols, rows_per_mod=seq)
    proj = _qkv_call(h1, w_perm, cos, sin, tm=_pick_tile(seq, 2048), j0=j0q, nj=n_tiles - j0q)
    nc = batch * ctx_len
    j0c = j0q + c_kb // TN
    cproj = _inproj_call(ctx.reshape(nc, d), csc1, csh1, g_attn_pre, w_perm,
                         jnp.ones((nc, HEAD_DIM), F32), jnp.zeros((nc, HEAD_DIM), F32),
                         tm=nc, rows_per_mod=nc, j0=j0c, nj=n_tiles - j0c)
    cc_kb, cc_vb, cc_ka, cc_va = 0, W_B, 2 * W_B, 2 * W_B + W_KA

    oa = _attn_a_call(proj, cproj, sink_a[0], batch=batch, seq=seq, ctx_len=ctx_len,
                      q_col=c_qa, k_col=c_ka, v_col=c_va, ck_col=cc_ka, cv_col=cc_va)
    bias = _rpb_call(rpb_b[0])
    ob = _attn_b_call(proj, cproj, bias, batch=batch, seq=seq, ctx_len=ctx_len,
                      q_col=c_qb, k_col=c_kb, v_col=c_vb, ck_col=cc_kb, cv_col=cc_vb)

    tm2 = _pick_tile(seq, 512)
    z = _mix_call(oa, ob, gates, w_br_a[0], w_br_b[0], tm=tm2)
    x1 = _outnorm_call(z, w_o[0], x2, gt1, g_attn_post, tm=tm2, rows_per_mod=seq)

    dff = w_down.shape[1]
    act = _ffn_up_call(x1, sc2, sh2, g_ffn_pre, w_up[0].astype(BF16), conv_w[0], conv_b[0][None, :],
                       tm=tm, tn=TN, seq=seq)
    out = _rowmm_norm_call(act, w_down[0].astype(BF16), x1, gt2, g_ffn_post, tm=_pick_tile(seq, 256), n_chunks=2,
                           rows_per_mod=seq)
    return out.reshape(batch, seq, d)
```

```python
import functools
from typing import NamedTuple

import numpy as np
import jax
import jax.numpy as jnp
from jax import lax
from jax.experimental import pallas as pl
from jax.experimental.pallas import tpu as pltpu

F32 = jnp.float32
BF16 = jnp.bfloat16

GRID_W = 64
HEAD_DIM = 128
N_HEADS_A = 8
N_KV_A = 2
GQA_GROUP = N_HEADS_A // N_KV_A
WINDOW = 128
BLOCK = 128
N_HEADS_B = 8
NB_ROWS = 8
NB_COLS = 16
CONV_W = 3
ROPE_BASE = 10000.0
EPS = 1e-6
NEG_INF = -1e30
ATTN_SCALE = HEAD_DIM ** -0.5
LOG2E = 1.4426950408889634
Q_SCALE = ATTN_SCALE * LOG2E
F32_MAX = 3.4028234663852886e38

W_QA = N_HEADS_A * HEAD_DIM
W_KA = N_KV_A * HEAD_DIM
W_B = N_HEADS_B * HEAD_DIM

LANES = 128
BF16_ROWS = 16
VMEM_LIMIT_BYTES = 56 * 1024 * 1024
VMEM_LIMIT_RESIDENT_BYTES = 60 * 1024 * 1024

TN = 512
TN_GATE = 1024
ROW_CHUNK = 256
ROW_GROUP_B = 4
GROUPS_PER_STEP_B = 2
QB_A = 1024


def _rms(x, g):
    return (x * lax.rsqrt(jnp.mean(x * x, axis=-1, keepdims=True) + EPS)) * g


def _mod_kernel(c_ref, w_ref, b_ref, o_ref):
    a = jax.nn.silu(c_ref[...]).astype(BF16)
    o_ref[...] = jnp.dot(a, w_ref[...].astype(BF16), preferred_element_type=F32) + b_ref[...]


def _mod_call(cs, w, b):
    m, d = cs.shape
    n = w.shape[1]
    tn = 1536 if n % 1536 == 0 else n
    return pl.pallas_call(
        _mod_kernel,
        grid=(n // tn,),
        in_specs=[pl.BlockSpec((m, d), lambda j: (0, 0)),
                  pl.BlockSpec((d, tn), lambda j: (0, j)),
                  pl.BlockSpec((1, tn), lambda j: (0, j))],
        out_specs=pl.BlockSpec((m, tn), lambda j: (0, j)),
        out_shape=jax.ShapeDtypeStruct((m, n), F32),
        compiler_params=pltpu.CompilerParams(dimension_semantics=("arbitrary",),
                                             vmem_limit_bytes=VMEM_LIMIT_BYTES),
        name="mod",
    )(cs, w, b)


def _rpb_kernel(rpb_ref, o_ref):
    n_dr = 2 * NB_ROWS - 1
    n_dc = 2 * NB_COLS - 1
    r = rpb_ref[0] * LOG2E
    m = lax.broadcasted_iota(jnp.int32, r.shape, 1)

    def dc_index(delta):
        return jnp.clip(delta, -(NB_COLS - 1), NB_COLS - 1) + (NB_COLS - 1)

    idx_lo = dc_index(jnp.where(m < GRID_W, m, m - LANES))
    idx_hi = dc_index(m - GRID_W)
    base_lo = jnp.zeros(r.shape, F32)
    base_hi = jnp.zeros(r.shape, F32)
    for dc in range(n_dc):
        col = jnp.broadcast_to(r[:, dc:dc + 1], r.shape)
        base_lo = jnp.where(idx_lo == dc, col, base_lo)
        base_hi = jnp.where(idx_hi == dc, col, base_hi)

    lane = lax.broadcasted_iota(jnp.int32, (GRID_W, LANES), 1)

    def rotated(base, dr):
        return pltpu.roll(jnp.broadcast_to(base[dr:dr + 1], (GRID_W, LANES)), 0, 1, stride=1, stride_axis=0)

    pair = [jnp.where(lane < GRID_W, rotated(base_lo, d), rotated(base_hi, d + 1)) for d in range(n_dr - 1)]
    n_kr = 3 * ROW_GROUP_B
    dr0 = NB_ROWS - 1 - ROW_GROUP_B
    for rq in range(ROW_GROUP_B):
        for p in range(n_kr // 2):
            o_ref[0, rq * GRID_W:(rq + 1) * GRID_W, p * LANES:(p + 1) * LANES] = pair[2 * p - rq + dr0]


def _rpb_call(rpb):
    nh = rpb.shape[0]
    nq = ROW_GROUP_B * GRID_W
    nk = 3 * ROW_GROUP_B * GRID_W
    assert 2 * ROW_GROUP_B == NB_ROWS and 2 * GRID_W == LANES
    n_dr, n_dc = rpb.shape[1], rpb.shape[2]
    rows = -(-n_dr // 8) * 8
    rpb_tiles = jnp.pad(rpb, ((0, 0), (0, rows - n_dr), (0, LANES - n_dc)))
    return pl.pallas_call(
        _rpb_kernel,
        grid=(nh,),
        in_specs=[pl.BlockSpec((1, rows, LANES), lambda h: (h, 0, 0))],
        out_specs=pl.BlockSpec((1, nq, nk), lambda h: (h, 0, 0)),
        out_shape=jax.ShapeDtypeStruct((nh, nq, nk), F32),
        compiler_params=pltpu.CompilerParams(dimension_semantics=("arbitrary",)),
        name="rpb_table",
    )(rpb_tiles)


def _rope(t, cos, sin, first_half):
    partner = jnp.where(first_half, pltpu.roll(t, LANES - 32, 1), pltpu.roll(t, 32, 1))
    return t * cos + partner * sin


def _row_chunked_matmul(h_ref, w_ref, epilogue, between=None):
    n_chunks = h_ref.shape[0] // ROW_CHUNK
    acc = jnp.dot(h_ref[0:ROW_CHUNK], w_ref[...], preferred_element_type=F32)
    for c in range(n_chunks):
        nxt = None
        if c + 1 < n_chunks:
            nxt = jnp.dot(h_ref[(c + 1) * ROW_CHUNK:(c + 2) * ROW_CHUNK], w_ref[...], preferred_element_type=F32)
        epilogue(slice(c * ROW_CHUNK, (c + 1) * ROW_CHUNK), acc)
        if c == 0 and between is not None:
            between()
        acc = nxt


def _gate_kernel(x_ref, sc_ref, sh_ref, g_ref, w_ref, o_ref, hout_ref, hs_ref, *, n_rows):
    i = pl.program_id(0)
    j = pl.program_id(1)
    pr = x_ref.shape[0]

    def norm_piece():
        h = (_rms(x_ref[...], g_ref[...]) * (1 + sc_ref[0]) + sh_ref[0]).astype(BF16)
        hout_ref[...] = h
        hs_ref[i % 2, pl.ds(pl.multiple_of(j * pr, pr), pr), :] = h

    def epilogue(rows, acc):
        o_ref[rows] = jax.nn.sigmoid(acc).astype(BF16)

    @pl.when(i == 0)
    def _():
        norm_piece()

    @pl.when((i > 0) & (i < n_rows))
    def _():
        _row_chunked_matmul(hs_ref.at[(i + 1) % 2], w_ref, epilogue, between=norm_piece)

    @pl.when(i == n_rows)
    def _():
        _row_chunked_matmul(hs_ref.at[(i + 1) % 2], w_ref, epilogue)


def _gate_call(x2, sc, sh, g, w, *, tm, tn, n_cols, rows_per_mod):
    n, d = x2.shape
    n_rows = n // tm
    nj = n_cols // tn
    pr = tm // nj
    assert pr * nj == tm and pr % BF16_ROWS == 0
    mod_tiles = rows_per_mod // tm
    last = n_rows * nj - 1
    piece_map = lambda i, j: (jnp.minimum(i * nj + j, last), 0)
    mod_map = lambda i, j: (jnp.minimum(i, n_rows - 1) // mod_tiles, 0, 0)
    return pl.pallas_call(
        functools.partial(_gate_kernel, n_rows=n_rows),
        grid=(n_rows + 1, nj),
        in_specs=[pl.BlockSpec((pr, d), piece_map),
                  pl.BlockSpec((1, 1, d), mod_map),
                  pl.BlockSpec((1, 1, d), mod_map),
                  pl.BlockSpec((1, d), lambda i, j: (0, 0)),
                  pl.BlockSpec((d, tn), lambda i, j: (0, j))],
        out_specs=[pl.BlockSpec((tm, tn), lambda i, j: (jnp.maximum(i - 1, 0), jnp.where(i > 0, j, 0))),
                   pl.BlockSpec((pr, d), piece_map)],
        out_shape=[jax.ShapeDtypeStruct((n, n_cols), BF16), jax.ShapeDtypeStruct((n, d), BF16)],
        scratch_shapes=[pltpu.VMEM((2, tm, d), BF16)],
        compiler_params=pltpu.CompilerParams(dimension_semantics=("arbitrary", "arbitrary"),
                                             vmem_limit_bytes=VMEM_LIMIT_BYTES),
        name="gate_proj",
    )(x2, sc, sh, g, w)


def _qkv_weight_tile(jc):
    n_qa = W_QA // TN
    last = (W_QA + 3 * W_B) // TN
    return jnp.where(jc < n_qa, jc, jnp.where(jc == last, n_qa, jc + 1))


def _qkv_kernel(h_ref, w_ref, cos_ref, sin_ref, o_ref, *, j0):
    _proj_tile(h_ref, w_ref, cos_ref, sin_ref, o_ref, pl.program_id(1) + j0, 0)


def _qkv_call(h, w, cos, sin, *, tm, j0, nj):
    n, d = h.shape
    pos_tiles = cos.shape[0] // tm
    return pl.pallas_call(
        functools.partial(_qkv_kernel, j0=j0),
        grid=(n // tm, nj),
        in_specs=[pl.BlockSpec((tm, d), lambda i, j: (i, 0)),
                  pl.BlockSpec((d, TN), lambda i, j: (0, _qkv_weight_tile(j + j0))),
                  pl.BlockSpec((tm, HEAD_DIM), lambda i, j: (i % pos_tiles, 0)),
                  pl.BlockSpec((tm, HEAD_DIM), lambda i, j: (i % pos_tiles, 0))],
        out_specs=pl.BlockSpec((tm, TN), lambda i, j: (i, j)),
        out_shape=jax.ShapeDtypeStruct((n, nj * TN), BF16),
        compiler_params=pltpu.CompilerParams(dimension_semantics=("arbitrary", "arbitrary"),
                                             vmem_limit_bytes=VMEM_LIMIT_BYTES),
        name="qkv_proj",
    )(h, w, cos, sin)


def _inproj_kernel(x_ref, sc_ref, sh_ref, g_ref, w_ref, cos_ref, sin_ref, o_ref, h_ref, *, j0):
    j = pl.program_id(1)

    @pl.when(j == 0)
    def _():
        h = _rms(x_ref[...], g_ref[...]) * (1 + sc_ref[0]) + sh_ref[0]
        h_ref[...] = h.astype(BF16)

    _proj_tile(h_ref, w_ref, cos_ref, sin_ref, o_ref, j + j0, 0)


def _proj_tile(h_ref, w_ref, cos_ref, sin_ref, o_ref, jj, n_gate):
    lane = lax.broadcasted_iota(jnp.int32, (1, HEAD_DIM), 1)
    first_half = (lane & 63) < 32

    def row_chunks(epilogue):
        _row_chunked_matmul(h_ref, w_ref, epilogue)

    def rope_heads(rows, acc, n_heads, scale):
        cos = cos_ref[rows]
        sin = sin_ref[rows]
        for hd in range(n_heads):
            sl = slice(hd * HEAD_DIM, (hd + 1) * HEAD_DIM)
            r = _rope(acc[:, sl], cos, sin, first_half)
            if scale is not None:
                r = r * scale
            o_ref[rows, sl] = r.astype(BF16)

    @pl.when((jj >= n_gate) & (jj < n_gate + 2))
    def _():
        row_chunks(lambda rows, acc: rope_heads(rows, acc, TN // HEAD_DIM, Q_SCALE))

    @pl.when((jj >= n_gate + 2) & (jj < n_gate + 4))
    def _():
        def epilogue(rows, acc):
            o_ref[rows] = (acc * Q_SCALE).astype(BF16)
        row_chunks(epilogue)

    @pl.when((jj >= n_gate + 4) & (jj < n_gate + 8))
    def _():
        def epilogue(rows, acc):
            o_ref[rows] = acc.astype(BF16)
        row_chunks(epilogue)

    @pl.when(jj == n_gate + 8)
    def _():
        def epilogue(rows, acc):
            rope_heads(rows, acc, N_KV_A, None)
            o_ref[rows, W_KA:] = acc[:, W_KA:].astype(BF16)
        row_chunks(epilogue)


def _inproj_call(x2, sc, sh, g, w, cos, sin, *, tm, rows_per_mod, j0, nj):
    n, d = x2.shape
    pos_tiles = cos.shape[0] // tm
    mod_tiles = rows_per_mod // tm
    return pl.pallas_call(
        functools.partial(_inproj_kernel, j0=j0),
        grid=(n // tm, nj),
        in_specs=[pl.BlockSpec((tm, d), lambda i, j: (i, 0)),
                  pl.BlockSpec((1, 1, d), lambda i, j: (i // mod_tiles, 0, 0)),
                  pl.BlockSpec((1, 1, d), lambda i, j: (i // mod_tiles, 0, 0)),
                  pl.BlockSpec((1, d), lambda i, j: (0, 0)),
                  pl.BlockSpec((d, TN), lambda i, j: (0, _qkv_weight_tile(j + j0))),
                  pl.BlockSpec((tm, HEAD_DIM), lambda i, j: (i % pos_tiles, 0)),
                  pl.BlockSpec((tm, HEAD_DIM), lambda i, j: (i % pos_tiles, 0))],
        out_specs=pl.BlockSpec((tm, TN), lambda i, j: (i, j)),
        out_shape=jax.ShapeDtypeStruct((n, nj * TN), BF16),
        scratch_shapes=[pltpu.VMEM((tm, d), BF16)],
        compiler_params=pltpu.CompilerParams(dimension_semantics=("arbitrary", "arbitrary"),
                                             vmem_limit_bytes=VMEM_LIMIT_BYTES),
        name="in_proj",
    )(x2, sc, sh, g, w, cos, sin)


def _attn_a_kernel(sink_ref, q_ref, kp_ref, km_ref, kn_ref, vp_ref, vm_ref, vn_ref, kc_ref, vc_ref, o_ref,
                   *, n_blocks):
    nq = pl.program_id(1)
    sub = QB_A // BLOCK
    rows = GQA_GROUP * BLOCK
    n_loc = 3 * BLOCK
    r_io = lax.broadcasted_iota(jnp.int32, (rows, n_loc), 0)
    c_io = lax.broadcasted_iota(jnp.int32, (rows, n_loc), 1)
    qi_io = r_io & (BLOCK - 1)
    in_band = (c_io >= qi_io) & (c_io <= qi_io + 2 * WINDOW)
    c_row = lax.broadcasted_iota(jnp.int32, (1, n_loc), 1)
    g_io = lax.shift_right_logical(lax.broadcasted_iota(jnp.int32, (rows, 1), 0), BLOCK.bit_length() - 1)
    nt = (((1,), (1,)), ((), ()))
    for kvh in range(N_KV_A):
        hs = slice(kvh * HEAD_DIM, (kvh + 1) * HEAD_DIM)
        kband = jnp.concatenate([kp_ref[:, hs], km_ref[:, hs], kn_ref[:, hs]], axis=0)
        vband = jnp.concatenate([vp_ref[:, hs], vm_ref[:, hs], vn_ref[:, hs]], axis=0)
        kc = kc_ref[:, hs]
        vc = vc_ref[:, hs]
        sink = jnp.zeros((rows, 1), F32)
        for g in range(GQA_GROUP):
            sink = jnp.where(g_io == g, sink_ref[kvh * GQA_GROUP + g] * LOG2E, sink)
        for qi in range(sub):
            blk = nq * sub + qi
            lo = jnp.where(blk == 0, BLOCK, 0)
            hi = jnp.where(blk == n_blocks - 1, 2 * BLOCK - 1, n_loc - 1)
            valid = in_band & ((c_row >= lo) & (c_row <= hi))
            q4 = jnp.concatenate(
                [q_ref[qi * BLOCK:(qi + 1) * BLOCK, (kvh * GQA_GROUP + g) * HEAD_DIM:(kvh * GQA_GROUP + g + 1) * HEAD_DIM]
                 for g in range(GQA_GROUP)], axis=0)
            s_loc = lax.dot_general(q4, kband[qi * BLOCK:qi * BLOCK + n_loc], nt, preferred_element_type=F32)
            s_loc = jnp.where(valid, s_loc, NEG_INF)
            s_ctx = lax.dot_general(q4, kc, nt, preferred_element_type=F32)
            m = jnp.maximum(jnp.maximum(jnp.max(s_loc, axis=-1, keepdims=True),
                                        jnp.max(s_ctx, axis=-1, keepdims=True)), sink)
            e_loc = jnp.exp2(s_loc - m)
            e_ctx = jnp.exp2(s_ctx - m)
            den = (jnp.sum(e_loc, axis=-1, keepdims=True) + jnp.sum(e_ctx, axis=-1, keepdims=True)
                   + jnp.exp2(sink - m))
            o = (jnp.dot(e_loc.astype(BF16), vband[qi * BLOCK:qi * BLOCK + n_loc], preferred_element_type=F32)
                 + jnp.dot(e_ctx.astype(BF16), vc, preferred_element_type=F32)) * (1.0 / den)
            for g in range(GQA_GROUP):
                hd = kvh * GQA_GROUP + g
                o_ref[qi * BLOCK:(qi + 1) * BLOCK, hd * HEAD_DIM:(hd + 1) * HEAD_DIM] = (
                    o[g * BLOCK:(g + 1) * BLOCK].astype(BF16))


def _attn_a_call(proj, cproj, sink, *, batch, seq, ctx_len, q_col, k_col, v_col, ck_col, cv_col):
    nqb = seq // QB_A
    sub = QB_A // BLOCK
    n_blocks = seq // BLOCK

    def prev_map(b, n, col):
        return (b * n_blocks + jnp.maximum(n * sub - 1, 0), col // W_KA)

    def next_map(b, n, col):
        return (b * n_blocks + jnp.minimum(n * sub + sub, n_blocks - 1), col // W_KA)

    def main_map(b, n, col):
        return (b * nqb + n, col // W_KA)

    edge = lambda f, col: pl.BlockSpec((BLOCK, W_KA), functools.partial(f, col=col))
    main = lambda col: pl.BlockSpec((QB_A, W_KA), functools.partial(main_map, col=col))
    ctxs = lambda col: pl.BlockSpec((ctx_len, W_KA), lambda b, n: (b, col // W_KA))
    return pl.pallas_call(
        functools.partial(_attn_a_kernel, n_blocks=n_blocks),
        grid=(batch, nqb),
        in_specs=[pl.BlockSpec(memory_space=pltpu.SMEM),
                  pl.BlockSpec((QB_A, W_QA), lambda b, n: (b * nqb + n, q_col // W_QA)),
                  edge(prev_map, k_col), main(k_col), edge(next_map, k_col),
                  edge(prev_map, v_col), main(v_col), edge(next_map, v_col),
                  ctxs(ck_col), ctxs(cv_col)],
        out_specs=pl.BlockSpec((QB_A, W_QA), lambda b, n: (b * nqb + n, 0)),
        out_shape=jax.ShapeDtypeStruct((batch * seq, W_QA), BF16),
        compiler_params=pltpu.CompilerParams(dimension_semantics=("arbitrary",) * 2,
                                             vmem_limit_bytes=VMEM_LIMIT_BYTES),
        name="attn_a",
    )(sink, proj, proj, proj, proj, proj, proj, proj, cproj, cproj)


def _attn_b_kernel(q_ref, kp_ref, km_ref, kn_ref, vp_ref, vm_ref, vn_ref, kc_ref, vc_ref, bias_ref, o_ref,
                   *, grid_rows):
    n = pl.program_id(1)
    nq = ROW_GROUP_B * GRID_W
    nk = 3 * nq
    heads = q_ref.shape[1] // HEAD_DIM
    r_io = lax.broadcasted_iota(jnp.int32, (nq, nk), 0)
    c_io = lax.broadcasted_iota(jnp.int32, (nq, nk), 1)
    log_w = GRID_W.bit_length() - 1
    q_col = r_io & (GRID_W - 1)
    k_col = c_io & (GRID_W - 1)
    c_start = jnp.clip(q_col - NB_COLS // 2, 0, GRID_W - NB_COLS)
    col_ok = (k_col >= c_start) & (k_col < c_start + NB_COLS)
    nt = (((1,), (1,)), ((), ()))
    for grp in range(GROUPS_PER_STEP_B):
        gi = n * GROUPS_PER_STEP_B + grp
        r_abs = gi * ROW_GROUP_B + lax.shift_right_logical(r_io, log_w)
        kr_abs = (gi - 1) * ROW_GROUP_B + lax.shift_right_logical(c_io, log_w)
        r_start = jnp.clip(r_abs - NB_ROWS // 2, 0, grid_rows - NB_ROWS)
        valid = (kr_abs >= r_start) & (kr_abs < r_start + NB_ROWS) & col_ok
        cap = jnp.where(valid, F32_MAX, NEG_INF)
        rows = slice(grp * nq, (grp + 1) * nq)
        for hd in range(heads):
            sl = slice(hd * HEAD_DIM, (hd + 1) * HEAD_DIM)
            q = q_ref[rows, sl]
            keys = jnp.concatenate([kp_ref[:, sl], km_ref[:, sl], kn_ref[:, sl]], axis=0)[grp * nq:grp * nq + nk]
            vals = jnp.concatenate([vp_ref[:, sl], vm_ref[:, sl], vn_ref[:, sl]], axis=0)[grp * nq:grp * nq + nk]
            s_nb = lax.dot_general(q, keys, nt, preferred_element_type=F32)
            s_nb = jnp.minimum(s_nb + bias_ref[hd], cap)
            s_ctx = lax.dot_general(q, kc_ref[:, sl], nt, preferred_element_type=F32)
            m = jnp.maximum(jnp.max(s_nb, axis=-1, keepdims=True), jnp.max(s_ctx, axis=-1, keepdims=True))
            e_nb = jnp.exp2(s_nb - m)
            e_ctx = jnp.exp2(s_ctx - m)
            den = jnp.sum(e_nb, axis=-1, keepdims=True) + jnp.sum(e_ctx, axis=-1, keepdims=True)
            o = (jnp.dot(e_nb.astype(BF16), vals, preferred_element_type=F32)
                 + jnp.dot(e_ctx.astype(BF16), vc_ref[:, sl], preferred_element_type=F32)) * (1.0 / den)
            o_ref[rows, sl] = o.astype(BF16)


def _attn_b_call(proj, cproj, bias, *, batch, seq, ctx_len, q_col, k_col, v_col, ck_col, cv_col):
    nq = ROW_GROUP_B * GRID_W
    ng = seq // nq
    gps = GROUPS_PER_STEP_B
    ns = ng // gps
    hw = W_B

    def edge(col, first):
        def imap(b, n):
            g = jnp.maximum(n * gps - 1, 0) if first else jnp.minimum(n * gps + gps, ng - 1)
            return (b * ng + g, col // hw)
        return pl.BlockSpec((nq, hw), imap)

    main = lambda col: pl.BlockSpec((gps * nq, hw), lambda b, n: (b * ns + n, col // hw))
    ctxs = lambda col: pl.BlockSpec((ctx_len, hw), lambda b, n: (b, col // hw))
    return pl.pallas_call(
        functools.partial(_attn_b_kernel, grid_rows=seq // GRID_W),
        grid=(batch, ns),
        in_specs=[main(q_col),
                  edge(k_col, True), main(k_col), edge(k_col, False),
                  edge(v_col, True), main(v_col), edge(v_col, False),
                  ctxs(ck_col), ctxs(cv_col),
                  pl.BlockSpec((hw // HEAD_DIM, nq, bias.shape[2]), lambda b, n: (0, 0, 0))],
        out_specs=pl.BlockSpec((gps * nq, hw), lambda b, n: (b * ns + n, 0)),
        out_shape=jax.ShapeDtypeStruct((batch * seq, W_B), BF16),
        compiler_params=pltpu.CompilerParams(dimension_semantics=("arbitrary",) * 2,
                                             vmem_limit_bytes=VMEM_LIMIT_BYTES),
        name="attn_b",
    )(proj, proj, proj, proj, proj, proj, proj, cproj, cproj, bias)


def _resident_bf16(w_ref, wb_ref):
    @pl.when(pl.program_id(0) == 0)
    def _():
        wb_ref[...] = w_ref[...].astype(BF16)


def _mix_kernel(oa_ref, ob_ref, ga_ref, gb_ref, wa_ref, wb_ref, z_ref, wa_bf, wb_bf):
    _resident_bf16(wa_ref, wa_bf)
    _resident_bf16(wb_ref, wb_bf)
    ya = jnp.dot(oa_ref[...], wa_bf[...], preferred_element_type=F32)
    yb = jnp.dot(ob_ref[...], wb_bf[...], preferred_element_type=F32)
    z_ref[...] = (ga_ref[...].astype(F32) * ya + gb_ref[...].astype(F32) * yb).astype(BF16)


def _mix_call(oa, ob, proj, wa, wb, *, tm):
    n = oa.shape[0]
    d = wa.shape[1]
    const = lambda shape: pl.BlockSpec(shape, lambda i: (0, 0), pipeline_mode=pl.Buffered(1))
    return pl.pallas_call(
        _mix_kernel,
        grid=(n // tm,),
        in_specs=[pl.BlockSpec((tm, oa.shape[1]), lambda i: (i, 0)),
                  pl.BlockSpec((tm, ob.shape[1]), lambda i: (i, 0)),
                  pl.BlockSpec((tm, d), lambda i: (i, 0)),
                  pl.BlockSpec((tm, d), lambda i: (i, 1)),
                  const(wa.shape), const(wb.shape)],
        out_specs=pl.BlockSpec((tm, d), lambda i: (i, 0)),
        out_shape=jax.ShapeDtypeStruct((n, d), BF16),
        scratch_shapes=[pltpu.VMEM(wa.shape, BF16), pltpu.VMEM(wb.shape, BF16)],
        compiler_params=pltpu.CompilerParams(dimension_semantics=("arbitrary",),
                                             vmem_limit_bytes=VMEM_LIMIT_BYTES),
        name="branch_mix",
    )(oa, ob, proj, proj, wa, wb)


def _outnorm_kernel(a_ref, w_ref, x_ref, gt_ref, g_ref, o_ref, w_bf):
    _resident_bf16(w_ref, w_bf)
    y = jnp.dot(a_ref[...], w_bf[...], preferred_element_type=F32)
    o_ref[...] = x_ref[...] + gt_ref[0] * _rms(y, g_ref[...])


def _outnorm_call(a, w, x2, gt, g, *, tm, rows_per_mod):
    n, kdim = a.shape
    d = w.shape[1]
    mod_tiles = rows_per_mod // tm
    return pl.pallas_call(
        _outnorm_kernel,
        grid=(n // tm,),
        in_specs=[pl.BlockSpec((tm, kdim), lambda i: (i, 0)),
                  pl.BlockSpec((kdim, d), lambda i: (0, 0), pipeline_mode=pl.Buffered(1)),
                  pl.BlockSpec((tm, d), lambda i: (i, 0)),
                  pl.BlockSpec((1, 1, d), lambda i: (i // mod_tiles, 0, 0)),
                  pl.BlockSpec((1, d), lambda i: (0, 0))],
        out_specs=pl.BlockSpec((tm, d), lambda i: (i, 0)),
        out_shape=jax.ShapeDtypeStruct((n, d), F32),
        scratch_shapes=[pltpu.VMEM(w.shape, BF16)],
        compiler_params=pltpu.CompilerParams(dimension_semantics=("arbitrary",),
                                             vmem_limit_bytes=VMEM_LIMIT_BYTES),
        name="out_norm",
    )(a, w, x2, gt, g)


def _rowmm_norm_kernel(a_ref, w_ref, x_ref, gt_ref, g_ref, o_ref, *, n_chunks):
    cm = a_ref.shape[0] // n_chunks
    y = jnp.dot(a_ref[0:cm], w_ref[...], preferred_element_type=F32)
    for c in range(n_chunks):
        nxt = None
        if c + 1 < n_chunks:
            nxt = jnp.dot(a_ref[(c + 1) * cm:(c + 2) * cm], w_ref[...], preferred_element_type=F32)
        rows = slice(c * cm, (c + 1) * cm)
        o_ref[rows] = x_ref[rows] + gt_ref[0] * _rms(y, g_ref[...])
        y = nxt


def _rowmm_norm_call(a, w, x2, gt, g, *, tm, n_chunks, rows_per_mod):
    n, kdim = a.shape
    d = w.shape[1]
    mod_tiles = rows_per_mod // tm
    return pl.pallas_call(
        functools.partial(_rowmm_norm_kernel, n_chunks=n_chunks),
        grid=(n // tm,),
        in_specs=[pl.BlockSpec((tm, kdim), lambda i: (i, 0)),
                  pl.BlockSpec((kdim, d), lambda i: (0, 0), pipeline_mode=pl.Buffered(1)),
                  pl.BlockSpec((tm, d), lambda i: (i, 0)),
                  pl.BlockSpec((1, 1, d), lambda i: (i // mod_tiles, 0, 0)),
                  pl.BlockSpec((1, d), lambda i: (0, 0))],
        out_specs=pl.BlockSpec((tm, d), lambda i: (i, 0)),
        out_shape=jax.ShapeDtypeStruct((n, d), F32),
        compiler_params=pltpu.CompilerParams(dimension_semantics=("arbitrary",),
                                             vmem_limit_bytes=VMEM_LIMIT_RESIDENT_BYTES),
        name="rowmm_norm",
    )(a, w, x2, gt, g)


def _ffn_up_kernel(x0p_ref, x0_ref, x0n_ref, sc0_ref, sh0_ref, xp_ref, xm_ref, xn_ref, sc_ref, sh_ref, g_ref,
                   wa_ref, wg_ref, cwa_ref, cwg_ref, cba_ref, cbg_ref,
                   o_ref, h_cur, h_nxt, ua_ref, ug_ref, *, nj, n_tiles, tiles_per_seq, n_pieces):
    t = pl.program_id(0)
    tm = x0_ref.shape[0]
    halo = xp_ref.shape[0]
    pr = xm_ref.shape[0]
    ext = tm + 2 * halo
    margin = 8

    def mod(x, sc, sh):
        return (_rms(x, g_ref[...]) * (1 + sc[0]) + sh[0]).astype(BF16)

    @pl.when(t == 0)
    def _():
        ua_ref[...] = jnp.zeros_like(ua_ref)
        ug_ref[...] = jnp.zeros_like(ug_ref)
        h_cur[0:halo] = mod(x0p_ref[...], sc0_ref, sh0_ref)
        h_cur[halo:halo + tm] = mod(x0_ref[...], sc0_ref, sh0_ref)
        h_cur[halo + tm:ext] = mod(x0n_ref[...], sc0_ref, sh0_ref)

    tc = jnp.minimum(t, n_tiles - 1)

    @pl.when((t > 0) & (t < n_tiles) & (t % nj == 0))
    def _():
        h_cur[...] = h_nxt[...]

    def next_row_piece():
        p = jnp.minimum(tc % nj, n_pieces - 1)
        h_nxt[0:halo] = mod(xp_ref[...], sc_ref, sh_ref)
        h_nxt[pl.ds(pl.multiple_of(halo + p * pr, BF16_ROWS), pr), :] = mod(xm_ref[...], sc_ref, sh_ref)
        h_nxt[halo + tm:ext] = mod(xn_ref[...], sc_ref, sh_ref)

    slot = t % 2
    prev = 1 - slot
    ip = jnp.maximum(t - 1, 0) // nj
    first_row = jnp.where((ip % tiles_per_seq) == 0, 0, -1)
    last_row = jnp.where((ip % tiles_per_seq) == tiles_per_seq - 1, tm - 1, -1)
    row = lax.broadcasted_iota(jnp.int32, (tm, 1), 0)

    def epilogue(lo, hi):
        cm = hi - lo
        rowc = row[lo:hi]

        def conv(u_ref, cw_ref, cb_ref):
            win = u_ref[prev, pl.ds(halo + lo - margin, cm + 2 * margin), :]
            up = pltpu.roll(win, 1, 0)[margin:margin + cm]
            un = pltpu.roll(win, cm + 2 * margin - 1, 0)[margin:margin + cm]
            up = jnp.where(rowc == first_row, 0.0, up)
            un = jnp.where(rowc == last_row, 0.0, un)
            out = cb_ref[...] + up * cw_ref[0:1]
            out = out + win[margin:margin + cm] * cw_ref[1:2]
            return out + un * cw_ref[2:3]

        a = conv(ua_ref, cwa_ref, cba_ref)
        gte = conv(ug_ref, cwg_ref, cbg_ref)
        o_ref[lo:hi] = (jax.nn.silu(gte) * a).astype(BF16)

    half = ext // 2
    third = (tm // 3) // BF16_ROWS * BF16_ROWS
    chunks = [(0, third), (third, 2 * third), (2 * third, tm)]
    ua_ref[slot, 0:half] = jnp.dot(h_cur[0:half], wa_ref[...], preferred_element_type=F32)
    next_row_piece()
    epilogue(*chunks[0])
    ua_ref[slot, half:ext] = jnp.dot(h_cur[half:ext], wa_ref[...], preferred_element_type=F32)
    epilogue(*chunks[1])
    ug_ref[slot, 0:half] = jnp.dot(h_cur[0:half], wg_ref[...], preferred_element_type=F32)
    epilogue(*chunks[2])
    ug_ref[slot, half:ext] = jnp.dot(h_cur[half:ext], wg_ref[...], preferred_element_type=F32)


def _ffn_up_call(x2, sc, sh, g, w_up, conv_w, conv_b, *, tm, tn, seq):
    n, d = x2.shape
    dff = w_up.shape[1] // 2
    nj = dff // tn
    halo = BF16_ROWS
    hb = tm // halo
    n_halo = n // halo
    tiles_per_seq = seq // tm
    n_rows = n // tm
    n_tiles = n_rows * nj
    pr = 128
    n_pieces = tm // pr
    assert (tm + 2 * halo) % (2 * BF16_ROWS) == 0 and n_pieces <= nj and n_pieces * pr == tm

    def cur(t):
        tc = jnp.minimum(t, n_tiles - 1)
        return tc // nj, tc % nj

    def prv(t):
        tp = jnp.maximum(t - 1, 0)
        return tp // nj, tp % nj

    def nxt(t):
        return jnp.minimum(cur(t)[0] + 1, n_rows - 1)

    once = lambda shape, imap: pl.BlockSpec(shape, imap, pipeline_mode=pl.Buffered(1))
    return pl.pallas_call(
        functools.partial(_ffn_up_kernel, nj=nj, n_tiles=n_tiles, tiles_per_seq=tiles_per_seq, n_pieces=n_pieces),
        grid=(n_tiles + 1,),
        in_specs=[once((halo, d), lambda t: (0, 0)),
                  once((tm, d), lambda t: (0, 0)),
                  once((halo, d), lambda t: (jnp.minimum(hb, n_halo - 1), 0)),
                  once((1, 1, d), lambda t: (0, 0, 0)),
                  once((1, 1, d), lambda t: (0, 0, 0)),
                  pl.BlockSpec((halo, d), lambda t: (jnp.maximum(nxt(t) * hb - 1, 0), 0)),
                  pl.BlockSpec((pr, d), lambda t: (nxt(t) * n_pieces + jnp.minimum(cur(t)[1], n_pieces - 1), 0)),
                  pl.BlockSpec((halo, d), lambda t: (jnp.minimum((nxt(t) + 1) * hb, n_halo - 1), 0)),
                  pl.BlockSpec((1, 1, d), lambda t: (nxt(t) // tiles_per_seq, 0, 0)),
                  pl.BlockSpec((1, 1, d), lambda t: (nxt(t) // tiles_per_seq, 0, 0)),
                  pl.BlockSpec((1, d), lambda t: (0, 0)),
                  pl.BlockSpec((d, tn), lambda t: (0, cur(t)[1])),
                  pl.BlockSpec((d, tn), lambda t: (0, cur(t)[1] + nj)),
                  pl.BlockSpec((CONV_W, tn), lambda t: (0, prv(t)[1])),
                  pl.BlockSpec((CONV_W, tn), lambda t: (0, prv(t)[1] + nj)),
                  pl.BlockSpec((1, tn), lambda t: (0, prv(t)[1])),
                  pl.BlockSpec((1, tn), lambda t: (0, prv(t)[1] + nj))],
        out_specs=pl.BlockSpec((tm, tn), lambda t: prv(t)),
        out_shape=jax.ShapeDtypeStruct((n, dff), BF16),
        scratch_shapes=[pltpu.VMEM((tm + 2 * halo, d), BF16),
                        pltpu.VMEM((tm + 2 * halo, d), BF16),
                        pltpu.VMEM((2, tm + 2 * halo, tn), F32),
                        pltpu.VMEM((2, tm + 2 * halo, tn), F32)],
        compiler_params=pltpu.CompilerParams(dimension_semantics=("arbitrary",),
                                             vmem_limit_bytes=VMEM_LIMIT_BYTES),
        name="ffn_up",
    )(x2, x2, x2, sc, sh, x2, x2, x2, sc, sh, g, w_up, w_up, conv_w, conv_w, conv_b, conv_b)


def _rope_tables(seq):
    t = np.arange(seq)
    half = HEAD_DIM // 4
    inv = np.float32(ROPE_BASE) ** (-np.arange(half, dtype=np.float32) / np.float32(half))

    def cs(pos):
        ang = pos.astype(np.float32)[:, None] * inv[None, :]
        c, s = np.cos(ang), np.sin(ang)
        return np.concatenate([c, c], axis=1), np.concatenate([-s, s], axis=1)

    cr, sr = cs(t // GRID_W)
    cc, sn = cs(t % GRID_W)
    return (jnp.asarray(np.concatenate([cr, cc], axis=1), F32), jnp.asarray(np.concatenate([sr, sn], axis=1), F32))


def _pick_tile(n, pref):
    while n % pref:
        pref //= 2
    return pref


class _RowTiles(NamedTuple):
    proj: int
    qkv: int
    mix: int
    down: int


def _row_tiles(seq):
    return _RowTiles(*(_pick_tile(seq, pref) for pref in (1024, 2048, 512, 256)))


def kernel(x, c, ctx, c_ctx, w_mod, b_mod, g_attn_pre, g_attn_post, g_ffn_pre, g_ffn_post, w_in, sink_a, rpb_b,
           w_br_a, w_br_b, w_o, w_up, conv_w, conv_b, w_down):
    batch, seq, d = x.shape
    ctx_len = ctx.shape[1]
    assert w_mod.shape[0] == 1 and d == 16 * HEAD_DIM
    assert seq % QB_A == 0 and seq % (GROUPS_PER_STEP_B * ROW_GROUP_B * GRID_W) == 0 and WINDOW == BLOCK
    n = batch * seq
    x2 = x.reshape(n, d)

    pad = (-(batch + 1)) % 8
    cs = jnp.concatenate([c, c_ctx[None, :], jnp.zeros((pad, d), F32)], axis=0)
    mod = _mod_call(cs, w_mod[0], b_mod[0][None, :])
    sh1, sc1, gt1, sh2, sc2, gt2 = [mod[:batch, k * d:(k + 1) * d].reshape(batch, 1, d) for k in range(6)]
    csh1 = mod[batch:batch + 1, 0:d].reshape(1, 1, d)
    csc1 = mod[batch:batch + 1, d:2 * d].reshape(1, 1, d)

    n_qkv_cols = W_QA + 2 * W_KA + 3 * W_B
    n_gate_cols = 2 * d
    w_qkv = w_in[0][:, :n_qkv_cols].astype(BF16)
    w_gate = w_in[0][:, n_qkv_cols:].astype(BF16)
    n_tiles = n_qkv_cols // TN
    c_qa = 0
    c_qb = c_qa + W_QA
    c_kb = c_qb + W_B
    c_vb = c_kb + W_B
    c_ka = c_vb + W_B
    c_va = c_ka + W_KA
    cos, sin = _rope_tables(seq)
    tiles = _row_tiles(seq)
    gates, h1 = _gate_call(x2, sc1, sh1, g_attn_pre, w_gate, tm=tiles.proj, tn=TN_GATE, n_cols=n_gate_cols,
                           rows_per_mod=seq)
    proj = _qkv_call(h1, w_qkv, cos, sin, tm=tiles.qkv, j0=0, nj=n_tiles)
    nc = batch * ctx_len
    j0c = c_kb // TN
    cproj = _inproj_call(ctx.reshape(nc, d), csc1, csh1, g_attn_pre, w_qkv,
                         jnp.ones((nc, HEAD_DIM), F32), jnp.zeros((nc, HEAD_DIM), F32),
                         tm=nc, rows_per_mod=nc, j0=j0c, nj=n_tiles - j0c)
    cc_kb, cc_vb, cc_ka, cc_va = 0, W_B, 2 * W_B, 2 * W_B + W_KA

    oa = _attn_a_call(proj, cproj, sink_a[0], batch=batch, seq=seq, ctx_len=ctx_len,
                      q_col=c_qa, k_col=c_ka, v_col=c_va, ck_col=cc_ka, cv_col=cc_va)
    bias = _rpb_call(rpb_b[0])
    ob = _attn_b_call(proj, cproj, bias, batch=batch, seq=seq, ctx_len=ctx_len,
                      q_col=c_qb, k_col=c_kb, v_col=c_vb, ck_col=cc_kb, cv_col=cc_vb)

    z = _mix_call(oa, ob, gates, w_br_a[0], w_br_b[0], tm=tiles.mix)
    x1 = _outnorm_call(z, w_o[0], x2, gt1, g_attn_post, tm=tiles.mix, rows_per_mod=seq)

    act = _ffn_up_call(x1, sc2, sh2, g_ffn_pre, w_up[0].astype(BF16), conv_w[0], conv_b[0][None, :],
                       tm=tiles.proj, tn=TN, seq=seq)
    out = _rowmm_norm_call(act, w_down[0].astype(BF16), x1, gt2, g_ffn_post, tm=tiles.down, n_chunks=2,
                           rows_per_mod=seq)
    return out.reshape(batch, seq, d)
```

```python
import functools
from typing import NamedTuple

import numpy as np
import jax
import jax.numpy as jnp
from jax import lax
from jax.experimental import pallas as pl
from jax.experimental.pallas import tpu as pltpu

F32 = jnp.float32
BF16 = jnp.bfloat16

GRID_W = 64
HEAD_DIM = 128
N_HEADS_A = 8
N_KV_A = 2
GQA_GROUP = N_HEADS_A // N_KV_A
WINDOW = 128
BLOCK = 128
N_HEADS_B = 8
NB_ROWS = 8
NB_COLS = 16
CONV_W = 3
ROPE_BASE = 10000.0
EPS = 1e-6
NEG_INF = -1e30
ATTN_SCALE = HEAD_DIM ** -0.5
LOG2E = 1.4426950408889634
Q_SCALE = ATTN_SCALE * LOG2E
F32_MAX = 3.4028234663852886e38

W_QA = N_HEADS_A * HEAD_DIM
W_KA = N_KV_A * HEAD_DIM
W_B = N_HEADS_B * HEAD_DIM

LANES = 128
BF16_ROWS = 16
VMEM_LIMIT_BYTES = 56 * 1024 * 1024
VMEM_LIMIT_RESIDENT_BYTES = 60 * 1024 * 1024

TN = 512
TN_GATE = 1024
ROW_CHUNK = 256
ROW_GROUP_B = 4
GROUPS_PER_STEP_B = 2
QB_A = 1024


def _rms(x, g):
    return (x * lax.rsqrt(jnp.mean(x * x, axis=-1, keepdims=True) + EPS)) * g


def _mod_kernel(c_ref, w_ref, b_ref, o_ref):
    a = jax.nn.silu(c_ref[...]).astype(BF16)
    o_ref[...] = jnp.dot(a, w_ref[...].astype(BF16), preferred_element_type=F32) + b_ref[...]


def _mod_call(cs, w, b):
    m, d = cs.shape
    n = w.shape[1]
    tn = 1536 if n % 1536 == 0 else n
    return pl.pallas_call(
        _mod_kernel,
        grid=(n // tn,),
        in_specs=[pl.BlockSpec((m, d), lambda j: (0, 0)),
                  pl.BlockSpec((d, tn), lambda j: (0, j)),
                  pl.BlockSpec((1, tn), lambda j: (0, j))],
        out_specs=pl.BlockSpec((m, tn), lambda j: (0, j)),
        out_shape=jax.ShapeDtypeStruct((m, n), F32),
        compiler_params=pltpu.CompilerParams(dimension_semantics=("arbitrary",),
                                             vmem_limit_bytes=VMEM_LIMIT_BYTES),
        name="mod",
    )(cs, w, b)


def _rpb_kernel(rpb_ref, o_ref):
    n_dr = 2 * NB_ROWS - 1
    n_dc = 2 * NB_COLS - 1
    r = rpb_ref[0] * LOG2E
    m = lax.broadcasted_iota(jnp.int32, r.shape, 1)

    def dc_index(delta):
        return jnp.clip(delta, -(NB_COLS - 1), NB_COLS - 1) + (NB_COLS - 1)

    idx_lo = dc_index(jnp.where(m < GRID_W, m, m - LANES))
    idx_hi = dc_index(m - GRID_W)
    base_lo = jnp.zeros(r.shape, F32)
    base_hi = jnp.zeros(r.shape, F32)
    for dc in range(n_dc):
        col = jnp.broadcast_to(r[:, dc:dc + 1], r.shape)
        base_lo = jnp.where(idx_lo == dc, col, base_lo)
        base_hi = jnp.where(idx_hi == dc, col, base_hi)

    lane = lax.broadcasted_iota(jnp.int32, (GRID_W, LANES), 1)

    def rotated(base, dr):
        return pltpu.roll(jnp.broadcast_to(base[dr:dr + 1], (GRID_W, LANES)), 0, 1, stride=1, stride_axis=0)

    pair = [jnp.where(lane < GRID_W, rotated(base_lo, d), rotated(base_hi, d + 1)) for d in range(n_dr - 1)]
    n_kr = 3 * ROW_GROUP_B
    dr0 = NB_ROWS - 1 - ROW_GROUP_B
    for rq in range(ROW_GROUP_B):
        for p in range(n_kr // 2):
            o_ref[0, rq * GRID_W:(rq + 1) * GRID_W, p * LANES:(p + 1) * LANES] = pair[2 * p - rq + dr0]


def _rpb_call(rpb):
    nh = rpb.shape[0]
    nq = ROW_GROUP_B * GRID_W
    nk = 3 * ROW_GROUP_B * GRID_W
    assert 2 * ROW_GROUP_B == NB_ROWS and 2 * GRID_W == LANES
    n_dr, n_dc = rpb.shape[1], rpb.shape[2]
    rows = -(-n_dr // 8) * 8
    rpb_tiles = jnp.pad(rpb, ((0, 0), (0, rows - n_dr), (0, LANES - n_dc)))
    return pl.pallas_call(
        _rpb_kernel,
        grid=(nh,),
        in_specs=[pl.BlockSpec((1, rows, LANES), lambda h: (h, 0, 0))],
        out_specs=pl.BlockSpec((1, nq, nk), lambda h: (h, 0, 0)),
        out_shape=jax.ShapeDtypeStruct((nh, nq, nk), F32),
        compiler_params=pltpu.CompilerParams(dimension_semantics=("arbitrary",)),
        name="rpb_table",
    )(rpb_tiles)


def _rope(t, cos, sin, first_half):
    partner = jnp.where(first_half, pltpu.roll(t, LANES - 32, 1), pltpu.roll(t, 32, 1))
    return t * cos + partner * sin


def _row_chunked_matmul(h_ref, w_ref, epilogue, between=None):
    n_chunks = h_ref.shape[0] // ROW_CHUNK

    def dots(c):
        h = h_ref[c * ROW_CHUNK:(c + 1) * ROW_CHUNK]
        if isinstance(w_ref, tuple):
            return tuple(jnp.dot(h, w[...], preferred_element_type=F32) for w in w_ref)
        return jnp.dot(h, w_ref[...], preferred_element_type=F32)

    acc = dots(0)
    for c in range(n_chunks):
        nxt = dots(c + 1) if c + 1 < n_chunks else None
        epilogue(slice(c * ROW_CHUNK, (c + 1) * ROW_CHUNK), acc)
        if c == 0 and between is not None:
            between()
        acc = nxt


def _gate_kernel(x_ref, sc_ref, sh_ref, g_ref, w0_ref, w1_ref, o_ref, hout_ref, hs_ref, *, n_rows):
    i = pl.program_id(0)
    j = pl.program_id(1)
    pr = x_ref.shape[0]
    tn = w0_ref.shape[1]

    def norm_piece():
        h = (_rms(x_ref[...], g_ref[...]) * (1 + sc_ref[0]) + sh_ref[0]).astype(BF16)
        hout_ref[...] = h
        hs_ref[i % 2, pl.ds(pl.multiple_of(j * pr, pr), pr), :] = h

    def epilogue(rows, accs):
        for k, acc in enumerate(accs):
            o_ref[rows, k * tn:(k + 1) * tn] = jax.nn.sigmoid(acc).astype(BF16)

    @pl.when(i == 0)
    def _():
        norm_piece()

    @pl.when((i > 0) & (i < n_rows))
    def _():
        _row_chunked_matmul(hs_ref.at[(i + 1) % 2], (w0_ref, w1_ref), epilogue, between=norm_piece)

    @pl.when(i == n_rows)
    def _():
        _row_chunked_matmul(hs_ref.at[(i + 1) % 2], (w0_ref, w1_ref), epilogue)


def _gate_call(x2, sc, sh, g, w, *, tm, tn, col0, n_cols, rows_per_mod):
    n, d = x2.shape
    n_rows = n // tm
    nj = n_cols // tn
    pr = tm // nj
    assert pr * nj == tm and pr % BF16_ROWS == 0 and tn == 2 * TN and col0 % TN == 0
    mod_tiles = rows_per_mod // tm
    last = n_rows * nj - 1
    piece_map = lambda i, j: (jnp.minimum(i * nj + j, last), 0)
    mod_map = lambda i, j: (jnp.minimum(i, n_rows - 1) // mod_tiles, 0, 0)
    return pl.pallas_call(
        functools.partial(_gate_kernel, n_rows=n_rows),
        grid=(n_rows + 1, nj),
        in_specs=[pl.BlockSpec((pr, d), piece_map),
                  pl.BlockSpec((1, 1, d), mod_map),
                  pl.BlockSpec((1, 1, d), mod_map),
                  pl.BlockSpec((1, d), lambda i, j: (0, 0)),
                  pl.BlockSpec((d, TN), lambda i, j: (0, col0 // TN + 2 * j)),
                  pl.BlockSpec((d, TN), lambda i, j: (0, col0 // TN + 2 * j + 1))],
        out_specs=[pl.BlockSpec((tm, tn), lambda i, j: (jnp.maximum(i - 1, 0), jnp.where(i > 0, j, 0))),
                   pl.BlockSpec((pr, d), piece_map)],
        out_shape=[jax.ShapeDtypeStruct((n, n_cols), BF16), jax.ShapeDtypeStruct((n, d), BF16)],
        scratch_shapes=[pltpu.VMEM((2, tm, d), BF16)],
        compiler_params=pltpu.CompilerParams(dimension_semantics=("arbitrary", "arbitrary"),
                                             vmem_limit_bytes=VMEM_LIMIT_BYTES),
        name="gate_proj",
    )(x2, sc, sh, g, w, w)


def _qkv_weight_tile(jc):
    n_qa = W_QA // TN
    last = (W_QA + 3 * W_B) // TN
    return jnp.where(jc < n_qa, jc, jnp.where(jc == last, n_qa, jc + 1))


def _qkv_kernel(h_ref, w_ref, cos_ref, sin_ref, o_ref, *, j0):
    _proj_tile(h_ref, w_ref, cos_ref, sin_ref, o_ref, pl.program_id(1) + j0, 0)


def _qkv_call(h, w, cos, sin, *, tm, j0, nj):
    n, d = h.shape
    pos_tiles = cos.shape[0] // tm
    return pl.pallas_call(
        functools.partial(_qkv_kernel, j0=j0),
        grid=(n // tm, nj),
        in_specs=[pl.BlockSpec((tm, d), lambda i, j: (i, 0)),
                  pl.BlockSpec((d, TN), lambda i, j: (0, _qkv_weight_tile(j + j0))),
                  pl.BlockSpec((tm, HEAD_DIM), lambda i, j: (i % pos_tiles, 0)),
                  pl.BlockSpec((tm, HEAD_DIM), lambda i, j: (i % pos_tiles, 0))],
        out_specs=pl.BlockSpec((tm, TN), lambda i, j: (i, j)),
        out_shape=jax.ShapeDtypeStruct((n, nj * TN), BF16),
        compiler_params=pltpu.CompilerParams(dimension_semantics=("arbitrary", "arbitrary"),
                                             vmem_limit_bytes=VMEM_LIMIT_BYTES),
        name="qkv_proj",
    )(h, w, cos, sin)


def _inproj_kernel(x_ref, sc_ref, sh_ref, g_ref, w_ref, cos_ref, sin_ref, o_ref, h_ref, *, j0):
    j = pl.program_id(1)

    @pl.when(j == 0)
    def _():
        h = _rms(x_ref[...], g_ref[...]) * (1 + sc_ref[0]) + sh_ref[0]
        h_ref[...] = h.astype(BF16)

    _proj_tile(h_ref, w_ref, cos_ref, sin_ref, o_ref, j + j0, 0)


def _proj_tile(h_ref, w_ref, cos_ref, sin_ref, o_ref, jj, n_gate):
    lane = lax.broadcasted_iota(jnp.int32, (1, HEAD_DIM), 1)
    first_half = (lane & 63) < 32

    def row_chunks(epilogue):
        _row_chunked_matmul(h_ref, w_ref, epilogue)

    def rope_heads(rows, acc, n_heads, scale):
        cos = cos_ref[rows]
        sin = sin_ref[rows]
        for hd in range(n_heads):
            sl = slice(hd * HEAD_DIM, (hd + 1) * HEAD_DIM)
            r = _rope(acc[:, sl], cos, sin, first_half)
            if scale is not None:
                r = r * scale
            o_ref[rows, sl] = r.astype(BF16)

    @pl.when((jj >= n_gate) & (jj < n_gate + 2))
    def _():
        row_chunks(lambda rows, acc: rope_heads(rows, acc, TN // HEAD_DIM, Q_SCALE))

    @pl.when((jj >= n_gate + 2) & (jj < n_gate + 4))
    def _():
        def epilogue(rows, acc):
            o_ref[rows] = (acc * Q_SCALE).astype(BF16)
        row_chunks(epilogue)

    @pl.when((jj >= n_gate + 4) & (jj < n_gate + 8))
    def _():
        def epilogue(rows, acc):
            o_ref[rows] = acc.astype(BF16)
        row_chunks(epilogue)

    @pl.when(jj == n_gate + 8)
    def _():
        def epilogue(rows, acc):
            rope_heads(rows, acc, N_KV_A, None)
            o_ref[rows, W_KA:] = acc[:, W_KA:].astype(BF16)
        row_chunks(epilogue)


def _inproj_call(x2, sc, sh, g, w, cos, sin, *, tm, rows_per_mod, j0, nj):
    n, d = x2.shape
    pos_tiles = cos.shape[0] // tm
    mod_tiles = rows_per_mod // tm
    return pl.pallas_call(
        functools.partial(_inproj_kernel, j0=j0),
        grid=(n // tm, nj),
        in_specs=[pl.BlockSpec((tm, d), lambda i, j: (i, 0)),
                  pl.BlockSpec((1, 1, d), lambda i, j: (i // mod_tiles, 0, 0)),
                  pl.BlockSpec((1, 1, d), lambda i, j: (i // mod_tiles, 0, 0)),
                  pl.BlockSpec((1, d), lambda i, j: (0, 0)),
                  pl.BlockSpec((d, TN), lambda i, j: (0, _qkv_weight_tile(j + j0))),
                  pl.BlockSpec((tm, HEAD_DIM), lambda i, j: (i % pos_tiles, 0)),
                  pl.BlockSpec((tm, HEAD_DIM), lambda i, j: (i % pos_tiles, 0))],
        out_specs=pl.BlockSpec((tm, TN), lambda i, j: (i, j)),
        out_shape=jax.ShapeDtypeStruct((n, nj * TN), BF16),
        scratch_shapes=[pltpu.VMEM((tm, d), BF16)],
        compiler_params=pltpu.CompilerParams(dimension_semantics=("arbitrary", "arbitrary"),
                                             vmem_limit_bytes=VMEM_LIMIT_BYTES),
        name="in_proj",
    )(x2, sc, sh, g, w, cos, sin)


def _attn_a_kernel(sink_ref, q_ref, kp_ref, km_ref, kn_ref, vp_ref, vm_ref, vn_ref, kc_ref, vc_ref, o_ref,
                   *, n_blocks):
    nq = pl.program_id(1)
    sub = QB_A // BLOCK
    rows = GQA_GROUP * BLOCK
    n_loc = 3 * BLOCK
    r_io = lax.broadcasted_iota(jnp.int32, (rows, n_loc), 0)
    c_io = lax.broadcasted_iota(jnp.int32, (rows, n_loc), 1)
    qi_io = r_io & (BLOCK - 1)
    in_band = (c_io >= qi_io) & (c_io <= qi_io + 2 * WINDOW)
    c_row = lax.broadcasted_iota(jnp.int32, (1, n_loc), 1)
    g_io = lax.shift_right_logical(lax.broadcasted_iota(jnp.int32, (rows, 1), 0), BLOCK.bit_length() - 1)
    nt = (((1,), (1,)), ((), ()))
    for kvh in range(N_KV_A):
        hs = slice(kvh * HEAD_DIM, (kvh + 1) * HEAD_DIM)
        kband = jnp.concatenate([kp_ref[:, hs], km_ref[:, hs], kn_ref[:, hs]], axis=0)
        vband = jnp.concatenate([vp_ref[:, hs], vm_ref[:, hs], vn_ref[:, hs]], axis=0)
        kc = kc_ref[:, hs]
        vc = vc_ref[:, hs]
        sink = jnp.zeros((rows, 1), F32)
        for g in range(GQA_GROUP):
            sink = jnp.where(g_io == g, sink_ref[kvh * GQA_GROUP + g] * LOG2E, sink)
        for qi in range(sub):
            blk = nq * sub + qi
            lo = jnp.where(blk == 0, BLOCK, 0)
            hi = jnp.where(blk == n_blocks - 1, 2 * BLOCK - 1, n_loc - 1)
            valid = in_band & ((c_row >= lo) & (c_row <= hi))
            q4 = jnp.concatenate(
                [q_ref[qi * BLOCK:(qi + 1) * BLOCK, (kvh * GQA_GROUP + g) * HEAD_DIM:(kvh * GQA_GROUP + g + 1) * HEAD_DIM]
                 for g in range(GQA_GROUP)], axis=0)
            s_loc = lax.dot_general(q4, kband[qi * BLOCK:qi * BLOCK + n_loc], nt, preferred_element_type=F32)
            s_loc = jnp.where(valid, s_loc, NEG_INF)
            s_ctx = lax.dot_general(q4, kc, nt, preferred_element_type=F32)
            m = jnp.maximum(jnp.maximum(jnp.max(s_loc, axis=-1, keepdims=True),
                                        jnp.max(s_ctx, axis=-1, keepdims=True)), sink)
            e_loc = jnp.exp2(s_loc - m)
            e_ctx = jnp.exp2(s_ctx - m)
            den = (jnp.sum(e_loc, axis=-1, keepdims=True) + jnp.sum(e_ctx, axis=-1, keepdims=True)
                   + jnp.exp2(sink - m))
            o = (jnp.dot(e_loc.astype(BF16), vband[qi * BLOCK:qi * BLOCK + n_loc], preferred_element_type=F32)
                 + jnp.dot(e_ctx.astype(BF16), vc, preferred_element_type=F32)) * (1.0 / den)
            for g in range(GQA_GROUP):
                hd = kvh * GQA_GROUP + g
                o_ref[qi * BLOCK:(qi + 1) * BLOCK, hd * HEAD_DIM:(hd + 1) * HEAD_DIM] = (
                    o[g * BLOCK:(g + 1) * BLOCK].astype(BF16))


def _attn_a_call(proj, cproj, sink, *, batch, seq, ctx_len, q_col, k_col, v_col, ck_col, cv_col):
    nqb = seq // QB_A
    sub = QB_A // BLOCK
    n_blocks = seq // BLOCK

    def prev_map(b, n, col):
        return (b * n_blocks + jnp.maximum(n * sub - 1, 0), col // W_KA)

    def next_map(b, n, col):
        return (b * n_blocks + jnp.minimum(n * sub + sub, n_blocks - 1), col // W_KA)

    def main_map(b, n, col):
        return (b * nqb + n, col // W_KA)

    edge = lambda f, col: pl.BlockSpec((BLOCK, W_KA), functools.partial(f, col=col))
    main = lambda col: pl.BlockSpec((QB_A, W_KA), functools.partial(main_map, col=col))
    ctxs = lambda col: pl.BlockSpec((ctx_len, W_KA), lambda b, n: (b, col // W_KA))
    return pl.pallas_call(
        functools.partial(_attn_a_kernel, n_blocks=n_blocks),
        grid=(batch, nqb),
        in_specs=[pl.BlockSpec(memory_space=pltpu.SMEM),
                  pl.BlockSpec((QB_A, W_QA), lambda b, n: (b * nqb + n, q_col // W_QA)),
                  edge(prev_map, k_col), main(k_col), edge(next_map, k_col),
                  edge(prev_map, v_col), main(v_col), edge(next_map, v_col),
                  ctxs(ck_col), ctxs(cv_col)],
        out_specs=pl.BlockSpec((QB_A, W_QA), lambda b, n: (b * nqb + n, 0)),
        out_shape=jax.ShapeDtypeStruct((batch * seq, W_QA), BF16),
        compiler_params=pltpu.CompilerParams(dimension_semantics=("arbitrary",) * 2,
                                             vmem_limit_bytes=VMEM_LIMIT_BYTES),
        name="attn_a",
    )(sink, proj, proj, proj, proj, proj, proj, proj, cproj, cproj)


def _attn_b_kernel(q_ref, kp_ref, km_ref, kn_ref, vp_ref, vm_ref, vn_ref, kc_ref, vc_ref, bias_ref, o_ref,
                   *, grid_rows):
    n = pl.program_id(1)
    nq = ROW_GROUP_B * GRID_W
    nk = 3 * nq
    heads = q_ref.shape[1] // HEAD_DIM
    r_io = lax.broadcasted_iota(jnp.int32, (nq, nk), 0)
    c_io = lax.broadcasted_iota(jnp.int32, (nq, nk), 1)
    log_w = GRID_W.bit_length() - 1
    q_col = r_io & (GRID_W - 1)
    k_col = c_io & (GRID_W - 1)
    c_start = jnp.clip(q_col - NB_COLS // 2, 0, GRID_W - NB_COLS)
    col_ok = (k_col >= c_start) & (k_col < c_start + NB_COLS)
    nt = (((1,), (1,)), ((), ()))
    for grp in range(GROUPS_PER_STEP_B):
        gi = n * GROUPS_PER_STEP_B + grp
        r_abs = gi * ROW_GROUP_B + lax.shift_right_logical(r_io, log_w)
        kr_abs = (gi - 1) * ROW_GROUP_B + lax.shift_right_logical(c_io, log_w)
        r_start = jnp.clip(r_abs - NB_ROWS // 2, 0, grid_rows - NB_ROWS)
        valid = (kr_abs >= r_start) & (kr_abs < r_start + NB_ROWS) & col_ok
        cap = jnp.where(valid, F32_MAX, NEG_INF)
        rows = slice(grp * nq, (grp + 1) * nq)
        for hd in range(heads):
            sl = slice(hd * HEAD_DIM, (hd + 1) * HEAD_DIM)
            q = q_ref[rows, sl]
            keys = jnp.concatenate([kp_ref[:, sl], km_ref[:, sl], kn_ref[:, sl]], axis=0)[grp * nq:grp * nq + nk]
            vals = jnp.concatenate([vp_ref[:, sl], vm_ref[:, sl], vn_ref[:, sl]], axis=0)[grp * nq:grp * nq + nk]
            s_nb = lax.dot_general(q, keys, nt, preferred_element_type=F32)
            s_nb = jnp.minimum(s_nb + bias_ref[hd], cap)
            s_ctx = lax.dot_general(q, kc_ref[:, sl], nt, preferred_element_type=F32)
            m = jnp.maximum(jnp.max(s_nb, axis=-1, keepdims=True), jnp.max(s_ctx, axis=-1, keepdims=True))
            e_nb = jnp.exp2(s_nb - m)
            e_ctx = jnp.exp2(s_ctx - m)
            den = jnp.sum(e_nb, axis=-1, keepdims=True) + jnp.sum(e_ctx, axis=-1, keepdims=True)
            o = (jnp.dot(e_nb.astype(BF16), vals, preferred_element_type=F32)
                 + jnp.dot(e_ctx.astype(BF16), vc_ref[:, sl], preferred_element_type=F32)) * (1.0 / den)
            o_ref[rows, sl] = o.astype(BF16)


def _attn_b_call(proj, cproj, bias, *, batch, seq, ctx_len, q_col, k_col, v_col, ck_col, cv_col):
    nq = ROW_GROUP_B * GRID_W
    ng = seq // nq
    gps = GROUPS_PER_STEP_B
    ns = ng // gps
    hw = W_B

    def edge(col, first):
        def imap(b, n):
            g = jnp.maximum(n * gps - 1, 0) if first else jnp.minimum(n * gps + gps, ng - 1)
            return (b * ng + g, col // hw)
        return pl.BlockSpec((nq, hw), imap)

    main = lambda col: pl.BlockSpec((gps * nq, hw), lambda b, n: (b * ns + n, col // hw))
    ctxs = lambda col: pl.BlockSpec((ctx_len, hw), lambda b, n: (b, col // hw))
    return pl.pallas_call(
        functools.partial(_attn_b_kernel, grid_rows=seq // GRID_W),
        grid=(batch, ns),
        in_specs=[main(q_col),
                  edge(k_col, True), main(k_col), edge(k_col, False),
                  edge(v_col, True), main(v_col), edge(v_col, False),
                  ctxs(ck_col), ctxs(cv_col),
                  pl.BlockSpec((hw // HEAD_DIM, nq, bias.shape[2]), lambda b, n: (0, 0, 0))],
        out_specs=pl.BlockSpec((gps * nq, hw), lambda b, n: (b * ns + n, 0)),
        out_shape=jax.ShapeDtypeStruct((batch * seq, W_B), BF16),
        compiler_params=pltpu.CompilerParams(dimension_semantics=("arbitrary",) * 2,
                                             vmem_limit_bytes=VMEM_LIMIT_BYTES),
        name="attn_b",
    )(proj, proj, proj, proj, proj, proj, proj, cproj, cproj, bias)


def _resident_bf16(w_ref, wb_ref):
    @pl.when(pl.program_id(0) == 0)
    def _():
        wb_ref[...] = w_ref[...].astype(BF16)


def _mix_kernel(oa_ref, ob_ref, ga_ref, gb_ref, wa_ref, wb_ref, z_ref, wa_bf, wb_bf):
    _resident_bf16(wa_ref, wa_bf)
    _resident_bf16(wb_ref, wb_bf)
    ya = jnp.dot(oa_ref[...], wa_bf[...], preferred_element_type=F32)
    yb = jnp.dot(ob_ref[...], wb_bf[...], preferred_element_type=F32)
    z_ref[...] = (ga_ref[...].astype(F32) * ya + gb_ref[...].astype(F32) * yb).astype(BF16)


def _mix_call(oa, ob, proj, wa, wb, *, tm):
    n = oa.shape[0]
    d = wa.shape[1]
    const = lambda shape: pl.BlockSpec(shape, lambda i: (0, 0), pipeline_mode=pl.Buffered(1))
    return pl.pallas_call(
        _mix_kernel,
        grid=(n // tm,),
        in_specs=[pl.BlockSpec((tm, oa.shape[1]), lambda i: (i, 0)),
                  pl.BlockSpec((tm, ob.shape[1]), lambda i: (i, 0)),
                  pl.BlockSpec((tm, d), lambda i: (i, 0)),
                  pl.BlockSpec((tm, d), lambda i: (i, 1)),
                  const(wa.shape), const(wb.shape)],
        out_specs=pl.BlockSpec((tm, d), lambda i: (i, 0)),
        out_shape=jax.ShapeDtypeStruct((n, d), BF16),
        scratch_shapes=[pltpu.VMEM(wa.shape, BF16), pltpu.VMEM(wb.shape, BF16)],
        compiler_params=pltpu.CompilerParams(dimension_semantics=("arbitrary",),
                                             vmem_limit_bytes=VMEM_LIMIT_BYTES),
        name="branch_mix",
    )(oa, ob, proj, proj, wa, wb)


def _outnorm_kernel(a_ref, w_ref, x_ref, gt_ref, g_ref, o_ref, w_bf):
    _resident_bf16(w_ref, w_bf)
    y = jnp.dot(a_ref[...], w_bf[...], preferred_element_type=F32)
    o_ref[...] = x_ref[...] + gt_ref[0] * _rms(y, g_ref[...])


def _outnorm_call(a, w, x2, gt, g, *, tm, rows_per_mod):
    n, kdim = a.shape
    d = w.shape[1]
    mod_tiles = rows_per_mod // tm
    return pl.pallas_call(
        _outnorm_kernel,
        grid=(n // tm,),
        in_specs=[pl.BlockSpec((tm, kdim), lambda i: (i, 0)),
                  pl.BlockSpec((kdim, d), lambda i: (0, 0), pipeline_mode=pl.Buffered(1)),
                  pl.BlockSpec((tm, d), lambda i: (i, 0)),
                  pl.BlockSpec((1, 1, d), lambda i: (i // mod_tiles, 0, 0)),
                  pl.BlockSpec((1, d), lambda i: (0, 0))],
        out_specs=pl.BlockSpec((tm, d), lambda i: (i, 0)),
        out_shape=jax.ShapeDtypeStruct((n, d), F32),
        scratch_shapes=[pltpu.VMEM(w.shape, BF16)],
        compiler_params=pltpu.CompilerParams(dimension_semantics=("arbitrary",),
                                             vmem_limit_bytes=VMEM_LIMIT_BYTES),
        name="out_norm",
    )(a, w, x2, gt, g)


def _rowmm_norm_kernel(a_ref, w_ref, x_ref, gt_ref, g_ref, o_ref, *, n_chunks):
    cm = a_ref.shape[0] // n_chunks
    y = jnp.dot(a_ref[0:cm], w_ref[...], preferred_element_type=F32)
    for c in range(n_chunks):
        nxt = None
        if c + 1 < n_chunks:
            nxt = jnp.dot(a_ref[(c + 1) * cm:(c + 2) * cm], w_ref[...], preferred_element_type=F32)
        rows = slice(c * cm, (c + 1) * cm)
        o_ref[rows] = x_ref[rows] + gt_ref[0] * _rms(y, g_ref[...])
        y = nxt


def _rowmm_norm_call(a, w, x2, gt, g, *, tm, n_chunks, rows_per_mod):
    n, kdim = a.shape
    d = w.shape[1]
    mod_tiles = rows_per_mod // tm
    return pl.pallas_call(
        functools.partial(_rowmm_norm_kernel, n_chunks=n_chunks),
        grid=(n // tm,),
        in_specs=[pl.BlockSpec((tm, kdim), lambda i: (i, 0)),
                  pl.BlockSpec((kdim, d), lambda i: (0, 0), pipeline_mode=pl.Buffered(1)),
                  pl.BlockSpec((tm, d), lambda i: (i, 0)),
                  pl.BlockSpec((1, 1, d), lambda i: (i // mod_tiles, 0, 0)),
                  pl.BlockSpec((1, d), lambda i: (0, 0))],
        out_specs=pl.BlockSpec((tm, d), lambda i: (i, 0)),
        out_shape=jax.ShapeDtypeStruct((n, d), F32),
        compiler_params=pltpu.CompilerParams(dimension_semantics=("arbitrary",),
                                             vmem_limit_bytes=VMEM_LIMIT_RESIDENT_BYTES),
        name="rowmm_norm",
    )(a, w, x2, gt, g)


def _ffn_up_kernel(x0p_ref, x0_ref, x0n_ref, sc0_ref, sh0_ref, xp_ref, xm_ref, xn_ref, sc_ref, sh_ref, g_ref,
                   wa_ref, wg_ref, cwa_ref, cwg_ref, cba_ref, cbg_ref,
                   o_ref, h_cur, h_nxt, ua_ref, ug_ref, *, nj, n_tiles, tiles_per_seq, n_pieces):
    t = pl.program_id(0)
    tm = x0_ref.shape[0]
    halo = xp_ref.shape[0]
    pr = xm_ref.shape[0]
    ext = tm + 2 * halo
    margin = 8

    def mod(x, sc, sh):
        return (_rms(x, g_ref[...]) * (1 + sc[0]) + sh[0]).astype(BF16)

    @pl.when(t == 0)
    def _():
        ua_ref[...] = jnp.zeros_like(ua_ref)
        ug_ref[...] = jnp.zeros_like(ug_ref)
        h_cur[0:halo] = mod(x0p_ref[...], sc0_ref, sh0_ref)
        h_cur[halo:halo + tm] = mod(x0_ref[...], sc0_ref, sh0_ref)
        h_cur[halo + tm:ext] = mod(x0n_ref[...], sc0_ref, sh0_ref)

    tc = jnp.minimum(t, n_tiles - 1)

    @pl.when((t > 0) & (t < n_tiles) & (t % nj == 0))
    def _():
        h_cur[...] = h_nxt[...]

    def next_row_piece():
        p = jnp.minimum(tc % nj, n_pieces - 1)
        h_nxt[0:halo] = mod(xp_ref[...], sc_ref, sh_ref)
        h_nxt[pl.ds(pl.multiple_of(halo + p * pr, BF16_ROWS), pr), :] = mod(xm_ref[...], sc_ref, sh_ref)
        h_nxt[halo + tm:ext] = mod(xn_ref[...], sc_ref, sh_ref)

    slot = t % 2
    prev = 1 - slot
    ip = jnp.maximum(t - 1, 0) // nj
    first_row = jnp.where((ip % tiles_per_seq) == 0, 0, -1)
    last_row = jnp.where((ip % tiles_per_seq) == tiles_per_seq - 1, tm - 1, -1)
    row = lax.broadcasted_iota(jnp.int32, (tm, 1), 0)

    def epilogue(lo, hi):
        cm = hi - lo
        rowc = row[lo:hi]

        def conv(u_ref, cw_ref, cb_ref):
            win = u_ref[prev, pl.ds(halo + lo - margin, cm + 2 * margin), :]
            up = pltpu.roll(win, 1, 0)[margin:margin + cm]
            un = pltpu.roll(win, cm + 2 * margin - 1, 0)[margin:margin + cm]
            up = jnp.where(rowc == first_row, 0.0, up)
            un = jnp.where(rowc == last_row, 0.0, un)
            out = cb_ref[...] + up * cw_ref[0:1]
            out = out + win[margin:margin + cm] * cw_ref[1:2]
            return out + un * cw_ref[2:3]

        a = conv(ua_ref, cwa_ref, cba_ref)
        gte = conv(ug_ref, cwg_ref, cbg_ref)
        o_ref[lo:hi] = (jax.nn.silu(gte) * a).astype(BF16)

    half = ext // 2
    third = (tm // 3) // BF16_ROWS * BF16_ROWS
    chunks = [(0, third), (third, 2 * third), (2 * third, tm)]
    ua_ref[slot, 0:half] = jnp.dot(h_cur[0:half], wa_ref[...], preferred_element_type=F32)
    next_row_piece()
    epilogue(*chunks[0])
    ua_ref[slot, half:ext] = jnp.dot(h_cur[half:ext], wa_ref[...], preferred_element_type=F32)
    epilogue(*chunks[1])
    ug_ref[slot, 0:half] = jnp.dot(h_cur[0:half], wg_ref[...], preferred_element_type=F32)
    epilogue(*chunks[2])
    ug_ref[slot, half:ext] = jnp.dot(h_cur[half:ext], wg_ref[...], preferred_element_type=F32)


def _ffn_up_call(x2, sc, sh, g, w_up, conv_w, conv_b, *, tm, tn, seq):
    n, d = x2.shape
    dff = w_up.shape[1] // 2
    nj = dff // tn
    halo = BF16_ROWS
    hb = tm // halo
    n_halo = n // halo
    tiles_per_seq = seq // tm
    n_rows = n // tm
    n_tiles = n_rows * nj
    pr = 128
    n_pieces = tm // pr
    assert (tm + 2 * halo) % (2 * BF16_ROWS) == 0 and n_pieces <= nj and n_pieces * pr == tm

    def cur(t):
        tc = jnp.minimum(t, n_tiles - 1)
        return tc // nj, tc % nj

    def prv(t):
        tp = jnp.maximum(t - 1, 0)
        return tp // nj, tp % nj

    def nxt(t):
        return jnp.minimum(cur(t)[0] + 1, n_rows - 1)

    once = lambda shape, imap: pl.BlockSpec(shape, imap, pipeline_mode=pl.Buffered(1))
    return pl.pallas_call(
        functools.partial(_ffn_up_kernel, nj=nj, n_tiles=n_tiles, tiles_per_seq=tiles_per_seq, n_pieces=n_pieces),
        grid=(n_tiles + 1,),
        in_specs=[once((halo, d), lambda t: (0, 0)),
                  once((tm, d), lambda t: (0, 0)),
                  once((halo, d), lambda t: (jnp.minimum(hb, n_halo - 1), 0)),
                  once((1, 1, d), lambda t: (0, 0, 0)),
                  once((1, 1, d), lambda t: (0, 0, 0)),
                  pl.BlockSpec((halo, d), lambda t: (jnp.maximum(nxt(t) * hb - 1, 0), 0)),
                  pl.BlockSpec((pr, d), lambda t: (nxt(t) * n_pieces + jnp.minimum(cur(t)[1], n_pieces - 1), 0)),
                  pl.BlockSpec((halo, d), lambda t: (jnp.minimum((nxt(t) + 1) * hb, n_halo - 1), 0)),
                  pl.BlockSpec((1, 1, d), lambda t: (nxt(t) // tiles_per_seq, 0, 0)),
                  pl.BlockSpec((1, 1, d), lambda t: (nxt(t) // tiles_per_seq, 0, 0)),
                  pl.BlockSpec((1, d), lambda t: (0, 0)),
                  pl.BlockSpec((d, tn), lambda t: (0, cur(t)[1])),
                  pl.BlockSpec((d, tn), lambda t: (0, cur(t)[1] + nj)),
                  pl.BlockSpec((CONV_W, tn), lambda t: (0, prv(t)[1])),
                  pl.BlockSpec((CONV_W, tn), lambda t: (0, prv(t)[1] + nj)),
                  pl.BlockSpec((1, tn), lambda t: (0, prv(t)[1])),
                  pl.BlockSpec((1, tn), lambda t: (0, prv(t)[1] + nj))],
        out_specs=pl.BlockSpec((tm, tn), lambda t: prv(t)),
        out_shape=jax.ShapeDtypeStruct((n, dff), BF16),
        scratch_shapes=[pltpu.VMEM((tm + 2 * halo, d), BF16),
                        pltpu.VMEM((tm + 2 * halo, d), BF16),
                        pltpu.VMEM((2, tm + 2 * halo, tn), F32),
                        pltpu.VMEM((2, tm + 2 * halo, tn), F32)],
        compiler_params=pltpu.CompilerParams(dimension_semantics=("arbitrary",),
                                             vmem_limit_bytes=VMEM_LIMIT_BYTES),
        name="ffn_up",
    )(x2, x2, x2, sc, sh, x2, x2, x2, sc, sh, g, w_up, w_up, conv_w, conv_w, conv_b, conv_b)


def _rope_tables(seq):
    t = np.arange(seq)
    half = HEAD_DIM // 4
    inv = np.float32(ROPE_BASE) ** (-np.arange(half, dtype=np.float32) / np.float32(half))

    def cs(pos):
        ang = pos.astype(np.float32)[:, None] * inv[None, :]
        c, s = np.cos(ang), np.sin(ang)
        return np.concatenate([c, c], axis=1), np.concatenate([-s, s], axis=1)

    cr, sr = cs(t // GRID_W)
    cc, sn = cs(t % GRID_W)
    return (jnp.asarray(np.concatenate([cr, cc], axis=1), F32), jnp.asarray(np.concatenate([sr, sn], axis=1), F32))


def _pick_tile(n, pref):
    while n % pref:
        pref //= 2
    return pref


class _RowTiles(NamedTuple):
    proj: int
    qkv: int
    mix: int
    down: int


def _row_tiles(seq):
    return _RowTiles(*(_pick_tile(seq, pref) for pref in (1024, 2048, 512, 256)))


def kernel(x, c, ctx, c_ctx, w_mod, b_mod, g_attn_pre, g_attn_post, g_ffn_pre, g_ffn_post, w_in, sink_a, rpb_b,
           w_br_a, w_br_b, w_o, w_up, conv_w, conv_b, w_down):
    batch, seq, d = x.shape
    ctx_len = ctx.shape[1]
    assert w_mod.shape[0] == 1 and d == 16 * HEAD_DIM
    assert seq % QB_A == 0 and seq % (GROUPS_PER_STEP_B * ROW_GROUP_B * GRID_W) == 0 and WINDOW == BLOCK
    n = batch * seq
    x2 = x.reshape(n, d)

    pad = (-(batch + 1)) % 8
    cs = jnp.concatenate([c, c_ctx[None, :], jnp.zeros((pad, d), F32)], axis=0)
    mod = _mod_call(cs, w_mod[0], b_mod[0][None, :])
    sh1, sc1, gt1, sh2, sc2, gt2 = [mod[:batch, k * d:(k + 1) * d].reshape(batch, 1, d) for k in range(6)]
    csh1 = mod[batch:batch + 1, 0:d].reshape(1, 1, d)
    csc1 = mod[batch:batch + 1, d:2 * d].reshape(1, 1, d)

    n_qkv_cols = W_QA + 2 * W_KA + 3 * W_B
    n_gate_cols = 2 * d
    w_in_bf = w_in[0].astype(BF16)
    n_tiles = n_qkv_cols // TN
    c_qa = 0
    c_qb = c_qa + W_QA
    c_kb = c_qb + W_B
    c_vb = c_kb + W_B
    c_ka = c_vb + W_B
    c_va = c_ka + W_KA
    cos, sin = _rope_tables(seq)
    tiles = _row_tiles(seq)
    gates, h1 = _gate_call(x2, sc1, sh1, g_attn_pre, w_in_bf, tm=tiles.proj, tn=TN_GATE, col0=n_qkv_cols,
                           n_cols=n_gate_cols, rows_per_mod=seq)
    proj = _qkv_call(h1, w_in_bf, cos, sin, tm=tiles.qkv, j0=0, nj=n_tiles)
    nc = batch * ctx_len
    j0c = c_kb // TN
    cproj = _inproj_call(ctx.reshape(nc, d), csc1, csh1, g_attn_pre, w_in_bf,
                         jnp.ones((nc, HEAD_DIM), F32), jnp.zeros((nc, HEAD_DIM), F32),
                         tm=nc, rows_per_mod=nc, j0=j0c, nj=n_tiles - j0c)
    cc_kb, cc_vb, cc_ka, cc_va = 0, W_B, 2 * W_B, 2 * W_B + W_KA

    oa = _attn_a_call(proj, cproj, sink_a[0], batch=batch, seq=seq, ctx_len=ctx_len,
                      q_col=c_qa, k_col=c_ka, v_col=c_va, ck_col=cc_ka, cv_col=cc_va)
    bias = _rpb_call(rpb_b[0])
    ob = _attn_b_call(proj, cproj, bias, batch=batch, seq=seq, ctx_len=ctx_len,
                      q_col=c_qb, k_col=c_kb, v_col=c_vb, ck_col=cc_kb, cv_col=cc_vb)

    z = _mix_call(oa, ob, gates, w_br_a[0], w_br_b[0], tm=tiles.mix)
    x1 = _outnorm_call(z, w_o[0], x2, gt1, g_attn_post, tm=tiles.mix, rows_per_mod=seq)

    act = _ffn_up_call(x1, sc2, sh2, g_ffn_pre, w_up[0].astype(BF16), conv_w[0], conv_b[0][None, :],
                       tm=tiles.proj, tn=TN, seq=seq)
    out = _rowmm_norm_call(act, w_down[0].astype(BF16), x1, gt2, g_ffn_post, tm=tiles.down, n_chunks=2,
                           rows_per_mod=seq)
    return out.reshape(batch, seq, d)
```

```python
import functools
from typing import NamedTuple

import numpy as np
import jax
import jax.numpy as jnp
from jax import lax
from jax.experimental import pallas as pl
from jax.experimental.pallas import tpu as pltpu

F32 = jnp.float32
BF16 = jnp.bfloat16

GRID_W = 64
HEAD_DIM = 128
N_HEADS_A = 8
N_KV_A = 2
GQA_GROUP = N_HEADS_A // N_KV_A
WINDOW = 128
BLOCK = 128
N_HEADS_B = 8
NB_ROWS = 8
NB_COLS = 16
CONV_W = 3
ROPE_BASE = 10000.0
EPS = 1e-6
NEG_INF = -1e30
ATTN_SCALE = HEAD_DIM ** -0.5
LOG2E = 1.4426950408889634
Q_SCALE = ATTN_SCALE * LOG2E
F32_MAX = 3.4028234663852886e38

W_QA = N_HEADS_A * HEAD_DIM
W_KA = N_KV_A * HEAD_DIM
W_B = N_HEADS_B * HEAD_DIM

LANES = 128
BF16_ROWS = 16
VMEM_LIMIT_BYTES = 56 * 1024 * 1024
VMEM_LIMIT_RESIDENT_BYTES = 60 * 1024 * 1024

TN = 512
TN_GATE = 1024
ROW_CHUNK = 256
ROW_GROUP_B = 4
GROUPS_PER_STEP_B = 2
QB_A = 1024


def _rms(x, g):
    return (x * lax.rsqrt(jnp.mean(x * x, axis=-1, keepdims=True) + EPS)) * g


def _mod_kernel(c_ref, w_ref, b_ref, o_ref):
    a = jax.nn.silu(c_ref[...]).astype(BF16)
    o_ref[...] = jnp.dot(a, w_ref[...].astype(BF16), preferred_element_type=F32) + b_ref[...]


def _mod_call(cs, w, b):
    m, d = cs.shape
    n = w.shape[1]
    tn = 1536 if n % 1536 == 0 else n
    return pl.pallas_call(
        _mod_kernel,
        grid=(n // tn,),
        in_specs=[pl.BlockSpec((m, d), lambda j: (0, 0)),
                  pl.BlockSpec((d, tn), lambda j: (0, j)),
                  pl.BlockSpec((1, tn), lambda j: (0, j))],
        out_specs=pl.BlockSpec((m, tn), lambda j: (0, j)),
        out_shape=jax.ShapeDtypeStruct((m, n), F32),
        compiler_params=pltpu.CompilerParams(dimension_semantics=("arbitrary",),
                                             vmem_limit_bytes=VMEM_LIMIT_BYTES),
        name="mod",
    )(cs, w, b)


def _rpb_kernel(rpb_ref, o_ref):
    n_dr = 2 * NB_ROWS - 1
    n_dc = 2 * NB_COLS - 1
    r = rpb_ref[0] * LOG2E
    m = lax.broadcasted_iota(jnp.int32, r.shape, 1)

    def dc_index(delta):
        return jnp.clip(delta, -(NB_COLS - 1), NB_COLS - 1) + (NB_COLS - 1)

    idx_lo = dc_index(jnp.where(m < GRID_W, m, m - LANES))
    idx_hi = dc_index(m - GRID_W)
    base_lo = jnp.zeros(r.shape, F32)
    base_hi = jnp.zeros(r.shape, F32)
    for dc in range(n_dc):
        col = jnp.broadcast_to(r[:, dc:dc + 1], r.shape)
        base_lo = jnp.where(idx_lo == dc, col, base_lo)
        base_hi = jnp.where(idx_hi == dc, col, base_hi)

    lane = lax.broadcasted_iota(jnp.int32, (GRID_W, LANES), 1)

    def rotated(base, dr):
        return pltpu.roll(jnp.broadcast_to(base[dr:dr + 1], (GRID_W, LANES)), 0, 1, stride=1, stride_axis=0)

    pair = [jnp.where(lane < GRID_W, rotated(base_lo, d), rotated(base_hi, d + 1)) for d in range(n_dr - 1)]
    n_kr = 3 * ROW_GROUP_B
    dr0 = NB_ROWS - 1 - ROW_GROUP_B
    for rq in range(ROW_GROUP_B):
        for p in range(n_kr // 2):
            o_ref[0, rq * GRID_W:(rq + 1) * GRID_W, p * LANES:(p + 1) * LANES] = pair[2 * p - rq + dr0]


def _rpb_call(rpb):
    nh = rpb.shape[0]
    nq = ROW_GROUP_B * GRID_W
    nk = 3 * ROW_GROUP_B * GRID_W
    assert 2 * ROW_GROUP_B == NB_ROWS and 2 * GRID_W == LANES
    n_dr, n_dc = rpb.shape[1], rpb.shape[2]
    rows = -(-n_dr // 8) * 8
    rpb_tiles = jnp.pad(rpb, ((0, 0), (0, rows - n_dr), (0, LANES - n_dc)))
    return pl.pallas_call(
        _rpb_kernel,
        grid=(nh,),
        in_specs=[pl.BlockSpec((1, rows, LANES), lambda h: (h, 0, 0))],
        out_specs=pl.BlockSpec((1, nq, nk), lambda h: (h, 0, 0)),
        out_shape=jax.ShapeDtypeStruct((nh, nq, nk), F32),
        compiler_params=pltpu.CompilerParams(dimension_semantics=("arbitrary",)),
        name="rpb_table",
    )(rpb_tiles)


def _rope(t, cos, sin, first_half):
    partner = jnp.where(first_half, pltpu.roll(t, LANES - 32, 1), pltpu.roll(t, 32, 1))
    return t * cos + partner * sin


def _row_chunked_matmul(h_ref, w_ref, epilogue, between=None):
    n_chunks = h_ref.shape[0] // ROW_CHUNK

    def dots(c):
        h = h_ref[c * ROW_CHUNK:(c + 1) * ROW_CHUNK]
        if isinstance(w_ref, tuple):
            return tuple(jnp.dot(h, w[...], preferred_element_type=F32) for w in w_ref)
        return jnp.dot(h, w_ref[...], preferred_element_type=F32)

    acc = dots(0)
    for c in range(n_chunks):
        nxt = dots(c + 1) if c + 1 < n_chunks else None
        epilogue(slice(c * ROW_CHUNK, (c + 1) * ROW_CHUNK), acc)
        if c == 0 and between is not None:
            between()
        acc = nxt


def _gate_kernel(x_ref, sc_ref, sh_ref, g_ref, w0_ref, w1_ref, o_ref, hout_ref, hs_ref, *, n_rows):
    i = pl.program_id(0)
    j = pl.program_id(1)
    pr = x_ref.shape[0]
    tn = w0_ref.shape[1]

    def norm_piece():
        h = (_rms(x_ref[...], g_ref[...]) * (1 + sc_ref[0]) + sh_ref[0]).astype(BF16)
        hout_ref[...] = h
        hs_ref[i % 2, pl.ds(pl.multiple_of(j * pr, pr), pr), :] = h

    def epilogue(rows, accs):
        for k, acc in enumerate(accs):
            o_ref[rows, k * tn:(k + 1) * tn] = jax.nn.sigmoid(acc).astype(BF16)

    @pl.when(i == 0)
    def _():
        norm_piece()

    @pl.when((i > 0) & (i < n_rows))
    def _():
        _row_chunked_matmul(hs_ref.at[(i + 1) % 2], (w0_ref, w1_ref), epilogue, between=norm_piece)

    @pl.when(i == n_rows)
    def _():
        _row_chunked_matmul(hs_ref.at[(i + 1) % 2], (w0_ref, w1_ref), epilogue)


def _gate_call(x2, sc, sh, g, w, *, tm, tn, col0, n_cols, rows_per_mod):
    n, d = x2.shape
    n_rows = n // tm
    nj = n_cols // tn
    pr = tm // nj
    assert pr * nj == tm and pr % BF16_ROWS == 0 and tn == 2 * TN and col0 % TN == 0
    mod_tiles = rows_per_mod // tm
    last = n_rows * nj - 1
    piece_map = lambda i, j: (jnp.minimum(i * nj + j, last), 0)
    mod_map = lambda i, j: (jnp.minimum(i, n_rows - 1) // mod_tiles, 0, 0)
    return pl.pallas_call(
        functools.partial(_gate_kernel, n_rows=n_rows),
        grid=(n_rows + 1, nj),
        in_specs=[pl.BlockSpec((pr, d), piece_map),
                  pl.BlockSpec((1, 1, d), mod_map),
                  pl.BlockSpec((1, 1, d), mod_map),
                  pl.BlockSpec((1, d), lambda i, j: (0, 0)),
                  pl.BlockSpec((d, TN), lambda i, j: (0, col0 // TN + 2 * j)),
                  pl.BlockSpec((d, TN), lambda i, j: (0, col0 // TN + 2 * j + 1))],
        out_specs=[pl.BlockSpec((tm, tn), lambda i, j: (jnp.maximum(i - 1, 0), jnp.where(i > 0, j, 0))),
                   pl.BlockSpec((pr, d), piece_map)],
        out_shape=[jax.ShapeDtypeStruct((n, n_cols), BF16), jax.ShapeDtypeStruct((n, d), BF16)],
        scratch_shapes=[pltpu.VMEM((2, tm, d), BF16)],
        compiler_params=pltpu.CompilerParams(dimension_semantics=("arbitrary", "arbitrary"),
                                             vmem_limit_bytes=VMEM_LIMIT_BYTES),
        name="gate_proj",
    )(x2, sc, sh, g, w, w)


def _qkv_weight_tile(jc):
    n_qa = W_QA // TN
    last = (W_QA + 3 * W_B) // TN
    return jnp.where(jc < n_qa, jc, jnp.where(jc == last, n_qa, jc + 1))


def _qkv_kernel(h_ref, w_ref, cos_ref, sin_ref, o_ref, *, j0):
    _proj_tile(h_ref, w_ref, cos_ref, sin_ref, o_ref, pl.program_id(1) + j0, 0)


def _qkv_call(h, w, cos, sin, *, tm, j0, nj):
    n, d = h.shape
    pos_tiles = cos.shape[0] // tm
    return pl.pallas_call(
        functools.partial(_qkv_kernel, j0=j0),
        grid=(n // tm, nj),
        in_specs=[pl.BlockSpec((tm, d), lambda i, j: (i, 0)),
                  pl.BlockSpec((d, TN), lambda i, j: (0, _qkv_weight_tile(j + j0))),
                  pl.BlockSpec((tm, HEAD_DIM), lambda i, j: (i % pos_tiles, 0)),
                  pl.BlockSpec((tm, HEAD_DIM), lambda i, j: (i % pos_tiles, 0))],
        out_specs=pl.BlockSpec((tm, TN), lambda i, j: (i, j)),
        out_shape=jax.ShapeDtypeStruct((n, nj * TN), BF16),
        compiler_params=pltpu.CompilerParams(dimension_semantics=("arbitrary", "arbitrary"),
                                             vmem_limit_bytes=VMEM_LIMIT_BYTES),
        name="qkv_proj",
    )(h, w, cos, sin)


def _inproj_kernel(x_ref, sc_ref, sh_ref, g_ref, w_ref, cos_ref, sin_ref, o_ref, h_ref, *, j0):
    j = pl.program_id(1)

    @pl.when(j == 0)
    def _():
        h = _rms(x_ref[...], g_ref[...]) * (1 + sc_ref[0]) + sh_ref[0]
        h_ref[...] = h.astype(BF16)

    _proj_tile(h_ref, w_ref, cos_ref, sin_ref, o_ref, j + j0, 0)


def _proj_tile(h_ref, w_ref, cos_ref, sin_ref, o_ref, jj, n_gate):
    lane = lax.broadcasted_iota(jnp.int32, (1, HEAD_DIM), 1)
    first_half = (lane & 63) < 32

    def row_chunks(epilogue):
        _row_chunked_matmul(h_ref, w_ref, epilogue)

    def rope_heads(rows, acc, n_heads, scale):
        cos = cos_ref[rows]
        sin = sin_ref[rows]
        for hd in range(n_heads):
            sl = slice(hd * HEAD_DIM, (hd + 1) * HEAD_DIM)
            r = _rope(acc[:, sl], cos, sin, first_half)
            if scale is not None:
                r = r * scale
            o_ref[rows, sl] = r.astype(BF16)

    @pl.when((jj >= n_gate) & (jj < n_gate + 2))
    def _():
        row_chunks(lambda rows, acc: rope_heads(rows, acc, TN // HEAD_DIM, Q_SCALE))

    @pl.when((jj >= n_gate + 2) & (jj < n_gate + 4))
    def _():
        def epilogue(rows, acc):
            o_ref[rows] = (acc * Q_SCALE).astype(BF16)
        row_chunks(epilogue)

    @pl.when((jj >= n_gate + 4) & (jj < n_gate + 8))
    def _():
        def epilogue(rows, acc):
            o_ref[rows] = acc.astype(BF16)
        row_chunks(epilogue)

    @pl.when(jj == n_gate + 8)
    def _():
        def epilogue(rows, acc):
            rope_heads(rows, acc, N_KV_A, None)
            o_ref[rows, W_KA:] = acc[:, W_KA:].astype(BF16)
        row_chunks(epilogue)


def _inproj_call(x2, sc, sh, g, w, cos, sin, *, tm, rows_per_mod, j0, nj):
    n, d = x2.shape
    pos_tiles = cos.shape[0] // tm
    mod_tiles = rows_per_mod // tm
    return pl.pallas_call(
        functools.partial(_inproj_kernel, j0=j0),
        grid=(n // tm, nj),
        in_specs=[pl.BlockSpec((tm, d), lambda i, j: (i, 0)),
                  pl.BlockSpec((1, 1, d), lambda i, j: (i // mod_tiles, 0, 0)),
                  pl.BlockSpec((1, 1, d), lambda i, j: (i // mod_tiles, 0, 0)),
                  pl.BlockSpec((1, d), lambda i, j: (0, 0)),
                  pl.BlockSpec((d, TN), lambda i, j: (0, _qkv_weight_tile(j + j0))),
                  pl.BlockSpec((tm, HEAD_DIM), lambda i, j: (i % pos_tiles, 0)),
                  pl.BlockSpec((tm, HEAD_DIM), lambda i, j: (i % pos_tiles, 0))],
        out_specs=pl.BlockSpec((tm, TN), lambda i, j: (i, j)),
        out_shape=jax.ShapeDtypeStruct((n, nj * TN), BF16),
        scratch_shapes=[pltpu.VMEM((tm, d), BF16)],
        compiler_params=pltpu.CompilerParams(dimension_semantics=("arbitrary", "arbitrary"),
                                             vmem_limit_bytes=VMEM_LIMIT_BYTES),
        name="in_proj",
    )(x2, sc, sh, g, w, cos, sin)


def _attn_a_kernel(sink_ref, q_ref, kp_ref, km_ref, kn_ref, vp_ref, vm_ref, vn_ref, kc_ref, vc_ref, o_ref,
                   *, n_blocks):
    nq = pl.program_id(1)
    sub = QB_A // BLOCK
    rows = GQA_GROUP * BLOCK
    n_loc = 3 * BLOCK
    r_io = lax.broadcasted_iota(jnp.int32, (rows, n_loc), 0)
    c_io = lax.broadcasted_iota(jnp.int32, (rows, n_loc), 1)
    qi_io = r_io & (BLOCK - 1)
    in_band = (c_io >= qi_io) & (c_io <= qi_io + 2 * WINDOW)
    c_row = lax.broadcasted_iota(jnp.int32, (1, n_loc), 1)
    g_io = lax.shift_right_logical(lax.broadcasted_iota(jnp.int32, (rows, 1), 0), BLOCK.bit_length() - 1)
    nt = (((1,), (1,)), ((), ()))
    for kvh in range(N_KV_A):
        hs = slice(kvh * HEAD_DIM, (kvh + 1) * HEAD_DIM)
        kband = jnp.concatenate([kp_ref[:, hs], km_ref[:, hs], kn_ref[:, hs]], axis=0)
        vband = jnp.concatenate([vp_ref[:, hs], vm_ref[:, hs], vn_ref[:, hs]], axis=0)
        kc = kc_ref[:, hs]
        vc = vc_ref[:, hs]
        sink = jnp.zeros((rows, 1), F32)
        for g in range(GQA_GROUP):
            sink = jnp.where(g_io == g, sink_ref[kvh * GQA_GROUP + g] * LOG2E, sink)
        for qi in range(sub):
            blk = nq * sub + qi
            lo = jnp.where(blk == 0, BLOCK, 0)
            hi = jnp.where(blk == n_blocks - 1, 2 * BLOCK - 1, n_loc - 1)
            valid = in_band & ((c_row >= lo) & (c_row <= hi))
            q4 = jnp.concatenate(
                [q_ref[qi * BLOCK:(qi + 1) * BLOCK, (kvh * GQA_GROUP + g) * HEAD_DIM:(kvh * GQA_GROUP + g + 1) * HEAD_DIM]
                 for g in range(GQA_GROUP)], axis=0)
            s_loc = lax.dot_general(q4, kband[qi * BLOCK:qi * BLOCK + n_loc], nt, preferred_element_type=F32)
            s_loc = jnp.where(valid, s_loc, NEG_INF)
            s_ctx = lax.dot_general(q4, kc, nt, preferred_element_type=F32)
            m = jnp.maximum(jnp.maximum(jnp.max(s_loc, axis=-1, keepdims=True),
                                        jnp.max(s_ctx, axis=-1, keepdims=True)), sink)
            e_loc = jnp.exp2(s_loc - m)
            e_ctx = jnp.exp2(s_ctx - m)
            den = (jnp.sum(e_loc, axis=-1, keepdims=True) + jnp.sum(e_ctx, axis=-1, keepdims=True)
                   + jnp.exp2(sink - m))
            o = (jnp.dot(e_loc.astype(BF16), vband[qi * BLOCK:qi * BLOCK + n_loc], preferred_element_type=F32)
                 + jnp.dot(e_ctx.astype(BF16), vc, preferred_element_type=F32)) * (1.0 / den)
            for g in range(GQA_GROUP):
                hd = kvh * GQA_GROUP + g
                o_ref[qi * BLOCK:(qi + 1) * BLOCK, hd * HEAD_DIM:(hd + 1) * HEAD_DIM] = (
                    o[g * BLOCK:(g + 1) * BLOCK].astype(BF16))


def _attn_a_call(proj, cproj, sink, *, batch, seq, ctx_len, q_col, k_col, v_col, ck_col, cv_col):
    nqb = seq // QB_A
    sub = QB_A // BLOCK
    n_blocks = seq // BLOCK

    def prev_map(b, n, col):
        return (b * n_blocks + jnp.maximum(n * sub - 1, 0), col // W_KA)

    def next_map(b, n, col):
        return (b * n_blocks + jnp.minimum(n * sub + sub, n_blocks - 1), col // W_KA)

    def main_map(b, n, col):
        return (b * nqb + n, col // W_KA)

    edge = lambda f, col: pl.BlockSpec((BLOCK, W_KA), functools.partial(f, col=col))
    main = lambda col: pl.BlockSpec((QB_A, W_KA), functools.partial(main_map, col=col))
    ctxs = lambda col: pl.BlockSpec((ctx_len, W_KA), lambda b, n: (b, col // W_KA))
    return pl.pallas_call(
        functools.partial(_attn_a_kernel, n_blocks=n_blocks),
        grid=(batch, nqb),
        in_specs=[pl.BlockSpec(memory_space=pltpu.SMEM),
                  pl.BlockSpec((QB_A, W_QA), lambda b, n: (b * nqb + n, q_col // W_QA)),
                  edge(prev_map, k_col), main(k_col), edge(next_map, k_col),
                  edge(prev_map, v_col), main(v_col), edge(next_map, v_col),
                  ctxs(ck_col), ctxs(cv_col)],
        out_specs=pl.BlockSpec((QB_A, W_QA), lambda b, n: (b * nqb + n, 0)),
        out_shape=jax.ShapeDtypeStruct((batch * seq, W_QA), BF16),
        compiler_params=pltpu.CompilerParams(dimension_semantics=("arbitrary",) * 2,
                                             vmem_limit_bytes=VMEM_LIMIT_BYTES),
        name="attn_a",
    )(sink, proj, proj, proj, proj, proj, proj, proj, cproj, cproj)


def _attn_b_kernel(q_ref, kp_ref, km_ref, kn_ref, vp_ref, vm_ref, vn_ref, kc_ref, vc_ref, bias_ref, o_ref,
                   *, grid_rows):
    n = pl.program_id(1)
    nq = ROW_GROUP_B * GRID_W
    nk = 3 * nq
    heads = q_ref.shape[1] // HEAD_DIM
    r_io = lax.broadcasted_iota(jnp.int32, (nq, nk), 0)
    c_io = lax.broadcasted_iota(jnp.int32, (nq, nk), 1)
    log_w = GRID_W.bit_length() - 1
    q_col = r_io & (GRID_W - 1)
    k_col = c_io & (GRID_W - 1)
    c_start = jnp.clip(q_col - NB_COLS // 2, 0, GRID_W - NB_COLS)
    col_ok = (k_col >= c_start) & (k_col < c_start + NB_COLS)
    nt = (((1,), (1,)), ((), ()))
    for grp in range(GROUPS_PER_STEP_B):
        gi = n * GROUPS_PER_STEP_B + grp
        r_abs = gi * ROW_GROUP_B + lax.shift_right_logical(r_io, log_w)
        kr_abs = (gi - 1) * ROW_GROUP_B + lax.shift_right_logical(c_io, log_w)
        r_start = jnp.clip(r_abs - NB_ROWS // 2, 0, grid_rows - NB_ROWS)
        valid = (kr_abs >= r_start) & (kr_abs < r_start + NB_ROWS) & col_ok
        cap = jnp.where(valid, F32_MAX, NEG_INF)
        rows = slice(grp * nq, (grp + 1) * nq)
        for hd in range(heads):
            sl = slice(hd * HEAD_DIM, (hd + 1) * HEAD_DIM)
            q = q_ref[rows, sl]

            def band(first_ref, main_ref, last_ref, g):
                if g == 0:
                    return first_ref[:, sl]
                if g == GROUPS_PER_STEP_B + 1:
                    return last_ref[:, sl]
                return main_ref[(g - 1) * nq:g * nq, sl]

            s_parts = []
            for kg in range(3):
                cols = slice(kg * nq, (kg + 1) * nq)
                s = lax.dot_general(q, band(kp_ref, km_ref, kn_ref, grp + kg), nt, preferred_element_type=F32)
                s_parts.append(jnp.minimum(s + bias_ref[hd, :, cols], cap[:, cols]))
            s_parts.append(lax.dot_general(q, kc_ref[:, sl], nt, preferred_element_type=F32))
            m = functools.reduce(jnp.maximum, [jnp.max(s, axis=-1, keepdims=True) for s in s_parts])
            e_parts = [jnp.exp2(s - m) for s in s_parts]
            den = functools.reduce(jnp.add, [jnp.sum(e, axis=-1, keepdims=True) for e in e_parts])
            v_parts = [band(vp_ref, vm_ref, vn_ref, grp + kg) for kg in range(3)] + [vc_ref[:, sl]]
            o = functools.reduce(jnp.add, [jnp.dot(e.astype(BF16), v, preferred_element_type=F32)
                                           for e, v in zip(e_parts, v_parts)]) * (1.0 / den)
            o_ref[rows, sl] = o.astype(BF16)


def _attn_b_call(proj, cproj, bias, *, batch, seq, ctx_len, q_col, k_col, v_col, ck_col, cv_col):
    nq = ROW_GROUP_B * GRID_W
    ng = seq // nq
    gps = GROUPS_PER_STEP_B
    ns = ng // gps
    hw = W_B

    def edge(col, first):
        def imap(b, n):
            g = jnp.maximum(n * gps - 1, 0) if first else jnp.minimum(n * gps + gps, ng - 1)
            return (b * ng + g, col // hw)
        return pl.BlockSpec((nq, hw), imap)

    main = lambda col: pl.BlockSpec((gps * nq, hw), lambda b, n: (b * ns + n, col // hw))
    ctxs = lambda col: pl.BlockSpec((ctx_len, hw), lambda b, n: (b, col // hw))
    return pl.pallas_call(
        functools.partial(_attn_b_kernel, grid_rows=seq // GRID_W),
        grid=(batch, ns),
        in_specs=[main(q_col),
                  edge(k_col, True), main(k_col), edge(k_col, False),
                  edge(v_col, True), main(v_col), edge(v_col, False),
                  ctxs(ck_col), ctxs(cv_col),
                  pl.BlockSpec((hw // HEAD_DIM, nq, bias.shape[2]), lambda b, n: (0, 0, 0))],
        out_specs=pl.BlockSpec((gps * nq, hw), lambda b, n: (b * ns + n, 0)),
        out_shape=jax.ShapeDtypeStruct((batch * seq, W_B), BF16),
        compiler_params=pltpu.CompilerParams(dimension_semantics=("arbitrary",) * 2,
                                             vmem_limit_bytes=VMEM_LIMIT_BYTES),
        name="attn_b",
    )(proj, proj, proj, proj, proj, proj, proj, cproj, cproj, bias)


def _resident_bf16(w_ref, wb_ref):
    @pl.when(pl.program_id(0) == 0)
    def _():
        wb_ref[...] = w_ref[...].astype(BF16)


def _mix_kernel(oa_ref, ob_ref, ga_ref, gb_ref, wa_ref, wb_ref, z_ref, wa_bf, wb_bf):
    _resident_bf16(wa_ref, wa_bf)
    _resident_bf16(wb_ref, wb_bf)
    ya = jnp.dot(oa_ref[...], wa_bf[...], preferred_element_type=F32)
    yb = jnp.dot(ob_ref[...], wb_bf[...], preferred_element_type=F32)
    z_ref[...] = (ga_ref[...].astype(F32) * ya + gb_ref[...].astype(F32) * yb).astype(BF16)


def _mix_call(oa, ob, proj, wa, wb, *, tm):
    n = oa.shape[0]
    d = wa.shape[1]
    const = lambda shape: pl.BlockSpec(shape, lambda i: (0, 0), pipeline_mode=pl.Buffered(1))
    return pl.pallas_call(
        _mix_kernel,
        grid=(n // tm,),
        in_specs=[pl.BlockSpec((tm, oa.shape[1]), lambda i: (i, 0)),
                  pl.BlockSpec((tm, ob.shape[1]), lambda i: (i, 0)),
                  pl.BlockSpec((tm, d), lambda i: (i, 0)),
                  pl.BlockSpec((tm, d), lambda i: (i, 1)),
                  const(wa.shape), const(wb.shape)],
        out_specs=pl.BlockSpec((tm, d), lambda i: (i, 0)),
        out_shape=jax.ShapeDtypeStruct((n, d), BF16),
        scratch_shapes=[pltpu.VMEM(wa.shape, BF16), pltpu.VMEM(wb.shape, BF16)],
        compiler_params=pltpu.CompilerParams(dimension_semantics=("arbitrary",),
                                             vmem_limit_bytes=VMEM_LIMIT_BYTES),
        name="branch_mix",
    )(oa, ob, proj, proj, wa, wb)


def _outnorm_kernel(a_ref, w_ref, x_ref, gt_ref, g_ref, o_ref, w_bf):
    _resident_bf16(w_ref, w_bf)
    y = jnp.dot(a_ref[...], w_bf[...], preferred_element_type=F32)
    o_ref[...] = x_ref[...] + gt_ref[0] * _rms(y, g_ref[...])


def _outnorm_call(a, w, x2, gt, g, *, tm, rows_per_mod):
    n, kdim = a.shape
    d = w.shape[1]
    mod_tiles = rows_per_mod // tm
    return pl.pallas_call(
        _outnorm_kernel,
        grid=(n // tm,),
        in_specs=[pl.BlockSpec((tm, kdim), lambda i: (i, 0)),
                  pl.BlockSpec((kdim, d), lambda i: (0, 0), pipeline_mode=pl.Buffered(1)),
                  pl.BlockSpec((tm, d), lambda i: (i, 0)),
                  pl.BlockSpec((1, 1, d), lambda i: (i // mod_tiles, 0, 0)),
                  pl.BlockSpec((1, d), lambda i: (0, 0))],
        out_specs=pl.BlockSpec((tm, d), lambda i: (i, 0)),
        out_shape=jax.ShapeDtypeStruct((n, d), F32),
        scratch_shapes=[pltpu.VMEM(w.shape, BF16)],
        compiler_params=pltpu.CompilerParams(dimension_semantics=("arbitrary",),
                                             vmem_limit_bytes=VMEM_LIMIT_BYTES),
        name="out_norm",
    )(a, w, x2, gt, g)


def _rowmm_norm_kernel(a_ref, w_ref, x_ref, gt_ref, g_ref, o_ref, *, n_chunks):
    cm = a_ref.shape[0] // n_chunks
    y = jnp.dot(a_ref[0:cm], w_ref[...], preferred_element_type=F32)
    for c in range(n_chunks):
        nxt = None
        if c + 1 < n_chunks:
            nxt = jnp.dot(a_ref[(c + 1) * cm:(c + 2) * cm], w_ref[...], preferred_element_type=F32)
        rows = slice(c * cm, (c + 1) * cm)
        o_ref[rows] = x_ref[rows] + gt_ref[0] * _rms(y, g_ref[...])
        y = nxt


def _rowmm_norm_call(a, w, x2, gt, g, *, tm, n_chunks, rows_per_mod):
    n, kdim = a.shape
    d = w.shape[1]
    mod_tiles = rows_per_mod // tm
    return pl.pallas_call(
        functools.partial(_rowmm_norm_kernel, n_chunks=n_chunks),
        grid=(n // tm,),
        in_specs=[pl.BlockSpec((tm, kdim), lambda i: (i, 0)),
                  pl.BlockSpec((kdim, d), lambda i: (0, 0), pipeline_mode=pl.Buffered(1)),
                  pl.BlockSpec((tm, d), lambda i: (i, 0)),
                  pl.BlockSpec((1, 1, d), lambda i: (i // mod_tiles, 0, 0)),
                  pl.BlockSpec((1, d), lambda i: (0, 0))],
        out_specs=pl.BlockSpec((tm, d), lambda i: (i, 0)),
        out_shape=jax.ShapeDtypeStruct((n, d), F32),
        compiler_params=pltpu.CompilerParams(dimension_semantics=("arbitrary",),
                                             vmem_limit_bytes=VMEM_LIMIT_RESIDENT_BYTES),
        name="rowmm_norm",
    )(a, w, x2, gt, g)


def _ffn_up_kernel(x0p_ref, x0_ref, x0n_ref, sc0_ref, sh0_ref, xp_ref, xm_ref, xn_ref, sc_ref, sh_ref, g_ref,
                   wa_ref, wg_ref, cwa_ref, cwg_ref, cba_ref, cbg_ref,
                   o_ref, h_cur, h_nxt, ua_ref, ug_ref, *, nj, n_tiles, tiles_per_seq, n_pieces):
    t = pl.program_id(0)
    tm = x0_ref.shape[0]
    halo = xp_ref.shape[0]
    pr = xm_ref.shape[0]
    ext = tm + 2 * halo
    margin = 8

    def mod(x, sc, sh):
        return (_rms(x, g_ref[...]) * (1 + sc[0]) + sh[0]).astype(BF16)

    @pl.when(t == 0)
    def _():
        ua_ref[...] = jnp.zeros_like(ua_ref)
        ug_ref[...] = jnp.zeros_like(ug_ref)
        h_cur[0:halo] = mod(x0p_ref[...], sc0_ref, sh0_ref)
        h_cur[halo:halo + tm] = mod(x0_ref[...], sc0_ref, sh0_ref)
        h_cur[halo + tm:ext] = mod(x0n_ref[...], sc0_ref, sh0_ref)

    tc = jnp.minimum(t, n_tiles - 1)

    @pl.when((t > 0) & (t < n_tiles) & (t % nj == 0))
    def _():
        h_cur[...] = h_nxt[...]

    def next_row_piece():
        p = jnp.minimum(tc % nj, n_pieces - 1)
        h_nxt[0:halo] = mod(xp_ref[...], sc_ref, sh_ref)
        h_nxt[pl.ds(pl.multiple_of(halo + p * pr, BF16_ROWS), pr), :] = mod(xm_ref[...], sc_ref, sh_ref)
        h_nxt[halo + tm:ext] = mod(xn_ref[...], sc_ref, sh_ref)

    slot = t % 2
    prev = 1 - slot
    ip = jnp.maximum(t - 1, 0) // nj
    first_row = jnp.where((ip % tiles_per_seq) == 0, 0, -1)
    last_row = jnp.where((ip % tiles_per_seq) == tiles_per_seq - 1, tm - 1, -1)
    row = lax.broadcasted_iota(jnp.int32, (tm, 1), 0)

    def epilogue(lo, hi):
        cm = hi - lo
        rowc = row[lo:hi]

        def conv(u_ref, cw_ref, cb_ref):
            win = u_ref[prev, pl.ds(halo + lo - margin, cm + 2 * margin), :]
            up = pltpu.roll(win, 1, 0)[margin:margin + cm]
            un = pltpu.roll(win, cm + 2 * margin - 1, 0)[margin:margin + cm]
            up = jnp.where(rowc == first_row, 0.0, up)
            un = jnp.where(rowc == last_row, 0.0, un)
            out = cb_ref[...] + up * cw_ref[0:1]
            out = out + win[margin:margin + cm] * cw_ref[1:2]
            return out + un * cw_ref[2:3]

        a = conv(ua_ref, cwa_ref, cba_ref)
        gte = conv(ug_ref, cwg_ref, cbg_ref)
        o_ref[lo:hi] = (jax.nn.silu(gte) * a).astype(BF16)

    half = ext // 2
    third = (tm // 3) // BF16_ROWS * BF16_ROWS
    chunks = [(0, third), (third, 2 * third), (2 * third, tm)]
    ua_ref[slot, 0:half] = jnp.dot(h_cur[0:half], wa_ref[...], preferred_element_type=F32)
    next_row_piece()
    epilogue(*chunks[0])
    ua_ref[slot, half:ext] = jnp.dot(h_cur[half:ext], wa_ref[...], preferred_element_type=F32)
    epilogue(*chunks[1])
    ug_ref[slot, 0:half] = jnp.dot(h_cur[0:half], wg_ref[...], preferred_element_type=F32)
    epilogue(*chunks[2])
    ug_ref[slot, half:ext] = jnp.dot(h_cur[half:ext], wg_ref[...], preferred_element_type=F32)


def _ffn_up_call(x2, sc, sh, g, w_up, conv_w, conv_b, *, tm, tn, seq):
    n, d = x2.shape
    dff = w_up.shape[1] // 2
    nj = dff // tn
    halo = BF16_ROWS
    hb = tm // halo
    n_halo = n // halo
    tiles_per_seq = seq // tm
    n_rows = n // tm
    n_tiles = n_rows * nj
    pr = 128
    n_pieces = tm // pr
    assert (tm + 2 * halo) % (2 * BF16_ROWS) == 0 and n_pieces <= nj and n_pieces * pr == tm

    def cur(t):
        tc = jnp.minimum(t, n_tiles - 1)
        return tc // nj, tc % nj

    def prv(t):
        tp = jnp.maximum(t - 1, 0)
        return tp // nj, tp % nj

    def nxt(t):
        return jnp.minimum(cur(t)[0] + 1, n_rows - 1)

    once = lambda shape, imap: pl.BlockSpec(shape, imap, pipeline_mode=pl.Buffered(1))
    return pl.pallas_call(
        functools.partial(_ffn_up_kernel, nj=nj, n_tiles=n_tiles, tiles_per_seq=tiles_per_seq, n_pieces=n_pieces),
        grid=(n_tiles + 1,),
        in_specs=[once((halo, d), lambda t: (0, 0)),
                  once((tm, d), lambda t: (0, 0)),
                  once((halo, d), lambda t: (jnp.minimum(hb, n_halo - 1), 0)),
                  once((1, 1, d), lambda t: (0, 0, 0)),
                  once((1, 1, d), lambda t: (0, 0, 0)),
                  pl.BlockSpec((halo, d), lambda t: (jnp.maximum(nxt(t) * hb - 1, 0), 0)),
                  pl.BlockSpec((pr, d), lambda t: (nxt(t) * n_pieces + jnp.minimum(cur(t)[1], n_pieces - 1), 0)),
                  pl.BlockSpec((halo, d), lambda t: (jnp.minimum((nxt(t) + 1) * hb, n_halo - 1), 0)),
                  pl.BlockSpec((1, 1, d), lambda t: (nxt(t) // tiles_per_seq, 0, 0)),
                  pl.BlockSpec((1, 1, d), lambda t: (nxt(t) // tiles_per_seq, 0, 0)),
                  pl.BlockSpec((1, d), lambda t: (0, 0)),
                  pl.BlockSpec((d, tn), lambda t: (0, cur(t)[1])),
                  pl.BlockSpec((d, tn), lambda t: (0, cur(t)[1] + nj)),
                  pl.BlockSpec((CONV_W, tn), lambda t: (0, prv(t)[1])),
                  pl.BlockSpec((CONV_W, tn), lambda t: (0, prv(t)[1] + nj)),
                  pl.BlockSpec((1, tn), lambda t: (0, prv(t)[1])),
                  pl.BlockSpec((1, tn), lambda t: (0, prv(t)[1] + nj))],
        out_specs=pl.BlockSpec((tm, tn), lambda t: prv(t)),
        out_shape=jax.ShapeDtypeStruct((n, dff), BF16),
        scratch_shapes=[pltpu.VMEM((tm + 2 * halo, d), BF16),
                        pltpu.VMEM((tm + 2 * halo, d), BF16),
                        pltpu.VMEM((2, tm + 2 * halo, tn), F32),
                        pltpu.VMEM((2, tm + 2 * halo, tn), F32)],
        compiler_params=pltpu.CompilerParams(dimension_semantics=("arbitrary",),
                                             vmem_limit_bytes=VMEM_LIMIT_BYTES),
        name="ffn_up",
    )(x2, x2, x2, sc, sh, x2, x2, x2, sc, sh, g, w_up, w_up, conv_w, conv_w, conv_b, conv_b)


def _rope_tables(seq):
    t = np.arange(seq)
    half = HEAD_DIM // 4
    inv = np.float32(ROPE_BASE) ** (-np.arange(half, dtype=np.float32) / np.float32(half))

    def cs(pos):
        ang = pos.astype(np.float32)[:, None] * inv[None, :]
        c, s = np.cos(ang), np.sin(ang)
        return np.concatenate([c, c], axis=1), np.concatenate([-s, s], axis=1)

    cr, sr = cs(t // GRID_W)
    cc, sn = cs(t % GRID_W)
    return (jnp.asarray(np.concatenate([cr, cc], axis=1), F32), jnp.asarray(np.concatenate([sr, sn], axis=1), F32))


def _pick_tile(n, pref):
    while n % pref:
        pref //= 2
    return pref


class _RowTiles(NamedTuple):
    proj: int
    qkv: int
    mix: int
    down: int


def _row_tiles(seq):
    return _RowTiles(*(_pick_tile(seq, pref) for pref in (1024, 2048, 512, 256)))


def kernel(x, c, ctx, c_ctx, w_mod, b_mod, g_attn_pre, g_attn_post, g_ffn_pre, g_ffn_post, w_in, sink_a, rpb_b,
           w_br_a, w_br_b, w_o, w_up, conv_w, conv_b, w_down):
    batch, seq, d = x.shape
    ctx_len = ctx.shape[1]
    assert w_mod.shape[0] == 1 and d == 16 * HEAD_DIM
    assert seq % QB_A == 0 and seq % (GROUPS_PER_STEP_B * ROW_GROUP_B * GRID_W) == 0 and WINDOW == BLOCK
    n = batch * seq
    x2 = x.reshape(n, d)

    pad = (-(batch + 1)) % 8
    cs = jnp.concatenate([c, c_ctx[None, :], jnp.zeros((pad, d), F32)], axis=0)
    mod = _mod_call(cs, w_mod[0], b_mod[0][None, :])
    sh1, sc1, gt1, sh2, sc2, gt2 = [mod[:batch, k * d:(k + 1) * d].reshape(batch, 1, d) for k in range(6)]
    csh1 = mod[batch:batch + 1, 0:d].reshape(1, 1, d)
    csc1 = mod[batch:batch + 1, d:2 * d].reshape(1, 1, d)

    n_qkv_cols = W_QA + 2 * W_KA + 3 * W_B
    n_gate_cols = 2 * d
    w_in_bf = w_in[0].astype(BF16)
    n_tiles = n_qkv_cols // TN
    c_qa = 0
    c_qb = c_qa + W_QA
    c_kb = c_qb + W_B
    c_vb = c_kb + W_B
    c_ka = c_vb + W_B
    c_va = c_ka + W_KA
    cos, sin = _rope_tables(seq)
    tiles = _row_tiles(seq)
    gates, h1 = _gate_call(x2, sc1, sh1, g_attn_pre, w_in_bf, tm=tiles.proj, tn=TN_GATE, col0=n_qkv_cols,
                           n_cols=n_gate_cols, rows_per_mod=seq)
    proj = _qkv_call(h1, w_in_bf, cos, sin, tm=tiles.qkv, j0=0, nj=n_tiles)
    nc = batch * ctx_len
    j0c = c_kb // TN
    cproj = _inproj_call(ctx.reshape(nc, d), csc1, csh1, g_attn_pre, w_in_bf,
                         jnp.ones((nc, HEAD_DIM), F32), jnp.zeros((nc, HEAD_DIM), F32),
                         tm=nc, rows_per_mod=nc, j0=j0c, nj=n_tiles - j0c)
    cc_kb, cc_vb, cc_ka, cc_va = 0, W_B, 2 * W_B, 2 * W_B + W_KA

    oa = _attn_a_call(proj, cproj, sink_a[0], batch=batch, seq=seq, ctx_len=ctx_len,
                      q_col=c_qa, k_col=c_ka, v_col=c_va, ck_col=cc_ka, cv_col=cc_va)
    bias = _rpb_call(rpb_b[0])
    ob = _attn_b_call(proj, cproj, bias, batch=batch, seq=seq, ctx_len=ctx_len,
                      q_col=c_qb, k_col=c_kb, v_col=c_vb, ck_col=cc_kb, cv_col=cc_vb)

    z = _mix_call(oa, ob, gates, w_br_a[0], w_br_b[0], tm=tiles.mix)
    x1 = _outnorm_call(z, w_o[0], x2, gt1, g_attn_post, tm=tiles.mix, rows_per_mod=seq)

    act = _ffn_up_call(x1, sc2, sh2, g_ffn_pre, w_up[0].astype(BF16), conv_w[0], conv_b[0][None, :],
                       tm=tiles.proj, tn=TN, seq=seq)
    out = _rowmm_norm_call(act, w_down[0].astype(BF16), x1, gt2, g_ffn_post, tm=tiles.down, n_chunks=2,
                           rows_per_mod=seq)
    return out.reshape(batch, seq, d)
```

```python
import functools
from typing import NamedTuple

import numpy as np
import jax
import jax.numpy as jnp
from jax import lax
from jax.experimental import pallas as pl
from jax.experimental.pallas import tpu as pltpu

F32 = jnp.float32
BF16 = jnp.bfloat16

GRID_W = 64
HEAD_DIM = 128
N_HEADS_A = 8
N_KV_A = 2
GQA_GROUP = N_HEADS_A // N_KV_A
WINDOW = 128
BLOCK = 128
N_HEADS_B = 8
NB_ROWS = 8
NB_COLS = 16
CONV_W = 3
ROPE_BASE = 10000.0
EPS = 1e-6
NEG_INF = -1e30
ATTN_SCALE = HEAD_DIM ** -0.5
LOG2E = 1.4426950408889634
Q_SCALE = ATTN_SCALE * LOG2E
F32_MAX = 3.4028234663852886e38

W_QA = N_HEADS_A * HEAD_DIM
W_KA = N_KV_A * HEAD_DIM
W_B = N_HEADS_B * HEAD_DIM

LANES = 128
BF16_ROWS = 16
VMEM_LIMIT_BYTES = 56 * 1024 * 1024
VMEM_LIMIT_RESIDENT_BYTES = 60 * 1024 * 1024

TN = 512
TN_GATE = 1024
ROW_CHUNK = 256
ROW_GROUP_B = 4
GROUPS_PER_STEP_B = 2
QB_A = 1024


def _rms(x, g):
    return (x * lax.rsqrt(jnp.mean(x * x, axis=-1, keepdims=True) + EPS)) * g


def _mod_kernel(c_ref, w_ref, b_ref, o_ref):
    a = jax.nn.silu(c_ref[...]).astype(BF16)
    o_ref[...] = jnp.dot(a, w_ref[...].astype(BF16), preferred_element_type=F32) + b_ref[...]


def _mod_call(cs, w, b):
    m, d = cs.shape
    n = w.shape[1]
    tn = 1536 if n % 1536 == 0 else n
    return pl.pallas_call(
        _mod_kernel,
        grid=(n // tn,),
        in_specs=[pl.BlockSpec((m, d), lambda j: (0, 0)),
                  pl.BlockSpec((d, tn), lambda j: (0, j)),
                  pl.BlockSpec((1, tn), lambda j: (0, j))],
        out_specs=pl.BlockSpec((m, tn), lambda j: (0, j)),
        out_shape=jax.ShapeDtypeStruct((m, n), F32),
        compiler_params=pltpu.CompilerParams(dimension_semantics=("arbitrary",),
                                             vmem_limit_bytes=VMEM_LIMIT_BYTES),
        name="mod",
    )(cs, w, b)


def _rpb_kernel(rpb_ref, o_ref):
    n_dr = 2 * NB_ROWS - 1
    n_dc = 2 * NB_COLS - 1
    r = rpb_ref[0] * LOG2E
    m = lax.broadcasted_iota(jnp.int32, r.shape, 1)

    def dc_index(delta):
        return jnp.clip(delta, -(NB_COLS - 1), NB_COLS - 1) + (NB_COLS - 1)

    idx_lo = dc_index(jnp.where(m < GRID_W, m, m - LANES))
    idx_hi = dc_index(m - GRID_W)
    base_lo = jnp.zeros(r.shape, F32)
    base_hi = jnp.zeros(r.shape, F32)
    for dc in range(n_dc):
        col = jnp.broadcast_to(r[:, dc:dc + 1], r.shape)
        base_lo = jnp.where(idx_lo == dc, col, base_lo)
        base_hi = jnp.where(idx_hi == dc, col, base_hi)

    lane = lax.broadcasted_iota(jnp.int32, (GRID_W, LANES), 1)

    def rotated(base, dr):
        return pltpu.roll(jnp.broadcast_to(base[dr:dr + 1], (GRID_W, LANES)), 0, 1, stride=1, stride_axis=0)

    pair = [jnp.where(lane < GRID_W, rotated(base_lo, d), rotated(base_hi, d + 1)) for d in range(n_dr - 1)]
    n_kr = 3 * ROW_GROUP_B
    dr0 = NB_ROWS - 1 - ROW_GROUP_B
    for rq in range(ROW_GROUP_B):
        for p in range(n_kr // 2):
            o_ref[0, rq * GRID_W:(rq + 1) * GRID_W, p * LANES:(p + 1) * LANES] = pair[2 * p - rq + dr0]


def _rpb_call(rpb):
    nh = rpb.shape[0]
    nq = ROW_GROUP_B * GRID_W
    nk = 3 * ROW_GROUP_B * GRID_W
    assert 2 * ROW_GROUP_B == NB_ROWS and 2 * GRID_W == LANES
    n_dr, n_dc = rpb.shape[1], rpb.shape[2]
    rows = -(-n_dr // 8) * 8
    rpb_tiles = jnp.pad(rpb, ((0, 0), (0, rows - n_dr), (0, LANES - n_dc)))
    return pl.pallas_call(
        _rpb_kernel,
        grid=(nh,),
        in_specs=[pl.BlockSpec((1, rows, LANES), lambda h: (h, 0, 0))],
        out_specs=pl.BlockSpec((1, nq, nk), lambda h: (h, 0, 0)),
        out_shape=jax.ShapeDtypeStruct((nh, nq, nk), F32),
        compiler_params=pltpu.CompilerParams(dimension_semantics=("arbitrary",)),
        name="rpb_table",
    )(rpb_tiles)


def _rope(t, cos, sin, first_half):
    partner = jnp.where(first_half, pltpu.roll(t, LANES - 32, 1), pltpu.roll(t, 32, 1))
    return t * cos + partner * sin


def _row_chunked_matmul(h_ref, w_ref, epilogue, between=None):
    n_chunks = h_ref.shape[0] // ROW_CHUNK

    def dots(c):
        h = h_ref[c * ROW_CHUNK:(c + 1) * ROW_CHUNK]
        if isinstance(w_ref, tuple):
            return tuple(jnp.dot(h, w[...], preferred_element_type=F32) for w in w_ref)
        return jnp.dot(h, w_ref[...], preferred_element_type=F32)

    acc = dots(0)
    for c in range(n_chunks):
        nxt = dots(c + 1) if c + 1 < n_chunks else None
        epilogue(slice(c * ROW_CHUNK, (c + 1) * ROW_CHUNK), acc)
        if c == 0 and between is not None:
            between()
        acc = nxt


def _gate_kernel(x_ref, sc_ref, sh_ref, g_ref, w0_ref, w1_ref, o_ref, hout_ref, hs_ref, *, n_rows):
    i = pl.program_id(0)
    j = pl.program_id(1)
    pr = x_ref.shape[0]
    tn = w0_ref.shape[1]

    def norm_piece():
        h = (_rms(x_ref[...], g_ref[...]) * (1 + sc_ref[0]) + sh_ref[0]).astype(BF16)
        hout_ref[...] = h
        hs_ref[i % 2, pl.ds(pl.multiple_of(j * pr, pr), pr), :] = h

    def epilogue(rows, accs):
        for k, acc in enumerate(accs):
            o_ref[rows, k * tn:(k + 1) * tn] = jax.nn.sigmoid(acc).astype(BF16)

    @pl.when(i == 0)
    def _():
        norm_piece()

    @pl.when((i > 0) & (i < n_rows))
    def _():
        _row_chunked_matmul(hs_ref.at[(i + 1) % 2], (w0_ref, w1_ref), epilogue, between=norm_piece)

    @pl.when(i == n_rows)
    def _():
        _row_chunked_matmul(hs_ref.at[(i + 1) % 2], (w0_ref, w1_ref), epilogue)


def _gate_call(x2, sc, sh, g, w, *, tm, tn, col0, n_cols, rows_per_mod):
    n, d = x2.shape
    n_rows = n // tm
    nj = n_cols // tn
    pr = tm // nj
    assert pr * nj == tm and pr % BF16_ROWS == 0 and tn == 2 * TN and col0 % TN == 0
    mod_tiles = rows_per_mod // tm
    last = n_rows * nj - 1
    piece_map = lambda i, j: (jnp.minimum(i * nj + j, last), 0)
    mod_map = lambda i, j: (jnp.minimum(i, n_rows - 1) // mod_tiles, 0, 0)
    return pl.pallas_call(
        functools.partial(_gate_kernel, n_rows=n_rows),
        grid=(n_rows + 1, nj),
        in_specs=[pl.BlockSpec((pr, d), piece_map),
                  pl.BlockSpec((1, 1, d), mod_map),
                  pl.BlockSpec((1, 1, d), mod_map),
                  pl.BlockSpec((1, d), lambda i, j: (0, 0)),
                  pl.BlockSpec((d, TN), lambda i, j: (0, col0 // TN + 2 * j)),
                  pl.BlockSpec((d, TN), lambda i, j: (0, col0 // TN + 2 * j + 1))],
        out_specs=[pl.BlockSpec((tm, tn), lambda i, j: (jnp.maximum(i - 1, 0), jnp.where(i > 0, j, 0))),
                   pl.BlockSpec((pr, d), piece_map)],
        out_shape=[jax.ShapeDtypeStruct((n, n_cols), BF16), jax.ShapeDtypeStruct((n, d), BF16)],
        scratch_shapes=[pltpu.VMEM((2, tm, d), BF16)],
        compiler_params=pltpu.CompilerParams(dimension_semantics=("arbitrary", "arbitrary"),
                                             vmem_limit_bytes=VMEM_LIMIT_BYTES),
        name="gate_proj",
    )(x2, sc, sh, g, w, w)


def _qkv_weight_tile(jc):
    n_qa = W_QA // TN
    last = (W_QA + 3 * W_B) // TN
    return jnp.where(jc < n_qa, jc, jnp.where(jc == last, n_qa, jc + 1))


def _qkv_kernel(h_ref, w_ref, cos_ref, sin_ref, o_ref, *, j0):
    _proj_tile(h_ref, w_ref, cos_ref, sin_ref, o_ref, pl.program_id(1) + j0, 0)


def _qkv_call(h, w, cos, sin, *, tm, j0, nj):
    n, d = h.shape
    pos_tiles = cos.shape[0] // tm
    return pl.pallas_call(
        functools.partial(_qkv_kernel, j0=j0),
        grid=(n // tm, nj),
        in_specs=[pl.BlockSpec((tm, d), lambda i, j: (i, 0)),
                  pl.BlockSpec((d, TN), lambda i, j: (0, _qkv_weight_tile(j + j0))),
                  pl.BlockSpec((tm, HEAD_DIM), lambda i, j: (i % pos_tiles, 0)),
                  pl.BlockSpec((tm, HEAD_DIM), lambda i, j: (i % pos_tiles, 0))],
        out_specs=pl.BlockSpec((tm, TN), lambda i, j: (i, j)),
        out_shape=jax.ShapeDtypeStruct((n, nj * TN), BF16),
        compiler_params=pltpu.CompilerParams(dimension_semantics=("arbitrary", "arbitrary"),
                                             vmem_limit_bytes=VMEM_LIMIT_BYTES),
        name="qkv_proj",
    )(h, w, cos, sin)


def _inproj_kernel(x_ref, sc_ref, sh_ref, g_ref, w_ref, cos_ref, sin_ref, o_ref, h_ref, *, j0):
    j = pl.program_id(1)

    @pl.when(j == 0)
    def _():
        h = _rms(x_ref[...], g_ref[...]) * (1 + sc_ref[0]) + sh_ref[0]
        h_ref[...] = h.astype(BF16)

    _proj_tile(h_ref, w_ref, cos_ref, sin_ref, o_ref, j + j0, 0)


def _proj_tile(h_ref, w_ref, cos_ref, sin_ref, o_ref, jj, n_gate):
    lane = lax.broadcasted_iota(jnp.int32, (1, HEAD_DIM), 1)
    first_half = (lane & 63) < 32

    def row_chunks(epilogue):
        _row_chunked_matmul(h_ref, w_ref, epilogue)

    def rope_heads(rows, acc, n_heads, scale):
        cos = cos_ref[rows]
        sin = sin_ref[rows]
        for hd in range(n_heads):
            sl = slice(hd * HEAD_DIM, (hd + 1) * HEAD_DIM)
            r = _rope(acc[:, sl], cos, sin, first_half)
            if scale is not None:
                r = r * scale
            o_ref[rows, sl] = r.astype(BF16)

    @pl.when((jj >= n_gate) & (jj < n_gate + 2))
    def _():
        row_chunks(lambda rows, acc: rope_heads(rows, acc, TN // HEAD_DIM, Q_SCALE))

    @pl.when((jj >= n_gate + 2) & (jj < n_gate + 4))
    def _():
        def epilogue(rows, acc):
            o_ref[rows] = (acc * Q_SCALE).astype(BF16)
        row_chunks(epilogue)

    @pl.when((jj >= n_gate + 4) & (jj < n_gate + 8))
    def _():
        def epilogue(rows, acc):
            o_ref[rows] = acc.astype(BF16)
        row_chunks(epilogue)

    @pl.when(jj == n_gate + 8)
    def _():
        def epilogue(rows, acc):
            rope_heads(rows, acc, N_KV_A, None)
            o_ref[rows, W_KA:] = acc[:, W_KA:].astype(BF16)
        row_chunks(epilogue)


def _inproj_call(x2, sc, sh, g, w, cos, sin, *, tm, rows_per_mod, j0, nj):
    n, d = x2.shape
    pos_tiles = cos.shape[0] // tm
    mod_tiles = rows_per_mod // tm
    return pl.pallas_call(
        functools.partial(_inproj_kernel, j0=j0),
        grid=(n // tm, nj),
        in_specs=[pl.BlockSpec((tm, d), lambda i, j: (i, 0)),
                  pl.BlockSpec((1, 1, d), lambda i, j: (i // mod_tiles, 0, 0)),
                  pl.BlockSpec((1, 1, d), lambda i, j: (i // mod_tiles, 0, 0)),
                  pl.BlockSpec((1, d), lambda i, j: (0, 0)),
                  pl.BlockSpec((d, TN), lambda i, j: (0, _qkv_weight_tile(j + j0))),
                  pl.BlockSpec((tm, HEAD_DIM), lambda i, j: (i % pos_tiles, 0)),
                  pl.BlockSpec((tm, HEAD_DIM), lambda i, j: (i % pos_tiles, 0))],
        out_specs=pl.BlockSpec((tm, TN), lambda i, j: (i, j)),
        out_shape=jax.ShapeDtypeStruct((n, nj * TN), BF16),
        scratch_shapes=[pltpu.VMEM((tm, d), BF16)],
        compiler_params=pltpu.CompilerParams(dimension_semantics=("arbitrary", "arbitrary"),
                                             vmem_limit_bytes=VMEM_LIMIT_BYTES),
        name="in_proj",
    )(x2, sc, sh, g, w, cos, sin)


def _attn_a_kernel(sink_ref, q_ref, kp_ref, km_ref, kn_ref, vp_ref, vm_ref, vn_ref, kc_ref, vc_ref, o_ref,
                   *, n_blocks):
    nq = pl.program_id(1)
    sub = QB_A // BLOCK
    rows = GQA_GROUP * BLOCK
    n_loc = 3 * BLOCK
    r_io = lax.broadcasted_iota(jnp.int32, (rows, n_loc), 0)
    c_io = lax.broadcasted_iota(jnp.int32, (rows, n_loc), 1)
    qi_io = r_io & (BLOCK - 1)
    in_band = (c_io >= qi_io) & (c_io <= qi_io + 2 * WINDOW)
    c_row = lax.broadcasted_iota(jnp.int32, (1, n_loc), 1)
    g_io = lax.shift_right_logical(lax.broadcasted_iota(jnp.int32, (rows, 1), 0), BLOCK.bit_length() - 1)
    nt = (((1,), (1,)), ((), ()))
    for kvh in range(N_KV_A):
        hs = slice(kvh * HEAD_DIM, (kvh + 1) * HEAD_DIM)
        kband = jnp.concatenate([kp_ref[:, hs], km_ref[:, hs], kn_ref[:, hs]], axis=0)
        vband = jnp.concatenate([vp_ref[:, hs], vm_ref[:, hs], vn_ref[:, hs]], axis=0)
        kc = kc_ref[:, hs]
        vc = vc_ref[:, hs]
        sink = jnp.zeros((rows, 1), F32)
        for g in range(GQA_GROUP):
            sink = jnp.where(g_io == g, sink_ref[kvh * GQA_GROUP + g] * LOG2E, sink)
        for qi in range(sub):
            blk = nq * sub + qi
            lo = jnp.where(blk == 0, BLOCK, 0)
            hi = jnp.where(blk == n_blocks - 1, 2 * BLOCK - 1, n_loc - 1)
            valid = in_band & ((c_row >= lo) & (c_row <= hi))
            q4 = jnp.concatenate(
                [q_ref[qi * BLOCK:(qi + 1) * BLOCK, (kvh * GQA_GROUP + g) * HEAD_DIM:(kvh * GQA_GROUP + g + 1) * HEAD_DIM]
                 for g in range(GQA_GROUP)], axis=0)
            s_loc = lax.dot_general(q4, kband[qi * BLOCK:qi * BLOCK + n_loc], nt, preferred_element_type=F32)
            s_loc = jnp.where(valid, s_loc, NEG_INF)
            s_ctx = lax.dot_general(q4, kc, nt, preferred_element_type=F32)
            m = jnp.maximum(jnp.maximum(jnp.max(s_loc, axis=-1, keepdims=True),
                                        jnp.max(s_ctx, axis=-1, keepdims=True)), sink)
            e_loc = jnp.exp2(s_loc - m)
            e_ctx = jnp.exp2(s_ctx - m)
            den = (jnp.sum(e_loc, axis=-1, keepdims=True) + jnp.sum(e_ctx, axis=-1, keepdims=True)
                   + jnp.exp2(sink - m))
            o = (jnp.dot(e_loc.astype(BF16), vband[qi * BLOCK:qi * BLOCK + n_loc], preferred_element_type=F32)
                 + jnp.dot(e_ctx.astype(BF16), vc, preferred_element_type=F32)) * (1.0 / den)
            for g in range(GQA_GROUP):
                hd = kvh * GQA_GROUP + g
                o_ref[qi * BLOCK:(qi + 1) * BLOCK, hd * HEAD_DIM:(hd + 1) * HEAD_DIM] = (
                    o[g * BLOCK:(g + 1) * BLOCK].astype(BF16))


def _attn_a_call(proj, cproj, sink, *, batch, seq, ctx_len, q_col, k_col, v_col, ck_col, cv_col):
    nqb = seq // QB_A
    sub = QB_A // BLOCK
    n_blocks = seq // BLOCK

    def prev_map(b, n, col):
        return (b * n_blocks + jnp.maximum(n * sub - 1, 0), col // W_KA)

    def next_map(b, n, col):
        return (b * n_blocks + jnp.minimum(n * sub + sub, n_blocks - 1), col // W_KA)

    def main_map(b, n, col):
        return (b * nqb + n, col // W_KA)

    edge = lambda f, col: pl.BlockSpec((BLOCK, W_KA), functools.partial(f, col=col))
    main = lambda col: pl.BlockSpec((QB_A, W_KA), functools.partial(main_map, col=col))
    ctxs = lambda col: pl.BlockSpec((ctx_len, W_KA), lambda b, n: (b, col // W_KA))
    return pl.pallas_call(
        functools.partial(_attn_a_kernel, n_blocks=n_blocks),
        grid=(batch, nqb),
        in_specs=[pl.BlockSpec(memory_space=pltpu.SMEM),
                  pl.BlockSpec((QB_A, W_QA), lambda b, n: (b * nqb + n, q_col // W_QA)),
                  edge(prev_map, k_col), main(k_col), edge(next_map, k_col),
                  edge(prev_map, v_col), main(v_col), edge(next_map, v_col),
                  ctxs(ck_col), ctxs(cv_col)],
        out_specs=pl.BlockSpec((QB_A, W_QA), lambda b, n: (b * nqb + n, 0)),
        out_shape=jax.ShapeDtypeStruct((batch * seq, W_QA), BF16),
        compiler_params=pltpu.CompilerParams(dimension_semantics=("arbitrary",) * 2,
                                             vmem_limit_bytes=VMEM_LIMIT_BYTES),
        name="attn_a",
    )(sink, proj, proj, proj, proj, proj, proj, proj, cproj, cproj)


def _attn_b_kernel(q_ref, kp_ref, km_ref, kn_ref, vp_ref, vm_ref, vn_ref, kc_ref, vc_ref, bias_ref, o_ref,
                   *, grid_rows):
    n = pl.program_id(1)
    nq = ROW_GROUP_B * GRID_W
    nk = 3 * nq
    heads = q_ref.shape[1] // HEAD_DIM
    r_io = lax.broadcasted_iota(jnp.int32, (nq, nk), 0)
    c_io = lax.broadcasted_iota(jnp.int32, (nq, nk), 1)
    log_w = GRID_W.bit_length() - 1
    q_col = r_io & (GRID_W - 1)
    k_col = c_io & (GRID_W - 1)
    c_start = jnp.clip(q_col - NB_COLS // 2, 0, GRID_W - NB_COLS)
    col_ok = (k_col >= c_start) & (k_col < c_start + NB_COLS)
    nt = (((1,), (1,)), ((), ()))
    for grp in range(GROUPS_PER_STEP_B):
        gi = n * GROUPS_PER_STEP_B + grp
        r_abs = gi * ROW_GROUP_B + lax.shift_right_logical(r_io, log_w)
        kr_abs = (gi - 1) * ROW_GROUP_B + lax.shift_right_logical(c_io, log_w)
        r_start = jnp.clip(r_abs - NB_ROWS // 2, 0, grid_rows - NB_ROWS)
        valid = (kr_abs >= r_start) & (kr_abs < r_start + NB_ROWS) & col_ok
        cap = jnp.where(valid, F32_MAX, NEG_INF)
        rows = slice(grp * nq, (grp + 1) * nq)
        for hd in range(heads):
            sl = slice(hd * HEAD_DIM, (hd + 1) * HEAD_DIM)
            q = q_ref[rows, sl]
            keys = jnp.concatenate([kp_ref[:, sl], km_ref[:, sl], kn_ref[:, sl]], axis=0)[grp * nq:grp * nq + nk]
            vals = jnp.concatenate([vp_ref[:, sl], vm_ref[:, sl], vn_ref[:, sl]], axis=0)[grp * nq:grp * nq + nk]
            s_nb = lax.dot_general(q, keys, nt, preferred_element_type=F32)
            s_nb = jnp.minimum(s_nb + bias_ref[hd], cap)
            s_ctx = lax.dot_general(q, kc_ref[:, sl], nt, preferred_element_type=F32)
            m = jnp.maximum(jnp.max(s_nb, axis=-1, keepdims=True), jnp.max(s_ctx, axis=-1, keepdims=True))
            e_nb = jnp.exp2(s_nb - m)
            e_ctx = jnp.exp2(s_ctx - m)
            den = jnp.sum(e_nb, axis=-1, keepdims=True) + jnp.sum(e_ctx, axis=-1, keepdims=True)
            o = (jnp.dot(e_nb.astype(BF16), vals, preferred_element_type=F32)
                 + jnp.dot(e_ctx.astype(BF16), vc_ref[:, sl], preferred_element_type=F32)) * (1.0 / den)
            o_ref[rows, sl] = o.astype(BF16)


def _attn_b_call(proj, cproj, bias, *, batch, seq, ctx_len, q_col, k_col, v_col, ck_col, cv_col):
    nq = ROW_GROUP_B * GRID_W
    ng = seq // nq
    gps = GROUPS_PER_STEP_B
    ns = ng // gps
    hw = W_B

    def edge(col, first):
        def imap(b, n):
            g = jnp.maximum(n * gps - 1, 0) if first else jnp.minimum(n * gps + gps, ng - 1)
            return (b * ng + g, col // hw)
        return pl.BlockSpec((nq, hw), imap)

    main = lambda col: pl.BlockSpec((gps * nq, hw), lambda b, n: (b * ns + n, col // hw))
    ctxs = lambda col: pl.BlockSpec((ctx_len, hw), lambda b, n: (b, col // hw))
    return pl.pallas_call(
        functools.partial(_attn_b_kernel, grid_rows=seq // GRID_W),
        grid=(batch, ns),
        in_specs=[main(q_col),
                  edge(k_col, True), main(k_col), edge(k_col, False),
                  edge(v_col, True), main(v_col), edge(v_col, False),
                  ctxs(ck_col), ctxs(cv_col),
                  pl.BlockSpec((hw // HEAD_DIM, nq, bias.shape[2]), lambda b, n: (0, 0, 0))],
        out_specs=pl.BlockSpec((gps * nq, hw), lambda b, n: (b * ns + n, 0)),
        out_shape=jax.ShapeDtypeStruct((batch * seq, W_B), BF16),
        compiler_params=pltpu.CompilerParams(dimension_semantics=("arbitrary",) * 2,
                                             vmem_limit_bytes=VMEM_LIMIT_BYTES),
        name="attn_b",
    )(proj, proj, proj, proj, proj, proj, proj, cproj, cproj, bias)


def _resident_bf16(w_ref, wb_ref):
    @pl.when(pl.program_id(0) == 0)
    def _():
        wb_ref[...] = w_ref[...].astype(BF16)


def _mix_kernel(oa_ref, ob_ref, ga_ref, gb_ref, wa_ref, wb_ref, z_ref, wa_bf, wb_bf):
    _resident_bf16(wa_ref, wa_bf)
    _resident_bf16(wb_ref, wb_bf)
    ya = jnp.dot(oa_ref[...], wa_bf[...], preferred_element_type=F32)
    yb = jnp.dot(ob_ref[...], wb_bf[...], preferred_element_type=F32)
    z_ref[...] = (ga_ref[...].astype(F32) * ya + gb_ref[...].astype(F32) * yb).astype(BF16)


def _mix_call(oa, ob, proj, wa, wb, *, tm):
    n = oa.shape[0]
    d = wa.shape[1]
    const = lambda shape: pl.BlockSpec(shape, lambda i: (0, 0), pipeline_mode=pl.Buffered(1))
    return pl.pallas_call(
        _mix_kernel,
        grid=(n // tm,),
        in_specs=[pl.BlockSpec((tm, oa.shape[1]), lambda i: (i, 0)),
                  pl.BlockSpec((tm, ob.shape[1]), lambda i: (i, 0)),
                  pl.BlockSpec((tm, d), lambda i: (i, 0)),
                  pl.BlockSpec((tm, d), lambda i: (i, 1)),
                  const(wa.shape), const(wb.shape)],
        out_specs=pl.BlockSpec((tm, d), lambda i: (i, 0)),
        out_shape=jax.ShapeDtypeStruct((n, d), BF16),
        scratch_shapes=[pltpu.VMEM(wa.shape, BF16), pltpu.VMEM(wb.shape, BF16)],
        compiler_params=pltpu.CompilerParams(dimension_semantics=("arbitrary",),
                                             vmem_limit_bytes=VMEM_LIMIT_BYTES),
        name="branch_mix",
    )(oa, ob, proj, proj, wa, wb)


def _outnorm_kernel(a_ref, w_ref, x_ref, gt_ref, g_ref, o_ref, w_bf):
    _resident_bf16(w_ref, w_bf)
    y = jnp.dot(a_ref[...], w_bf[...], preferred_element_type=F32)
    o_ref[...] = x_ref[...] + gt_ref[0] * _rms(y, g_ref[...])


def _outnorm_call(a, w, x2, gt, g, *, tm, rows_per_mod):
    n, kdim = a.shape
    d = w.shape[1]
    mod_tiles = rows_per_mod // tm
    return pl.pallas_call(
        _outnorm_kernel,
        grid=(n // tm,),
        in_specs=[pl.BlockSpec((tm, kdim), lambda i: (i, 0)),
                  pl.BlockSpec((kdim, d), lambda i: (0, 0), pipeline_mode=pl.Buffered(1)),
                  pl.BlockSpec((tm, d), lambda i: (i, 0)),
                  pl.BlockSpec((1, 1, d), lambda i: (i // mod_tiles, 0, 0)),
                  pl.BlockSpec((1, d), lambda i: (0, 0))],
        out_specs=pl.BlockSpec((tm, d), lambda i: (i, 0)),
        out_shape=jax.ShapeDtypeStruct((n, d), F32),
        scratch_shapes=[pltpu.VMEM(w.shape, BF16)],
        compiler_params=pltpu.CompilerParams(dimension_semantics=("arbitrary",),
                                             vmem_limit_bytes=VMEM_LIMIT_BYTES),
        name="out_norm",
    )(a, w, x2, gt, g)


def _rowmm_norm_kernel(a_ref, w_ref, x_ref, gt_ref, g_ref, o_ref, *, n_chunks):
    cm = a_ref.shape[0] // n_chunks
    y = jnp.dot(a_ref[0:cm], w_ref[...], preferred_element_type=F32)
    for c in range(n_chunks):
        nxt = None
        if c + 1 < n_chunks:
            nxt = jnp.dot(a_ref[(c + 1) * cm:(c + 2) * cm], w_ref[...], preferred_element_type=F32)
        rows = slice(c * cm, (c + 1) * cm)
        o_ref[rows] = x_ref[rows] + gt_ref[0] * _rms(y, g_ref[...])
        y = nxt


def _rowmm_norm_call(a, w, x2, gt, g, *, tm, n_chunks, rows_per_mod):
    n, kdim = a.shape
    d = w.shape[1]
    mod_tiles = rows_per_mod // tm
    return pl.pallas_call(
        functools.partial(_rowmm_norm_kernel, n_chunks=n_chunks),
        grid=(n // tm,),
        in_specs=[pl.BlockSpec((tm, kdim), lambda i: (i, 0)),
                  pl.BlockSpec((kdim, d), lambda i: (0, 0), pipeline_mode=pl.Buffered(1)),
                  pl.BlockSpec((tm, d), lambda i: (i, 0)),
                  pl.BlockSpec((1, 1, d), lambda i: (i // mod_tiles, 0, 0)),
                  pl.BlockSpec((1, d), lambda i: (0, 0))],
        out_specs=pl.BlockSpec((tm, d), lambda i: (i, 0)),
        out_shape=jax.ShapeDtypeStruct((n, d), F32),
        compiler_params=pltpu.CompilerParams(dimension_semantics=("arbitrary",),
                                             vmem_limit_bytes=VMEM_LIMIT_RESIDENT_BYTES),
        name="rowmm_norm",
    )(a, w, x2, gt, g)


def _ffn_up_kernel(x0p_ref, x0_ref, x0n_ref, sc0_ref, sh0_ref, xp_ref, xm_ref, xn_ref, sc_ref, sh_ref, g_ref,
                   wa_ref, wg_ref, cwa_ref, cwg_ref, cba_ref, cbg_ref,
                   o_ref, h_cur, h_nxt, ua_ref, ug_ref, *, nj, n_tiles, tiles_per_seq, n_pieces):
    t = pl.program_id(0)
    tm = x0_ref.shape[0]
    halo = xp_ref.shape[0]
    pr = xm_ref.shape[0]
    ext = tm + 2 * halo
    margin = 8

    def mod(x, sc, sh):
        return (_rms(x, g_ref[...]) * (1 + sc[0]) + sh[0]).astype(BF16)

    @pl.when(t == 0)
    def _():
        ua_ref[...] = jnp.zeros_like(ua_ref)
        ug_ref[...] = jnp.zeros_like(ug_ref)
        h_cur[0:halo] = mod(x0p_ref[...], sc0_ref, sh0_ref)
        h_cur[halo:halo + tm] = mod(x0_ref[...], sc0_ref, sh0_ref)
        h_cur[halo + tm:ext] = mod(x0n_ref[...], sc0_ref, sh0_ref)

    tc = jnp.minimum(t, n_tiles - 1)

    @pl.when((t > 0) & (t < n_tiles) & (t % nj == 0))
    def _():
        h_cur[...] = h_nxt[...]

    def next_row_piece():
        p = jnp.minimum(tc % nj, n_pieces - 1)
        h_nxt[0:halo] = mod(xp_ref[...], sc_ref, sh_ref)
        h_nxt[pl.ds(pl.multiple_of(halo + p * pr, BF16_ROWS), pr), :] = mod(xm_ref[...], sc_ref, sh_ref)
        h_nxt[halo + tm:ext] = mod(xn_ref[...], sc_ref, sh_ref)

    slot = t % 2
    prev = 1 - slot
    ip = jnp.maximum(t - 1, 0) // nj
    first_row = jnp.where((ip % tiles_per_seq) == 0, 0, -1)
    last_row = jnp.where((ip % tiles_per_seq) == tiles_per_seq - 1, tm - 1, -1)
    row = lax.broadcasted_iota(jnp.int32, (tm, 1), 0)

    def epilogue(lo, hi):
        cm = hi - lo
        rowc = row[lo:hi]

        def conv(u_ref, cw_ref, cb_ref):
            win = u_ref[prev, pl.ds(halo + lo - margin, cm + 2 * margin), :]
            up = pltpu.roll(win, 1, 0)[margin:margin + cm]
            un = pltpu.roll(win, cm + 2 * margin - 1, 0)[margin:margin + cm]
            up = jnp.where(rowc == first_row, 0.0, up)
            un = jnp.where(rowc == last_row, 0.0, un)
            out = cb_ref[...] + up * cw_ref[0:1]
            out = out + win[margin:margin + cm] * cw_ref[1:2]
            return out + un * cw_ref[2:3]

        a = conv(ua_ref, cwa_ref, cba_ref)
        gte = conv(ug_ref, cwg_ref, cbg_ref)
        o_ref[lo:hi] = (jax.nn.silu(gte) * a).astype(BF16)

    half = ext // 2
    third = (tm // 3) // BF16_ROWS * BF16_ROWS
    chunks = [(0, third), (third, 2 * third), (2 * third, tm)]
    ua_ref[slot, 0:half] = jnp.dot(h_cur[0:half], wa_ref[...], preferred_element_type=F32)
    next_row_piece()
    epilogue(*chunks[0])
    ua_ref[slot, half:ext] = jnp.dot(h_cur[half:ext], wa_ref[...], preferred_element_type=F32)
    epilogue(*chunks[1])
    ug_ref[slot, 0:half] = jnp.dot(h_cur[0:half], wg_ref[...], preferred_element_type=F32)
    epilogue(*chunks[2])
    ug_ref[slot, half:ext] = jnp.dot(h_cur[half:ext], wg_ref[...], preferred_element_type=F32)


def _ffn_up_call(x2, sc, sh, g, w_up, conv_w, conv_b, *, tm, tn, seq):
    n, d = x2.shape
    dff = w_up.shape[1] // 2
    nj = dff // tn
    halo = BF16_ROWS
    hb = tm // halo
    n_halo = n // halo
    tiles_per_seq = seq // tm
    n_rows = n // tm
    n_tiles = n_rows * nj
    pr = 128
    n_pieces = tm // pr
    assert (tm + 2 * halo) % (2 * BF16_ROWS) == 0 and n_pieces <= nj and n_pieces * pr == tm

    def cur(t):
        tc = jnp.minimum(t, n_tiles - 1)
        return tc // nj, tc % nj

    def prv(t):
        tp = jnp.maximum(t - 1, 0)
        return tp // nj, tp % nj

    def nxt(t):
        return jnp.minimum(cur(t)[0] + 1, n_rows - 1)

    once = lambda shape, imap: pl.BlockSpec(shape, imap, pipeline_mode=pl.Buffered(1))
    return pl.pallas_call(
        functools.partial(_ffn_up_kernel, nj=nj, n_tiles=n_tiles, tiles_per_seq=tiles_per_seq, n_pieces=n_pieces),
        grid=(n_tiles + 1,),
        in_specs=[once((halo, d), lambda t: (0, 0)),
                  once((tm, d), lambda t: (0, 0)),
                  once((halo, d), lambda t: (jnp.minimum(hb, n_halo - 1), 0)),
                  once((1, 1, d), lambda t: (0, 0, 0)),
                  once((1, 1, d), lambda t: (0, 0, 0)),
                  pl.BlockSpec((halo, d), lambda t: (jnp.maximum(nxt(t) * hb - 1, 0), 0)),
                  pl.BlockSpec((pr, d), lambda t: (nxt(t) * n_pieces + jnp.minimum(cur(t)[1], n_pieces - 1), 0)),
                  pl.BlockSpec((halo, d), lambda t: (jnp.minimum((nxt(t) + 1) * hb, n_halo - 1), 0)),
                  pl.BlockSpec((1, 1, d), lambda t: (nxt(t) // tiles_per_seq, 0, 0)),
                  pl.BlockSpec((1, 1, d), lambda t: (nxt(t) // tiles_per_seq, 0, 0)),
                  pl.BlockSpec((1, d), lambda t: (0, 0)),
                  pl.BlockSpec((d, tn), lambda t: (0, cur(t)[1])),
                  pl.BlockSpec((d, tn), lambda t: (0, cur(t)[1] + nj)),
                  pl.BlockSpec((CONV_W, tn), lambda t: (0, prv(t)[1])),
                  pl.BlockSpec((CONV_W, tn), lambda t: (0, prv(t)[1] + nj)),
                  pl.BlockSpec((1, tn), lambda t: (0, prv(t)[1])),
                  pl.BlockSpec((1, tn), lambda t: (0, prv(t)[1] + nj))],
        out_specs=pl.BlockSpec((tm, tn), lambda t: prv(t)),
        out_shape=jax.ShapeDtypeStruct((n, dff), BF16),
        scratch_shapes=[pltpu.VMEM((tm + 2 * halo, d), BF16),
                        pltpu.VMEM((tm + 2 * halo, d), BF16),
                        pltpu.VMEM((2, tm + 2 * halo, tn), F32),
                        pltpu.VMEM((2, tm + 2 * halo, tn), F32)],
        compiler_params=pltpu.CompilerParams(dimension_semantics=("arbitrary",),
                                             vmem_limit_bytes=VMEM_LIMIT_BYTES),
        name="ffn_up",
    )(x2, x2, x2, sc, sh, x2, x2, x2, sc, sh, g, w_up, w_up, conv_w, conv_w, conv_b, conv_b)


def _rope_tables(seq):
    t = np.arange(seq)
    half = HEAD_DIM // 4
    inv = np.float32(ROPE_BASE) ** (-np.arange(half, dtype=np.float32) / np.float32(half))

    def cs(pos):
        ang = pos.astype(np.float32)[:, None] * inv[None, :]
        c, s = np.cos(ang), np.sin(ang)
        return np.concatenate([c, c], axis=1), np.concatenate([-s, s], axis=1)

    cr, sr = cs(t // GRID_W)
    cc, sn = cs(t % GRID_W)
    return (jnp.asarray(np.concatenate([cr, cc], axis=1), F32), jnp.asarray(np.concatenate([sr, sn], axis=1), F32))


def _pick_tile(n, pref):
    while n % pref:
        pref //= 2
    return pref


class _RowTiles(NamedTuple):
    proj: int
    qkv: int
    mix: int
    down: int


def _row_tiles(seq):
    return _RowTiles(*(_pick_tile(seq, pref) for pref in (1024, 4096, 512, 256)))


def kernel(x, c, ctx, c_ctx, w_mod, b_mod, g_attn_pre, g_attn_post, g_ffn_pre, g_ffn_post, w_in, sink_a, rpb_b,
           w_br_a, w_br_b, w_o, w_up, conv_w, conv_b, w_down):
    batch, seq, d = x.shape
    ctx_len = ctx.shape[1]
    assert w_mod.shape[0] == 1 and d == 16 * HEAD_DIM
    assert seq % QB_A == 0 and seq % (GROUPS_PER_STEP_B * ROW_GROUP_B * GRID_W) == 0 and WINDOW == BLOCK
    n = batch * seq
    x2 = x.reshape(n, d)

    pad = (-(batch + 1)) % 8
    cs = jnp.concatenate([c, c_ctx[None, :], jnp.zeros((pad, d), F32)], axis=0)
    mod = _mod_call(cs, w_mod[0], b_mod[0][None, :])
    sh1, sc1, gt1, sh2, sc2, gt2 = [mod[:batch, k * d:(k + 1) * d].reshape(batch, 1, d) for k in range(6)]
    csh1 = mod[batch:batch + 1, 0:d].reshape(1, 1, d)
    csc1 = mod[batch:batch + 1, d:2 * d].reshape(1, 1, d)

    n_qkv_cols = W_QA + 2 * W_KA + 3 * W_B
    n_gate_cols = 2 * d
    w_in_bf = w_in[0].astype(BF16)
    n_tiles = n_qkv_cols // TN
    c_qa = 0
    c_qb = c_qa + W_QA
    c_kb = c_qb + W_B
    c_vb = c_kb + W_B
    c_ka = c_vb + W_B
    c_va = c_ka + W_KA
    cos, sin = _rope_tables(seq)
    tiles = _row_tiles(seq)
    gates, h1 = _gate_call(x2, sc1, sh1, g_attn_pre, w_in_bf, tm=tiles.proj, tn=TN_GATE, col0=n_qkv_cols,
                           n_cols=n_gate_cols, rows_per_mod=seq)
    proj = _qkv_call(h1, w_in_bf, cos, sin, tm=tiles.qkv, j0=0, nj=n_tiles)
    nc = batch * ctx_len
    j0c = c_kb // TN
    cproj = _inproj_call(ctx.reshape(nc, d), csc1, csh1, g_attn_pre, w_in_bf,
                         jnp.ones((nc, HEAD_DIM), F32), jnp.zeros((nc, HEAD_DIM), F32),
                         tm=nc, rows_per_mod=nc, j0=j0c, nj=n_tiles - j0c)
    cc_kb, cc_vb, cc_ka, cc_va = 0, W_B, 2 * W_B, 2 * W_B + W_KA

    oa = _attn_a_call(proj, cproj, sink_a[0], batch=batch, seq=seq, ctx_len=ctx_len,
                      q_col=c_qa, k_col=c_ka, v_col=c_va, ck_col=cc_ka, cv_col=cc_va)
    bias = _rpb_call(rpb_b[0])
    ob = _attn_b_call(proj, cproj, bias, batch=batch, seq=seq, ctx_len=ctx_len,
                      q_col=c_qb, k_col=c_kb, v_col=c_vb, ck_col=cc_kb, cv_col=cc_vb)

    z = _mix_call(oa, ob, gates, w_br_a[0], w_br_b[0], tm=tiles.mix)
    x1 = _outnorm_call(z, w_o[0], x2, gt1, g_attn_post, tm=tiles.mix, rows_per_mod=seq)

    act = _ffn_up_call(x1, sc2, sh2, g_ffn_pre, w_up[0].astype(BF16), conv_w[0], conv_b[0][None, :],
                       tm=tiles.proj, tn=TN, seq=seq)
    out = _rowmm_norm_call(act, w_down[0].astype(BF16), x1, gt2, g_ffn_post, tm=tiles.down, n_chunks=2,
                           rows_per_mod=seq)
    return out.reshape(batch, seq, d)
```

```python
import functools
from typing import NamedTuple

import numpy as np
import jax
import jax.numpy as jnp
from jax import lax
from jax.experimental import pallas as pl
from jax.experimental.pallas import tpu as pltpu

F32 = jnp.float32
BF16 = jnp.bfloat16

GRID_W = 64
HEAD_DIM = 128
N_HEADS_A = 8
N_KV_A = 2
GQA_GROUP = N_HEADS_A // N_KV_A
WINDOW = 128
BLOCK = 128
N_HEADS_B = 8
NB_ROWS = 8
NB_COLS = 16
CONV_W = 3
ROPE_BASE = 10000.0
EPS = 1e-6
NEG_INF = -1e30
ATTN_SCALE = HEAD_DIM ** -0.5
LOG2E = 1.4426950408889634
Q_SCALE = ATTN_SCALE * LOG2E
F32_MAX = 3.4028234663852886e38

W_QA = N_HEADS_A * HEAD_DIM
W_KA = N_KV_A * HEAD_DIM
W_B = N_HEADS_B * HEAD_DIM

LANES = 128
BF16_ROWS = 16
VMEM_LIMIT_BYTES = 56 * 1024 * 1024
VMEM_LIMIT_RESIDENT_BYTES = 60 * 1024 * 1024

TN = 512
TN_GATE = 1024
ROW_CHUNK = 256
ROW_GROUP_B = 4
GROUPS_PER_STEP_B = 2
QB_A = 1024


def _rms(x, g):
    return (x * lax.rsqrt(jnp.mean(x * x, axis=-1, keepdims=True) + EPS)) * g


def _mod_kernel(c_ref, w_ref, b_ref, o_ref):
    a = jax.nn.silu(c_ref[...]).astype(BF16)
    o_ref[...] = jnp.dot(a, w_ref[...].astype(BF16), preferred_element_type=F32) + b_ref[...]


def _mod_call(cs, w, b):
    m, d = cs.shape
    n = w.shape[1]
    tn = 1536 if n % 1536 == 0 else n
    return pl.pallas_call(
        _mod_kernel,
        grid=(n // tn,),
        in_specs=[pl.BlockSpec((m, d), lambda j: (0, 0)),
                  pl.BlockSpec((d, tn), lambda j: (0, j)),
                  pl.BlockSpec((1, tn), lambda j: (0, j))],
        out_specs=pl.BlockSpec((m, tn), lambda j: (0, j)),
        out_shape=jax.ShapeDtypeStruct((m, n), F32),
        compiler_params=pltpu.CompilerParams(dimension_semantics=("arbitrary",),
                                             vmem_limit_bytes=VMEM_LIMIT_BYTES),
        name="mod",
    )(cs, w, b)


def _rpb_kernel(rpb_ref, o_ref):
    n_dr = 2 * NB_ROWS - 1
    n_dc = 2 * NB_COLS - 1
    r = rpb_ref[0] * LOG2E
    m = lax.broadcasted_iota(jnp.int32, r.shape, 1)

    def dc_index(delta):
        return jnp.clip(delta, -(NB_COLS - 1), NB_COLS - 1) + (NB_COLS - 1)

    idx_lo = dc_index(jnp.where(m < GRID_W, m, m - LANES))
    idx_hi = dc_index(m - GRID_W)
    base_lo = jnp.zeros(r.shape, F32)
    base_hi = jnp.zeros(r.shape, F32)
    for dc in range(n_dc):
        col = jnp.broadcast_to(r[:, dc:dc + 1], r.shape)
        base_lo = jnp.where(idx_lo == dc, col, base_lo)
        base_hi = jnp.where(idx_hi == dc, col, base_hi)

    lane = lax.broadcasted_iota(jnp.int32, (GRID_W, LANES), 1)

    def rotated(base, dr):
        return pltpu.roll(jnp.broadcast_to(base[dr:dr + 1], (GRID_W, LANES)), 0, 1, stride=1, stride_axis=0)

    pair = [jnp.where(lane < GRID_W, rotated(base_lo, d), rotated(base_hi, d + 1)) for d in range(n_dr - 1)]
    n_kr = 3 * ROW_GROUP_B
    dr0 = NB_ROWS - 1 - ROW_GROUP_B
    for rq in range(ROW_GROUP_B):
        for p in range(n_kr // 2):
            o_ref[0, rq * GRID_W:(rq + 1) * GRID_W, p * LANES:(p + 1) * LANES] = pair[2 * p - rq + dr0]


def _rpb_call(rpb):
    nh = rpb.shape[0]
    nq = ROW_GROUP_B * GRID_W
    nk = 3 * ROW_GROUP_B * GRID_W
    assert 2 * ROW_GROUP_B == NB_ROWS and 2 * GRID_W == LANES
    n_dr, n_dc = rpb.shape[1], rpb.shape[2]
    rows = -(-n_dr // 8) * 8
    rpb_tiles = jnp.pad(rpb, ((0, 0), (0, rows - n_dr), (0, LANES - n_dc)))
    return pl.pallas_call(
        _rpb_kernel,
        grid=(nh,),
        in_specs=[pl.BlockSpec((1, rows, LANES), lambda h: (h, 0, 0))],
        out_specs=pl.BlockSpec((1, nq, nk), lambda h: (h, 0, 0)),
        out_shape=jax.ShapeDtypeStruct((nh, nq, nk), F32),
        compiler_params=pltpu.CompilerParams(dimension_semantics=("arbitrary",)),
        name="rpb_table",
    )(rpb_tiles)


def _rope(t, cos, sin, first_half):
    partner = jnp.where(first_half, pltpu.roll(t, LANES - 32, 1), pltpu.roll(t, 32, 1))
    return t * cos + partner * sin


def _row_chunked_matmul(h_ref, w_ref, epilogue, between=None):
    n_chunks = h_ref.shape[0] // ROW_CHUNK

    def dots(c):
        h = h_ref[c * ROW_CHUNK:(c + 1) * ROW_CHUNK]
        if isinstance(w_ref, tuple):
            return tuple(jnp.dot(h, w[...], preferred_element_type=F32) for w in w_ref)
        return jnp.dot(h, w_ref[...], preferred_element_type=F32)

    acc = dots(0)
    for c in range(n_chunks):
        nxt = dots(c + 1) if c + 1 < n_chunks else None
        epilogue(slice(c * ROW_CHUNK, (c + 1) * ROW_CHUNK), acc)
        if c == 0 and between is not None:
            between()
        acc = nxt


def _gate_kernel(x_ref, sc_ref, sh_ref, g_ref, w0_ref, w1_ref, o_ref, hout_ref, hs_ref, *, n_rows):
    i = pl.program_id(0)
    j = pl.program_id(1)
    pr = x_ref.shape[0]
    tn = w0_ref.shape[1]

    def norm_piece():
        h = (_rms(x_ref[...], g_ref[...]) * (1 + sc_ref[0]) + sh_ref[0]).astype(BF16)
        hout_ref[...] = h
        hs_ref[i % 2, pl.ds(pl.multiple_of(j * pr, pr), pr), :] = h

    def epilogue(rows, accs):
        for k, acc in enumerate(accs):
            o_ref[rows, k * tn:(k + 1) * tn] = jax.nn.sigmoid(acc).astype(BF16)

    @pl.when(i == 0)
    def _():
        norm_piece()

    @pl.when((i > 0) & (i < n_rows))
    def _():
        _row_chunked_matmul(hs_ref.at[(i + 1) % 2], (w0_ref, w1_ref), epilogue, between=norm_piece)

    @pl.when(i == n_rows)
    def _():
        _row_chunked_matmul(hs_ref.at[(i + 1) % 2], (w0_ref, w1_ref), epilogue)


def _gate_call(x2, sc, sh, g, w, *, tm, tn, col0, n_cols, rows_per_mod):
    n, d = x2.shape
    n_rows = n // tm
    nj = n_cols // tn
    pr = tm // nj
    assert pr * nj == tm and pr % BF16_ROWS == 0 and tn == 2 * TN and col0 % TN == 0
    mod_tiles = rows_per_mod // tm
    last = n_rows * nj - 1
    piece_map = lambda i, j: (jnp.minimum(i * nj + j, last), 0)
    mod_map = lambda i, j: (jnp.minimum(i, n_rows - 1) // mod_tiles, 0, 0)
    return pl.pallas_call(
        functools.partial(_gate_kernel, n_rows=n_rows),
        grid=(n_rows + 1, nj),
        in_specs=[pl.BlockSpec((pr, d), piece_map),
                  pl.BlockSpec((1, 1, d), mod_map),
                  pl.BlockSpec((1, 1, d), mod_map),
                  pl.BlockSpec((1, d), lambda i, j: (0, 0)),
                  pl.BlockSpec((d, TN), lambda i, j: (0, col0 // TN + 2 * j)),
                  pl.BlockSpec((d, TN), lambda i, j: (0, col0 // TN + 2 * j + 1))],
        out_specs=[pl.BlockSpec((tm, tn), lambda i, j: (jnp.maximum(i - 1, 0), jnp.where(i > 0, j, 0))),
                   pl.BlockSpec((pr, d), piece_map)],
        out_shape=[jax.ShapeDtypeStruct((n, n_cols), BF16), jax.ShapeDtypeStruct((n, d), BF16)],
        scratch_shapes=[pltpu.VMEM((2, tm, d), BF16)],
        compiler_params=pltpu.CompilerParams(dimension_semantics=("arbitrary", "arbitrary"),
                                             vmem_limit_bytes=VMEM_LIMIT_BYTES),
        name="gate_proj",
    )(x2, sc, sh, g, w, w)


def _qkv_weight_tile(jc):
    n_qa = W_QA // TN
    last = (W_QA + 3 * W_B) // TN
    return jnp.where(jc < n_qa, jc, jnp.where(jc == last, n_qa, jc + 1))


def _qkv_kernel(h_ref, w_ref, cos_ref, sin_ref, o_ref, *, j0):
    _proj_tile(h_ref, w_ref, cos_ref, sin_ref, o_ref, pl.program_id(1) + j0, 0)


def _qkv_call(h, w, cos, sin, *, tm, j0, nj):
    n, d = h.shape
    pos_tiles = cos.shape[0] // tm
    return pl.pallas_call(
        functools.partial(_qkv_kernel, j0=j0),
        grid=(n // tm, nj),
        in_specs=[pl.BlockSpec((tm, d), lambda i, j: (i, 0)),
                  pl.BlockSpec((d, TN), lambda i, j: (0, _qkv_weight_tile(j + j0))),
                  pl.BlockSpec((tm, HEAD_DIM), lambda i, j: (i % pos_tiles, 0)),
                  pl.BlockSpec((tm, HEAD_DIM), lambda i, j: (i % pos_tiles, 0))],
        out_specs=pl.BlockSpec((tm, TN), lambda i, j: (i, j)),
        out_shape=jax.ShapeDtypeStruct((n, nj * TN), BF16),
        compiler_params=pltpu.CompilerParams(dimension_semantics=("arbitrary", "arbitrary"),
                                             vmem_limit_bytes=VMEM_LIMIT_BYTES),
        name="qkv_proj",
    )(h, w, cos, sin)


def _inproj_kernel(x_ref, sc_ref, sh_ref, g_ref, w_ref, cos_ref, sin_ref, o_ref, h_ref, *, j0):
    j = pl.program_id(1)

    @pl.when(j == 0)
    def _():
        h = _rms(x_ref[...], g_ref[...]) * (1 + sc_ref[0]) + sh_ref[0]
        h_ref[...] = h.astype(BF16)

    _proj_tile(h_ref, w_ref, cos_ref, sin_ref, o_ref, j + j0, 0)


def _proj_tile(h_ref, w_ref, cos_ref, sin_ref, o_ref, jj, n_gate):
    lane = lax.broadcasted_iota(jnp.int32, (1, HEAD_DIM), 1)
    first_half = (lane & 63) < 32

    def row_chunks(epilogue):
        _row_chunked_matmul(h_ref, w_ref, epilogue)

    def rope_heads(rows, acc, n_heads, scale):
        cos = cos_ref[rows]
        sin = sin_ref[rows]
        for hd in range(n_heads):
            sl = slice(hd * HEAD_DIM, (hd + 1) * HEAD_DIM)
            r = _rope(acc[:, sl], cos, sin, first_half)
            if scale is not None:
                r = r * scale
            o_ref[rows, sl] = r.astype(BF16)

    @pl.when((jj >= n_gate) & (jj < n_gate + 2))
    def _():
        row_chunks(lambda rows, acc: rope_heads(rows, acc, TN // HEAD_DIM, Q_SCALE))

    @pl.when((jj >= n_gate + 2) & (jj < n_gate + 4))
    def _():
        def epilogue(rows, acc):
            o_ref[rows] = (acc * Q_SCALE).astype(BF16)
        row_chunks(epilogue)

    @pl.when((jj >= n_gate + 4) & (jj < n_gate + 8))
    def _():
        def epilogue(rows, acc):
            o_ref[rows] = acc.astype(BF16)
        row_chunks(epilogue)

    @pl.when(jj == n_gate + 8)
    def _():
        def epilogue(rows, acc):
            rope_heads(rows, acc, N_KV_A, None)
            o_ref[rows, W_KA:] = acc[:, W_KA:].astype(BF16)
        row_chunks(epilogue)


def _inproj_call(x2, sc, sh, g, w, cos, sin, *, tm, rows_per_mod, j0, nj):
    n, d = x2.shape
    pos_tiles = cos.shape[0] // tm
    mod_tiles = rows_per_mod // tm
    return pl.pallas_call(
        functools.partial(_inproj_kernel, j0=j0),
        grid=(n // tm, nj),
        in_specs=[pl.BlockSpec((tm, d), lambda i, j: (i, 0)),
                  pl.BlockSpec((1, 1, d), lambda i, j: (i // mod_tiles, 0, 0)),
                  pl.BlockSpec((1, 1, d), lambda i, j: (i // mod_tiles, 0, 0)),
                  pl.BlockSpec((1, d), lambda i, j: (0, 0)),
                  pl.BlockSpec((d, TN), lambda i, j: (0, _qkv_weight_tile(j + j0))),
                  pl.BlockSpec((tm, HEAD_DIM), lambda i, j: (i % pos_tiles, 0)),
                  pl.BlockSpec((tm, HEAD_DIM), lambda i, j: (i % pos_tiles, 0))],
        out_specs=pl.BlockSpec((tm, TN), lambda i, j: (i, j)),
        out_shape=jax.ShapeDtypeStruct((n, nj * TN), BF16),
        scratch_shapes=[pltpu.VMEM((tm, d), BF16)],
        compiler_params=pltpu.CompilerParams(dimension_semantics=("arbitrary", "arbitrary"),
                                             vmem_limit_bytes=VMEM_LIMIT_BYTES),
        name="in_proj",
    )(x2, sc, sh, g, w, cos, sin)


def _attn_a_kernel(sink_ref, q_ref, kp_ref, km_ref, kn_ref, vp_ref, vm_ref, vn_ref, kc_ref, vc_ref, o_ref,
                   *, n_blocks):
    nq = pl.program_id(1)
    sub = QB_A // BLOCK
    rows = GQA_GROUP * BLOCK
    n_loc = 3 * BLOCK
    r_io = lax.broadcasted_iota(jnp.int32, (rows, n_loc), 0)
    c_io = lax.broadcasted_iota(jnp.int32, (rows, n_loc), 1)
    qi_io = r_io & (BLOCK - 1)
    in_band = (c_io >= qi_io) & (c_io <= qi_io + 2 * WINDOW)
    c_row = lax.broadcasted_iota(jnp.int32, (1, n_loc), 1)
    g_io = lax.shift_right_logical(lax.broadcasted_iota(jnp.int32, (rows, 1), 0), BLOCK.bit_length() - 1)
    nt = (((1,), (1,)), ((), ()))
    for kvh in range(N_KV_A):
        hs = slice(kvh * HEAD_DIM, (kvh + 1) * HEAD_DIM)
        kband = jnp.concatenate([kp_ref[:, hs], km_ref[:, hs], kn_ref[:, hs]], axis=0)
        vband = jnp.concatenate([vp_ref[:, hs], vm_ref[:, hs], vn_ref[:, hs]], axis=0)
        kc = kc_ref[:, hs]
        vc = vc_ref[:, hs]
        sink = jnp.zeros((rows, 1), F32)
        for g in range(GQA_GROUP):
            sink = jnp.where(g_io == g, sink_ref[kvh * GQA_GROUP + g] * LOG2E, sink)
        for qi in range(sub):
            blk = nq * sub + qi
            lo = jnp.where(blk == 0, BLOCK, 0)
            hi = jnp.where(blk == n_blocks - 1, 2 * BLOCK - 1, n_loc - 1)
            valid = in_band & ((c_row >= lo) & (c_row <= hi))
            q4 = jnp.concatenate(
                [q_ref[qi * BLOCK:(qi + 1) * BLOCK, (kvh * GQA_GROUP + g) * HEAD_DIM:(kvh * GQA_GROUP + g + 1) * HEAD_DIM]
                 for g in range(GQA_GROUP)], axis=0)
            s_loc = lax.dot_general(q4, kband[qi * BLOCK:qi * BLOCK + n_loc], nt, preferred_element_type=F32)
            s_loc = jnp.where(valid, s_loc, NEG_INF)
            s_ctx = lax.dot_general(q4, kc, nt, preferred_element_type=F32)
            m = jnp.maximum(jnp.maximum(jnp.max(s_loc, axis=-1, keepdims=True),
                                        jnp.max(s_ctx, axis=-1, keepdims=True)), sink)
            e_loc = jnp.exp2(s_loc - m)
            e_ctx = jnp.exp2(s_ctx - m)
            den = (jnp.sum(e_loc, axis=-1, keepdims=True) + jnp.sum(e_ctx, axis=-1, keepdims=True)
                   + jnp.exp2(sink - m))
            o = (jnp.dot(e_loc.astype(BF16), vband[qi * BLOCK:qi * BLOCK + n_loc], preferred_element_type=F32)
                 + jnp.dot(e_ctx.astype(BF16), vc, preferred_element_type=F32)) * (1.0 / den)
            for g in range(GQA_GROUP):
                hd = kvh * GQA_GROUP + g
                o_ref[qi * BLOCK:(qi + 1) * BLOCK, hd * HEAD_DIM:(hd + 1) * HEAD_DIM] = (
                    o[g * BLOCK:(g + 1) * BLOCK].astype(BF16))


def _attn_a_call(proj, cproj, sink, *, batch, seq, ctx_len, q_col, k_col, v_col, ck_col, cv_col):
    nqb = seq // QB_A
    sub = QB_A // BLOCK
    n_blocks = seq // BLOCK

    def prev_map(b, n, col):
        return (b * n_blocks + jnp.maximum(n * sub - 1, 0), col // W_KA)

    def next_map(b, n, col):
        return (b * n_blocks + jnp.minimum(n * sub + sub, n_blocks - 1), col // W_KA)

    def main_map(b, n, col):
        return (b * nqb + n, col // W_KA)

    edge = lambda f, col: pl.BlockSpec((BLOCK, W_KA), functools.partial(f, col=col))
    main = lambda col: pl.BlockSpec((QB_A, W_KA), functools.partial(main_map, col=col))
    ctxs = lambda col: pl.BlockSpec((ctx_len, W_KA), lambda b, n: (b, col // W_KA))
    return pl.pallas_call(
        functools.partial(_attn_a_kernel, n_blocks=n_blocks),
        grid=(batch, nqb),
        in_specs=[pl.BlockSpec(memory_space=pltpu.SMEM),
                  pl.BlockSpec((QB_A, W_QA), lambda b, n: (b * nqb + n, q_col // W_QA)),
                  edge(prev_map, k_col), main(k_col), edge(next_map, k_col),
                  edge(prev_map, v_col), main(v_col), edge(next_map, v_col),
                  ctxs(ck_col), ctxs(cv_col)],
        out_specs=pl.BlockSpec((QB_A, W_QA), lambda b, n: (b * nqb + n, 0)),
        out_shape=jax.ShapeDtypeStruct((batch * seq, W_QA), BF16),
        compiler_params=pltpu.CompilerParams(dimension_semantics=("arbitrary",) * 2,
                                             vmem_limit_bytes=VMEM_LIMIT_BYTES),
        name="attn_a",
    )(sink, proj, proj, proj, proj, proj, proj, proj, cproj, cproj)


def _attn_b_kernel(q_ref, kp_ref, km_ref, kn_ref, vp_ref, vm_ref, vn_ref, kc_ref, vc_ref, bias_ref, o_ref,
                   *, grid_rows):
    n = pl.program_id(1)
    nq = ROW_GROUP_B * GRID_W
    nk = 3 * nq
    heads = q_ref.shape[1] // HEAD_DIM
    r_io = lax.broadcasted_iota(jnp.int32, (nq, nk), 0)
    c_io = lax.broadcasted_iota(jnp.int32, (nq, nk), 1)
    log_w = GRID_W.bit_length() - 1
    q_col = r_io & (GRID_W - 1)
    k_col = c_io & (GRID_W - 1)
    c_start = jnp.clip(q_col - NB_COLS // 2, 0, GRID_W - NB_COLS)
    col_ok = (k_col >= c_start) & (k_col < c_start + NB_COLS)
    nt = (((1,), (1,)), ((), ()))
    for grp in range(GROUPS_PER_STEP_B):
        gi = n * GROUPS_PER_STEP_B + grp
        r_abs = gi * ROW_GROUP_B + lax.shift_right_logical(r_io, log_w)
        kr_abs = (gi - 1) * ROW_GROUP_B + lax.shift_right_logical(c_io, log_w)
        r_start = jnp.clip(r_abs - NB_ROWS // 2, 0, grid_rows - NB_ROWS)
        valid = (kr_abs >= r_start) & (kr_abs < r_start + NB_ROWS) & col_ok
        cap = jnp.where(valid, F32_MAX, NEG_INF)
        rows = slice(grp * nq, (grp + 1) * nq)
        for hd in range(heads):
            sl = slice(hd * HEAD_DIM, (hd + 1) * HEAD_DIM)
            q = q_ref[rows, sl]
            keys = jnp.concatenate([kp_ref[:, sl], km_ref[:, sl], kn_ref[:, sl]], axis=0)[grp * nq:grp * nq + nk]
            vals = jnp.concatenate([vp_ref[:, sl], vm_ref[:, sl], vn_ref[:, sl]], axis=0)[grp * nq:grp * nq + nk]
            s_nb = lax.dot_general(q, keys, nt, preferred_element_type=F32)
            s_nb = jnp.minimum(s_nb + bias_ref[hd], cap)
            s_ctx = lax.dot_general(q, kc_ref[:, sl], nt, preferred_element_type=F32)
            m = jnp.maximum(jnp.max(s_nb, axis=-1, keepdims=True), jnp.max(s_ctx, axis=-1, keepdims=True))
            e_nb = jnp.exp2(s_nb - m)
            e_ctx = jnp.exp2(s_ctx - m)
            den = jnp.sum(e_nb, axis=-1, keepdims=True) + jnp.sum(e_ctx, axis=-1, keepdims=True)
            o = (jnp.dot(e_nb.astype(BF16), vals, preferred_element_type=F32)
                 + jnp.dot(e_ctx.astype(BF16), vc_ref[:, sl], preferred_element_type=F32)) * (1.0 / den)
            o_ref[rows, sl] = o.astype(BF16)


def _attn_b_call(proj, cproj, bias, *, batch, seq, ctx_len, q_col, k_col, v_col, ck_col, cv_col):
    nq = ROW_GROUP_B * GRID_W
    ng = seq // nq
    gps = GROUPS_PER_STEP_B
    ns = ng // gps
    hw = W_B

    def edge(col, first):
        def imap(b, n):
            g = jnp.maximum(n * gps - 1, 0) if first else jnp.minimum(n * gps + gps, ng - 1)
            return (b * ng + g, col // hw)
        return pl.BlockSpec((nq, hw), imap)

    main = lambda col: pl.BlockSpec((gps * nq, hw), lambda b, n: (b * ns + n, col // hw))
    ctxs = lambda col: pl.BlockSpec((ctx_len, hw), lambda b, n: (b, col // hw))
    return pl.pallas_call(
        functools.partial(_attn_b_kernel, grid_rows=seq // GRID_W),
        grid=(batch, ns),
        in_specs=[main(q_col),
                  edge(k_col, True), main(k_col), edge(k_col, False),
                  edge(v_col, True), main(v_col), edge(v_col, False),
                  ctxs(ck_col), ctxs(cv_col),
                  pl.BlockSpec((hw // HEAD_DIM, nq, bias.shape[2]), lambda b, n: (0, 0, 0))],
        out_specs=pl.BlockSpec((gps * nq, hw), lambda b, n: (b * ns + n, 0)),
        out_shape=jax.ShapeDtypeStruct((batch * seq, W_B), BF16),
        compiler_params=pltpu.CompilerParams(dimension_semantics=("arbitrary",) * 2,
                                             vmem_limit_bytes=VMEM_LIMIT_BYTES),
        name="attn_b",
    )(proj, proj, proj, proj, proj, proj, proj, cproj, cproj, bias)


def _resident_bf16(w_ref, wb_ref):
    @pl.when(pl.program_id(0) == 0)
    def _():
        wb_ref[...] = w_ref[...].astype(BF16)


def _mix_kernel(oa_ref, ob_ref, ga_ref, gb_ref, wa_ref, wb_ref, z_ref, wa_bf, wb_bf):
    _resident_bf16(wa_ref, wa_bf)
    _resident_bf16(wb_ref, wb_bf)
    ya = jnp.dot(oa_ref[...], wa_bf[...], preferred_element_type=F32)
    yb = jnp.dot(ob_ref[...], wb_bf[...], preferred_element_type=F32)
    z_ref[...] = (ga_ref[...].astype(F32) * ya + gb_ref[...].astype(F32) * yb).astype(BF16)


def _mix_call(oa, ob, proj, wa, wb, *, tm):
    n = oa.shape[0]
    d = wa.shape[1]
    const = lambda shape: pl.BlockSpec(shape, lambda i: (0, 0), pipeline_mode=pl.Buffered(1))
    return pl.pallas_call(
        _mix_kernel,
        grid=(n // tm,),
        in_specs=[pl.BlockSpec((tm, oa.shape[1]), lambda i: (i, 0)),
                  pl.BlockSpec((tm, ob.shape[1]), lambda i: (i, 0)),
                  pl.BlockSpec((tm, d), lambda i: (i, 0)),
                  pl.BlockSpec((tm, d), lambda i: (i, 1)),
                  const(wa.shape), const(wb.shape)],
        out_specs=pl.BlockSpec((tm, d), lambda i: (i, 0)),
        out_shape=jax.ShapeDtypeStruct((n, d), BF16),
        scratch_shapes=[pltpu.VMEM(wa.shape, BF16), pltpu.VMEM(wb.shape, BF16)],
        compiler_params=pltpu.CompilerParams(dimension_semantics=("arbitrary",),
                                             vmem_limit_bytes=VMEM_LIMIT_BYTES),
        name="branch_mix",
    )(oa, ob, proj, proj, wa, wb)


def _outnorm_kernel(a_ref, w_ref, x_ref, gt_ref, g_ref, o_ref, w_bf):
    _resident_bf16(w_ref, w_bf)
    y = jnp.dot(a_ref[...], w_bf[...], preferred_element_type=F32)
    o_ref[...] = x_ref[...] + gt_ref[0] * _rms(y, g_ref[...])


def _outnorm_call(a, w, x2, gt, g, *, tm, rows_per_mod):
    n, kdim = a.shape
    d = w.shape[1]
    mod_tiles = rows_per_mod // tm
    return pl.pallas_call(
        _outnorm_kernel,
        grid=(n // tm,),
        in_specs=[pl.BlockSpec((tm, kdim), lambda i: (i, 0)),
                  pl.BlockSpec((kdim, d), lambda i: (0, 0), pipeline_mode=pl.Buffered(1)),
                  pl.BlockSpec((tm, d), lambda i: (i, 0)),
                  pl.BlockSpec((1, 1, d), lambda i: (i // mod_tiles, 0, 0)),
                  pl.BlockSpec((1, d), lambda i: (0, 0))],
        out_specs=pl.BlockSpec((tm, d), lambda i: (i, 0)),
        out_shape=jax.ShapeDtypeStruct((n, d), F32),
        scratch_shapes=[pltpu.VMEM(w.shape, BF16)],
        compiler_params=pltpu.CompilerParams(dimension_semantics=("arbitrary",),
                                             vmem_limit_bytes=VMEM_LIMIT_BYTES),
        name="out_norm",
    )(a, w, x2, gt, g)


def _rowmm_norm_kernel(a_ref, w_ref, x_ref, gt_ref, g_ref, o_ref, *, n_chunks):
    cm = a_ref.shape[0] // n_chunks
    y = jnp.dot(a_ref[0:cm], w_ref[...], preferred_element_type=F32)
    for c in range(n_chunks):
        nxt = None
        if c + 1 < n_chunks:
            nxt = jnp.dot(a_ref[(c + 1) * cm:(c + 2) * cm], w_ref[...], preferred_element_type=F32)
        rows = slice(c * cm, (c + 1) * cm)
        o_ref[rows] = x_ref[rows] + gt_ref[0] * _rms(y, g_ref[...])
        y = nxt


def _rowmm_norm_call(a, w, x2, gt, g, *, tm, n_chunks, rows_per_mod):
    n, kdim = a.shape
    d = w.shape[1]
    mod_tiles = rows_per_mod // tm
    return pl.pallas_call(
        functools.partial(_rowmm_norm_kernel, n_chunks=n_chunks),
        grid=(n // tm,),
        in_specs=[pl.BlockSpec((tm, kdim), lambda i: (i, 0)),
                  pl.BlockSpec((kdim, d), lambda i: (0, 0), pipeline_mode=pl.Buffered(1)),
                  pl.BlockSpec((tm, d), lambda i: (i, 0)),
                  pl.BlockSpec((1, 1, d), lambda i: (i // mod_tiles, 0, 0)),
                  pl.BlockSpec((1, d), lambda i: (0, 0))],
        out_specs=pl.BlockSpec((tm, d), lambda i: (i, 0)),
        out_shape=jax.ShapeDtypeStruct((n, d), F32),
        compiler_params=pltpu.CompilerParams(dimension_semantics=("arbitrary",),
                                             allow_input_fusion=[False, True, False, False, False],
                                             vmem_limit_bytes=VMEM_LIMIT_RESIDENT_BYTES),
        name="rowmm_norm",
    )(a, w, x2, gt, g)


def _ffn_up_kernel(x0p_ref, x0_ref, x0n_ref, sc0_ref, sh0_ref, xp_ref, xm_ref, xn_ref, sc_ref, sh_ref, g_ref,
                   wa_ref, wg_ref, cwa_ref, cwg_ref, cba_ref, cbg_ref,
                   o_ref, h_cur, h_nxt, ua_ref, ug_ref, *, nj, n_tiles, tiles_per_seq, n_pieces):
    t = pl.program_id(0)
    tm = x0_ref.shape[0]
    halo = xp_ref.shape[0]
    pr = xm_ref.shape[0]
    ext = tm + 2 * halo
    margin = 8

    def mod(x, sc, sh):
        return (_rms(x, g_ref[...]) * (1 + sc[0]) + sh[0]).astype(BF16)

    @pl.when(t == 0)
    def _():
        ua_ref[...] = jnp.zeros_like(ua_ref)
        ug_ref[...] = jnp.zeros_like(ug_ref)
        h_cur[0:halo] = mod(x0p_ref[...], sc0_ref, sh0_ref)
        h_cur[halo:halo + tm] = mod(x0_ref[...], sc0_ref, sh0_ref)
        h_cur[halo + tm:ext] = mod(x0n_ref[...], sc0_ref, sh0_ref)

    tc = jnp.minimum(t, n_tiles - 1)

    @pl.when((t > 0) & (t < n_tiles) & (t % nj == 0))
    def _():
        h_cur[...] = h_nxt[...]

    def next_row_piece():
        p = jnp.minimum(tc % nj, n_pieces - 1)
        h_nxt[0:halo] = mod(xp_ref[...], sc_ref, sh_ref)
        h_nxt[pl.ds(pl.multiple_of(halo + p * pr, BF16_ROWS), pr), :] = mod(xm_ref[...], sc_ref, sh_ref)
        h_nxt[halo + tm:ext] = mod(xn_ref[...], sc_ref, sh_ref)

    slot = t % 2
    prev = 1 - slot
    ip = jnp.maximum(t - 1, 0) // nj
    first_row = jnp.where((ip % tiles_per_seq) == 0, 0, -1)
    last_row = jnp.where((ip % tiles_per_seq) == tiles_per_seq - 1, tm - 1, -1)
    row = lax.broadcasted_iota(jnp.int32, (tm, 1), 0)

    def epilogue(lo, hi):
        cm = hi - lo
        rowc = row[lo:hi]

        def conv(u_ref, cw_ref, cb_ref):
            win = u_ref[prev, pl.ds(halo + lo - margin, cm + 2 * margin), :]
            up = pltpu.roll(win, 1, 0)[margin:margin + cm]
            un = pltpu.roll(win, cm + 2 * margin - 1, 0)[margin:margin + cm]
            up = jnp.where(rowc == first_row, 0.0, up)
            un = jnp.where(rowc == last_row, 0.0, un)
            out = cb_ref[...] + up * cw_ref[0:1]
            out = out + win[margin:margin + cm] * cw_ref[1:2]
            return out + un * cw_ref[2:3]

        a = conv(ua_ref, cwa_ref, cba_ref)
        gte = conv(ug_ref, cwg_ref, cbg_ref)
        o_ref[lo:hi] = (jax.nn.silu(gte) * a).astype(BF16)

    half = ext // 2
    third = (tm // 3) // BF16_ROWS * BF16_ROWS
    chunks = [(0, third), (third, 2 * third), (2 * third, tm)]
    ua_ref[slot, 0:half] = jnp.dot(h_cur[0:half], wa_ref[...], preferred_element_type=F32)
    next_row_piece()
    epilogue(*chunks[0])
    ua_ref[slot, half:ext] = jnp.dot(h_cur[half:ext], wa_ref[...], preferred_element_type=F32)
    epilogue(*chunks[1])
    ug_ref[slot, 0:half] = jnp.dot(h_cur[0:half], wg_ref[...], preferred_element_type=F32)
    epilogue(*chunks[2])
    ug_ref[slot, half:ext] = jnp.dot(h_cur[half:ext], wg_ref[...], preferred_element_type=F32)


def _ffn_up_call(x2, sc, sh, g, w_up, conv_w, conv_b, *, tm, tn, seq):
    n, d = x2.shape
    dff = w_up.shape[1] // 2
    nj = dff // tn
    halo = BF16_ROWS
    hb = tm // halo
    n_halo = n // halo
    tiles_per_seq = seq // tm
    n_rows = n // tm
    n_tiles = n_rows * nj
    pr = 128
    n_pieces = tm // pr
    assert (tm + 2 * halo) % (2 * BF16_ROWS) == 0 and n_pieces <= nj and n_pieces * pr == tm

    def cur(t):
        tc = jnp.minimum(t, n_tiles - 1)
        return tc // nj, tc % nj

    def prv(t):
        tp = jnp.maximum(t - 1, 0)
        return tp // nj, tp % nj

    def nxt(t):
        return jnp.minimum(cur(t)[0] + 1, n_rows - 1)

    once = lambda shape, imap: pl.BlockSpec(shape, imap, pipeline_mode=pl.Buffered(1))
    return pl.pallas_call(
        functools.partial(_ffn_up_kernel, nj=nj, n_tiles=n_tiles, tiles_per_seq=tiles_per_seq, n_pieces=n_pieces),
        grid=(n_tiles + 1,),
        in_specs=[once((halo, d), lambda t: (0, 0)),
                  once((tm, d), lambda t: (0, 0)),
                  once((halo, d), lambda t: (jnp.minimum(hb, n_halo - 1), 0)),
                  once((1, 1, d), lambda t: (0, 0, 0)),
                  once((1, 1, d), lambda t: (0, 0, 0)),
                  pl.BlockSpec((halo, d), lambda t: (jnp.maximum(nxt(t) * hb - 1, 0), 0)),
                  pl.BlockSpec((pr, d), lambda t: (nxt(t) * n_pieces + jnp.minimum(cur(t)[1], n_pieces - 1), 0)),
                  pl.BlockSpec((halo, d), lambda t: (jnp.minimum((nxt(t) + 1) * hb, n_halo - 1), 0)),
                  pl.BlockSpec((1, 1, d), lambda t: (nxt(t) // tiles_per_seq, 0, 0)),
                  pl.BlockSpec((1, 1, d), lambda t: (nxt(t) // tiles_per_seq, 0, 0)),
                  pl.BlockSpec((1, d), lambda t: (0, 0)),
                  pl.BlockSpec((d, tn), lambda t: (0, cur(t)[1])),
                  pl.BlockSpec((d, tn), lambda t: (0, cur(t)[1] + nj)),
                  pl.BlockSpec((CONV_W, tn), lambda t: (0, prv(t)[1])),
                  pl.BlockSpec((CONV_W, tn), lambda t: (0, prv(t)[1] + nj)),
                  pl.BlockSpec((1, tn), lambda t: (0, prv(t)[1])),
                  pl.BlockSpec((1, tn), lambda t: (0, prv(t)[1] + nj))],
        out_specs=pl.BlockSpec((tm, tn), lambda t: prv(t)),
        out_shape=jax.ShapeDtypeStruct((n, dff), BF16),
        scratch_shapes=[pltpu.VMEM((tm + 2 * halo, d), BF16),
                        pltpu.VMEM((tm + 2 * halo, d), BF16),
                        pltpu.VMEM((2, tm + 2 * halo, tn), F32),
                        pltpu.VMEM((2, tm + 2 * halo, tn), F32)],
        compiler_params=pltpu.CompilerParams(dimension_semantics=("arbitrary",),
                                             allow_input_fusion=[False] * 11 + [True, True] + [False] * 4,
                                             vmem_limit_bytes=VMEM_LIMIT_BYTES),
        name="ffn_up",
    )(x2, x2, x2, sc, sh, x2, x2, x2, sc, sh, g, w_up, w_up, conv_w, conv_w, conv_b, conv_b)


def _rope_tables(seq):
    t = np.arange(seq)
    half = HEAD_DIM // 4
    inv = np.float32(ROPE_BASE) ** (-np.arange(half, dtype=np.float32) / np.float32(half))

    def cs(pos):
        ang = pos.astype(np.float32)[:, None] * inv[None, :]
        c, s = np.cos(ang), np.sin(ang)
        return np.concatenate([c, c], axis=1), np.concatenate([-s, s], axis=1)

    cr, sr = cs(t // GRID_W)
    cc, sn = cs(t % GRID_W)
    return (jnp.asarray(np.concatenate([cr, cc], axis=1), F32), jnp.asarray(np.concatenate([sr, sn], axis=1), F32))


def _pick_tile(n, pref):
    while n % pref:
        pref //= 2
    return pref


class _RowTiles(NamedTuple):
    proj: int
    qkv: int
    mix: int
    down: int


def _row_tiles(seq):
    return _RowTiles(*(_pick_tile(seq, pref) for pref in (1024, 4096, 512, 256)))


def kernel(x, c, ctx, c_ctx, w_mod, b_mod, g_attn_pre, g_attn_post, g_ffn_pre, g_ffn_post, w_in, sink_a, rpb_b,
           w_br_a, w_br_b, w_o, w_up, conv_w, conv_b, w_down):
    batch, seq, d = x.shape
    ctx_len = ctx.shape[1]
    assert w_mod.shape[0] == 1 and d == 16 * HEAD_DIM
    assert seq % QB_A == 0 and seq % (GROUPS_PER_STEP_B * ROW_GROUP_B * GRID_W) == 0 and WINDOW == BLOCK
    n = batch * seq
    x2 = x.reshape(n, d)

    pad = (-(batch + 1)) % 8
    cs = jnp.concatenate([c, c_ctx[None, :], jnp.zeros((pad, d), F32)], axis=0)
    mod = _mod_call(cs, w_mod[0], b_mod[0][None, :])
    sh1, sc1, gt1, sh2, sc2, gt2 = [mod[:batch, k * d:(k + 1) * d].reshape(batch, 1, d) for k in range(6)]
    csh1 = mod[batch:batch + 1, 0:d].reshape(1, 1, d)
    csc1 = mod[batch:batch + 1, d:2 * d].reshape(1, 1, d)

    n_qkv_cols = W_QA + 2 * W_KA + 3 * W_B
    n_gate_cols = 2 * d
    w_in_bf = w_in[0].astype(BF16)
    n_tiles = n_qkv_cols // TN
    c_qa = 0
    c_qb = c_qa + W_QA
    c_kb = c_qb + W_B
    c_vb = c_kb + W_B
    c_ka = c_vb + W_B
    c_va = c_ka + W_KA
    cos, sin = _rope_tables(seq)
    tiles = _row_tiles(seq)
    gates, h1 = _gate_call(x2, sc1, sh1, g_attn_pre, w_in_bf, tm=tiles.proj, tn=TN_GATE, col0=n_qkv_cols,
                           n_cols=n_gate_cols, rows_per_mod=seq)
    proj = _qkv_call(h1, w_in_bf, cos, sin, tm=tiles.qkv, j0=0, nj=n_tiles)
    nc = batch * ctx_len
    j0c = c_kb // TN
    cproj = _inproj_call(ctx.reshape(nc, d), csc1, csh1, g_attn_pre, w_in_bf,
                         jnp.ones((nc, HEAD_DIM), F32), jnp.zeros((nc, HEAD_DIM), F32),
                         tm=nc, rows_per_mod=nc, j0=j0c, nj=n_tiles - j0c)
    cc_kb, cc_vb, cc_ka, cc_va = 0, W_B, 2 * W_B, 2 * W_B + W_KA

    oa = _attn_a_call(proj, cproj, sink_a[0], batch=batch, seq=seq, ctx_len=ctx_len,
                      q_col=c_qa, k_col=c_ka, v_col=c_va, ck_col=cc_ka, cv_col=cc_va)
    bias = _rpb_call(rpb_b[0])
    ob = _attn_b_call(proj, cproj, bias, batch=batch, seq=seq, ctx_len=ctx_len,
                      q_col=c_qb, k_col=c_kb, v_col=c_vb, ck_col=cc_kb, cv_col=cc_vb)

    z = _mix_call(oa, ob, gates, w_br_a[0], w_br_b[0], tm=tiles.mix)
    x1 = _outnorm_call(z, w_o[0], x2, gt1, g_attn_post, tm=tiles.mix, rows_per_mod=seq)

    act = _ffn_up_call(x1, sc2, sh2, g_ffn_pre, w_up[0].astype(BF16), conv_w[0], conv_b[0][None, :],
                       tm=tiles.proj, tn=TN, seq=seq)
    out = _rowmm_norm_call(act, w_down[0].astype(BF16), x1, gt2, g_ffn_post, tm=tiles.down, n_chunks=2,
                           rows_per_mod=seq)
    return out.reshape(batch, seq, d)
```
